```python
import jax, jax.numpy as jnp
from jax import lax
import numpy as np

D_MODEL = 1024
BATCH = 4
SEQ = 8192
DEPTH = 1

CHUNK = 64
HEAD_DIM = 64
ATT_WIDTH = D_MODEL // 2
ATT_HEADS = ATT_WIDTH // HEAD_DIM
LEFT_CHUNKS = 8
BAND = LEFT_CHUNKS + 1
MAX_LEFT_REL = 128
REL_TABLE = MAX_LEFT_REL + CHUNK
RWKV_HEAD = 64
RWKV_WIDTH = D_MODEL - ATT_WIDTH
RWKV_HEADS = RWKV_WIDTH // RWKV_HEAD
DECAY_LORA = 64
AAA_LORA = 64
GATE_LORA = 128
RWKV_COLS = 3 * RWKV_WIDTH + DECAY_LORA + AAA_LORA + GATE_LORA
IN_COLS = 3 * ATT_WIDTH + RWKV_COLS
MIX_WIDTH = ATT_WIDTH + RWKV_WIDTH
LN_X_EPS = 64e-5
N_EXPERTS = 32
TOP_K = 4
D_EXPERT = D_MODEL
SWIGLU_ALPHA = 1.702
SWIGLU_LIMIT = 7.0
MOE_BLOCK = 256
NORM_EPS = 1e-6

kernel_name = "hymba_chunkattn_rwkv7_moe_block"


def rms_norm(x, g):
    xf = x.astype(jnp.float32)
    y = xf * lax.rsqrt(jnp.mean(xf * xf, axis=-1, keepdims=True) + NORM_EPS)
    return (y * g.astype(jnp.float32)).astype(x.dtype)


def chunk_band_attention(q, k, v, rel_bias):
    B, S = q.shape[:2]
    NC = S // CHUNK

    def to_chunks(t):
        return t.reshape(B, NC, CHUNK, ATT_HEADS, HEAD_DIM).transpose(0, 3, 1, 2, 4)

    qc = to_chunks(q) * (HEAD_DIM ** -0.5)
    pad = ((0, 0), (0, 0), (LEFT_CHUNKS, 0), (0, 0), (0, 0))
    kp = jnp.pad(to_chunks(k), pad)
    vp = jnp.pad(to_chunks(v), pad)
    scores = jnp.stack(
        [jnp.einsum('bhncd,bhnkd->bhnck', qc, kp[:, :, j:j + NC]) for j in range(BAND)],
        axis=-2).astype(jnp.float32)
    qi = jnp.arange(CHUNK)[:, None, None]
    bj = jnp.arange(BAND)[None, :, None]
    kc = jnp.arange(CHUNK)[None, None, :]
    rel = (bj - LEFT_CHUNKS) * CHUNK + kc - qi
    idx = jnp.clip(rel, -MAX_LEFT_REL, CHUNK - 1) + MAX_LEFT_REL
    bias = rel_bias[:, idx].astype(jnp.float32)
    valid = (jnp.arange(NC)[:, None] + jnp.arange(BAND)[None, :]) >= LEFT_CHUNKS
    scores = jnp.where(valid[None, None, :, None, :, None],
                       scores + bias[None, :, None],
                       jnp.finfo(jnp.float32).min)
    probs = jax.nn.softmax(scores.reshape(B, ATT_HEADS, NC, CHUNK, BAND * CHUNK), axis=-1)
    probs = probs.reshape(scores.shape).astype(v.dtype)
    out = jnp.einsum('bhnck,bhnkd->bhncd', probs[..., 0, :], vp[:, :, 0:NC])
    for j in range(1, BAND):
        out = out + jnp.einsum('bhnck,bhnkd->bhncd', probs[..., j, :], vp[:, :, j:j + NC])
    return out.transpose(0, 2, 3, 1, 4).reshape(B, S, ATT_WIDTH)


def rwkv7_scan(r, w, k, v, kk, a):
    B, S, H, N = r.shape

    def step(state, inp):
        r_t, w_t, k_t, v_t, kk_t, a_t = inp
        sa = jnp.einsum('bhvk,bhk->bhv', state, -kk_t)
        state = (state * w_t[:, :, None, :]
                 + sa[..., None] * (kk_t * a_t)[:, :, None, :]
                 + v_t[..., None] * k_t[:, :, None, :])
        y = jnp.einsum('bhvk,bhk->bhv', state, r_t)
        return state, y

    xs = tuple(jnp.moveaxis(t, 1, 0) for t in (r, w, k, v, kk, a))
    state0 = jnp.zeros((B, H, N, N), jnp.float32)
    _, ys = lax.scan(step, state0, xs)
    return jnp.moveaxis(ys, 0, 1)


def rwkv7_mixer(p, mu, w0, w_decay_up, a0, w_aaa_up, w_gate_up, k_k, k_a, r_k, ln_g, ln_b):
    B, S, _ = p.shape
    f32 = jnp.float32
    prev = jnp.pad(p, ((0, 0), (1, 0), (0, 0)))[:, :S]
    p = p + (prev - p) * mu
    C = RWKV_WIDTH
    r, k, v, wd, ad, gd = jnp.split(
        p, [C, 2 * C, 3 * C, 3 * C + DECAY_LORA, 3 * C + DECAY_LORA + AAA_LORA], axis=-1)
    w_logit = (w0 + jnp.tanh(wd) @ w_decay_up).astype(f32)
    decay = jnp.exp(-jnp.exp(-jax.nn.softplus(-w_logit) - 0.5))
    a = jax.nn.sigmoid((a0 + ad @ w_aaa_up).astype(f32))
    g = jax.nn.sigmoid(gd) @ w_gate_up

    def heads(t):
        return t.reshape(B, S, RWKV_HEADS, RWKV_HEAD).astype(f32)

    r, k, v, decay, a = heads(r), heads(k), heads(v), heads(decay), heads(a)
    kk = k * k_k.astype(f32)
    kk = kk / jnp.maximum(jnp.linalg.norm(kk, axis=-1, keepdims=True), 1e-12)
    k = k * (1.0 + (a - 1.0) * k_a.astype(f32))
    y = rwkv7_scan(r, decay, k, v, kk, a)
    mean = jnp.mean(y, axis=-1, keepdims=True)
    var = jnp.mean(jnp.square(y - mean), axis=-1, keepdims=True)
    y = ((y - mean) * lax.rsqrt(var + LN_X_EPS)).reshape(B, S, C)
    y = y * ln_g.astype(f32) + ln_b.astype(f32)
    bonus = jnp.sum(r * k * r_k.astype(f32), axis=-1, keepdims=True) * v
    y = (y + bonus.reshape(B, S, C)) * g.astype(f32)
    return y.astype(p.dtype)


def moe_ffn(xn, w_router, b_router, w_up, b_up, w_down, b_down):
    B, S, D = xn.shape
    T = B * S
    xf = xn.reshape(T, D)
    logits = (xf @ w_router).astype(jnp.float32) + b_router.astype(jnp.float32)
    top_vals, top_idx = lax.top_k(logits, TOP_K)
    gates = jax.nn.softmax(top_vals, axis=-1).astype(xn.dtype)
    A = T * TOP_K
    e_flat = top_idx.reshape(A).astype(jnp.int32)
    t_flat = jnp.repeat(jnp.arange(T, dtype=jnp.int32), TOP_K)
    g_flat = gates.reshape(A)
    counts = jnp.bincount(e_flat, length=N_EXPERTS).astype(jnp.int32)
    padded = (counts + MOE_BLOCK - 1) // MOE_BLOCK * MOE_BLOCK
    pad_end = jnp.cumsum(padded)
    pad_start = pad_end - padded
    start = jnp.cumsum(counts) - counts
    order = jnp.argsort(e_flat)
    e_sorted = e_flat[order]
    dest = pad_start[e_sorted] + jnp.arange(A, dtype=jnp.int32) - start[e_sorted]
    n_blocks = -(-A // MOE_BLOCK) + N_EXPERTS
    n_rows = n_blocks * MOE_BLOCK
    row_tok = jnp.full((n_rows,), T, jnp.int32).at[dest].set(t_flat[order])
    row_gate = jnp.zeros((n_rows,), xn.dtype).at[dest].set(g_flat[order])
    blk_start = jnp.arange(n_blocks, dtype=jnp.int32) * MOE_BLOCK
    blk_expert = jnp.minimum(jnp.searchsorted(pad_end, blk_start, side='right'), N_EXPERTS - 1)
    x_pad = jnp.concatenate([xf, jnp.zeros((1, D), xf.dtype)], axis=0)

    def expert_block(args):
        tok, gate, e = args
        h = x_pad[tok] @ w_up[e] + b_up[e]
        h_glu = jnp.minimum(h[:, :D_EXPERT], SWIGLU_LIMIT)
        h_lin = jnp.clip(h[:, D_EXPERT:], -SWIGLU_LIMIT, SWIGLU_LIMIT)
        act = h_glu * jax.nn.sigmoid(SWIGLU_ALPHA * h_glu) * (h_lin + 1.0)
        return (act @ w_down[e] + b_down[e]) * gate[:, None]

    y_rows = lax.map(expert_block, (row_tok.reshape(n_blocks, MOE_BLOCK),
                                    row_gate.reshape(n_blocks, MOE_BLOCK),
                                    blk_expert))
    y = jax.ops.segment_sum(y_rows.reshape(n_rows, D), row_tok, num_segments=T + 1)[:T]
    return y.reshape(B, S, D).astype(xn.dtype)


def setup_inputs(seed: int = 0) -> dict:
    key = jax.random.key(seed)
    ks = jax.random.split(key, 32)
    L = DEPTH

    def nrm(k, shape, scale):
        return jax.random.normal(k, shape, jnp.float32) * scale

    return {
        "x": nrm(ks[0], (BATCH, SEQ, D_MODEL), 1.0),
        "norm1_g": 1.0 + nrm(ks[1], (L, D_MODEL), 0.01),
        "w_in": nrm(ks[2], (L, D_MODEL, IN_COLS), D_MODEL ** -0.5),
        "tshift_mu": jax.random.uniform(ks[3], (L, RWKV_COLS), jnp.float32),
        "q_norm_g": 1.0 + nrm(ks[4], (L, HEAD_DIM), 0.01),
        "k_norm_g": 1.0 + nrm(ks[5], (L, HEAD_DIM), 0.01),
        "rel_bias": nrm(ks[6], (L, ATT_HEADS, REL_TABLE), 0.5),
        "w0": nrm(ks[7], (L, RWKV_WIDTH), 0.5),
        "w_decay_up": nrm(ks[8], (L, DECAY_LORA, RWKV_WIDTH), 0.5 * DECAY_LORA ** -0.5),
        "a0": nrm(ks[9], (L, RWKV_WIDTH), 0.5),
        "w_aaa_up": nrm(ks[10], (L, AAA_LORA, RWKV_WIDTH), 0.5 * AAA_LORA ** -0.5),
        "w_gate_up": nrm(ks[11], (L, GATE_LORA, RWKV_WIDTH), GATE_LORA ** -0.5),
        "k_k": 1.0 + nrm(ks[12], (L, RWKV_HEADS, RWKV_HEAD), 0.1),
        "k_a": 1.0 + nrm(ks[13], (L, RWKV_HEADS, RWKV_HEAD), 0.1),
        "r_k": nrm(ks[14], (L, RWKV_HEADS, RWKV_HEAD), 0.1),
        "ln_x_g": 1.0 + nrm(ks[15], (L, RWKV_WIDTH), 0.01),
        "ln_x_b": nrm(ks[16], (L, RWKV_WIDTH), 0.01),
        "w_out": nrm(ks[17], (L, MIX_WIDTH, D_MODEL), MIX_WIDTH ** -0.5),
        "norm2_g": 1.0 + nrm(ks[18], (L, D_MODEL), 0.01),
        "w_router": nrm(ks[19], (L, D_MODEL, N_EXPERTS), D_MODEL ** -0.5),
        "b_router": nrm(ks[20], (L, N_EXPERTS), 0.01),
        "w_up": nrm(ks[21], (L, N_EXPERTS, D_MODEL, 2 * D_EXPERT), D_MODEL ** -0.5),
        "b_up": nrm(ks[22], (L, N_EXPERTS, 2 * D_EXPERT), 0.01),
        "w_down": nrm(ks[23], (L, N_EXPERTS, D_EXPERT, D_MODEL), D_EXPERT ** -0.5),
        "b_down": nrm(ks[24], (L, N_EXPERTS, D_MODEL), 0.01),
    }


def reference(x, norm1_g, w_in, tshift_mu, q_norm_g, k_norm_g, rel_bias, w0, w_decay_up,
              a0, w_aaa_up, w_gate_up, k_k, k_a, r_k, ln_x_g, ln_x_b, w_out, norm2_g,
              w_router, b_router, w_up, b_up, w_down, b_down):
    B, S, _ = x.shape
    for l in range(DEPTH):
        h = rms_norm(x, norm1_g[l])
        proj = h @ w_in[l]
        aq, ak, av, p_rwkv = jnp.split(proj, [ATT_WIDTH, 2 * ATT_WIDTH, 3 * ATT_WIDTH], axis=-1)
        q = rms_norm(aq.reshape(B, S, ATT_HEADS, HEAD_DIM), q_norm_g[l])
        k = rms_norm(ak.reshape(B, S, ATT_HEADS, HEAD_DIM), k_norm_g[l])
        v = av.reshape(B, S, ATT_HEADS, HEAD_DIM)
        att = chunk_band_attention(q, k, v, rel_bias[l])
        rwk = rwkv7_mixer(p_rwkv, tshift_mu[l], w0[l], w_decay_up[l], a0[l], w_aaa_up[l],
                          w_gate_up[l], k_k[l], k_a[l], r_k[l], ln_x_g[l], ln_x_b[l])
        x = x + (jnp.concatenate([att, rwk], axis=-1) @ w_out[l]).astype(x.dtype)
        x = x + moe_ffn(rms_norm(x, norm2_g[l]), w_router[l], b_router[l], w_up[l], b_up[l],
                        w_down[l], b_down[l])
    return x
```

```python
import functools
import math

import jax
import jax.numpy as jnp
from jax import lax
from jax.experimental import pallas as pl
from jax.experimental.pallas import tpu as pltpu

F32 = jnp.float32
BF16 = jnp.bfloat16

CHUNK = 64
HEAD_DIM = 64
LEFT_CHUNKS = 8
MAX_LEFT_REL = 128
TOP_K = 4
SWIGLU_ALPHA = 1.702
SWIGLU_LIMIT = 7.0
NORM_EPS = 1e-6
LN_X_EPS = 64e-5
DECAY_LORA = 64
AAA_LORA = 64
GATE_LORA = 128

LANES = 128
GROUP = 256
NEG_BIG = -1e30

VMEM_LIMIT = 56 * 1024 * 1024

ATT_QROWS = 512
ATT_SUB = 128
ATT_KEYS = ATT_SUB + LEFT_CHUNKS * CHUNK

MOE_ROWS = 256
COMBINE_ROWS = 128


def _dot(a, b):
    return jnp.dot(a, b, preferred_element_type=F32)


def _dot_nt(a, b):
    return lax.dot_general(a, b, (((1,), (1,)), ((), ())), preferred_element_type=F32)


def _split_dot(a, b_exact, terms=2):
    acc = None
    rem = a
    for _ in range(terms):
        part = rem.astype(BF16)
        rem = rem - part.astype(F32)
        d = _dot(part, b_exact)
        acc = d if acc is None else acc + d
    return acc


def _sigmoid(x):
    return 1.0 / (1.0 + jnp.exp(-x))


def _inproj_kernel(x_ref, g_ref, w_ref, qg_ref, kg_ref, bd_ref, qkv_ref, pr_ref, *, att_w):
    x = x_ref[...]
    ms = jnp.mean(x * x, axis=-1, keepdims=True)
    h = (x * lax.rsqrt(ms + NORM_EPS) * g_ref[...]).astype(BF16)
    qkv = _dot(h, w_ref[:, : 3 * att_w])
    bd = bd_ref[...]

    def head_norm(t, g):
        ss = _split_dot(t * t, bd)
        return t * lax.rsqrt(ss * (1.0 / HEAD_DIM) + NORM_EPS) * g

    q = head_norm(qkv[:, :att_w], qg_ref[...]) * (HEAD_DIM ** -0.5)
    k = head_norm(qkv[:, att_w: 2 * att_w], kg_ref[...])
    qkv_ref[:, :att_w] = q.astype(BF16)
    qkv_ref[:, att_w: 2 * att_w] = k.astype(BF16)
    qkv_ref[:, 2 * att_w:] = qkv[:, 2 * att_w:].astype(BF16)
    pr_ref[...] = _dot(h, w_ref[:, 3 * att_w:]).astype(BF16)


def _inproj(x2d, g1, w_in_bf, qg, kg, bd, att_w, tm=512):
    T, D = x2d.shape
    in_cols = w_in_bf.shape[1]
    r_cols = in_cols - 3 * att_w
    const = lambda i: (0, 0)
    return pl.pallas_call(
        functools.partial(_inproj_kernel, att_w=att_w),
        grid=(T // tm,),
        in_specs=[
            pl.BlockSpec((tm, D), lambda i: (i, 0)),
            pl.BlockSpec((1, D), const),
            pl.BlockSpec((D, in_cols), const),
            pl.BlockSpec((1, att_w), const),
            pl.BlockSpec((1, att_w), const),
            pl.BlockSpec((att_w, att_w), const),
        ],
        out_specs=[
            pl.BlockSpec((tm, 3 * att_w), lambda i: (i, 0)),
            pl.BlockSpec((tm, r_cols), lambda i: (i, 0)),
        ],
        out_shape=[
            jax.ShapeDtypeStruct((T, 3 * att_w), BF16),
            jax.ShapeDtypeStruct((T, r_cols), BF16),
        ],
        compiler_params=pltpu.CompilerParams(
            dimension_semantics=("arbitrary",), vmem_limit_bytes=VMEM_LIMIT),
        name="inproj",
    )(x2d, g1, w_in_bf, qg, kg, bd)


def _attn_kernel(q_ref, kp_ref, kc_ref, vp_ref, vc_ref, bias_ref, o_ref, kbuf, vbuf):
    i = pl.program_id(2)
    kbuf[:ATT_QROWS, :] = kp_ref[...]
    kbuf[ATT_QROWS:, :] = kc_ref[...]
    vbuf[:ATT_QROWS, :] = vp_ref[...]
    vbuf[ATT_QROWS:, :] = vc_ref[...]
    col = lax.broadcasted_iota(jnp.int32, (ATT_SUB, ATT_KEYS), 1)
    for qs in range(ATT_QROWS // ATT_SUB):
        r0 = qs * ATT_SUB
        first_valid = ATT_QROWS - i * ATT_QROWS - r0
        outs = []
        for h in range(LANES // HEAD_DIM):
            lo = h * HEAD_DIM
            qh = q_ref[r0: r0 + ATT_SUB, lo: lo + HEAD_DIM]
            kh = kbuf[r0: r0 + ATT_KEYS, lo: lo + HEAD_DIM]
            vh = vbuf[r0: r0 + ATT_KEYS, lo: lo + HEAD_DIM]
            s = _dot_nt(qh, kh) + bias_ref[h]
            s = jnp.where(col >= first_valid, s, NEG_BIG)
            m = jnp.max(s, axis=-1, keepdims=True)
            p = jnp.exp(s - m)
            l = jnp.sum(p, axis=-1, keepdims=True)
            o = _dot(p.astype(BF16), vh)
            outs.append(o / l)
        o_ref[r0: r0 + ATT_SUB, :] = jnp.concatenate(outs, axis=-1).astype(BF16)


def _attention(qkv, bias, B, S, att_w):
    T = B * S
    n_i = S // ATT_QROWS
    n_hp = att_w // LANES
    heads_per = LANES // HEAD_DIM

    def cur(col0):
        return lambda b, hp, i: (b * n_i + i, col0 + hp)

    def prev(col0):
        return lambda b, hp, i: (b * n_i + jnp.maximum(i - 1, 0), col0 + hp)

    blk = (ATT_QROWS, LANES)
    return pl.pallas_call(
        _attn_kernel,
        grid=(B, n_hp, n_i),
        in_specs=[
            pl.BlockSpec(blk, cur(0)),
            pl.BlockSpec(blk, prev(n_hp)),
            pl.BlockSpec(blk, cur(n_hp)),
            pl.BlockSpec(blk, prev(2 * n_hp)),
            pl.BlockSpec(blk, cur(2 * n_hp)),
            pl.BlockSpec((heads_per, ATT_SUB, ATT_KEYS), lambda b, hp, i: (hp, 0, 0)),
        ],
        out_specs=pl.BlockSpec(blk, lambda b, hp, i: (b * n_i + i, hp)),
        out_shape=jax.ShapeDtypeStruct((T, att_w), BF16),
        scratch_shapes=[
            pltpu.VMEM((2 * ATT_QROWS, LANES), BF16),
            pltpu.VMEM((2 * ATT_QROWS, LANES), BF16),
        ],
        compiler_params=pltpu.CompilerParams(
            dimension_semantics=("arbitrary", "arbitrary", "arbitrary"),
            vmem_limit_bytes=VMEM_LIMIT),
        name="band_attention",
    )(qkv, qkv, qkv, qkv, qkv, bias)


def _attention_bias(rel_bias):
    r = jnp.arange(ATT_SUB)[:, None]
    c = jnp.arange(ATT_KEYS)[None, :]
    rel = c - LEFT_CHUNKS * CHUNK - r
    idx = jnp.clip(rel, -MAX_LEFT_REL, CHUNK - 1) + MAX_LEFT_REL
    band = c // CHUNK - r // CHUNK
    ok = (band >= 0) & (band <= LEFT_CHUNKS)
    return jnp.where(ok[None], rel_bias[:, idx].astype(F32), NEG_BIG)


def _rwkv_kernel(p_ref, pprev_ref, mu_ref, w0_ref, a0_ref, lora_ref, wg_ref, kk_ref, ka_ref,
                 rk_ref, lng_ref, lnb_ref, bd_ref, tri_ref, o_ref,
                 rt_s, at_s, bt_s, kt_s, bh_s, kh_s, v_s, wl_s, y_s, z_s, *, width, tm):
    i = pl.program_id(1)
    n_groups = width // GROUP
    heads_per_group = GROUP // HEAD_DIM

    @pl.when(i == 0)
    def _():
        z_s[...] = jnp.zeros_like(z_s)

    p = p_ref[...].astype(F32)
    first = jnp.where(i > 0, 1.0, 0.0)
    prev_row = pprev_ref[7:8, :].astype(F32) * first
    row = lax.broadcasted_iota(jnp.int32, (tm, 1), 0)
    prev = jnp.where(row == 0, prev_row, pltpu.roll(p, 1, axis=0))
    ps = p + (prev - p) * mu_ref[...]

    r = ps[:, :width]
    k = ps[:, width: 2 * width]
    v = ps[:, 2 * width: 3 * width]
    wa = ps[:, 3 * width: 3 * width + DECAY_LORA + AAA_LORA]
    gd = ps[:, 3 * width + DECAY_LORA + AAA_LORA:]
    lane = lax.broadcasted_iota(jnp.int32, (1, DECAY_LORA + AAA_LORA), 1)
    wa = jnp.where(lane < DECAY_LORA, jnp.tanh(wa), wa)
    lw = _dot(wa.astype(BF16), lora_ref[...])
    logw = -math.exp(-0.5) * _sigmoid(w0_ref[...] + lw[:, :width])
    eta = _sigmoid(a0_ref[...] + lw[:, width:])
    g = _dot(_sigmoid(gd).astype(BF16), wg_ref[...])

    bd = bd_ref[...]
    kk = k * kk_ref[...]
    kk = kk / jnp.maximum(jnp.sqrt(_split_dot(kk * kk, bd)), 1e-12)
    km = k * (1.0 + (eta - 1.0) * ka_ref[...])
    bb = kk * eta

    sums = _split_dot_left(tri_ref[...], logw)
    cs = sums[:tm]
    tot = sums[tm:]
    e_pos = jnp.exp(cs)
    e_neg = jnp.exp(-cs)
    e_tail = jnp.exp(tot - cs)
    rt_s[...] = r * e_pos
    at_s[...] = -kk * jnp.exp(cs - logw)
    bt_s[...] = bb * e_neg
    kt_s[...] = km * e_neg
    bh_s[...] = bb * e_tail
    kh_s[...] = km * e_tail
    v_s[...] = v
    wl_s[...] = jnp.exp(tot)

    rr = lax.broadcasted_iota(jnp.int32, (GROUP, GROUP), 0)
    cc = lax.broadcasted_iota(jnp.int32, (GROUP, GROUP), 1)
    shift = CHUNK.bit_length() - 1
    same = (rr >> shift) == (cc >> shift)
    strict = same & (rr > cc)
    incl = same & (rr >= cc)
    eye = jnp.where(rr == cc, 1.0, 0.0)
    lane_head = lax.broadcasted_iota(jnp.int32, (1, GROUP), 1) >> shift

    def stack_masked(x):
        return jnp.concatenate(
            [jnp.where(lane_head == h, x, 0.0) for h in range(heads_per_group)], axis=0)

    def stack_plain(x):
        return jnp.concatenate([x] * heads_per_group, axis=0)

    def bdot(a, b):
        return _dot(a.astype(BF16), b.astype(BF16))

    def bdot_nt(a, b):
        return _dot_nt(a.astype(BF16), b.astype(BF16))

    def chunk_body(c, carry):
        r0 = pl.multiple_of(c * CHUNK, CHUNK)
        rows = pl.ds(r0, CHUNK)
        for gi in range(n_groups):
            cols = slice(gi * GROUP, (gi + 1) * GROUP)
            at4 = stack_masked(at_s[rows, cols])
            rt4 = stack_masked(rt_s[rows, cols])
            v4 = stack_masked(v_s[rows, cols])
            bh4 = stack_masked(bh_s[rows, cols])
            kh4 = stack_masked(kh_s[rows, cols])
            bt_rep = stack_plain(bt_s[rows, cols])
            kt_rep = stack_plain(kt_s[rows, cols])
            a_ab = jnp.where(strict, bdot_nt(at4, bt_rep), 0.0)
            a_ak = jnp.where(strict, bdot_nt(at4, kt_rep), 0.0)
            a_rb = jnp.where(incl, bdot_nt(rt4, bt_rep), 0.0)
            a_rk = jnp.where(incl, bdot_nt(rt4, kt_rep), 0.0)
            tinv = eye + a_ab
            pw = a_ab
            span = 2
            while span < CHUNK:
                pw = bdot(pw, pw)
                tinv = tinv + bdot(pw, tinv)
                span *= 2
            zt = z_s[gi]
            rhs = bdot_nt(at4, zt) + bdot(a_ak, v4)
            u4 = bdot(tinv, rhs)
            y4 = bdot_nt(rt4, zt) + bdot(a_rb, u4) + bdot(a_rk, v4)
            z_s[gi] = (zt * wl_s[pl.ds(r0, 1), cols]
                       + bdot(u4.T, bh4) + bdot(v4.T, kh4))
            ytok = y4[:CHUNK]
            for h in range(1, heads_per_group):
                ytok = ytok + y4[h * CHUNK: (h + 1) * CHUNK]
            y_s[rows, cols] = ytok
        return carry

    lax.fori_loop(0, tm // CHUNK, chunk_body, 0)

    y = y_s[...]
    inv_n = 1.0 / HEAD_DIM
    mean = _split_dot(y, bd) * inv_n
    d = y - mean
    var = _split_dot(d * d, bd) * inv_n
    yn = d * lax.rsqrt(var + LN_X_EPS) * lng_ref[...] + lnb_ref[...]
    bonus = _split_dot(r * km * rk_ref[...], bd) * v
    o_ref[...] = ((yn + bonus) * g).astype(BF16)


def _split_dot_left(a_exact, b, terms=3):
    acc = None
    rem = b
    for _ in range(terms):
        part = rem.astype(BF16)
        rem = rem - part.astype(F32)
        d = _dot(a_exact, part)
        acc = d if acc is None else acc + d
    return acc


def _rwkv(pr, mu, w0, a0, lora, wg, k_k, k_a, r_k, ln_g, ln_b, bd, B, S, width, tm=512):
    T = B * S
    n_t = S // tm
    cols = pr.shape[1]
    ti = jnp.arange(tm)
    same = (ti[:, None] // CHUNK) == (ti[None, :] // CHUNK)
    tri = jnp.concatenate([same & (ti[:, None] >= ti[None, :]), same], axis=0).astype(BF16)
    const = lambda b, i: (0, 0)
    vec = lambda n: pl.BlockSpec((1, n), const)
    scr = lambda: pltpu.VMEM((tm, width), F32)
    return pl.pallas_call(
        functools.partial(_rwkv_kernel, width=width, tm=tm),
        grid=(B, n_t),
        in_specs=[
            pl.BlockSpec((tm, cols), lambda b, i: (b * n_t + i, 0)),
            pl.BlockSpec((8, cols), lambda b, i: (jnp.maximum((b * n_t + i) * (tm // 8) - 1, 0), 0)),
            vec(cols), vec(width), vec(width),
            pl.BlockSpec(lora.shape, const),
            pl.BlockSpec(wg.shape, const),
            vec(width), vec(width), vec(width), vec(width), vec(width),
            pl.BlockSpec((width, width), const),
            pl.BlockSpec((2 * tm, tm), const),
        ],
        out_specs=pl.BlockSpec((tm, width), lambda b, i: (b * n_t + i, 0)),
        out_shape=jax.ShapeDtypeStruct((T, width), BF16),
        scratch_shapes=[scr() for _ in range(9)]
        + [pltpu.VMEM((width // GROUP, GROUP, GROUP), F32)],
        compiler_params=pltpu.CompilerParams(
            dimension_semantics=("arbitrary", "arbitrary"), vmem_limit_bytes=VMEM_LIMIT),
        name="rwkv7",
    )(pr, pr, mu, w0, a0, lora, wg, k_k, k_a, r_k, ln_g, ln_b, bd, tri)


def _outproj_kernel(att_ref, rwk_ref, x_ref, wa_ref, wr_ref, g_ref, wrt_ref, br_ref, upper_ref,
                    x2_ref, xn_ref, eidx_ref, rank_ref, gate_ref, cnt_ref, carry, *, n_exp, tm):
    i = pl.program_id(0)

    @pl.when(i == 0)
    def _():
        carry[...] = jnp.zeros_like(carry)

    x2 = x_ref[...] + _dot(att_ref[...], wa_ref[...]) + _dot(rwk_ref[...], wr_ref[...])
    x2_ref[...] = x2
    ms = jnp.mean(x2 * x2, axis=-1, keepdims=True)
    xn = x2 * lax.rsqrt(ms + NORM_EPS) * g_ref[...]
    xn_ref[...] = xn

    x_hi = xn.astype(BF16)
    x_lo = (xn - x_hi.astype(F32)).astype(BF16)
    logits = (_dot_nt(wrt_ref[0], x_hi) + _dot_nt(wrt_ref[0], x_lo)
              + _dot_nt(wrt_ref[1], x_hi)) + br_ref[...]

    eio = lax.broadcasted_iota(jnp.int32, (n_exp, tm), 0)
    work = logits
    vals, onehots = [], []
    for k in range(TOP_K):
        m = jnp.max(work, axis=0, keepdims=True)
        idx = jnp.min(jnp.where(work == m, eio, n_exp), axis=0, keepdims=True)
        oh = eio == idx
        vals.append(m)
        onehots.append(oh)
        eidx_ref[k: k + 1, :] = idx
        work = jnp.where(oh, -jnp.inf, work)

    exps = [jnp.exp(vk - vals[0]) for vk in vals]
    denom = exps[0]
    for e in exps[1:]:
        denom = denom + e
    for k in range(TOP_K):
        gate_ref[k: k + 1, :] = exps[k] / denom

    sel = onehots[0]
    for oh in onehots[1:]:
        sel = sel | oh
    sel_f = jnp.where(sel, 1.0, 0.0)
    before = _dot(sel_f.astype(BF16), upper_ref[...]) + carry[:, :1]
    for k in range(TOP_K):
        rank_ref[k: k + 1, :] = jnp.sum(
            jnp.where(onehots[k], before, 0.0), axis=0, keepdims=True).astype(jnp.int32)
    carry[...] = carry[...] + jnp.sum(sel_f, axis=1, keepdims=True)
    cnt_ref[...] = carry[...].astype(jnp.int32)


def _outproj(att, rwk, x2d, w_att, w_rwk, g2, wrt, br, n_exp, tm=512):
    T, D = x2d.shape
    upper = (jnp.arange(tm)[:, None] < jnp.arange(tm)[None, :]).astype(BF16)
    const = lambda i: (0, 0)
    rows = lambda n: pl.BlockSpec((tm, n), lambda i: (i, 0))
    lanes = pl.BlockSpec((TOP_K, tm), lambda i: (0, i))
    return pl.pallas_call(
        functools.partial(_outproj_kernel, n_exp=n_exp, tm=tm),
        grid=(T // tm,),
        in_specs=[
            rows(att.shape[1]), rows(rwk.shape[1]), rows(D),
            pl.BlockSpec(w_att.shape, const), pl.BlockSpec(w_rwk.shape, const),
            pl.BlockSpec((1, D), const),
            pl.BlockSpec(wrt.shape, lambda i: (0, 0, 0)),
            pl.BlockSpec((n_exp, 1), const),
            pl.BlockSpec((tm, tm), const),
        ],
        out_specs=[rows(D), rows(D), lanes, lanes, lanes,
                   pl.BlockSpec((n_exp, LANES), const)],
        out_shape=[
            jax.ShapeDtypeStruct((T, D), F32),
            jax.ShapeDtypeStruct((T, D), F32),
            jax.ShapeDtypeStruct((TOP_K, T), jnp.int32),
            jax.ShapeDtypeStruct((TOP_K, T), jnp.int32),
            jax.ShapeDtypeStruct((TOP_K, T), F32),
            jax.ShapeDtypeStruct((n_exp, LANES), jnp.int32),
        ],
        scratch_shapes=[pltpu.VMEM((n_exp, LANES), F32)],
        compiler_params=pltpu.CompilerParams(
            dimension_semantics=("arbitrary",), vmem_limit_bytes=VMEM_LIMIT),
        name="outproj_router",
    )(att, rwk, x2d, w_att, w_rwk, g2, wrt, br, upper)


def _row_copy(src_hbm, dst_vmem, sem, src_row, dst_row):
    return pltpu.make_async_copy(
        src_hbm.at[pl.ds(src_row, 1)], dst_vmem.at[pl.ds(dst_row, 1)], sem)


def _moe_kernel(bexp_ref, bact_ref, tok_ref, x_hbm, wup_ref, bup_ref, wdn_ref, bdn_ref,
                y_ref, xbuf, wup_bf, wdn_bf, sem, *, d_exp):
    i = pl.program_id(0)

    @pl.when(bact_ref[i] == 1)
    def _():
        def issue(j, c):
            _row_copy(x_hbm, xbuf, sem, tok_ref[0, 0, j], j).start()
            return c

        lax.fori_loop(0, MOE_ROWS, issue, 0)

        new_expert = jnp.logical_or(i == 0, bexp_ref[i] != bexp_ref[jnp.maximum(i - 1, 0)])

        @pl.when(new_expert)
        def _():
            wup_bf[...] = wup_ref[0].astype(BF16)
            wdn_bf[...] = wdn_ref[0].astype(BF16)

        def drain(j, c):
            _row_copy(x_hbm, xbuf, sem, 0, j).wait()
            return c

        lax.fori_loop(0, MOE_ROWS, drain, 0)

        h = _dot(xbuf[...].astype(BF16), wup_bf[...]) + bup_ref[0]
        h_glu = jnp.minimum(h[:, :d_exp], SWIGLU_LIMIT)
        h_lin = jnp.clip(h[:, d_exp:], -SWIGLU_LIMIT, SWIGLU_LIMIT)
        act = h_glu * _sigmoid(SWIGLU_ALPHA * h_glu) * (h_lin + 1.0)
        y_ref[...] = _dot(act.astype(BF16), wdn_bf[...]) + bdn_ref[0]

    @pl.when(bact_ref[i] == 0)
    def _():
        y_ref[...] = jnp.zeros_like(y_ref)


def _moe(blk_expert, blk_active, row_tok, xn, w_up, b_up, w_down, b_down):
    n_blocks = blk_expert.shape[0]
    n_exp, D, two_f = w_up.shape
    d_exp = two_f // 2
    grid_spec = pltpu.PrefetchScalarGridSpec(
        num_scalar_prefetch=2,
        grid=(n_blocks,),
        in_specs=[
            pl.BlockSpec((1, 1, MOE_ROWS), lambda i, be, ba: (i, 0, 0), memory_space=pltpu.SMEM),
            pl.BlockSpec(memory_space=pl.ANY),
            pl.BlockSpec((1, D, two_f), lambda i, be, ba: (be[i], 0, 0)),
            pl.BlockSpec((1, 1, two_f), lambda i, be, ba: (be[i], 0, 0)),
            pl.BlockSpec((1, d_exp, D), lambda i, be, ba: (be[i], 0, 0)),
            pl.BlockSpec((1, 1, D), lambda i, be, ba: (be[i], 0, 0)),
        ],
        out_specs=pl.BlockSpec((MOE_ROWS, D), lambda i, be, ba: (i, 0)),
        scratch_shapes=[
            pltpu.VMEM((MOE_ROWS, D), F32),
            pltpu.VMEM((D, two_f), BF16),
            pltpu.VMEM((d_exp, D), BF16),
            pltpu.SemaphoreType.DMA,
        ],
    )
    return pl.pallas_call(
        functools.partial(_moe_kernel, d_exp=d_exp),
        grid_spec=grid_spec,
        out_shape=jax.ShapeDtypeStruct((n_blocks * MOE_ROWS, D), F32),
        compiler_params=pltpu.CompilerParams(
            dimension_semantics=("arbitrary",), vmem_limit_bytes=VMEM_LIMIT),
        name="moe_ffn",
    )(blk_expert, blk_active, row_tok.reshape(n_blocks, 1, MOE_ROWS), xn,
      w_up, b_up.reshape(n_exp, 1, two_f), w_down, b_down.reshape(n_exp, 1, D))


def _combine_kernel(dest_ref, gate_ref, x2_ref, y_hbm, o_ref, buf, sem):
    tm = COMBINE_ROWS

    def issue(j, c):
        for k in range(TOP_K):
            _row_copy(y_hbm, buf.at[k], sem, dest_ref[0, 0, k * tm + j], j).start()
        return c

    lax.fori_loop(0, tm, issue, 0)

    def drain(j, c):
        for k in range(TOP_K):
            _row_copy(y_hbm, buf.at[k], sem, 0, j).wait()
        return c

    lax.fori_loop(0, tm, drain, 0)
    acc = x2_ref[...]
    gates = gate_ref[...]
    for k in range(TOP_K):
        acc = acc + gates[:, k: k + 1] * buf[k]
    o_ref[...] = acc


def _combine(dest_tiles, gates_t, x2, y_rows):
    T, D = x2.shape
    tm = COMBINE_ROWS
    return pl.pallas_call(
        _combine_kernel,
        grid=(T // tm,),
        in_specs=[
            pl.BlockSpec((1, 1, TOP_K * tm), lambda i: (i, 0, 0), memory_space=pltpu.SMEM),
            pl.BlockSpec((tm, TOP_K), lambda i: (i, 0)),
            pl.BlockSpec((tm, D), lambda i: (i, 0)),
            pl.BlockSpec(memory_space=pl.ANY),
        ],
        out_specs=pl.BlockSpec((tm, D), lambda i: (i, 0)),
        out_shape=jax.ShapeDtypeStruct((T, D), F32),
        scratch_shapes=[pltpu.VMEM((TOP_K, tm, D), F32), pltpu.SemaphoreType.DMA],
        compiler_params=pltpu.CompilerParams(
            dimension_semantics=("arbitrary",), vmem_limit_bytes=VMEM_LIMIT),
        name="moe_combine",
    )(dest_tiles, gates_t, x2, y_rows)


def _block_diag_ones(n, blk):
    i = jnp.arange(n)
    return ((i[:, None] // blk) == (i[None, :] // blk)).astype(BF16)


def _layer(x, norm1_g, w_in, tshift_mu, q_norm_g, k_norm_g, rel_bias, w0, w_decay_up, a0,
           w_aaa_up, w_gate_up, k_k, k_a, r_k, ln_x_g, ln_x_b, w_out, norm2_g, w_router,
           b_router, w_up, b_up, w_down, b_down):
    B, S, D = x.shape
    T = B * S
    att_heads = rel_bias.shape[0]
    att_w = att_heads * HEAD_DIM
    width = w0.shape[0]
    n_exp = w_router.shape[1]
    assert att_w % LANES == 0 and width % GROUP == 0 and S % ATT_QROWS == 0
    assert T % COMBINE_ROWS == 0

    x2d = x.reshape(T, D)
    row = lambda a: a.reshape(1, -1).astype(F32)
    bd_att = _block_diag_ones(att_w, HEAD_DIM)
    bd_rwkv = _block_diag_ones(width, HEAD_DIM)

    qkv, pr = _inproj(
        x2d, row(norm1_g), w_in.astype(BF16),
        row(jnp.tile(q_norm_g, att_heads)), row(jnp.tile(k_norm_g, att_heads)), bd_att, att_w)

    att = _attention(qkv, _attention_bias(rel_bias), B, S, att_w)

    lora = jnp.zeros((DECAY_LORA + AAA_LORA, 2 * width), F32)
    lora = lora.at[:DECAY_LORA, :width].set(w_decay_up).at[DECAY_LORA:, width:].set(w_aaa_up)
    rwk = _rwkv(pr, row(tshift_mu), row(w0), row(a0), lora.astype(BF16),
                w_gate_up.astype(BF16), row(k_k), row(k_a), row(r_k), row(ln_x_g),
                row(ln_x_b), bd_rwkv, B, S, width)

    wrt = w_router.T.astype(F32)
    wrt_hi = wrt.astype(BF16)
    wrt_lo = (wrt - wrt_hi.astype(F32)).astype(BF16)
    x2, xn, eidx, rank, gates, counts = _outproj(
        att, rwk, x2d, w_out[:att_w].astype(BF16), w_out[att_w:].astype(BF16), row(norm2_g),
        jnp.stack([wrt_hi, wrt_lo]), b_router.reshape(n_exp, 1).astype(F32), n_exp)

    counts = counts[:, 0]
    padded = (counts + MOE_ROWS - 1) // MOE_ROWS * MOE_ROWS
    pad_end = jnp.cumsum(padded)
    pad_start = pad_end - padded
    dest = pad_start[eidx] + rank
    n_blocks = -(-(T * TOP_K) // MOE_ROWS) + n_exp
    tok = jnp.broadcast_to(jnp.arange(T, dtype=jnp.int32)[None, :], (TOP_K, T))
    row_tok = jnp.zeros((n_blocks * MOE_ROWS,), jnp.int32).at[dest.reshape(-1)].set(
        tok.reshape(-1))
    blk_start = jnp.arange(n_blocks, dtype=jnp.int32) * MOE_ROWS
    blk_expert = jnp.minimum(
        jnp.searchsorted(pad_end, blk_start, side='right'), n_exp - 1).astype(jnp.int32)
    blk_active = (blk_start < pad_end[-1]).astype(jnp.int32)

    y_rows = _moe(blk_expert, blk_active, row_tok, xn, w_up, b_up, w_down, b_down)

    n_tiles = T // COMBINE_ROWS
    dest_tiles = dest.reshape(TOP_K, n_tiles, COMBINE_ROWS).transpose(1, 0, 2).reshape(
        n_tiles, 1, TOP_K * COMBINE_ROWS)
    out = _combine(dest_tiles, gates.T, x2, y_rows)
    return out.reshape(B, S, D)


def kernel(x, norm1_g, w_in, tshift_mu, q_norm_g, k_norm_g, rel_bias, w0, w_decay_up, a0,
           w_aaa_up, w_gate_up, k_k, k_a, r_k, ln_x_g, ln_x_b, w_out, norm2_g, w_router,
           b_router, w_up, b_up, w_down, b_down):
    params = (norm1_g, w_in, tshift_mu, q_norm_g, k_norm_g, rel_bias, w0, w_decay_up, a0,
              w_aaa_up, w_gate_up, k_k, k_a, r_k, ln_x_g, ln_x_b, w_out, norm2_g, w_router,
              b_router, w_up, b_up, w_down, b_down)
    for l in range(norm1_g.shape[0]):
        x = _layer(x, *[p[l] for p in params])
    return x
```

```python
import functools
import math

import jax
import jax.numpy as jnp
from jax import lax
from jax.experimental import pallas as pl
from jax.experimental.pallas import tpu as pltpu

F32 = jnp.float32
BF16 = jnp.bfloat16

CHUNK = 64
HEAD_DIM = 64
LEFT_CHUNKS = 8
MAX_LEFT_REL = 128
TOP_K = 4
SWIGLU_ALPHA = 1.702
SWIGLU_LIMIT = 7.0
NORM_EPS = 1e-6
LN_X_EPS = 64e-5
DECAY_LORA = 64
AAA_LORA = 64
GATE_LORA = 128

LANES = 128
GROUP = 256
NEG_BIG = -1e30

VMEM_LIMIT = 56 * 1024 * 1024

ATT_QROWS = 512
ATT_SUB = 128
ATT_KEYS = ATT_SUB + LEFT_CHUNKS * CHUNK

LOCAL_UNROLL = 2

MOE_ROWS = 512
COMBINE_ROWS = 128


def _dot(a, b):
    return jnp.dot(a, b, preferred_element_type=F32)


def _dot_nt(a, b):
    return lax.dot_general(a, b, (((1,), (1,)), ((), ())), preferred_element_type=F32)


def _split_dot(a, b_exact, terms=2):
    acc = None
    rem = a
    for _ in range(terms):
        part = rem.astype(BF16)
        rem = rem - part.astype(F32)
        d = _dot(part, b_exact)
        acc = d if acc is None else acc + d
    return acc


def _sigmoid(x):
    return 1.0 / (1.0 + jnp.exp(-x))


def _inproj_kernel(x_ref, g_ref, w_ref, qg_ref, kg_ref, bd_ref, qkv_ref, pr_ref, *, att_w):
    x = x_ref[...]
    ms = jnp.mean(x * x, axis=-1, keepdims=True)
    h = (x * lax.rsqrt(ms + NORM_EPS) * g_ref[...]).astype(BF16)
    qkv = _dot(h, w_ref[:, : 3 * att_w])
    bd = bd_ref[...]

    def head_norm(t, g):
        ss = _split_dot(t * t, bd)
        return t * lax.rsqrt(ss * (1.0 / HEAD_DIM) + NORM_EPS) * g

    q = head_norm(qkv[:, :att_w], qg_ref[...]) * (HEAD_DIM ** -0.5)
    k = head_norm(qkv[:, att_w: 2 * att_w], kg_ref[...])
    qkv_ref[:, :att_w] = q.astype(BF16)
    qkv_ref[:, att_w: 2 * att_w] = k.astype(BF16)
    qkv_ref[:, 2 * att_w:] = qkv[:, 2 * att_w:].astype(BF16)
    pr_ref[...] = _dot(h, w_ref[:, 3 * att_w:]).astype(BF16)


def _inproj(x2d, g1, w_in_bf, qg, kg, bd, att_w, tm=512):
    T, D = x2d.shape
    in_cols = w_in_bf.shape[1]
    r_cols = in_cols - 3 * att_w
    const = lambda i: (0, 0)
    return pl.pallas_call(
        functools.partial(_inproj_kernel, att_w=att_w),
        grid=(T // tm,),
        in_specs=[
            pl.BlockSpec((tm, D), lambda i: (i, 0)),
            pl.BlockSpec((1, D), const),
            pl.BlockSpec((D, in_cols), const),
            pl.BlockSpec((1, att_w), const),
            pl.BlockSpec((1, att_w), const),
            pl.BlockSpec((att_w, att_w), const),
        ],
        out_specs=[
            pl.BlockSpec((tm, 3 * att_w), lambda i: (i, 0)),
            pl.BlockSpec((tm, r_cols), lambda i: (i, 0)),
        ],
        out_shape=[
            jax.ShapeDtypeStruct((T, 3 * att_w), BF16),
            jax.ShapeDtypeStruct((T, r_cols), BF16),
        ],
        compiler_params=pltpu.CompilerParams(
            dimension_semantics=("arbitrary",), vmem_limit_bytes=VMEM_LIMIT),
        name="inproj",
    )(x2d, g1, w_in_bf, qg, kg, bd)


def _attn_kernel(q_ref, kp_ref, kc_ref, vp_ref, vc_ref, bias_ref, o_ref, kbuf, vbuf):
    i = pl.program_id(2)
    kbuf[:ATT_QROWS, :] = kp_ref[...]
    kbuf[ATT_QROWS:, :] = kc_ref[...]
    vbuf[:ATT_QROWS, :] = vp_ref[...]
    vbuf[ATT_QROWS:, :] = vc_ref[...]
    col = lax.broadcasted_iota(jnp.int32, (ATT_SUB, ATT_KEYS), 1)
    for qs in range(ATT_QROWS // ATT_SUB):
        r0 = qs * ATT_SUB
        first_valid = ATT_QROWS - i * ATT_QROWS - r0
        outs = []
        for h in range(LANES // HEAD_DIM):
            lo = h * HEAD_DIM
            qh = q_ref[r0: r0 + ATT_SUB, lo: lo + HEAD_DIM]
            kh = kbuf[r0: r0 + ATT_KEYS, lo: lo + HEAD_DIM]
            vh = vbuf[r0: r0 + ATT_KEYS, lo: lo + HEAD_DIM]
            s = _dot_nt(qh, kh) + bias_ref[h]
            s = jnp.where(col >= first_valid, s, NEG_BIG)
            m = jnp.max(s, axis=-1, keepdims=True)
            p = jnp.exp(s - m)
            l = jnp.sum(p, axis=-1, keepdims=True)
            o = _dot(p.astype(BF16), vh)
            outs.append(o / l)
        o_ref[r0: r0 + ATT_SUB, :] = jnp.concatenate(outs, axis=-1).astype(BF16)


def _attention(qkv, bias, B, S, att_w):
    T = B * S
    n_i = S // ATT_QROWS
    n_hp = att_w // LANES
    heads_per = LANES // HEAD_DIM

    def cur(col0):
        return lambda b, hp, i: (b * n_i + i, col0 + hp)

    def prev(col0):
        return lambda b, hp, i: (b * n_i + jnp.maximum(i - 1, 0), col0 + hp)

    blk = (ATT_QROWS, LANES)
    return pl.pallas_call(
        _attn_kernel,
        grid=(B, n_hp, n_i),
        in_specs=[
            pl.BlockSpec(blk, cur(0)),
            pl.BlockSpec(blk, prev(n_hp)),
            pl.BlockSpec(blk, cur(n_hp)),
            pl.BlockSpec(blk, prev(2 * n_hp)),
            pl.BlockSpec(blk, cur(2 * n_hp)),
            pl.BlockSpec((heads_per, ATT_SUB, ATT_KEYS), lambda b, hp, i: (hp, 0, 0)),
        ],
        out_specs=pl.BlockSpec(blk, lambda b, hp, i: (b * n_i + i, hp)),
        out_shape=jax.ShapeDtypeStruct((T, att_w), BF16),
        scratch_shapes=[
            pltpu.VMEM((2 * ATT_QROWS, LANES), BF16),
            pltpu.VMEM((2 * ATT_QROWS, LANES), BF16),
        ],
        compiler_params=pltpu.CompilerParams(
            dimension_semantics=("arbitrary", "arbitrary", "arbitrary"),
            vmem_limit_bytes=VMEM_LIMIT),
        name="band_attention",
    )(qkv, qkv, qkv, qkv, qkv, bias)


def _attention_bias(rel_bias):
    H = rel_bias.shape[0]
    table = rel_bias.astype(F32)
    far = LEFT_CHUNKS * CHUNK + ATT_SUB - 1
    n_left = far - MAX_LEFT_REL + 1
    n_ext = ATT_SUB + ATT_KEYS - 1
    ext = jnp.concatenate([
        jnp.broadcast_to(table[:, :1], (H, n_left)), table[:, 1:],
        jnp.broadcast_to(table[:, -1:], (H, n_ext - n_left - (table.shape[1] - 1)))], axis=1)
    skew = jnp.tile(ext, (1, ATT_SUB + 1))[:, : ATT_SUB * (n_ext + 1)].reshape(H, ATT_SUB, n_ext + 1)
    bias = skew[:, ::-1, :ATT_KEYS]
    r = jnp.arange(ATT_SUB)[:, None]
    c = jnp.arange(ATT_KEYS)[None, :]
    band = c // CHUNK - r // CHUNK
    ok = (band >= 0) & (band <= LEFT_CHUNKS)
    return jnp.where(ok[None], bias, NEG_BIG)


def _rwkv_kernel(p_ref, pprev_ref, mu_ref, w0_ref, a0_ref, lora_ref, wg_ref, kk_ref, ka_ref,
                 rk_ref, lng_ref, lnb_ref, bd_ref, tri_ref, o_ref,
                 rt_s, at_s, bt_s, kt_s, bh_s, kh_s, v_s, wl_s, y_s, z_s, g_s, y0_s, m_s, n_s,
                 *, width, tm):
    i = pl.program_id(1)
    n_groups = width // GROUP
    heads_per_group = GROUP // HEAD_DIM

    @pl.when(i == 0)
    def _():
        z_s[...] = jnp.zeros_like(z_s)

    p = p_ref[...].astype(F32)
    first = jnp.where(i > 0, 1.0, 0.0)
    prev_row = pprev_ref[7:8, :].astype(F32) * first
    row = lax.broadcasted_iota(jnp.int32, (tm, 1), 0)
    prev = jnp.where(row == 0, prev_row, pltpu.roll(p, 1, axis=0))
    ps = p + (prev - p) * mu_ref[...]

    r = ps[:, :width]
    k = ps[:, width: 2 * width]
    v = ps[:, 2 * width: 3 * width]
    wa = ps[:, 3 * width: 3 * width + DECAY_LORA + AAA_LORA]
    gd = ps[:, 3 * width + DECAY_LORA + AAA_LORA:]
    lane = lax.broadcasted_iota(jnp.int32, (1, DECAY_LORA + AAA_LORA), 1)
    wa = jnp.where(lane < DECAY_LORA, jnp.tanh(wa), wa)
    lw = _dot(wa.astype(BF16), lora_ref[...])
    logw = -math.exp(-0.5) * _sigmoid(w0_ref[...] + lw[:, :width])
    eta = _sigmoid(a0_ref[...] + lw[:, width:])
    g = _dot(_sigmoid(gd).astype(BF16), wg_ref[...])

    bd = bd_ref[...]
    kk = k * kk_ref[...]
    kk = kk / jnp.maximum(jnp.sqrt(_split_dot(kk * kk, bd)), 1e-12)
    km = k * (1.0 + (eta - 1.0) * ka_ref[...])
    bb = kk * eta

    sums = _split_dot_left(tri_ref[...], logw)
    cs = sums[:tm]
    tot = sums[tm:]
    e_pos = jnp.exp(cs)
    e_neg = jnp.exp(-cs)
    e_tail = jnp.exp(tot - cs)
    rt_s[...] = r * e_pos
    at_s[...] = -kk * jnp.exp(cs - logw)
    bt_s[...] = bb * e_neg
    kt_s[...] = km * e_neg
    bh_s[...] = bb * e_tail
    kh_s[...] = km * e_tail
    v_s[...] = v
    wl_s[...] = jnp.exp(tot)

    rr = lax.broadcasted_iota(jnp.int32, (GROUP, GROUP), 0)
    cc = lax.broadcasted_iota(jnp.int32, (GROUP, GROUP), 1)
    shift = CHUNK.bit_length() - 1
    same = (rr >> shift) == (cc >> shift)
    strict = same & (rr > cc)
    incl = same & (rr >= cc)
    eye = jnp.where(rr == cc, 1.0, 0.0)
    lane_head = lax.broadcasted_iota(jnp.int32, (1, GROUP), 1) >> shift

    def stack_masked(x):
        return jnp.concatenate(
            [jnp.where(lane_head == h, x, 0.0) for h in range(heads_per_group)], axis=0)

    def stack_plain(x):
        return jnp.concatenate([x] * heads_per_group, axis=0)

    def bdot(a, b):
        return _dot(a.astype(BF16), b.astype(BF16))

    def bdot_nt(a, b):
        return _dot_nt(a.astype(BF16), b.astype(BF16))

    def local_body(c, carry):
        r0 = pl.multiple_of(c * CHUNK, CHUNK)
        rows = pl.ds(r0, CHUNK)
        for gi in range(n_groups):
            cols = slice(gi * GROUP, (gi + 1) * GROUP)
            at4 = stack_masked(at_s[rows, cols])
            rt4 = stack_masked(rt_s[rows, cols])
            v4 = stack_masked(v_s[rows, cols])
            bh4 = stack_masked(bh_s[rows, cols])
            kh4 = stack_masked(kh_s[rows, cols])
            bt_rep = stack_plain(bt_s[rows, cols])
            kt_rep = stack_plain(kt_s[rows, cols])
            a_ab = jnp.where(strict, bdot_nt(at4, bt_rep), 0.0)
            a_ak = jnp.where(strict, bdot_nt(at4, kt_rep), 0.0)
            a_rb = jnp.where(incl, bdot_nt(rt4, bt_rep), 0.0)
            a_rk = jnp.where(incl, bdot_nt(rt4, kt_rep), 0.0)
            tinv = eye + a_ab
            pw = a_ab
            span = 2
            while span < CHUNK:
                pw = bdot(pw, pw)
                tinv = tinv + bdot(pw, tinv)
                span *= 2
            p4 = bdot(tinv, at4)
            q4 = bdot(tinv, bdot(a_ak, v4))
            g_s[c, gi] = (rt4 + bdot(a_rb, p4)).astype(BF16)
            y0_s[c, gi] = bdot(a_rb, q4) + bdot(a_rk, v4)
            m_s[c, gi] = bdot(p4.T, bh4).astype(BF16)
            n_s[c, gi] = bdot(q4.T, bh4) + bdot(v4.T, kh4)
        return carry

    lax.fori_loop(0, tm // CHUNK, local_body, 0, unroll=LOCAL_UNROLL)

    def carry_body(c, carry):
        r0 = pl.multiple_of(c * CHUNK, CHUNK)
        rows = pl.ds(r0, CHUNK)
        for gi in range(n_groups):
            cols = slice(gi * GROUP, (gi + 1) * GROUP)
            zt = z_s[gi]
            zb = zt.astype(BF16)
            y4 = _dot_nt(g_s[c, gi], zb) + y0_s[c, gi]
            z_s[gi] = zt * wl_s[pl.ds(r0, 1), cols] + _dot(zb, m_s[c, gi]) + n_s[c, gi]
            ytok = y4[:CHUNK]
            for h in range(1, heads_per_group):
                ytok = ytok + y4[h * CHUNK: (h + 1) * CHUNK]
            y_s[rows, cols] = ytok
        return carry

    lax.fori_loop(0, tm // CHUNK, carry_body, 0)

    y = y_s[...]
    inv_n = 1.0 / HEAD_DIM
    mean = _split_dot(y, bd) * inv_n
    d = y - mean
    var = _split_dot(d * d, bd) * inv_n
    yn = d * lax.rsqrt(var + LN_X_EPS) * lng_ref[...] + lnb_ref[...]
    bonus = _split_dot(r * km * rk_ref[...], bd) * v
    o_ref[...] = ((yn + bonus) * g).astype(BF16)


def _split_dot_left(a_exact, b, terms=3):
    acc = None
    rem = b
    for _ in range(terms):
        part = rem.astype(BF16)
        rem = rem - part.astype(F32)
        d = _dot(a_exact, part)
        acc = d if acc is None else acc + d
    return acc


def _rwkv(pr, mu, w0, a0, lora, wg, k_k, k_a, r_k, ln_g, ln_b, bd, B, S, width, tm=512):
    T = B * S
    n_t = S // tm
    cols = pr.shape[1]
    ti = jnp.arange(tm)
    same = (ti[:, None] // CHUNK) == (ti[None, :] // CHUNK)
    tri = jnp.concatenate([same & (ti[:, None] >= ti[None, :]), same], axis=0).astype(BF16)
    const = lambda b, i: (0, 0)
    vec = lambda n: pl.BlockSpec((1, n), const)
    scr = lambda: pltpu.VMEM((tm, width), F32)
    return pl.pallas_call(
        functools.partial(_rwkv_kernel, width=width, tm=tm),
        grid=(B, n_t),
        in_specs=[
            pl.BlockSpec((tm, cols), lambda b, i: (b * n_t + i, 0)),
            pl.BlockSpec((8, cols), lambda b, i: (jnp.maximum((b * n_t + i) * (tm // 8) - 1, 0), 0)),
            vec(cols), vec(width), vec(width),
            pl.BlockSpec(lora.shape, const),
            pl.BlockSpec(wg.shape, const),
            vec(width), vec(width), vec(width), vec(width), vec(width),
            pl.BlockSpec((width, width), const),
            pl.BlockSpec((2 * tm, tm), const),
        ],
        out_specs=pl.BlockSpec((tm, width), lambda b, i: (b * n_t + i, 0)),
        out_shape=jax.ShapeDtypeStruct((T, width), BF16),
        scratch_shapes=[scr() for _ in range(9)]
        + [pltpu.VMEM((width // GROUP, GROUP, GROUP), F32)]
        + [pltpu.VMEM((tm // CHUNK, width // GROUP, GROUP, GROUP), dt)
           for dt in (BF16, F32, BF16, F32)],
        compiler_params=pltpu.CompilerParams(
            dimension_semantics=("arbitrary", "arbitrary"), vmem_limit_bytes=VMEM_LIMIT),
        name="rwkv7",
    )(pr, pr, mu, w0, a0, lora, wg, k_k, k_a, r_k, ln_g, ln_b, bd, tri)


def _outproj_kernel(att_ref, rwk_ref, x_ref, wa_ref, wr_ref, g_ref, wrt_ref, br_ref, upper_ref,
                    x2_ref, xn_ref, eidx_ref, rank_ref, gate_ref, cnt_ref, carry, *, n_exp, tm):
    i = pl.program_id(0)

    @pl.when(i == 0)
    def _():
        carry[...] = jnp.zeros_like(carry)

    x2 = x_ref[...] + _dot(att_ref[...], wa_ref[...]) + _dot(rwk_ref[...], wr_ref[...])
    x2_ref[...] = x2
    ms = jnp.mean(x2 * x2, axis=-1, keepdims=True)
    xn = x2 * lax.rsqrt(ms + NORM_EPS) * g_ref[...]
    xn_ref[...] = xn

    x_hi = xn.astype(BF16)
    x_lo = (xn - x_hi.astype(F32)).astype(BF16)
    logits = (_dot_nt(wrt_ref[0], x_hi) + _dot_nt(wrt_ref[0], x_lo)
              + _dot_nt(wrt_ref[1], x_hi)) + br_ref[...]

    eio = lax.broadcasted_iota(jnp.int32, (n_exp, tm), 0)
    work = logits
    vals, onehots = [], []
    for k in range(TOP_K):
        m = jnp.max(work, axis=0, keepdims=True)
        idx = jnp.min(jnp.where(work == m, eio, n_exp), axis=0, keepdims=True)
        oh = eio == idx
        vals.append(m)
        onehots.append(oh)
        eidx_ref[k: k + 1, :] = idx
        work = jnp.where(oh, -jnp.inf, work)

    exps = [jnp.exp(vk - vals[0]) for vk in vals]
    denom = exps[0]
    for e in exps[1:]:
        denom = denom + e
    for k in range(TOP_K):
        gate_ref[k: k + 1, :] = exps[k] / denom

    sel = onehots[0]
    for oh in onehots[1:]:
        sel = sel | oh
    sel_f = jnp.where(sel, 1.0, 0.0)
    before = _dot(sel_f.astype(BF16), upper_ref[...]) + carry[:, :1]
    for k in range(TOP_K):
        rank_ref[k: k + 1, :] = jnp.sum(
            jnp.where(onehots[k], before, 0.0), axis=0, keepdims=True).astype(jnp.int32)
    carry[...] = carry[...] + jnp.sum(sel_f, axis=1, keepdims=True)
    cnt_ref[...] = carry[...].astype(jnp.int32)


def _outproj(att, rwk, x2d, w_att, w_rwk, g2, wrt, br, n_exp, tm=512):
    T, D = x2d.shape
    upper = (jnp.arange(tm)[:, None] < jnp.arange(tm)[None, :]).astype(BF16)
    const = lambda i: (0, 0)
    rows = lambda n: pl.BlockSpec((tm, n), lambda i: (i, 0))
    lanes = pl.BlockSpec((TOP_K, tm), lambda i: (0, i))
    return pl.pallas_call(
        functools.partial(_outproj_kernel, n_exp=n_exp, tm=tm),
        grid=(T // tm,),
        in_specs=[
            rows(att.shape[1]), rows(rwk.shape[1]), rows(D),
            pl.BlockSpec(w_att.shape, const), pl.BlockSpec(w_rwk.shape, const),
            pl.BlockSpec((1, D), const),
            pl.BlockSpec(wrt.shape, lambda i: (0, 0, 0)),
            pl.BlockSpec((n_exp, 1), const),
            pl.BlockSpec((tm, tm), const),
        ],
        out_specs=[rows(D), rows(D), lanes, lanes, lanes,
                   pl.BlockSpec((n_exp, LANES), const)],
        out_shape=[
            jax.ShapeDtypeStruct((T, D), F32),
            jax.ShapeDtypeStruct((T, D), F32),
            jax.ShapeDtypeStruct((TOP_K, T), jnp.int32),
            jax.ShapeDtypeStruct((TOP_K, T), jnp.int32),
            jax.ShapeDtypeStruct((TOP_K, T), F32),
            jax.ShapeDtypeStruct((n_exp, LANES), jnp.int32),
        ],
        scratch_shapes=[pltpu.VMEM((n_exp, LANES), F32)],
        compiler_params=pltpu.CompilerParams(
            dimension_semantics=("arbitrary",), vmem_limit_bytes=VMEM_LIMIT),
        name="outproj_router",
    )(att, rwk, x2d, w_att, w_rwk, g2, wrt, br, upper)


def _row_copy(src, dst, sem, src_row, dst_row):
    return pltpu.make_async_copy(src.at[pl.ds(src_row, 1)], dst.at[pl.ds(dst_row, 1)], sem)


def _dispatch_kernel(pad_base_ref, pad_n_ref, dest_ref, xn_ref, xs_hbm, zrow, sem, *, n_exp):
    tm = COMBINE_ROWS

    @pl.when(pl.program_id(0) == 0)
    def _():
        zrow[...] = jnp.zeros_like(zrow)
        for e in range(n_exp):
            base = pad_base_ref[e]
            n_pad = pad_n_ref[e]

            def issue(j, c):
                _row_copy(zrow, xs_hbm, sem, 0, base + j).start()
                return c

            def drain(j, c):
                _row_copy(zrow, xs_hbm, sem, 0, base + j).wait()
                return c

            lax.fori_loop(0, n_pad, issue, 0)
            lax.fori_loop(0, n_pad, drain, 0)

    def issue(j, c):
        for k in range(TOP_K):
            _row_copy(xn_ref, xs_hbm, sem, j, dest_ref[0, 0, k * tm + j]).start()
        return c

    def drain(j, c):
        for k in range(TOP_K):
            _row_copy(xn_ref, xs_hbm, sem, j, 0).wait()
        return c

    lax.fori_loop(0, tm, issue, 0)
    lax.fori_loop(0, tm, drain, 0)


def _dispatch(pad_base, pad_n, dest_tiles, xn, n_rows):
    T, D = xn.shape
    tm = COMBINE_ROWS
    n_exp = pad_base.shape[0]
    grid_spec = pltpu.PrefetchScalarGridSpec(
        num_scalar_prefetch=2,
        grid=(T // tm,),
        in_specs=[
            pl.BlockSpec((1, 1, TOP_K * tm), lambda i, pb, pn: (i, 0, 0),
                         memory_space=pltpu.SMEM),
            pl.BlockSpec((tm, D), lambda i, pb, pn: (i, 0)),
        ],
        out_specs=pl.BlockSpec(memory_space=pl.ANY),
        scratch_shapes=[pltpu.VMEM((1, D), F32), pltpu.SemaphoreType.DMA],
    )
    return pl.pallas_call(
        functools.partial(_dispatch_kernel, n_exp=n_exp),
        grid_spec=grid_spec,
        out_shape=jax.ShapeDtypeStruct((n_rows, D), F32),
        compiler_params=pltpu.CompilerParams(
            dimension_semantics=("arbitrary",), vmem_limit_bytes=VMEM_LIMIT),
        name="moe_dispatch",
    )(pad_base, pad_n, dest_tiles, xn)


def _moe_kernel(bexp_ref, nact_ref, x_ref, wup_ref, bup_ref, wdn_ref, bdn_ref,
                y_ref, wup_bf, wdn_bf, *, d_exp):
    i = pl.program_id(0)

    @pl.when(i < nact_ref[0])
    def _():
        new_expert = jnp.logical_or(i == 0, bexp_ref[i] != bexp_ref[jnp.maximum(i - 1, 0)])

        @pl.when(new_expert)
        def _():
            wup_bf[...] = wup_ref[0].astype(BF16)
            wdn_bf[...] = wdn_ref[0].astype(BF16)

        x = x_ref[...].astype(BF16)
        h_glu = _dot(x, wup_bf[:, :d_exp]) + bup_ref[0, :, :d_exp]
        h_lin = _dot(x, wup_bf[:, d_exp:]) + bup_ref[0, :, d_exp:]
        h_glu = jnp.minimum(h_glu, SWIGLU_LIMIT)
        h_lin = jnp.clip(h_lin, -SWIGLU_LIMIT, SWIGLU_LIMIT)
        act = h_glu * _sigmoid(SWIGLU_ALPHA * h_glu) * (h_lin + 1.0)
        y_ref[...] = _dot(act.astype(BF16), wdn_bf[...]) + bdn_ref[0]


def _moe(blk_expert, n_active, xs, w_up, b_up, w_down, b_down):
    n_blocks = blk_expert.shape[0]
    n_exp, D, two_f = w_up.shape
    d_exp = two_f // 2
    blk = lambda i, na: jnp.minimum(i, na[0] - 1)
    grid_spec = pltpu.PrefetchScalarGridSpec(
        num_scalar_prefetch=2,
        grid=(n_blocks,),
        in_specs=[
            pl.BlockSpec((MOE_ROWS, D), lambda i, be, na: (blk(i, na), 0)),
            pl.BlockSpec((1, D, two_f), lambda i, be, na: (be[blk(i, na)], 0, 0)),
            pl.BlockSpec((1, 1, two_f), lambda i, be, na: (be[blk(i, na)], 0, 0)),
            pl.BlockSpec((1, d_exp, D), lambda i, be, na: (be[blk(i, na)], 0, 0)),
            pl.BlockSpec((1, 1, D), lambda i, be, na: (be[blk(i, na)], 0, 0)),
        ],
        out_specs=pl.BlockSpec((MOE_ROWS, D), lambda i, be, na: (blk(i, na), 0)),
        scratch_shapes=[
            pltpu.VMEM((D, two_f), BF16),
            pltpu.VMEM((d_exp, D), BF16),
        ],
    )
    return pl.pallas_call(
        functools.partial(_moe_kernel, d_exp=d_exp),
        grid_spec=grid_spec,
        out_shape=jax.ShapeDtypeStruct((n_blocks * MOE_ROWS, D), F32),
        compiler_params=pltpu.CompilerParams(
            dimension_semantics=("arbitrary",), vmem_limit_bytes=VMEM_LIMIT),
        name="moe_ffn",
    )(blk_expert, n_active, xs,
      w_up, b_up.reshape(n_exp, 1, two_f), w_down, b_down.reshape(n_exp, 1, D))


def _combine_kernel(dest_ref, gate_ref, x2_ref, y_hbm, o_ref, buf, sem):
    tm = COMBINE_ROWS

    def issue(j, c):
        for k in range(TOP_K):
            _row_copy(y_hbm, buf.at[k], sem, dest_ref[0, 0, k * tm + j], j).start()
        return c

    lax.fori_loop(0, tm, issue, 0)

    def drain(j, c):
        for k in range(TOP_K):
            _row_copy(y_hbm, buf.at[k], sem, 0, j).wait()
        return c

    lax.fori_loop(0, tm, drain, 0)
    acc = x2_ref[...]
    gates = gate_ref[...]
    for k in range(TOP_K):
        acc = acc + gates[:, k: k + 1] * buf[k]
    o_ref[...] = acc


def _combine(dest_tiles, gates_t, x2, y_rows):
    T, D = x2.shape
    tm = COMBINE_ROWS
    return pl.pallas_call(
        _combine_kernel,
        grid=(T // tm,),
        in_specs=[
            pl.BlockSpec((1, 1, TOP_K * tm), lambda i: (i, 0, 0), memory_space=pltpu.SMEM),
            pl.BlockSpec((tm, TOP_K), lambda i: (i, 0)),
            pl.BlockSpec((tm, D), lambda i: (i, 0)),
            pl.BlockSpec(memory_space=pl.ANY),
        ],
        out_specs=pl.BlockSpec((tm, D), lambda i: (i, 0)),
        out_shape=jax.ShapeDtypeStruct((T, D), F32),
        scratch_shapes=[pltpu.VMEM((TOP_K, tm, D), F32), pltpu.SemaphoreType.DMA],
        compiler_params=pltpu.CompilerParams(
            dimension_semantics=("arbitrary",), vmem_limit_bytes=VMEM_LIMIT),
        name="moe_combine",
    )(dest_tiles, gates_t, x2, y_rows)


def _block_diag_ones(n, blk):
    i = jnp.arange(n)
    return ((i[:, None] // blk) == (i[None, :] // blk)).astype(BF16)


def _layer(x, norm1_g, w_in, tshift_mu, q_norm_g, k_norm_g, rel_bias, w0, w_decay_up, a0,
           w_aaa_up, w_gate_up, k_k, k_a, r_k, ln_x_g, ln_x_b, w_out, norm2_g, w_router,
           b_router, w_up, b_up, w_down, b_down):
    B, S, D = x.shape
    T = B * S
    att_heads = rel_bias.shape[0]
    att_w = att_heads * HEAD_DIM
    width = w0.shape[0]
    n_exp = w_router.shape[1]
    assert att_w % LANES == 0 and width % GROUP == 0 and S % ATT_QROWS == 0
    assert T % COMBINE_ROWS == 0

    x2d = x.reshape(T, D)
    row = lambda a: a.reshape(1, -1).astype(F32)
    bd_att = _block_diag_ones(att_w, HEAD_DIM)
    bd_rwkv = _block_diag_ones(width, HEAD_DIM)

    qkv, pr = _inproj(
        x2d, row(norm1_g), w_in.astype(BF16),
        row(jnp.tile(q_norm_g, att_heads)), row(jnp.tile(k_norm_g, att_heads)), bd_att, att_w)

    att = _attention(qkv, _attention_bias(rel_bias), B, S, att_w)

    lora = jnp.zeros((DECAY_LORA + AAA_LORA, 2 * width), F32)
    lora = lora.at[:DECAY_LORA, :width].set(w_decay_up).at[DECAY_LORA:, width:].set(w_aaa_up)
    rwk = _rwkv(pr, row(tshift_mu), row(w0), row(a0), lora.astype(BF16),
                w_gate_up.astype(BF16), row(k_k), row(k_a), row(r_k), row(ln_x_g),
                row(ln_x_b), bd_rwkv, B, S, width)

    wrt = w_router.T.astype(F32)
    wrt_hi = wrt.astype(BF16)
    wrt_lo = (wrt - wrt_hi.astype(F32)).astype(BF16)
    x2, xn, eidx, rank, gates, counts = _outproj(
        att, rwk, x2d, w_out[:att_w].astype(BF16), w_out[att_w:].astype(BF16), row(norm2_g),
        jnp.stack([wrt_hi, wrt_lo]), b_router.reshape(n_exp, 1).astype(F32), n_exp)

    counts = counts[:, 0]
    padded = (counts + MOE_ROWS - 1) // MOE_ROWS * MOE_ROWS
    pad_end = jnp.cumsum(padded)
    pad_start = pad_end - padded
    experts = jnp.arange(n_exp, dtype=jnp.int32)
    dest = rank + jnp.sum(
        jnp.where(eidx[None] == experts[:, None, None], pad_start[:, None, None], 0), axis=0)
    n_blocks = -(-(T * TOP_K) // MOE_ROWS) + n_exp
    blk_start = jnp.arange(n_blocks, dtype=jnp.int32) * MOE_ROWS
    blk_expert = jnp.minimum(
        jnp.sum(blk_start[:, None] >= pad_end[None, :], axis=1), n_exp - 1).astype(jnp.int32)
    n_active = (pad_end[-1:] // MOE_ROWS).astype(jnp.int32)
    n_tiles = T // COMBINE_ROWS
    dest_tiles = dest.reshape(TOP_K, n_tiles, COMBINE_ROWS).transpose(1, 0, 2).reshape(
        n_tiles, 1, TOP_K * COMBINE_ROWS)

    xs = _dispatch((pad_start + counts).astype(jnp.int32), (padded - counts).astype(jnp.int32),
                   dest_tiles, xn, n_blocks * MOE_ROWS)
    y_rows = _moe(blk_expert, n_active, xs, w_up, b_up, w_down, b_down)
    out = _combine(dest_tiles, gates.T, x2, y_rows)
    return out.reshape(B, S, D)


def kernel(x, norm1_g, w_in, tshift_mu, q_norm_g, k_norm_g, rel_bias, w0, w_decay_up, a0,
           w_aaa_up, w_gate_up, k_k, k_a, r_k, ln_x_g, ln_x_b, w_out, norm2_g, w_router,
           b_router, w_up, b_up, w_down, b_down):
    params = (norm1_g, w_in, tshift_mu, q_norm_g, k_norm_g, rel_bias, w0, w_decay_up, a0,
              w_aaa_up, w_gate_up, k_k, k_a, r_k, ln_x_g, ln_x_b, w_out, norm2_g, w_router,
              b_router, w_up, b_up, w_down, b_down)
    for l in range(norm1_g.shape[0]):
        x = _layer(x, *[p[l] for p in params])
    return x
```

```python
import functools
import math

import jax
import jax.numpy as jnp
from jax import lax
from jax.experimental import pallas as pl
from jax.experimental.pallas import tpu as pltpu

F32 = jnp.float32
BF16 = jnp.bfloat16

CHUNK = 64
HEAD_DIM = 64
LEFT_CHUNKS = 8
MAX_LEFT_REL = 128
TOP_K = 4
SWIGLU_ALPHA = 1.702
SWIGLU_LIMIT = 7.0
NORM_EPS = 1e-6
LN_X_EPS = 64e-5
DECAY_LORA = 64
AAA_LORA = 64
GATE_LORA = 128

LANES = 128
GROUP = 256
NEG_BIG = -1e30

VMEM_LIMIT = 56 * 1024 * 1024

ATT_QROWS = 512
ATT_SUB = 256
ATT_KEYS = ATT_SUB + LEFT_CHUNKS * CHUNK

LOCAL_UNROLL = 2

MOE_ROWS = 512
ROUTE_TILE = 512
RUN_ALIGN = 8
RUN_SIZES = tuple(ROUTE_TILE >> s for s in range((ROUTE_TILE // RUN_ALIGN).bit_length()))
assert MOE_ROWS <= ROUTE_TILE and RUN_SIZES[-1] == RUN_ALIGN


def _dot(a, b):
    return jnp.dot(a, b, preferred_element_type=F32)


def _dot_nt(a, b):
    return lax.dot_general(a, b, (((1,), (1,)), ((), ())), preferred_element_type=F32)


def _split_dot(a, b_exact, terms=2):
    acc = None
    rem = a
    for _ in range(terms):
        part = rem.astype(BF16)
        rem = rem - part.astype(F32)
        d = _dot(part, b_exact)
        acc = d if acc is None else acc + d
    return acc


def _sigmoid(x):
    return 1.0 / (1.0 + jnp.exp(-x))


def _inproj_kernel(x_ref, g_ref, w_ref, qg_ref, kg_ref, bd_ref, qkv_ref, pr_ref, *, att_w):
    x = x_ref[...]
    ms = jnp.mean(x * x, axis=-1, keepdims=True)
    h = (x * lax.rsqrt(ms + NORM_EPS) * g_ref[...]).astype(BF16)
    qkv = _dot(h, w_ref[:, : 3 * att_w])
    bd = bd_ref[...]

    def head_norm(t, g):
        ss = _split_dot(t * t, bd)
        return t * lax.rsqrt(ss * (1.0 / HEAD_DIM) + NORM_EPS) * g

    q = head_norm(qkv[:, :att_w], qg_ref[...]) * (HEAD_DIM ** -0.5)
    k = head_norm(qkv[:, att_w: 2 * att_w], kg_ref[...])
    qkv_ref[:, :att_w] = q.astype(BF16)
    qkv_ref[:, att_w: 2 * att_w] = k.astype(BF16)
    qkv_ref[:, 2 * att_w:] = qkv[:, 2 * att_w:].astype(BF16)
    pr_ref[...] = _dot(h, w_ref[:, 3 * att_w:]).astype(BF16)


def _inproj(x2d, g1, w_in_bf, qg, kg, bd, att_w, tm=512):
    T, D = x2d.shape
    in_cols = w_in_bf.shape[1]
    r_cols = in_cols - 3 * att_w
    const = lambda i: (0, 0)
    return pl.pallas_call(
        functools.partial(_inproj_kernel, att_w=att_w),
        grid=(T // tm,),
        in_specs=[
            pl.BlockSpec((tm, D), lambda i: (i, 0)),
            pl.BlockSpec((1, D), const),
            pl.BlockSpec((D, in_cols), const),
            pl.BlockSpec((1, att_w), const),
            pl.BlockSpec((1, att_w), const),
            pl.BlockSpec((att_w, att_w), const),
        ],
        out_specs=[
            pl.BlockSpec((tm, 3 * att_w), lambda i: (i, 0)),
            pl.BlockSpec((tm, r_cols), lambda i: (i, 0)),
        ],
        out_shape=[
            jax.ShapeDtypeStruct((T, 3 * att_w), BF16),
            jax.ShapeDtypeStruct((T, r_cols), BF16),
        ],
        compiler_params=pltpu.CompilerParams(
            dimension_semantics=("arbitrary",), vmem_limit_bytes=VMEM_LIMIT),
        name="inproj",
    )(x2d, g1, w_in_bf, qg, kg, bd)


def _attn_kernel(q_ref, kp_ref, kc_ref, vp_ref, vc_ref, bias_ref, o_ref, kbuf, vbuf):
    i = pl.program_id(2)
    kbuf[:ATT_QROWS, :] = kp_ref[...]
    kbuf[ATT_QROWS:, :] = kc_ref[...]
    vbuf[:ATT_QROWS, :] = vp_ref[...]
    vbuf[ATT_QROWS:, :] = vc_ref[...]
    for qs in range(ATT_QROWS // ATT_SUB):
        r0 = qs * ATT_SUB
        variant = jnp.where(i == 0, 1 + qs, 0)
        outs = []
        for h in range(LANES // HEAD_DIM):
            lo = h * HEAD_DIM
            qh = q_ref[r0: r0 + ATT_SUB, lo: lo + HEAD_DIM]
            kh = kbuf[r0: r0 + ATT_KEYS, lo: lo + HEAD_DIM]
            vh = vbuf[r0: r0 + ATT_KEYS, lo: lo + HEAD_DIM]
            s = _dot_nt(qh, kh) + bias_ref[h, variant]
            m = jnp.max(s, axis=-1, keepdims=True)
            p = jnp.exp(s - m)
            l = jnp.sum(p, axis=-1, keepdims=True)
            o = _dot(p.astype(BF16), vh)
            outs.append(o / l)
        o_ref[r0: r0 + ATT_SUB, :] = jnp.concatenate(outs, axis=-1).astype(BF16)


def _attention(qkv, bias, B, S, att_w):
    T = B * S
    n_i = S // ATT_QROWS
    n_hp = att_w // LANES
    heads_per = LANES // HEAD_DIM

    def cur(col0):
        return lambda b, hp, i: (b * n_i + i, col0 + hp)

    def prev(col0):
        return lambda b, hp, i: (b * n_i + jnp.maximum(i - 1, 0), col0 + hp)

    blk = (ATT_QROWS, LANES)
    return pl.pallas_call(
        _attn_kernel,
        grid=(B, n_hp, n_i),
        in_specs=[
            pl.BlockSpec(blk, cur(0)),
            pl.BlockSpec(blk, prev(n_hp)),
            pl.BlockSpec(blk, cur(n_hp)),
            pl.BlockSpec(blk, prev(2 * n_hp)),
            pl.BlockSpec(blk, cur(2 * n_hp)),
            pl.BlockSpec((heads_per,) + bias.shape[1:], lambda b, hp, i: (hp, 0, 0, 0)),
        ],
        out_specs=pl.BlockSpec(blk, lambda b, hp, i: (b * n_i + i, hp)),
        out_shape=jax.ShapeDtypeStruct((T, att_w), BF16),
        scratch_shapes=[
            pltpu.VMEM((2 * ATT_QROWS, LANES), BF16),
            pltpu.VMEM((2 * ATT_QROWS, LANES), BF16),
        ],
        compiler_params=pltpu.CompilerParams(
            dimension_semantics=("arbitrary", "arbitrary", "arbitrary"),
            vmem_limit_bytes=VMEM_LIMIT),
        name="band_attention",
    )(qkv, qkv, qkv, qkv, qkv, bias)


def _attention_bias(rel_bias):
    H = rel_bias.shape[0]
    table = rel_bias.astype(F32)
    far = LEFT_CHUNKS * CHUNK + ATT_SUB - 1
    n_left = far - MAX_LEFT_REL + 1
    n_ext = ATT_SUB + ATT_KEYS - 1
    ext = jnp.concatenate([
        jnp.broadcast_to(table[:, :1], (H, n_left)), table[:, 1:],
        jnp.broadcast_to(table[:, -1:], (H, n_ext - n_left - (table.shape[1] - 1)))], axis=1)
    skew = jnp.tile(ext, (1, ATT_SUB + 1))[:, : ATT_SUB * (n_ext + 1)].reshape(H, ATT_SUB, n_ext + 1)
    bias = skew[:, ::-1, :ATT_KEYS]
    r = jnp.arange(ATT_SUB)[:, None]
    c = jnp.arange(ATT_KEYS)[None, :]
    band = c // CHUNK - r // CHUNK
    ok = (band >= 0) & (band <= LEFT_CHUNKS)
    variants = [ok] + [ok & (c >= ATT_QROWS - qs * ATT_SUB) for qs in range(ATT_QROWS // ATT_SUB)]
    return jnp.where(jnp.stack(variants)[None], bias[:, None], NEG_BIG)


def _rwkv_kernel(p_ref, pprev_ref, mu_ref, w0_ref, a0_ref, lora_ref, wg_ref, kk_ref, ka_ref,
                 rk_ref, lng_ref, lnb_ref, bd_ref, tri_ref, o_ref,
                 rt_s, at_s, bt_s, kt_s, bh_s, kh_s, v_s, wl_s, y_s, z_s, g_s, y0_s, m_s, n_s,
                 *, width, tm):
    i = pl.program_id(1)
    n_groups = width // GROUP
    heads_per_group = GROUP // HEAD_DIM

    @pl.when(i == 0)
    def _():
        z_s[...] = jnp.zeros_like(z_s)

    p = p_ref[...].astype(F32)
    first = jnp.where(i > 0, 1.0, 0.0)
    prev_row = pprev_ref[7:8, :].astype(F32) * first
    row = lax.broadcasted_iota(jnp.int32, (tm, 1), 0)
    prev = jnp.where(row == 0, prev_row, pltpu.roll(p, 1, axis=0))
    ps = p + (prev - p) * mu_ref[...]

    r = ps[:, :width]
    k = ps[:, width: 2 * width]
    v = ps[:, 2 * width: 3 * width]
    wa = ps[:, 3 * width: 3 * width + DECAY_LORA + AAA_LORA]
    gd = ps[:, 3 * width + DECAY_LORA + AAA_LORA:]
    lane = lax.broadcasted_iota(jnp.int32, (1, DECAY_LORA + AAA_LORA), 1)
    wa = jnp.where(lane < DECAY_LORA, jnp.tanh(wa), wa)
    lw = _dot(wa.astype(BF16), lora_ref[...])
    logw = -math.exp(-0.5) * _sigmoid(w0_ref[...] + lw[:, :width])
    eta = _sigmoid(a0_ref[...] + lw[:, width:])
    g = _dot(_sigmoid(gd).astype(BF16), wg_ref[...])

    bd = bd_ref[...]
    kk = k * kk_ref[...]
    kk = kk / jnp.maximum(jnp.sqrt(_split_dot(kk * kk, bd)), 1e-12)
    km = k * (1.0 + (eta - 1.0) * ka_ref[...])
    bb = kk * eta

    sums = _split_dot_left(tri_ref[...], logw)
    cs = sums[:tm]
    tot = sums[tm:]
    e_pos = jnp.exp(cs)
    e_neg = jnp.exp(-cs)
    e_tail = jnp.exp(tot - cs)
    rt_s[...] = r * e_pos
    at_s[...] = -kk * jnp.exp(cs - logw)
    bt_s[...] = bb * e_neg
    kt_s[...] = km * e_neg
    bh_s[...] = bb * e_tail
    kh_s[...] = km * e_tail
    v_s[...] = v
    wl_s[...] = jnp.exp(tot)

    rr = lax.broadcasted_iota(jnp.int32, (GROUP, GROUP), 0)
    cc = lax.broadcasted_iota(jnp.int32, (GROUP, GROUP), 1)
    shift = CHUNK.bit_length() - 1
    same = (rr >> shift) == (cc >> shift)
    strict = same & (rr > cc)
    incl = same & (rr >= cc)
    eye = jnp.where(rr == cc, 1.0, 0.0)
    lane_head = lax.broadcasted_iota(jnp.int32, (1, GROUP), 1) >> shift

    def stack_masked(x):
        return jnp.concatenate(
            [jnp.where(lane_head == h, x, 0.0) for h in range(heads_per_group)], axis=0)

    def stack_plain(x):
        return jnp.concatenate([x] * heads_per_group, axis=0)

    def bdot(a, b):
        return _dot(a.astype(BF16), b.astype(BF16))

    def bdot_nt(a, b):
        return _dot_nt(a.astype(BF16), b.astype(BF16))

    def local_body(c, carry):
        r0 = pl.multiple_of(c * CHUNK, CHUNK)
        rows = pl.ds(r0, CHUNK)
        for gi in range(n_groups):
            cols = slice(gi * GROUP, (gi + 1) * GROUP)
            at4 = stack_masked(at_s[rows, cols])
            rt4 = stack_masked(rt_s[rows, cols])
            v4 = stack_masked(v_s[rows, cols])
            bh4 = stack_masked(bh_s[rows, cols])
            kh4 = stack_masked(kh_s[rows, cols])
            bt_rep = stack_plain(bt_s[rows, cols])
            kt_rep = stack_plain(kt_s[rows, cols])
            a_ab = jnp.where(strict, bdot_nt(at4, bt_rep), 0.0)
            a_ak = jnp.where(strict, bdot_nt(at4, kt_rep), 0.0)
            a_rb = jnp.where(incl, bdot_nt(rt4, bt_rep), 0.0)
            a_rk = jnp.where(incl, bdot_nt(rt4, kt_rep), 0.0)
            tinv = eye + a_ab
            pw = a_ab
            span = 2
            while span < CHUNK:
                pw = bdot(pw, pw)
                tinv = tinv + bdot(pw, tinv)
                span *= 2
            p4 = bdot(tinv, at4)
            q4 = bdot(tinv, bdot(a_ak, v4))
            g_s[c, gi] = (rt4 + bdot(a_rb, p4)).astype(BF16)
            y0_s[c, gi] = bdot(a_rb, q4) + bdot(a_rk, v4)
            m_s[c, gi] = bdot(p4.T, bh4).astype(BF16)
            n_s[c, gi] = bdot(q4.T, bh4) + bdot(v4.T, kh4)
        return carry

    lax.fori_loop(0, tm // CHUNK, local_body, 0, unroll=LOCAL_UNROLL)

    def carry_body(c, carry):
        r0 = pl.multiple_of(c * CHUNK, CHUNK)
        rows = pl.ds(r0, CHUNK)
        for gi in range(n_groups):
            cols = slice(gi * GROUP, (gi + 1) * GROUP)
            zt = z_s[gi]
            zb = zt.astype(BF16)
            y4 = _dot_nt(g_s[c, gi], zb) + y0_s[c, gi]
            z_s[gi] = zt * wl_s[pl.ds(r0, 1), cols] + _dot(zb, m_s[c, gi]) + n_s[c, gi]
            ytok = y4[:CHUNK]
            for h in range(1, heads_per_group):
                ytok = ytok + y4[h * CHUNK: (h + 1) * CHUNK]
            y_s[rows, cols] = ytok
        return carry

    lax.fori_loop(0, tm // CHUNK, carry_body, 0)

    y = y_s[...]
    inv_n = 1.0 / HEAD_DIM
    mean = _split_dot(y, bd) * inv_n
    d = y - mean
    var = _split_dot(d * d, bd) * inv_n
    yn = d * lax.rsqrt(var + LN_X_EPS) * lng_ref[...] + lnb_ref[...]
    bonus = _split_dot(r * km * rk_ref[...], bd) * v
    o_ref[...] = ((yn + bonus) * g).astype(BF16)


def _split_dot_left(a_exact, b, terms=3):
    acc = None
    rem = b
    for _ in range(terms):
        part = rem.astype(BF16)
        rem = rem - part.astype(F32)
        d = _dot(a_exact, part)
        acc = d if acc is None else acc + d
    return acc


def _rwkv(pr, mu, w0, a0, lora, wg, k_k, k_a, r_k, ln_g, ln_b, bd, B, S, width, tm=512):
    T = B * S
    n_t = S // tm
    cols = pr.shape[1]
    ti = jnp.arange(tm)
    same = (ti[:, None] // CHUNK) == (ti[None, :] // CHUNK)
    tri = jnp.concatenate([same & (ti[:, None] >= ti[None, :]), same], axis=0).astype(BF16)
    const = lambda b, i: (0, 0)
    vec = lambda n: pl.BlockSpec((1, n), const)
    scr = lambda: pltpu.VMEM((tm, width), F32)
    return pl.pallas_call(
        functools.partial(_rwkv_kernel, width=width, tm=tm),
        grid=(B, n_t),
        in_specs=[
            pl.BlockSpec((tm, cols), lambda b, i: (b * n_t + i, 0)),
            pl.BlockSpec((8, cols), lambda b, i: (jnp.maximum((b * n_t + i) * (tm // 8) - 1, 0), 0)),
            vec(cols), vec(width), vec(width),
            pl.BlockSpec(lora.shape, const),
            pl.BlockSpec(wg.shape, const),
            vec(width), vec(width), vec(width), vec(width), vec(width),
            pl.BlockSpec((width, width), const),
            pl.BlockSpec((2 * tm, tm), const),
        ],
        out_specs=pl.BlockSpec((tm, width), lambda b, i: (b * n_t + i, 0)),
        out_shape=jax.ShapeDtypeStruct((T, width), BF16),
        scratch_shapes=[scr() for _ in range(9)]
        + [pltpu.VMEM((width // GROUP, GROUP, GROUP), F32)]
        + [pltpu.VMEM((tm // CHUNK, width // GROUP, GROUP, GROUP), dt)
           for dt in (BF16, F32, BF16, F32)],
        compiler_params=pltpu.CompilerParams(
            dimension_semantics=("arbitrary", "arbitrary"), vmem_limit_bytes=VMEM_LIMIT),
        name="rwkv7",
    )(pr, pr, mu, w0, a0, lora, wg, k_k, k_a, r_k, ln_g, ln_b, bd, tri)


def _outproj_kernel(att_ref, rwk_ref, x_ref, wa_ref, wr_ref, g_ref, wrt_ref, br_ref, upper_ref,
                    x2_ref, xn_ref, eidx_ref, rank_ref, gate_ref, cnt_ref, *, n_exp, tm):
    x2 = x_ref[...] + _dot(att_ref[...], wa_ref[...]) + _dot(rwk_ref[...], wr_ref[...])
    x2_ref[...] = x2
    ms = jnp.mean(x2 * x2, axis=-1, keepdims=True)
    xn = x2 * lax.rsqrt(ms + NORM_EPS) * g_ref[...]
    xn_ref[...] = xn.astype(BF16)

    x_hi = xn.astype(BF16)
    x_lo = (xn - x_hi.astype(F32)).astype(BF16)
    logits = (_dot_nt(wrt_ref[0], x_hi) + _dot_nt(wrt_ref[0], x_lo)
              + _dot_nt(wrt_ref[1], x_hi)) + br_ref[...]

    eio = lax.broadcasted_iota(jnp.int32, (n_exp, tm), 0)
    work = logits
    vals, onehots = [], []
    for k in range(TOP_K):
        m = jnp.max(work, axis=0, keepdims=True)
        idx = jnp.min(jnp.where(work == m, eio, n_exp), axis=0, keepdims=True)
        oh = eio == idx
        vals.append(m)
        onehots.append(oh)
        eidx_ref[k: k + 1, :] = idx
        work = jnp.where(oh, -jnp.inf, work)

    exps = [jnp.exp(vk - vals[0]) for vk in vals]
    denom = exps[0]
    for e in exps[1:]:
        denom = denom + e
    for k in range(TOP_K):
        gate_ref[k: k + 1, :] = exps[k] / denom

    sel = onehots[0]
    for oh in onehots[1:]:
        sel = sel | oh
    sel_f = jnp.where(sel, 1.0, 0.0)
    before = _dot(sel_f.astype(BF16), upper_ref[...])
    for k in range(TOP_K):
        rank_ref[k: k + 1, :] = jnp.sum(
            jnp.where(onehots[k], before, 0.0), axis=0, keepdims=True).astype(jnp.int32)
    cnt_ref[...] = jnp.broadcast_to(
        jnp.sum(sel_f, axis=1, keepdims=True), cnt_ref.shape).astype(jnp.int32)


def _outproj(att, rwk, x2d, w_att, w_rwk, g2, wrt, br, n_exp):
    T, D = x2d.shape
    tm = ROUTE_TILE
    upper = (jnp.arange(tm)[:, None] < jnp.arange(tm)[None, :]).astype(BF16)
    const = lambda i: (0, 0)
    rows = lambda n: pl.BlockSpec((tm, n), lambda i: (i, 0))
    lanes = pl.BlockSpec((TOP_K, tm), lambda i: (0, i))
    return pl.pallas_call(
        functools.partial(_outproj_kernel, n_exp=n_exp, tm=tm),
        grid=(T // tm,),
        in_specs=[
            rows(att.shape[1]), rows(rwk.shape[1]), rows(D),
            pl.BlockSpec(w_att.shape, const), pl.BlockSpec(w_rwk.shape, const),
            pl.BlockSpec((1, D), const),
            pl.BlockSpec(wrt.shape, lambda i: (0, 0, 0)),
            pl.BlockSpec((n_exp, 1), const),
            pl.BlockSpec((tm, tm), const),
        ],
        out_specs=[rows(D), rows(D), lanes, lanes, lanes,
                   pl.BlockSpec((n_exp, LANES), lambda i: (i, 0))],
        out_shape=[
            jax.ShapeDtypeStruct((T, D), F32),
            jax.ShapeDtypeStruct((T, D), BF16),
            jax.ShapeDtypeStruct((TOP_K, T), jnp.int32),
            jax.ShapeDtypeStruct((TOP_K, T), jnp.int32),
            jax.ShapeDtypeStruct((TOP_K, T), F32),
            jax.ShapeDtypeStruct((T // tm * n_exp, LANES), jnp.int32),
        ],
        compiler_params=pltpu.CompilerParams(
            dimension_semantics=("arbitrary",), vmem_limit_bytes=VMEM_LIMIT),
        name="outproj_router",
    )(att, rwk, x2d, w_att, w_rwk, g2, wrt, br, upper)


def _run_copies(action, length, src, src_off, dst, dst_off, sem):
    for size in RUN_SIZES:
        start = length & ~(2 * size - 1)

        @pl.when((length & size) != 0)
        def _():
            cp = pltpu.make_async_copy(
                src.at[pl.ds(pl.multiple_of(src_off + start, RUN_ALIGN), size)],
                dst.at[pl.ds(pl.multiple_of(dst_off + start, RUN_ALIGN), size)], sem)
            cp.start() if action == "start" else cp.wait()


def _dispatch_kernel(soff_ref, rlen_ref, rdst_ref, tail_dst_ref, tail_len_ref, nact_ref,
                     slot_ref, xn_ref, xs_hbm, buf, zbuf, sem, *, n_exp, slots, n_blocks):
    j = pl.program_id(0)
    tm = ROUTE_TILE

    @pl.when(j == 0)
    def _():
        zbuf[...] = jnp.zeros_like(zbuf)

        def unused_block(action):
            def body(b, c):
                cp = pltpu.make_async_copy(
                    zbuf, xs_hbm.at[pl.ds(pl.multiple_of(b * MOE_ROWS, MOE_ROWS), MOE_ROWS)], sem)
                cp.start() if action == "start" else cp.wait()
                return c
            return body

        for action in ("start", "wait"):
            for e in range(n_exp):
                _run_copies(action, tail_len_ref[e], zbuf, 0, xs_hbm, tail_dst_ref[e], sem)
            lax.fori_loop(nact_ref[0], n_blocks, unused_block(action), 0)

    x = xn_ref[...]
    for sb in range(slots // GROUP):
        sio = lax.broadcasted_iota(jnp.int32, (GROUP, tm), 0) + sb * GROUP
        p = jnp.zeros((GROUP, tm), F32)
        for k in range(TOP_K):
            p = jnp.where(sio == slot_ref[0, k: k + 1, :], 1.0, p)
        buf[sb * GROUP: (sb + 1) * GROUP, :] = _dot(p.astype(BF16), x)

    for action in ("start", "wait"):
        for e in range(n_exp):
            r = j * n_exp + e
            _run_copies(action, rlen_ref[r], buf, soff_ref[r], xs_hbm, rdst_ref[r], sem)


def _dispatch(soff, rlen, rdst, tail_dst, tail_len, n_active, slot_lane, xn, n_blocks, slots):
    T, D = xn.shape
    tm = ROUTE_TILE
    n_exp = tail_dst.shape[0]
    n_rows = n_blocks * MOE_ROWS
    grid_spec = pltpu.PrefetchScalarGridSpec(
        num_scalar_prefetch=6,
        grid=(T // tm,),
        in_specs=[
            pl.BlockSpec((1, TOP_K, tm), lambda i, *_: (i, 0, 0)),
            pl.BlockSpec((tm, D), lambda i, *_: (i, 0)),
        ],
        out_specs=pl.BlockSpec(memory_space=pl.ANY),
        scratch_shapes=[pltpu.VMEM((slots, D), F32), pltpu.VMEM((MOE_ROWS, D), F32),
                        pltpu.SemaphoreType.DMA],
    )
    return pl.pallas_call(
        functools.partial(_dispatch_kernel, n_exp=n_exp, slots=slots, n_blocks=n_blocks),
        grid_spec=grid_spec,
        out_shape=jax.ShapeDtypeStruct((n_rows, D), F32),
        compiler_params=pltpu.CompilerParams(
            dimension_semantics=("arbitrary",), vmem_limit_bytes=VMEM_LIMIT),
        name="moe_dispatch",
    )(soff, rlen, rdst, tail_dst, tail_len, n_active, slot_lane, xn)


def _moe_kernel(bexp_ref, nact_ref, x_ref, wup_ref, bup_ref, wdn_ref, bdn_ref,
                y_ref, wup_bf, wdn_bf, *, d_exp):
    i = pl.program_id(0)

    @pl.when(i < nact_ref[0])
    def _():
        new_expert = jnp.logical_or(i == 0, bexp_ref[i] != bexp_ref[jnp.maximum(i - 1, 0)])

        @pl.when(new_expert)
        def _():
            wup_bf[...] = wup_ref[0].astype(BF16)
            wdn_bf[...] = wdn_ref[0].astype(BF16)

        x = x_ref[...].astype(BF16)
        h_glu = _dot(x, wup_bf[:, :d_exp]) + bup_ref[0, :, :d_exp]
        h_lin = _dot(x, wup_bf[:, d_exp:]) + bup_ref[0, :, d_exp:]
        h_glu = jnp.minimum(h_glu, SWIGLU_LIMIT)
        h_lin = jnp.clip(h_lin, -SWIGLU_LIMIT, SWIGLU_LIMIT)
        act = h_glu * _sigmoid(SWIGLU_ALPHA * h_glu) * (h_lin + 1.0)
        y_ref[...] = _dot(act.astype(BF16), wdn_bf[...]) + bdn_ref[0]

    @pl.when(i >= nact_ref[0])
    def _():
        y_ref[...] = jnp.zeros_like(y_ref)


def _moe(blk_expert, n_active, xs, w_up, b_up, w_down, b_down):
    n_blocks = blk_expert.shape[0]
    n_exp, D, two_f = w_up.shape
    d_exp = two_f // 2
    blk = lambda i, na: jnp.minimum(i, na[0] - 1)
    grid_spec = pltpu.PrefetchScalarGridSpec(
        num_scalar_prefetch=2,
        grid=(n_blocks,),
        in_specs=[
            pl.BlockSpec((MOE_ROWS, D), lambda i, be, na: (blk(i, na), 0)),
            pl.BlockSpec((1, D, two_f), lambda i, be, na: (be[blk(i, na)], 0, 0)),
            pl.BlockSpec((1, 1, two_f), lambda i, be, na: (be[blk(i, na)], 0, 0)),
            pl.BlockSpec((1, d_exp, D), lambda i, be, na: (be[blk(i, na)], 0, 0)),
            pl.BlockSpec((1, 1, D), lambda i, be, na: (be[blk(i, na)], 0, 0)),
        ],
        out_specs=pl.BlockSpec((MOE_ROWS, D), lambda i, be, na: (i, 0)),
        scratch_shapes=[
            pltpu.VMEM((D, two_f), BF16),
            pltpu.VMEM((d_exp, D), BF16),
        ],
    )
    return pl.pallas_call(
        functools.partial(_moe_kernel, d_exp=d_exp),
        grid_spec=grid_spec,
        out_shape=jax.ShapeDtypeStruct((n_blocks * MOE_ROWS, D), F32),
        compiler_params=pltpu.CompilerParams(
            dimension_semantics=("arbitrary",), vmem_limit_bytes=VMEM_LIMIT),
        name="moe_ffn",
    )(blk_expert, n_active, xs,
      w_up, b_up.reshape(n_exp, 1, two_f), w_down, b_down.reshape(n_exp, 1, D))


def _combine_kernel(soff_ref, rlen_ref, rdst_ref, slot_ref, gate_ref, x2_ref, y_hbm, o_ref,
                    buf, sem, *, n_exp, slots):
    j = pl.program_id(0)
    tm = ROUTE_TILE

    @pl.when(j == 0)
    def _():
        buf[...] = jnp.zeros_like(buf)

    for action in ("start", "wait"):
        for e in range(n_exp):
            r = j * n_exp + e
            _run_copies(action, rlen_ref[r], y_hbm, rdst_ref[r], buf, soff_ref[r], sem)

    slot = slot_ref[...]
    gate = gate_ref[...]
    acc = x2_ref[...]
    for sb in range(slots // GROUP):
        lio = lax.broadcasted_iota(jnp.int32, (tm, GROUP), 1) + sb * GROUP
        w = jnp.zeros((tm, GROUP), F32)
        for k in range(TOP_K):
            w = jnp.where(lio == slot[:, k: k + 1], gate[:, k: k + 1], w)
        acc = acc + _dot(w.astype(BF16), buf[sb * GROUP: (sb + 1) * GROUP, :].astype(BF16))
    o_ref[...] = acc


def _combine(soff, rlen, rdst, slot_row, gates_row, x2, y_rows, slots):
    T, D = x2.shape
    tm = ROUTE_TILE
    n_exp = soff.shape[0] // (T // tm)
    grid_spec = pltpu.PrefetchScalarGridSpec(
        num_scalar_prefetch=3,
        grid=(T // tm,),
        in_specs=[
            pl.BlockSpec((tm, TOP_K), lambda i, *_: (i, 0)),
            pl.BlockSpec((tm, TOP_K), lambda i, *_: (i, 0)),
            pl.BlockSpec((tm, D), lambda i, *_: (i, 0)),
            pl.BlockSpec(memory_space=pl.ANY),
        ],
        out_specs=pl.BlockSpec((tm, D), lambda i, *_: (i, 0)),
        scratch_shapes=[pltpu.VMEM((slots, D), F32), pltpu.SemaphoreType.DMA],
    )
    return pl.pallas_call(
        functools.partial(_combine_kernel, n_exp=n_exp, slots=slots),
        grid_spec=grid_spec,
        out_shape=jax.ShapeDtypeStruct((T, D), F32),
        compiler_params=pltpu.CompilerParams(
            dimension_semantics=("arbitrary",), vmem_limit_bytes=VMEM_LIMIT),
        name="moe_combine",
    )(soff, rlen, rdst, slot_row, gates_row, x2, y_rows)


def _block_diag_ones(n, blk):
    i = jnp.arange(n)
    return ((i[:, None] // blk) == (i[None, :] // blk)).astype(BF16)


def _layer(x, norm1_g, w_in, tshift_mu, q_norm_g, k_norm_g, rel_bias, w0, w_decay_up, a0,
           w_aaa_up, w_gate_up, k_k, k_a, r_k, ln_x_g, ln_x_b, w_out, norm2_g, w_router,
           b_router, w_up, b_up, w_down, b_down):
    B, S, D = x.shape
    T = B * S
    att_heads = rel_bias.shape[0]
    att_w = att_heads * HEAD_DIM
    width = w0.shape[0]
    n_exp = w_router.shape[1]
    assert att_w % LANES == 0 and width % GROUP == 0 and S % ATT_QROWS == 0
    assert T % ROUTE_TILE == 0

    x2d = x.reshape(T, D)
    row = lambda a: a.reshape(1, -1).astype(F32)
    bd_att = _block_diag_ones(att_w, HEAD_DIM)
    bd_rwkv = _block_diag_ones(width, HEAD_DIM)

    qkv, pr = _inproj(
        x2d, row(norm1_g), w_in.astype(BF16),
        row(jnp.tile(q_norm_g, att_heads)), row(jnp.tile(k_norm_g, att_heads)), bd_att, att_w)

    att = _attention(qkv, _attention_bias(rel_bias), B, S, att_w)

    lora = jnp.zeros((DECAY_LORA + AAA_LORA, 2 * width), F32)
    lora = lora.at[:DECAY_LORA, :width].set(w_decay_up).at[DECAY_LORA:, width:].set(w_aaa_up)
    rwk = _rwkv(pr, row(tshift_mu), row(w0), row(a0), lora.astype(BF16),
                w_gate_up.astype(BF16), row(k_k), row(k_a), row(r_k), row(ln_x_g),
                row(ln_x_b), bd_rwkv, B, S, width)

    wrt = w_router.T.astype(F32)
    wrt_hi = wrt.astype(BF16)
    wrt_lo = (wrt - wrt_hi.astype(F32)).astype(BF16)
    x2, xn, eidx, rank, gates, counts = _outproj(
        att, rwk, x2d, w_out[:att_w].astype(BF16), w_out[att_w:].astype(BF16), row(norm2_g),
        jnp.stack([wrt_hi, wrt_lo]), b_router.reshape(n_exp, 1).astype(F32), n_exp)

    tm = ROUTE_TILE
    n_tiles = T // tm
    i32 = lambda a: a.astype(jnp.int32)
    run_n = counts.reshape(n_tiles, n_exp, LANES)[:, :, 0]
    run_len = (run_n + RUN_ALIGN - 1) // RUN_ALIGN * RUN_ALIGN
    run_off = jnp.cumsum(run_len, axis=1) - run_len
    exp_rows = jnp.sum(run_len, axis=0)
    exp_pad = (exp_rows + MOE_ROWS - 1) // MOE_ROWS * MOE_ROWS
    exp_end = jnp.cumsum(exp_pad)
    exp_start = exp_end - exp_pad
    run_dst = exp_start[None, :] + jnp.cumsum(run_len, axis=0) - run_len
    eidx_t = eidx.reshape(TOP_K, n_tiles, tm)
    experts = jnp.arange(n_exp, dtype=jnp.int32)[:, None, None, None]
    slot = rank.reshape(TOP_K, n_tiles, tm) + jnp.sum(
        jnp.where(eidx_t[None] == experts, run_off.T[:, None, :, None], 0), axis=0)
    slot_lane = i32(slot.transpose(1, 0, 2))
    slot_row = i32(slot.transpose(1, 2, 0).reshape(T, TOP_K))
    slots = -(-(tm * TOP_K + n_exp * (RUN_ALIGN - 1)) // GROUP) * GROUP
    n_blocks = -(-(T * TOP_K + n_tiles * n_exp * (RUN_ALIGN - 1)) // MOE_ROWS) + n_exp
    blk_start = jnp.arange(n_blocks, dtype=jnp.int32) * MOE_ROWS
    blk_expert = i32(jnp.minimum(
        jnp.sum(blk_start[:, None] >= exp_end[None, :], axis=1), n_exp - 1))
    n_active = i32(exp_end[-1:] // MOE_ROWS)
    soff, rlen, rdst = (i32(a.reshape(-1)) for a in (run_off, run_len, run_dst))

    xs = _dispatch(soff, rlen, rdst, i32(exp_start + exp_rows), i32(exp_pad - exp_rows),
                   n_active, slot_lane, xn, n_blocks, slots)
    y_rows = _moe(blk_expert, n_active, xs, w_up, b_up, w_down, b_down)
    out = _combine(soff, rlen, rdst, slot_row, gates.T, x2, y_rows, slots)
    return out.reshape(B, S, D)


def kernel(x, norm1_g, w_in, tshift_mu, q_norm_g, k_norm_g, rel_bias, w0, w_decay_up, a0,
           w_aaa_up, w_gate_up, k_k, k_a, r_k, ln_x_g, ln_x_b, w_out, norm2_g, w_router,
           b_router, w_up, b_up, w_down, b_down):
    params = (norm1_g, w_in, tshift_mu, q_norm_g, k_norm_g, rel_bias, w0, w_decay_up, a0,
              w_aaa_up, w_gate_up, k_k, k_a, r_k, ln_x_g, ln_x_b, w_out, norm2_g, w_router,
              b_router, w_up, b_up, w_down, b_down)
    for l in range(norm1_g.shape[0]):
        x = _layer(x, *[p[l] for p in params])
    return x
```

```python
import functools
import math

import jax
import jax.numpy as jnp
from jax import lax
from jax.experimental import pallas as pl
from jax.experimental.pallas import tpu as pltpu

F32 = jnp.float32
BF16 = jnp.bfloat16

CHUNK = 64
HEAD_DIM = 64
LEFT_CHUNKS = 8
MAX_LEFT_REL = 128
TOP_K = 4
SWIGLU_ALPHA = 1.702
SWIGLU_LIMIT = 7.0
NORM_EPS = 1e-6
LN_X_EPS = 64e-5
DECAY_LORA = 64
AAA_LORA = 64
GATE_LORA = 128

LANES = 128
GROUP = 256
NEG_BIG = -1e30

VMEM_LIMIT = 56 * 1024 * 1024

ATT_QROWS = 512
ATT_SUB = 256
ATT_KEYS = ATT_SUB + LEFT_CHUNKS * CHUNK
ATT_INTERLEAVE = 4

LOCAL_UNROLL = 2

MOE_ROWS = 512
ROUTE_TILE = 512
RUN_ALIGN = 8
RUN_SIZES = tuple(ROUTE_TILE >> s for s in range((ROUTE_TILE // RUN_ALIGN).bit_length()))
assert MOE_ROWS <= ROUTE_TILE and RUN_SIZES[-1] == RUN_ALIGN


def _dot(a, b):
    return jnp.dot(a, b, preferred_element_type=F32)


def _dot_nt(a, b):
    return lax.dot_general(a, b, (((1,), (1,)), ((), ())), preferred_element_type=F32)


def _split_dot(a, b_exact, terms=2):
    acc = None
    rem = a
    for _ in range(terms):
        part = rem.astype(BF16)
        rem = rem - part.astype(F32)
        d = _dot(part, b_exact)
        acc = d if acc is None else acc + d
    return acc


def _sigmoid(x):
    return 1.0 / (1.0 + jnp.exp(-x))


def _inproj_kernel(x_ref, g_ref, w_ref, qg_ref, kg_ref, bd_ref, qkv_ref, pr_ref, *, att_w):
    x = x_ref[...]
    ms = jnp.mean(x * x, axis=-1, keepdims=True)
    h = (x * lax.rsqrt(ms + NORM_EPS) * g_ref[...]).astype(BF16)
    qkv = _dot(h, w_ref[:, : 3 * att_w])
    bd = bd_ref[...]

    def head_norm(t, g):
        ss = _split_dot(t * t, bd)
        return t * lax.rsqrt(ss * (1.0 / HEAD_DIM) + NORM_EPS) * g

    q = head_norm(qkv[:, :att_w], qg_ref[...]) * (HEAD_DIM ** -0.5)
    k = head_norm(qkv[:, att_w: 2 * att_w], kg_ref[...])
    qkv_ref[:, :att_w] = q.astype(BF16)
    qkv_ref[:, att_w: 2 * att_w] = k.astype(BF16)
    qkv_ref[:, 2 * att_w:] = qkv[:, 2 * att_w:].astype(BF16)
    pr_ref[...] = _dot(h, w_ref[:, 3 * att_w:]).astype(BF16)


def _inproj(x2d, g1, w_in_bf, qg, kg, bd, att_w, tm=512):
    T, D = x2d.shape
    in_cols = w_in_bf.shape[1]
    r_cols = in_cols - 3 * att_w
    const = lambda i: (0, 0)
    return pl.pallas_call(
        functools.partial(_inproj_kernel, att_w=att_w),
        grid=(T // tm,),
        in_specs=[
            pl.BlockSpec((tm, D), lambda i: (i, 0)),
            pl.BlockSpec((1, D), const),
            pl.BlockSpec((D, in_cols), const),
            pl.BlockSpec((1, att_w), const),
            pl.BlockSpec((1, att_w), const),
            pl.BlockSpec((att_w, att_w), const),
        ],
        out_specs=[
            pl.BlockSpec((tm, 3 * att_w), lambda i: (i, 0)),
            pl.BlockSpec((tm, r_cols), lambda i: (i, 0)),
        ],
        out_shape=[
            jax.ShapeDtypeStruct((T, 3 * att_w), BF16),
            jax.ShapeDtypeStruct((T, r_cols), BF16),
        ],
        compiler_params=pltpu.CompilerParams(
            dimension_semantics=("arbitrary",), vmem_limit_bytes=VMEM_LIMIT),
        name="inproj",
    )(x2d, g1, w_in_bf, qg, kg, bd)


def _attn_kernel(q_ref, kp_ref, kc_ref, vp_ref, vc_ref, bias_ref, o_ref, kbuf, vbuf):
    i = pl.program_id(2)
    kbuf[:ATT_QROWS, :] = kp_ref[...]
    kbuf[ATT_QROWS:, :] = kc_ref[...]
    vbuf[:ATT_QROWS, :] = vp_ref[...]
    vbuf[ATT_QROWS:, :] = vc_ref[...]
    tiles = range(ATT_QROWS // ATT_SUB)
    heads = range(LANES // HEAD_DIM)
    all_units = [(qs * ATT_SUB, h * HEAD_DIM, h, qs) for qs in tiles for h in heads]
    o = []
    for u0 in range(0, len(all_units), ATT_INTERLEAVE):
        units = all_units[u0: u0 + ATT_INTERLEAVE]
        s = [_dot_nt(q_ref[r0: r0 + ATT_SUB, lo: lo + HEAD_DIM],
                     kbuf[r0: r0 + ATT_KEYS, lo: lo + HEAD_DIM])
             + bias_ref[h, jnp.where(i == 0, 1 + qs, 0)] for (r0, lo, h, qs) in units]
        m = [jnp.max(x, axis=-1, keepdims=True) for x in s]
        p = [jnp.exp(x - mx) for x, mx in zip(s, m)]
        l = [jnp.sum(x, axis=-1, keepdims=True) for x in p]
        o += [_dot(x.astype(BF16), vbuf[r0: r0 + ATT_KEYS, lo: lo + HEAD_DIM]) / lx
              for x, lx, (r0, lo, _, _) in zip(p, l, units)]
    for qs in tiles:
        o_ref[qs * ATT_SUB: (qs + 1) * ATT_SUB, :] = jnp.concatenate(
            o[qs * len(heads): (qs + 1) * len(heads)], axis=-1).astype(BF16)


def _attention(qkv, bias, B, S, att_w):
    T = B * S
    n_i = S // ATT_QROWS
    n_hp = att_w // LANES
    heads_per = LANES // HEAD_DIM

    def cur(col0):
        return lambda b, hp, i: (b * n_i + i, col0 + hp)

    def prev(col0):
        return lambda b, hp, i: (b * n_i + jnp.maximum(i - 1, 0), col0 + hp)

    blk = (ATT_QROWS, LANES)
    return pl.pallas_call(
        _attn_kernel,
        grid=(B, n_hp, n_i),
        in_specs=[
            pl.BlockSpec(blk, cur(0)),
            pl.BlockSpec(blk, prev(n_hp)),
            pl.BlockSpec(blk, cur(n_hp)),
            pl.BlockSpec(blk, prev(2 * n_hp)),
            pl.BlockSpec(blk, cur(2 * n_hp)),
            pl.BlockSpec((heads_per,) + bias.shape[1:], lambda b, hp, i: (hp, 0, 0, 0)),
        ],
        out_specs=pl.BlockSpec(blk, lambda b, hp, i: (b * n_i + i, hp)),
        out_shape=jax.ShapeDtypeStruct((T, att_w), BF16),
        scratch_shapes=[
            pltpu.VMEM((2 * ATT_QROWS, LANES), BF16),
            pltpu.VMEM((2 * ATT_QROWS, LANES), BF16),
        ],
        compiler_params=pltpu.CompilerParams(
            dimension_semantics=("arbitrary", "arbitrary", "arbitrary"),
            vmem_limit_bytes=VMEM_LIMIT),
        name="band_attention",
    )(qkv, qkv, qkv, qkv, qkv, bias)


def _attention_bias(rel_bias):
    H = rel_bias.shape[0]
    table = rel_bias.astype(F32)
    far = LEFT_CHUNKS * CHUNK + ATT_SUB - 1
    n_left = far - MAX_LEFT_REL + 1
    n_ext = ATT_SUB + ATT_KEYS - 1
    ext = jnp.concatenate([
        jnp.broadcast_to(table[:, :1], (H, n_left)), table[:, 1:],
        jnp.broadcast_to(table[:, -1:], (H, n_ext - n_left - (table.shape[1] - 1)))], axis=1)
    flat = jnp.tile(ext, (1, ATT_SUB + 1))[:, ATT_SUB - 1: ATT_SUB - 1 + ATT_SUB * (n_ext - 1)]
    bias = flat.reshape(H, ATT_SUB, n_ext - 1)[:, :, :ATT_KEYS]
    r = jnp.arange(ATT_SUB)[:, None]
    c = jnp.arange(ATT_KEYS)[None, :]
    band = c // CHUNK - r // CHUNK
    ok = (band >= 0) & (band <= LEFT_CHUNKS)
    variants = [ok] + [ok & (c >= ATT_QROWS - qs * ATT_SUB) for qs in range(ATT_QROWS // ATT_SUB)]
    return jnp.where(jnp.stack(variants)[None], bias[:, None], NEG_BIG)


def _rwkv_kernel(p_ref, pprev_ref, mu_ref, w0_ref, a0_ref, lora_ref, wg_ref, kk_ref, ka_ref,
                 rk_ref, lng_ref, lnb_ref, bd_ref, tri_ref, o_ref,
                 rt_s, at_s, bt_s, kt_s, bh_s, kh_s, v_s, wl_s, y_s, z_s, g_s, y0_s, m_s, n_s,
                 *, width, tm):
    i = pl.program_id(1)
    n_groups = width // GROUP
    heads_per_group = GROUP // HEAD_DIM

    @pl.when(i == 0)
    def _():
        z_s[...] = jnp.zeros_like(z_s)

    p = p_ref[...].astype(F32)
    first = jnp.where(i > 0, 1.0, 0.0)
    prev_row = pprev_ref[7:8, :].astype(F32) * first
    row = lax.broadcasted_iota(jnp.int32, (tm, 1), 0)
    prev = jnp.where(row == 0, prev_row, pltpu.roll(p, 1, axis=0))
    ps = p + (prev - p) * mu_ref[...]

    r = ps[:, :width]
    k = ps[:, width: 2 * width]
    v = ps[:, 2 * width: 3 * width]
    wa = ps[:, 3 * width: 3 * width + DECAY_LORA + AAA_LORA]
    gd = ps[:, 3 * width + DECAY_LORA + AAA_LORA:]
    lane = lax.broadcasted_iota(jnp.int32, (1, DECAY_LORA + AAA_LORA), 1)
    wa = jnp.where(lane < DECAY_LORA, jnp.tanh(wa), wa)
    lw = _dot(wa.astype(BF16), lora_ref[...])
    logw = -math.exp(-0.5) * _sigmoid(w0_ref[...] + lw[:, :width])
    eta = _sigmoid(a0_ref[...] + lw[:, width:])
    g = _dot(_sigmoid(gd).astype(BF16), wg_ref[...])

    bd = bd_ref[...]
    kk = k * kk_ref[...]
    kk = kk / jnp.maximum(jnp.sqrt(_split_dot(kk * kk, bd)), 1e-12)
    km = k * (1.0 + (eta - 1.0) * ka_ref[...])
    bb = kk * eta

    sums = _split_dot_left(tri_ref[...], logw)
    cs = sums[:tm]
    tot = sums[tm:]
    e_pos = jnp.exp(cs)
    e_neg = jnp.exp(-cs)
    e_tail = jnp.exp(tot - cs)
    rt_s[...] = r * e_pos
    at_s[...] = -kk * jnp.exp(cs - logw)
    bt_s[...] = bb * e_neg
    kt_s[...] = km * e_neg
    bh_s[...] = bb * e_tail
    kh_s[...] = km * e_tail
    v_s[...] = v
    wl_s[...] = jnp.exp(tot)

    rr = lax.broadcasted_iota(jnp.int32, (GROUP, GROUP), 0)
    cc = lax.broadcasted_iota(jnp.int32, (GROUP, GROUP), 1)
    shift = CHUNK.bit_length() - 1
    same = (rr >> shift) == (cc >> shift)
    strict = same & (rr > cc)
    incl = same & (rr >= cc)
    eye = jnp.where(rr == cc, 1.0, 0.0)
    lane_head = lax.broadcasted_iota(jnp.int32, (1, GROUP), 1) >> shift

    def stack_masked(x):
        return jnp.concatenate(
            [jnp.where(lane_head == h, x, 0.0) for h in range(heads_per_group)], axis=0)

    def stack_plain(x):
        return jnp.concatenate([x] * heads_per_group, axis=0)

    def bdot(a, b):
        return _dot(a.astype(BF16), b.astype(BF16))

    def bdot_nt(a, b):
        return _dot_nt(a.astype(BF16), b.astype(BF16))

    def local_body(it, carry):
        units = []
        for u in range(LOCAL_UNROLL):
            c = it * LOCAL_UNROLL + u
            rows = pl.ds(pl.multiple_of(c * CHUNK, CHUNK), CHUNK)
            units += [(c, gi, rows, slice(gi * GROUP, (gi + 1) * GROUP)) for gi in range(n_groups)]

        def load(ref, stack):
            return [stack(ref[rows, cols]) for (_, _, rows, cols) in units]

        def each(f, *lists):
            return [f(*args) for args in zip(*lists)]

        b16 = lambda xs: [x.astype(BF16) for x in xs]
        rt4 = load(rt_s, stack_masked)
        at4, rt4b, v4 = b16(load(at_s, stack_masked)), b16(rt4), load(v_s, stack_masked)
        v4b, bh4 = b16(v4), b16(load(bh_s, stack_masked))
        kh4 = b16(load(kh_s, stack_masked))
        bt_rep, kt_rep = b16(load(bt_s, stack_plain)), b16(load(kt_s, stack_plain))
        a_ab = each(lambda a, b: jnp.where(strict, _dot_nt(a, b), 0.0), at4, bt_rep)
        a_ak = b16(each(lambda a, b: jnp.where(strict, _dot_nt(a, b), 0.0), at4, kt_rep))
        a_rb = b16(each(lambda a, b: jnp.where(incl, _dot_nt(a, b), 0.0), rt4b, bt_rep))
        a_rk = b16(each(lambda a, b: jnp.where(incl, _dot_nt(a, b), 0.0), rt4b, kt_rep))
        tinv = [eye + a for a in a_ab]
        pw = b16(a_ab)
        span = 2
        while span < CHUNK:
            pw = b16(each(_dot, pw, pw))
            tinv = each(lambda t, p: t + _dot(p, t.astype(BF16)), tinv, pw)
            span *= 2
        tinv = b16(tinv)
        p4 = each(_dot, tinv, at4)
        q4 = each(_dot, tinv, b16(each(_dot, a_ak, v4b)))
        p4b, q4b = b16(p4), b16(q4)
        g4 = each(lambda r, a, p: r + _dot(a, p), rt4, a_rb, p4b)
        y0 = each(lambda a, q, ak, v: _dot(a, q) + _dot(ak, v), a_rb, q4b, a_rk, v4b)
        m4 = each(lambda p, b: _dot(p.T.astype(BF16), b), p4, bh4)
        n4 = each(lambda q, b, v, k: _dot(q.T.astype(BF16), b) + _dot(v.T.astype(BF16), k),
                  q4, bh4, v4, kh4)
        for (c, gi, _, _), g, y, m, n in zip(units, g4, y0, m4, n4):
            g_s[c, gi] = g.astype(BF16)
            y0_s[c, gi] = y
            m_s[c, gi] = m.astype(BF16)
            n_s[c, gi] = n
        return carry

    lax.fori_loop(0, tm // CHUNK // LOCAL_UNROLL, local_body, 0)

    def carry_body(c, carry):
        r0 = pl.multiple_of(c * CHUNK, CHUNK)
        rows = pl.ds(r0, CHUNK)
        for gi in range(n_groups):
            cols = slice(gi * GROUP, (gi + 1) * GROUP)
            zt = z_s[gi]
            zb = zt.astype(BF16)
            y4 = _dot_nt(g_s[c, gi], zb) + y0_s[c, gi]
            z_s[gi] = zt * wl_s[pl.ds(r0, 1), cols] + _dot(zb, m_s[c, gi]) + n_s[c, gi]
            ytok = y4[:CHUNK]
            for h in range(1, heads_per_group):
                ytok = ytok + y4[h * CHUNK: (h + 1) * CHUNK]
            y_s[rows, cols] = ytok
        return carry

    lax.fori_loop(0, tm // CHUNK, carry_body, 0)

    y = y_s[...]
    inv_n = 1.0 / HEAD_DIM
    mean = _split_dot(y, bd) * inv_n
    d = y - mean
    var = _split_dot(d * d, bd) * inv_n
    yn = d * lax.rsqrt(var + LN_X_EPS) * lng_ref[...] + lnb_ref[...]
    bonus = _split_dot(r * km * rk_ref[...], bd) * v
    o_ref[...] = ((yn + bonus) * g).astype(BF16)


def _split_dot_left(a_exact, b, terms=3):
    acc = None
    rem = b
    for _ in range(terms):
        part = rem.astype(BF16)
        rem = rem - part.astype(F32)
        d = _dot(a_exact, part)
        acc = d if acc is None else acc + d
    return acc


def _rwkv(pr, mu, w0, a0, lora, wg, k_k, k_a, r_k, ln_g, ln_b, bd, B, S, width, tm=512):
    T = B * S
    n_t = S // tm
    cols = pr.shape[1]
    ti = jnp.arange(tm)
    same = (ti[:, None] // CHUNK) == (ti[None, :] // CHUNK)
    tri = jnp.concatenate([same & (ti[:, None] >= ti[None, :]), same], axis=0).astype(BF16)
    const = lambda b, i: (0, 0)
    vec = lambda n: pl.BlockSpec((1, n), const)
    scr = lambda: pltpu.VMEM((tm, width), F32)
    return pl.pallas_call(
        functools.partial(_rwkv_kernel, width=width, tm=tm),
        grid=(B, n_t),
        in_specs=[
            pl.BlockSpec((tm, cols), lambda b, i: (b * n_t + i, 0)),
            pl.BlockSpec((8, cols), lambda b, i: (jnp.maximum((b * n_t + i) * (tm // 8) - 1, 0), 0)),
            vec(cols), vec(width), vec(width),
            pl.BlockSpec(lora.shape, const),
            pl.BlockSpec(wg.shape, const),
            vec(width), vec(width), vec(width), vec(width), vec(width),
            pl.BlockSpec((width, width), const),
            pl.BlockSpec((2 * tm, tm), const),
        ],
        out_specs=pl.BlockSpec((tm, width), lambda b, i: (b * n_t + i, 0)),
        out_shape=jax.ShapeDtypeStruct((T, width), BF16),
        scratch_shapes=[scr() for _ in range(9)]
        + [pltpu.VMEM((width // GROUP, GROUP, GROUP), F32)]
        + [pltpu.VMEM((tm // CHUNK, width // GROUP, GROUP, GROUP), dt)
           for dt in (BF16, F32, BF16, F32)],
        compiler_params=pltpu.CompilerParams(
            dimension_semantics=("arbitrary", "arbitrary"), vmem_limit_bytes=VMEM_LIMIT),
        name="rwkv7",
    )(pr, pr, mu, w0, a0, lora, wg, k_k, k_a, r_k, ln_g, ln_b, bd, tri)


def _outproj_kernel(att_ref, rwk_ref, x_ref, wa_ref, wr_ref, g_ref, wrt_ref, br_ref, upper_ref,
                    x2_ref, xn_ref, eidx_ref, rank_ref, gate_ref, cnt_ref, *, n_exp, tm):
    x2 = x_ref[...] + _dot(att_ref[...], wa_ref[...]) + _dot(rwk_ref[...], wr_ref[...])
    x2_ref[...] = x2
    ms = jnp.mean(x2 * x2, axis=-1, keepdims=True)
    xn = x2 * lax.rsqrt(ms + NORM_EPS) * g_ref[...]
    xn_ref[...] = xn.astype(BF16)

    x_hi = xn.astype(BF16)
    x_lo = (xn - x_hi.astype(F32)).astype(BF16)
    logits = (_dot_nt(wrt_ref[0], x_hi) + _dot_nt(wrt_ref[0], x_lo)
              + _dot_nt(wrt_ref[1], x_hi)) + br_ref[...]

    eio = lax.broadcasted_iota(jnp.int32, (n_exp, tm), 0)
    work = logits
    vals, onehots = [], []
    for k in range(TOP_K):
        m = jnp.max(work, axis=0, keepdims=True)
        idx = jnp.min(jnp.where(work == m, eio, n_exp), axis=0, keepdims=True)
        oh = eio == idx
        vals.append(m)
        onehots.append(oh)
        eidx_ref[k: k + 1, :] = idx
        work = jnp.where(oh, -jnp.inf, work)

    exps = [jnp.exp(vk - vals[0]) for vk in vals]
    denom = exps[0]
    for e in exps[1:]:
        denom = denom + e
    for k in range(TOP_K):
        gate_ref[k: k + 1, :] = exps[k] / denom

    sel = onehots[0]
    for oh in onehots[1:]:
        sel = sel | oh
    sel_f = jnp.where(sel, 1.0, 0.0)
    before = _dot(sel_f.astype(BF16), upper_ref[...])
    for k in range(TOP_K):
        rank_ref[k: k + 1, :] = jnp.sum(
            jnp.where(onehots[k], before, 0.0), axis=0, keepdims=True).astype(jnp.int32)
    cnt_ref[...] = jnp.broadcast_to(
        jnp.sum(sel_f, axis=1, keepdims=True), cnt_ref.shape).astype(jnp.int32)


def _outproj(att, rwk, x2d, w_att, w_rwk, g2, wrt, br, n_exp):
    T, D = x2d.shape
    tm = ROUTE_TILE
    upper = (jnp.arange(tm)[:, None] < jnp.arange(tm)[None, :]).astype(BF16)
    const = lambda i: (0, 0)
    rows = lambda n: pl.BlockSpec((tm, n), lambda i: (i, 0))
    lanes = pl.BlockSpec((TOP_K, tm), lambda i: (0, i))
    return pl.pallas_call(
        functools.partial(_outproj_kernel, n_exp=n_exp, tm=tm),
        grid=(T // tm,),
        in_specs=[
            rows(att.shape[1]), rows(rwk.shape[1]), rows(D),
            pl.BlockSpec(w_att.shape, const), pl.BlockSpec(w_rwk.shape, const),
            pl.BlockSpec((1, D), const),
            pl.BlockSpec(wrt.shape, lambda i: (0, 0, 0)),
            pl.BlockSpec((n_exp, 1), const),
            pl.BlockSpec((tm, tm), const),
        ],
        out_specs=[rows(D), rows(D), lanes, lanes, lanes,
                   pl.BlockSpec((n_exp, LANES), lambda i: (i, 0))],
        out_shape=[
            jax.ShapeDtypeStruct((T, D), F32),
            jax.ShapeDtypeStruct((T, D), BF16),
            jax.ShapeDtypeStruct((TOP_K, T), jnp.int32),
            jax.ShapeDtypeStruct((TOP_K, T), jnp.int32),
            jax.ShapeDtypeStruct((TOP_K, T), F32),
            jax.ShapeDtypeStruct((T // tm * n_exp, LANES), jnp.int32),
        ],
        compiler_params=pltpu.CompilerParams(
            dimension_semantics=("arbitrary",), vmem_limit_bytes=VMEM_LIMIT),
        name="outproj_router",
    )(att, rwk, x2d, w_att, w_rwk, g2, wrt, br, upper)


def _run_copies(action, length, src, src_off, dst, dst_off, sem):
    for size in RUN_SIZES:
        start = length & ~(2 * size - 1)

        @pl.when((length & size) != 0)
        def _():
            cp = pltpu.make_async_copy(
                src.at[pl.ds(pl.multiple_of(src_off + start, RUN_ALIGN), size)],
                dst.at[pl.ds(pl.multiple_of(dst_off + start, RUN_ALIGN), size)], sem)
            cp.start() if action == "start" else cp.wait()


def _dispatch_kernel(soff_ref, rlen_ref, rdst_ref, tail_dst_ref, tail_len_ref, nact_ref,
                     slot_ref, xn_ref, xs_hbm, buf, zbuf, sem, *, n_exp, slots, n_blocks):
    j = pl.program_id(0)
    tm = ROUTE_TILE

    @pl.when(j == 0)
    def _():
        zbuf[...] = jnp.zeros_like(zbuf)

        def unused_block(action):
            def body(b, c):
                cp = pltpu.make_async_copy(
                    zbuf, xs_hbm.at[pl.ds(pl.multiple_of(b * MOE_ROWS, MOE_ROWS), MOE_ROWS)], sem)
                cp.start() if action == "start" else cp.wait()
                return c
            return body

        for action in ("start", "wait"):
            for e in range(n_exp):
                _run_copies(action, tail_len_ref[e], zbuf, 0, xs_hbm, tail_dst_ref[e], sem)
            lax.fori_loop(nact_ref[0], n_blocks, unused_block(action), 0)

    x = xn_ref[...]
    for sb in range(slots // GROUP):
        sio = lax.broadcasted_iota(jnp.int32, (GROUP, tm), 0) + sb * GROUP
        p = jnp.zeros((GROUP, tm), F32)
        for k in range(TOP_K):
            p = jnp.where(sio == slot_ref[0, k: k + 1, :], 1.0, p)
        buf[sb * GROUP: (sb + 1) * GROUP, :] = _dot(p.astype(BF16), x)

    for action in ("start", "wait"):
        for e in range(n_exp):
            r = j * n_exp + e
            _run_copies(action, rlen_ref[r], buf, soff_ref[r], xs_hbm, rdst_ref[r], sem)


def _dispatch(soff, rlen, rdst, tail_dst, tail_len, n_active, slot_lane, xn, n_blocks, slots):
    T, D = xn.shape
    tm = ROUTE_TILE
    n_exp = tail_dst.shape[0]
    n_rows = n_blocks * MOE_ROWS
    grid_spec = pltpu.PrefetchScalarGridSpec(
        num_scalar_prefetch=6,
        grid=(T // tm,),
        in_specs=[
            pl.BlockSpec((1, TOP_K, tm), lambda i, *_: (i, 0, 0)),
            pl.BlockSpec((tm, D), lambda i, *_: (i, 0)),
        ],
        out_specs=pl.BlockSpec(memory_space=pl.ANY),
        scratch_shapes=[pltpu.VMEM((slots, D), F32), pltpu.VMEM((MOE_ROWS, D), F32),
                        pltpu.SemaphoreType.DMA],
    )
    return pl.pallas_call(
        functools.partial(_dispatch_kernel, n_exp=n_exp, slots=slots, n_blocks=n_blocks),
        grid_spec=grid_spec,
        out_shape=jax.ShapeDtypeStruct((n_rows, D), F32),
        compiler_params=pltpu.CompilerParams(
            dimension_semantics=("arbitrary",), vmem_limit_bytes=VMEM_LIMIT),
        name="moe_dispatch",
    )(soff, rlen, rdst, tail_dst, tail_len, n_active, slot_lane, xn)


def _moe_kernel(bexp_ref, nact_ref, x_ref, wup_ref, bup_ref, wdn_ref, bdn_ref,
                y_ref, wup_bf, wdn_bf, *, d_exp):
    i = pl.program_id(0)

    @pl.when(i < nact_ref[0])
    def _():
        new_expert = jnp.logical_or(i == 0, bexp_ref[i] != bexp_ref[jnp.maximum(i - 1, 0)])

        @pl.when(new_expert)
        def _():
            wup_bf[...] = wup_ref[0].astype(BF16)
            wdn_bf[...] = wdn_ref[0].astype(BF16)

        x = x_ref[...].astype(BF16)
        h_glu = _dot(x, wup_bf[:, :d_exp]) + bup_ref[0, :, :d_exp]
        h_lin = _dot(x, wup_bf[:, d_exp:]) + bup_ref[0, :, d_exp:]
        h_glu = jnp.minimum(h_glu, SWIGLU_LIMIT)
        h_lin = jnp.clip(h_lin, -SWIGLU_LIMIT, SWIGLU_LIMIT)
        act = h_glu * _sigmoid(SWIGLU_ALPHA * h_glu) * (h_lin + 1.0)
        y_ref[...] = _dot(act.astype(BF16), wdn_bf[...]) + bdn_ref[0]

    @pl.when(i >= nact_ref[0])
    def _():
        y_ref[...] = jnp.zeros_like(y_ref)


def _moe(blk_expert, n_active, xs, w_up, b_up, w_down, b_down):
    n_blocks = blk_expert.shape[0]
    n_exp, D, two_f = w_up.shape
    d_exp = two_f // 2
    blk = lambda i, na: jnp.minimum(i, na[0] - 1)
    grid_spec = pltpu.PrefetchScalarGridSpec(
        num_scalar_prefetch=2,
        grid=(n_blocks,),
        in_specs=[
            pl.BlockSpec((MOE_ROWS, D), lambda i, be, na: (blk(i, na), 0)),
            pl.BlockSpec((1, D, two_f), lambda i, be, na: (be[blk(i, na)], 0, 0)),
            pl.BlockSpec((1, 1, two_f), lambda i, be, na: (be[blk(i, na)], 0, 0)),
            pl.BlockSpec((1, d_exp, D), lambda i, be, na: (be[blk(i, na)], 0, 0)),
            pl.BlockSpec((1, 1, D), lambda i, be, na: (be[blk(i, na)], 0, 0)),
        ],
        out_specs=pl.BlockSpec((MOE_ROWS, D), lambda i, be, na: (i, 0)),
        scratch_shapes=[
            pltpu.VMEM((D, two_f), BF16),
            pltpu.VMEM((d_exp, D), BF16),
        ],
    )
    return pl.pallas_call(
        functools.partial(_moe_kernel, d_exp=d_exp),
        grid_spec=grid_spec,
        out_shape=jax.ShapeDtypeStruct((n_blocks * MOE_ROWS, D), F32),
        compiler_params=pltpu.CompilerParams(
            dimension_semantics=("arbitrary",), vmem_limit_bytes=VMEM_LIMIT),
        name="moe_ffn",
    )(blk_expert, n_active, xs,
      w_up, b_up.reshape(n_exp, 1, two_f), w_down, b_down.reshape(n_exp, 1, D))


def _combine_kernel(soff_ref, rlen_ref, rdst_ref, slot_ref, gate_ref, x2_ref, y_hbm, o_ref,
                    buf, sem, *, n_exp, slots):
    j = pl.program_id(0)
    tm = ROUTE_TILE

    @pl.when(j == 0)
    def _():
        buf[...] = jnp.zeros_like(buf)

    for action in ("start", "wait"):
        for e in range(n_exp):
            r = j * n_exp + e
            _run_copies(action, rlen_ref[r], y_hbm, rdst_ref[r], buf, soff_ref[r], sem)

    slot = slot_ref[...]
    gate = gate_ref[...]
    acc = x2_ref[...]
    for sb in range(slots // GROUP):
        lio = lax.broadcasted_iota(jnp.int32, (tm, GROUP), 1) + sb * GROUP
        w = jnp.zeros((tm, GROUP), F32)
        for k in range(TOP_K):
            w = jnp.where(lio == slot[:, k: k + 1], gate[:, k: k + 1], w)
        acc = acc + _dot(w.astype(BF16), buf[sb * GROUP: (sb + 1) * GROUP, :].astype(BF16))
    o_ref[...] = acc


def _combine(soff, rlen, rdst, slot_row, gates_row, x2, y_rows, slots):
    T, D = x2.shape
    tm = ROUTE_TILE
    n_exp = soff.shape[0] // (T // tm)
    grid_spec = pltpu.PrefetchScalarGridSpec(
        num_scalar_prefetch=3,
        grid=(T // tm,),
        in_specs=[
            pl.BlockSpec((tm, TOP_K), lambda i, *_: (i, 0)),
            pl.BlockSpec((tm, TOP_K), lambda i, *_: (i, 0)),
            pl.BlockSpec((tm, D), lambda i, *_: (i, 0)),
            pl.BlockSpec(memory_space=pl.ANY),
        ],
        out_specs=pl.BlockSpec((tm, D), lambda i, *_: (i, 0)),
        scratch_shapes=[pltpu.VMEM((slots, D), F32), pltpu.SemaphoreType.DMA],
    )
    return pl.pallas_call(
        functools.partial(_combine_kernel, n_exp=n_exp, slots=slots),
        grid_spec=grid_spec,
        out_shape=jax.ShapeDtypeStruct((T, D), F32),
        compiler_params=pltpu.CompilerParams(
            dimension_semantics=("arbitrary",), vmem_limit_bytes=VMEM_LIMIT),
        name="moe_combine",
    )(soff, rlen, rdst, slot_row, gates_row, x2, y_rows)


def _block_diag_ones(n, blk):
    i = jnp.arange(n)
    return ((i[:, None] // blk) == (i[None, :] // blk)).astype(BF16)


def _layer(x, norm1_g, w_in, tshift_mu, q_norm_g, k_norm_g, rel_bias, w0, w_decay_up, a0,
           w_aaa_up, w_gate_up, k_k, k_a, r_k, ln_x_g, ln_x_b, w_out, norm2_g, w_router,
           b_router, w_up, b_up, w_down, b_down):
    B, S, D = x.shape
    T = B * S
    att_heads = rel_bias.shape[0]
    att_w = att_heads * HEAD_DIM
    width = w0.shape[0]
    n_exp = w_router.shape[1]
    assert att_w % LANES == 0 and width % GROUP == 0 and S % ATT_QROWS == 0
    assert T % ROUTE_TILE == 0

    x2d = x.reshape(T, D)
    row = lambda a: a.reshape(1, -1).astype(F32)
    bd_att = _block_diag_ones(att_w, HEAD_DIM)
    bd_rwkv = _block_diag_ones(width, HEAD_DIM)

    qkv, pr = _inproj(
        x2d, row(norm1_g), w_in.astype(BF16),
        row(jnp.tile(q_norm_g, att_heads)), row(jnp.tile(k_norm_g, att_heads)), bd_att, att_w)

    att = _attention(qkv, _attention_bias(rel_bias), B, S, att_w)

    lora = jnp.zeros((DECAY_LORA + AAA_LORA, 2 * width), F32)
    lora = lora.at[:DECAY_LORA, :width].set(w_decay_up).at[DECAY_LORA:, width:].set(w_aaa_up)
    rwk = _rwkv(pr, row(tshift_mu), row(w0), row(a0), lora.astype(BF16),
                w_gate_up.astype(BF16), row(k_k), row(k_a), row(r_k), row(ln_x_g),
                row(ln_x_b), bd_rwkv, B, S, width)

    wrt = w_router.T.astype(F32)
    wrt_hi = wrt.astype(BF16)
    wrt_lo = (wrt - wrt_hi.astype(F32)).astype(BF16)
    x2, xn, eidx, rank, gates, counts = _outproj(
        att, rwk, x2d, w_out[:att_w].astype(BF16), w_out[att_w:].astype(BF16), row(norm2_g),
        jnp.stack([wrt_hi, wrt_lo]), b_router.reshape(n_exp, 1).astype(F32), n_exp)

    tm = ROUTE_TILE
    n_tiles = T // tm
    i32 = lambda a: a.astype(jnp.int32)
    run_n = counts.reshape(n_tiles, n_exp, LANES)[:, :, 0]
    run_len = (run_n + RUN_ALIGN - 1) // RUN_ALIGN * RUN_ALIGN
    run_off = jnp.cumsum(run_len, axis=1) - run_len
    exp_rows = jnp.sum(run_len, axis=0)
    exp_pad = (exp_rows + MOE_ROWS - 1) // MOE_ROWS * MOE_ROWS
    exp_end = jnp.cumsum(exp_pad)
    exp_start = exp_end - exp_pad
    run_dst = exp_start[None, :] + jnp.cumsum(run_len, axis=0) - run_len
    eidx_t = eidx.reshape(TOP_K, n_tiles, tm)
    experts = jnp.arange(n_exp, dtype=jnp.int32)[:, None, None, None]
    slot = rank.reshape(TOP_K, n_tiles, tm) + jnp.sum(
        jnp.where(eidx_t[None] == experts, run_off.T[:, None, :, None], 0), axis=0)
    slot_lane = i32(slot.transpose(1, 0, 2))
    slot_row = i32(slot.transpose(1, 2, 0).reshape(T, TOP_K))
    slots = -(-(tm * TOP_K + n_exp * (RUN_ALIGN - 1)) // GROUP) * GROUP
    n_blocks = -(-(T * TOP_K + n_tiles * n_exp * (RUN_ALIGN - 1)) // MOE_ROWS) + n_exp
    blk_start = jnp.arange(n_blocks, dtype=jnp.int32) * MOE_ROWS
    blk_expert = i32(jnp.minimum(
        jnp.sum(blk_start[:, None] >= exp_end[None, :], axis=1), n_exp - 1))
    n_active = i32(exp_end[-1:] // MOE_ROWS)
    soff, rlen, rdst = (i32(a.reshape(-1)) for a in (run_off, run_len, run_dst))

    xs = _dispatch(soff, rlen, rdst, i32(exp_start + exp_rows), i32(exp_pad - exp_rows),
                   n_active, slot_lane, xn, n_blocks, slots)
    y_rows = _moe(blk_expert, n_active, xs, w_up, b_up, w_down, b_down)
    out = _combine(soff, rlen, rdst, slot_row, gates.T, x2, y_rows, slots)
    return out.reshape(B, S, D)


def kernel(x, norm1_g, w_in, tshift_mu, q_norm_g, k_norm_g, rel_bias, w0, w_decay_up, a0,
           w_aaa_up, w_gate_up, k_k, k_a, r_k, ln_x_g, ln_x_b, w_out, norm2_g, w_router,
           b_router, w_up, b_up, w_down, b_down):
    params = (norm1_g, w_in, tshift_mu, q_norm_g, k_norm_g, rel_bias, w0, w_decay_up, a0,
              w_aaa_up, w_gate_up, k_k, k_a, r_k, ln_x_g, ln_x_b, w_out, norm2_g, w_router,
              b_router, w_up, b_up, w_down, b_down)
    for l in range(norm1_g.shape[0]):
        x = _layer(x, *[p[l] for p in params])
    return x
```

```python
import functools
import math

import jax
import jax.numpy as jnp
from jax import lax
from jax.experimental import pallas as pl
from jax.experimental.pallas import tpu as pltpu

F32 = jnp.float32
BF16 = jnp.bfloat16

CHUNK = 64
HEAD_DIM = 64
LEFT_CHUNKS = 8
MAX_LEFT_REL = 128
TOP_K = 4
SWIGLU_ALPHA = 1.702
SWIGLU_LIMIT = 7.0
NORM_EPS = 1e-6
LN_X_EPS = 64e-5
DECAY_LORA = 64
AAA_LORA = 64
GATE_LORA = 128

LANES = 128
GROUP = 256
NEG_BIG = -1e30

VMEM_LIMIT = 56 * 1024 * 1024

ATT_QROWS = 512
ATT_SUB = 256
ATT_KEYS = ATT_SUB + LEFT_CHUNKS * CHUNK
ATT_INTERLEAVE = 4

LOCAL_UNROLL = 2

MOE_ROWS = 512
ROUTE_TILE = 512
RUN_ALIGN = 8
RUN_SIZES = tuple(ROUTE_TILE >> s for s in range((ROUTE_TILE // RUN_ALIGN).bit_length()))
assert MOE_ROWS <= ROUTE_TILE and RUN_SIZES[-1] == RUN_ALIGN


def _dot(a, b):
    return jnp.dot(a, b, preferred_element_type=F32)


def _dot_nt(a, b):
    return lax.dot_general(a, b, (((1,), (1,)), ((), ())), preferred_element_type=F32)


def _head_sums(a, block_ones):
    return _dot(a.astype(BF16), block_ones)


def _sigmoid(x):
    return 1.0 / (1.0 + jnp.exp(-x))


def _inproj_kernel(x_ref, g_ref, w_ref, qg_ref, kg_ref, bd_ref, qkv_ref, pr_ref, *, att_w):
    x = x_ref[...]
    ms = jnp.mean(x * x, axis=-1, keepdims=True)
    h = (x * lax.rsqrt(ms + NORM_EPS) * g_ref[...]).astype(BF16)
    qkv = _dot(h, w_ref[:, : 3 * att_w])
    bd = bd_ref[...]

    def head_norm(t, g):
        ss = _head_sums(t * t, bd)
        return t * lax.rsqrt(ss * (1.0 / HEAD_DIM) + NORM_EPS) * g

    q = head_norm(qkv[:, :att_w], qg_ref[...]) * (HEAD_DIM ** -0.5)
    k = head_norm(qkv[:, att_w: 2 * att_w], kg_ref[...])
    qkv_ref[:, :att_w] = q.astype(BF16)
    qkv_ref[:, att_w: 2 * att_w] = k.astype(BF16)
    qkv_ref[:, 2 * att_w:] = qkv[:, 2 * att_w:].astype(BF16)
    pr_ref[...] = _dot(h, w_ref[:, 3 * att_w:]).astype(BF16)


def _inproj(x2d, g1, w_in_bf, qg, kg, bd, att_w, tm=512):
    T, D = x2d.shape
    in_cols = w_in_bf.shape[1]
    r_cols = in_cols - 3 * att_w
    const = lambda i: (0, 0)
    return pl.pallas_call(
        functools.partial(_inproj_kernel, att_w=att_w),
        grid=(T // tm,),
        in_specs=[
            pl.BlockSpec((tm, D), lambda i: (i, 0)),
            pl.BlockSpec((1, D), const),
            pl.BlockSpec((D, in_cols), const),
            pl.BlockSpec((1, att_w), const),
            pl.BlockSpec((1, att_w), const),
            pl.BlockSpec((att_w, att_w), const),
        ],
        out_specs=[
            pl.BlockSpec((tm, 3 * att_w), lambda i: (i, 0)),
            pl.BlockSpec((tm, r_cols), lambda i: (i, 0)),
        ],
        out_shape=[
            jax.ShapeDtypeStruct((T, 3 * att_w), BF16),
            jax.ShapeDtypeStruct((T, r_cols), BF16),
        ],
        compiler_params=pltpu.CompilerParams(
            dimension_semantics=("arbitrary",), vmem_limit_bytes=VMEM_LIMIT),
        name="inproj",
    )(x2d, g1, w_in_bf, qg, kg, bd)


def _attn_kernel(q_ref, kp_ref, kc_ref, vp_ref, vc_ref, bias_ref, o_ref, kbuf, vbuf):
    i = pl.program_id(2)
    kbuf[:ATT_QROWS, :] = kp_ref[...]
    kbuf[ATT_QROWS:, :] = kc_ref[...]
    vbuf[:ATT_QROWS, :] = vp_ref[...]
    vbuf[ATT_QROWS:, :] = vc_ref[...]
    tiles = range(ATT_QROWS // ATT_SUB)
    heads = range(LANES // HEAD_DIM)
    all_units = [(qs * ATT_SUB, h * HEAD_DIM, h, qs) for qs in tiles for h in heads]
    o = []
    for u0 in range(0, len(all_units), ATT_INTERLEAVE):
        units = all_units[u0: u0 + ATT_INTERLEAVE]
        s = [_dot_nt(q_ref[r0: r0 + ATT_SUB, lo: lo + HEAD_DIM],
                     kbuf[r0: r0 + ATT_KEYS, lo: lo + HEAD_DIM])
             + bias_ref[h, jnp.where(i == 0, 1 + qs, 0)] for (r0, lo, h, qs) in units]
        m = [jnp.max(x, axis=-1, keepdims=True) for x in s]
        p = [jnp.exp(x - mx) for x, mx in zip(s, m)]
        l = [jnp.sum(x, axis=-1, keepdims=True) for x in p]
        o += [_dot(x.astype(BF16), vbuf[r0: r0 + ATT_KEYS, lo: lo + HEAD_DIM]) / lx
              for x, lx, (r0, lo, _, _) in zip(p, l, units)]
    for qs in tiles:
        o_ref[qs * ATT_SUB: (qs + 1) * ATT_SUB, :] = jnp.concatenate(
            o[qs * len(heads): (qs + 1) * len(heads)], axis=-1).astype(BF16)


def _attention(qkv, bias, B, S, att_w):
    T = B * S
    n_i = S // ATT_QROWS
    n_hp = att_w // LANES
    heads_per = LANES // HEAD_DIM

    def cur(col0):
        return lambda b, hp, i: (b * n_i + i, col0 + hp)

    def prev(col0):
        return lambda b, hp, i: (b * n_i + jnp.maximum(i - 1, 0), col0 + hp)

    blk = (ATT_QROWS, LANES)
    return pl.pallas_call(
        _attn_kernel,
        grid=(B, n_hp, n_i),
        in_specs=[
            pl.BlockSpec(blk, cur(0)),
            pl.BlockSpec(blk, prev(n_hp)),
            pl.BlockSpec(blk, cur(n_hp)),
            pl.BlockSpec(blk, prev(2 * n_hp)),
            pl.BlockSpec(blk, cur(2 * n_hp)),
            pl.BlockSpec((heads_per,) + bias.shape[1:], lambda b, hp, i: (hp, 0, 0, 0)),
        ],
        out_specs=pl.BlockSpec(blk, lambda b, hp, i: (b * n_i + i, hp)),
        out_shape=jax.ShapeDtypeStruct((T, att_w), BF16),
        scratch_shapes=[
            pltpu.VMEM((2 * ATT_QROWS, LANES), BF16),
            pltpu.VMEM((2 * ATT_QROWS, LANES), BF16),
        ],
        compiler_params=pltpu.CompilerParams(
            dimension_semantics=("arbitrary", "arbitrary", "arbitrary"),
            vmem_limit_bytes=VMEM_LIMIT),
        name="band_attention",
    )(qkv, qkv, qkv, qkv, qkv, bias)


def _attention_bias(rel_bias):
    H = rel_bias.shape[0]
    table = rel_bias.astype(F32)
    far = LEFT_CHUNKS * CHUNK + ATT_SUB - 1
    n_left = far - MAX_LEFT_REL + 1
    n_ext = ATT_SUB + ATT_KEYS - 1
    ext = jnp.concatenate([
        jnp.broadcast_to(table[:, :1], (H, n_left)), table[:, 1:],
        jnp.broadcast_to(table[:, -1:], (H, n_ext - n_left - (table.shape[1] - 1)))], axis=1)
    flat = jnp.tile(ext, (1, ATT_SUB + 1))[:, ATT_SUB - 1: ATT_SUB - 1 + ATT_SUB * (n_ext - 1)]
    bias = flat.reshape(H, ATT_SUB, n_ext - 1)[:, :, :ATT_KEYS]
    r = jnp.arange(ATT_SUB)[:, None]
    c = jnp.arange(ATT_KEYS)[None, :]
    band = c // CHUNK - r // CHUNK
    ok = (band >= 0) & (band <= LEFT_CHUNKS)
    variants = [ok] + [ok & (c >= ATT_QROWS - qs * ATT_SUB) for qs in range(ATT_QROWS // ATT_SUB)]
    return jnp.where(jnp.stack(variants)[None], bias[:, None], NEG_BIG)


def _rwkv_kernel(p_ref, pprev_ref, mu_ref, w0_ref, a0_ref, lora_ref, wg_ref, kk_ref, ka_ref,
                 rk_ref, lng_ref, lnb_ref, bd_ref, tri_ref, o_ref,
                 rt_s, at_s, bt_s, kt_s, bh_s, kh_s, v_s, wl_s, y_s, z_s, g_s, y0_s, m_s, n_s,
                 *, width, tm):
    i = pl.program_id(1)
    n_groups = width // GROUP
    heads_per_group = GROUP // HEAD_DIM

    @pl.when(i == 0)
    def _():
        z_s[...] = jnp.zeros_like(z_s)

    p = p_ref[...].astype(F32)
    first = jnp.where(i > 0, 1.0, 0.0)
    prev_row = pprev_ref[7:8, :].astype(F32) * first
    row = lax.broadcasted_iota(jnp.int32, (tm, 1), 0)
    prev = jnp.where(row == 0, prev_row, pltpu.roll(p, 1, axis=0))
    ps = p + (prev - p) * mu_ref[...]

    r = ps[:, :width]
    k = ps[:, width: 2 * width]
    v = ps[:, 2 * width: 3 * width]
    wa = ps[:, 3 * width: 3 * width + DECAY_LORA + AAA_LORA]
    gd = ps[:, 3 * width + DECAY_LORA + AAA_LORA:]
    lane = lax.broadcasted_iota(jnp.int32, (1, DECAY_LORA + AAA_LORA), 1)
    wa = jnp.where(lane < DECAY_LORA, jnp.tanh(wa), wa)
    lw = _dot(wa.astype(BF16), lora_ref[...])
    logw = -math.exp(-0.5) * _sigmoid(w0_ref[...] + lw[:, :width])
    eta = _sigmoid(a0_ref[...] + lw[:, width:])
    g = _dot(_sigmoid(gd).astype(BF16), wg_ref[...])

    bd = bd_ref[...]
    kk = k * kk_ref[...]
    kk = kk / jnp.maximum(jnp.sqrt(_head_sums(kk * kk, bd)), 1e-12)
    km = k * (1.0 + (eta - 1.0) * ka_ref[...])
    bb = kk * eta

    tri = tri_ref[...]
    cs_chunks = [_split_dot_left(tri, logw[c * CHUNK: (c + 1) * CHUNK]) for c in range(tm // CHUNK)]
    cs = jnp.concatenate(cs_chunks, axis=0)
    tot = jnp.concatenate(
        [jnp.broadcast_to(x[CHUNK - 1:], (CHUNK, width)) for x in cs_chunks], axis=0)
    e_pos = jnp.exp(cs)
    e_neg = jnp.exp(-cs)
    e_tail = jnp.exp(tot - cs)
    rt_s[...] = r * e_pos
    at_s[...] = -kk * jnp.exp(cs - logw)
    bt_s[...] = bb * e_neg
    kt_s[...] = km * e_neg
    bh_s[...] = bb * e_tail
    kh_s[...] = km * e_tail
    v_s[...] = v
    wl_s[...] = jnp.exp(tot)

    rr = lax.broadcasted_iota(jnp.int32, (GROUP, GROUP), 0)
    cc = lax.broadcasted_iota(jnp.int32, (GROUP, GROUP), 1)
    shift = CHUNK.bit_length() - 1
    same = (rr >> shift) == (cc >> shift)
    strict = same & (rr > cc)
    incl = same & (rr >= cc)
    eye = jnp.where(rr == cc, 1.0, 0.0)
    lane_head = lax.broadcasted_iota(jnp.int32, (1, GROUP), 1) >> shift

    def stack_masked(x):
        return jnp.concatenate(
            [jnp.where(lane_head == h, x, 0.0) for h in range(heads_per_group)], axis=0)

    def stack_plain(x):
        return jnp.concatenate([x] * heads_per_group, axis=0)

    def bdot(a, b):
        return _dot(a.astype(BF16), b.astype(BF16))

    def bdot_nt(a, b):
        return _dot_nt(a.astype(BF16), b.astype(BF16))

    def local_body(it, carry):
        units = []
        for u in range(LOCAL_UNROLL):
            c = it * LOCAL_UNROLL + u
            rows = pl.ds(pl.multiple_of(c * CHUNK, CHUNK), CHUNK)
            units += [(c, gi, rows, slice(gi * GROUP, (gi + 1) * GROUP)) for gi in range(n_groups)]

        def load(ref, stack):
            return [stack(ref[rows, cols]) for (_, _, rows, cols) in units]

        def each(f, *lists):
            return [f(*args) for args in zip(*lists)]

        b16 = lambda xs: [x.astype(BF16) for x in xs]
        rt4 = load(rt_s, stack_masked)
        at4, rt4b, v4 = b16(load(at_s, stack_masked)), b16(rt4), load(v_s, stack_masked)
        v4b, bh4 = b16(v4), b16(load(bh_s, stack_masked))
        kh4 = b16(load(kh_s, stack_masked))
        bt_rep, kt_rep = b16(load(bt_s, stack_plain)), b16(load(kt_s, stack_plain))
        a_ab = each(lambda a, b: jnp.where(strict, _dot_nt(a, b), 0.0), at4, bt_rep)
        a_ak = b16(each(lambda a, b: jnp.where(strict, _dot_nt(a, b), 0.0), at4, kt_rep))
        a_rb = b16(each(lambda a, b: jnp.where(incl, _dot_nt(a, b), 0.0), rt4b, bt_rep))
        a_rk = b16(each(lambda a, b: jnp.where(incl, _dot_nt(a, b), 0.0), rt4b, kt_rep))
        tinv = [eye + a for a in a_ab]
        pw = b16(a_ab)
        span = 2
        while span < CHUNK:
            pw = b16(each(_dot, pw, pw))
            tinv = each(lambda t, p: t + _dot(p, t.astype(BF16)), tinv, pw)
            span *= 2
        tinv = b16(tinv)
        p4 = each(_dot, tinv, at4)
        q4 = each(_dot, tinv, b16(each(_dot, a_ak, v4b)))
        p4b, q4b = b16(p4), b16(q4)
        g4 = each(lambda r, a, p: r + _dot(a, p), rt4, a_rb, p4b)
        y0 = each(lambda a, q, ak, v: _dot(a, q) + _dot(ak, v), a_rb, q4b, a_rk, v4b)
        m4 = each(lambda p, b: _dot(p.T.astype(BF16), b), p4, bh4)
        n4 = each(lambda q, b, v, k: _dot(q.T.astype(BF16), b) + _dot(v.T.astype(BF16), k),
                  q4, bh4, v4, kh4)
        for (c, gi, _, _), g, y, m, n in zip(units, g4, y0, m4, n4):
            g_s[c, gi] = g.astype(BF16)
            y0_s[c, gi] = y
            m_s[c, gi] = m.astype(BF16)
            n_s[c, gi] = n
        return carry

    lax.fori_loop(0, tm // CHUNK // LOCAL_UNROLL, local_body, 0)

    def carry_body(c, carry):
        r0 = pl.multiple_of(c * CHUNK, CHUNK)
        rows = pl.ds(r0, CHUNK)
        for gi in range(n_groups):
            cols = slice(gi * GROUP, (gi + 1) * GROUP)
            zt = z_s[gi]
            zb = zt.astype(BF16)
            y4 = _dot_nt(g_s[c, gi], zb) + y0_s[c, gi]
            z_s[gi] = zt * wl_s[pl.ds(r0, 1), cols] + _dot(zb, m_s[c, gi]) + n_s[c, gi]
            ytok = y4[:CHUNK]
            for h in range(1, heads_per_group):
                ytok = ytok + y4[h * CHUNK: (h + 1) * CHUNK]
            y_s[rows, cols] = ytok
        return carry

    lax.fori_loop(0, tm // CHUNK, carry_body, 0)

    y = y_s[...]
    inv_n = 1.0 / HEAD_DIM
    mean = _head_sums(y, bd) * inv_n
    d = y - mean
    var = _head_sums(d * d, bd) * inv_n
    yn = d * lax.rsqrt(var + LN_X_EPS) * lng_ref[...] + lnb_ref[...]
    bonus = _head_sums(r * km * rk_ref[...], bd) * v
    o_ref[...] = ((yn + bonus) * g).astype(BF16)


def _split_dot_left(a_exact, b, terms=2):
    acc = None
    rem = b
    for _ in range(terms):
        part = rem.astype(BF16)
        rem = rem - part.astype(F32)
        d = _dot(a_exact, part)
        acc = d if acc is None else acc + d
    return acc


def _rwkv(pr, mu, w0, a0, lora, wg, k_k, k_a, r_k, ln_g, ln_b, bd, B, S, width, tm=512):
    T = B * S
    n_t = S // tm
    cols = pr.shape[1]
    ti = jnp.arange(CHUNK)
    tri = (ti[:, None] >= ti[None, :]).astype(BF16)
    const = lambda b, i: (0, 0)
    vec = lambda n: pl.BlockSpec((1, n), const)
    scr = lambda: pltpu.VMEM((tm, width), F32)
    return pl.pallas_call(
        functools.partial(_rwkv_kernel, width=width, tm=tm),
        grid=(B, n_t),
        in_specs=[
            pl.BlockSpec((tm, cols), lambda b, i: (b * n_t + i, 0)),
            pl.BlockSpec((8, cols), lambda b, i: (jnp.maximum((b * n_t + i) * (tm // 8) - 1, 0), 0)),
            vec(cols), vec(width), vec(width),
            pl.BlockSpec(lora.shape, const),
            pl.BlockSpec(wg.shape, const),
            vec(width), vec(width), vec(width), vec(width), vec(width),
            pl.BlockSpec((width, width), const),
            pl.BlockSpec((CHUNK, CHUNK), const),
        ],
        out_specs=pl.BlockSpec((tm, width), lambda b, i: (b * n_t + i, 0)),
        out_shape=jax.ShapeDtypeStruct((T, width), BF16),
        scratch_shapes=[scr() for _ in range(9)]
        + [pltpu.VMEM((width // GROUP, GROUP, GROUP), F32)]
        + [pltpu.VMEM((tm // CHUNK, width // GROUP, GROUP, GROUP), dt)
           for dt in (BF16, F32, BF16, F32)],
        compiler_params=pltpu.CompilerParams(
            dimension_semantics=("arbitrary", "arbitrary"), vmem_limit_bytes=VMEM_LIMIT),
        name="rwkv7",
    )(pr, pr, mu, w0, a0, lora, wg, k_k, k_a, r_k, ln_g, ln_b, bd, tri)


def _outproj_kernel(att_ref, rwk_ref, x_ref, wa_ref, wr_ref, g_ref, wrt_ref, br_ref, upper_ref,
                    x2_ref, xn_ref, eidx_ref, rank_ref, gate_ref, cnt_ref, *, n_exp, tm):
    x2 = x_ref[...] + _dot(att_ref[...], wa_ref[...]) + _dot(rwk_ref[...], wr_ref[...])
    x2_ref[...] = x2
    ms = jnp.mean(x2 * x2, axis=-1, keepdims=True)
    xn = x2 * lax.rsqrt(ms + NORM_EPS) * g_ref[...]
    xn_ref[...] = xn.astype(BF16)

    x_hi = xn.astype(BF16)
    x_lo = (xn - x_hi.astype(F32)).astype(BF16)
    logits = (_dot_nt(wrt_ref[0], x_hi) + _dot_nt(wrt_ref[0], x_lo)
              + _dot_nt(wrt_ref[1], x_hi)) + br_ref[...]

    eio = lax.broadcasted_iota(jnp.int32, (n_exp, tm), 0)
    work = logits
    vals, onehots = [], []
    for k in range(TOP_K):
        m = jnp.max(work, axis=0, keepdims=True)
        idx = jnp.min(jnp.where(work == m, eio, n_exp), axis=0, keepdims=True)
        oh = eio == idx
        vals.append(m)
        onehots.append(oh)
        eidx_ref[k: k + 1, :] = idx
        work = jnp.where(oh, -jnp.inf, work)

    exps = [jnp.exp(vk - vals[0]) for vk in vals]
    denom = exps[0]
    for e in exps[1:]:
        denom = denom + e
    for k in range(TOP_K):
        gate_ref[k: k + 1, :] = exps[k] / denom

    sel = onehots[0]
    for oh in onehots[1:]:
        sel = sel | oh
    sel_f = jnp.where(sel, 1.0, 0.0)
    before = _dot(sel_f.astype(BF16), upper_ref[...])
    for k in range(TOP_K):
        rank_ref[k: k + 1, :] = jnp.sum(
            jnp.where(onehots[k], before, 0.0), axis=0, keepdims=True).astype(jnp.int32)
    cnt_ref[...] = jnp.broadcast_to(
        jnp.sum(sel_f, axis=1, keepdims=True), cnt_ref.shape).astype(jnp.int32)


def _outproj(att, rwk, x2d, w_att, w_rwk, g2, wrt, br, n_exp):
    T, D = x2d.shape
    tm = ROUTE_TILE
    upper = (jnp.arange(tm)[:, None] < jnp.arange(tm)[None, :]).astype(BF16)
    const = lambda i: (0, 0)
    rows = lambda n: pl.BlockSpec((tm, n), lambda i: (i, 0))
    lanes = pl.BlockSpec((TOP_K, tm), lambda i: (0, i))
    return pl.pallas_call(
        functools.partial(_outproj_kernel, n_exp=n_exp, tm=tm),
        grid=(T // tm,),
        in_specs=[
            rows(att.shape[1]), rows(rwk.shape[1]), rows(D),
            pl.BlockSpec(w_att.shape, const), pl.BlockSpec(w_rwk.shape, const),
            pl.BlockSpec((1, D), const),
            pl.BlockSpec(wrt.shape, lambda i: (0, 0, 0)),
            pl.BlockSpec((n_exp, 1), const),
            pl.BlockSpec((tm, tm), const),
        ],
        out_specs=[rows(D), rows(D), lanes, lanes, lanes,
                   pl.BlockSpec((n_exp, LANES), lambda i: (i, 0))],
        out_shape=[
            jax.ShapeDtypeStruct((T, D), F32),
            jax.ShapeDtypeStruct((T, D), BF16),
            jax.ShapeDtypeStruct((TOP_K, T), jnp.int32),
            jax.ShapeDtypeStruct((TOP_K, T), jnp.int32),
            jax.ShapeDtypeStruct((TOP_K, T), F32),
            jax.ShapeDtypeStruct((T // tm * n_exp, LANES), jnp.int32),
        ],
        compiler_params=pltpu.CompilerParams(
            dimension_semantics=("arbitrary",), vmem_limit_bytes=VMEM_LIMIT),
        name="outproj_router",
    )(att, rwk, x2d, w_att, w_rwk, g2, wrt, br, upper)


def _run_copies(action, length, src, src_off, dst, dst_off, sem):
    for size in RUN_SIZES:
        start = length & ~(2 * size - 1)

        @pl.when((length & size) != 0)
        def _():
            cp = pltpu.make_async_copy(
                src.at[pl.ds(pl.multiple_of(src_off + start, RUN_ALIGN), size)],
                dst.at[pl.ds(pl.multiple_of(dst_off + start, RUN_ALIGN), size)], sem)
            cp.start() if action == "start" else cp.wait()


def _dispatch_kernel(soff_ref, rlen_ref, rdst_ref, tail_dst_ref, tail_len_ref, nact_ref,
                     slot_ref, xn_ref, xs_hbm, buf, zbuf, sem, run_sems, *, n_exp, slots, n_blocks):
    j = pl.program_id(0)
    tm = ROUTE_TILE

    @pl.when(j == 0)
    def _():
        zbuf[...] = jnp.zeros_like(zbuf)

        def unused_block(action):
            def body(b, c):
                cp = pltpu.make_async_copy(
                    zbuf, xs_hbm.at[pl.ds(pl.multiple_of(b * MOE_ROWS, MOE_ROWS), MOE_ROWS)], sem)
                cp.start() if action == "start" else cp.wait()
                return c
            return body

        for action in ("start", "wait"):
            for e in range(n_exp):
                _run_copies(action, tail_len_ref[e], zbuf, 0, xs_hbm, tail_dst_ref[e], sem)
            lax.fori_loop(nact_ref[0], n_blocks, unused_block(action), 0)

    cur = lax.rem(j, 2)
    x = xn_ref[...]
    for sb in range(slots // GROUP):
        sio = lax.broadcasted_iota(jnp.int32, (GROUP, tm), 0) + sb * GROUP
        p = jnp.zeros((GROUP, tm), F32)
        for k in range(TOP_K):
            p = jnp.where(sio == slot_ref[0, k: k + 1, :], 1.0, p)
        buf[cur, sb * GROUP: (sb + 1) * GROUP, :] = _dot(p.astype(BF16), x)

    def runs(action, tile, half):
        for e in range(n_exp):
            r = tile * n_exp + e
            _run_copies(action, rlen_ref[r], buf.at[half], soff_ref[r], xs_hbm, rdst_ref[r],
                        run_sems.at[half])

    runs("start", j, cur)

    @pl.when(j > 0)
    def _():
        runs("wait", j - 1, 1 - cur)

    @pl.when(j == pl.num_programs(0) - 1)
    def _():
        runs("wait", j, cur)


def _dispatch(soff, rlen, rdst, tail_dst, tail_len, n_active, slot_lane, xn, n_blocks, slots):
    T, D = xn.shape
    tm = ROUTE_TILE
    n_exp = tail_dst.shape[0]
    n_rows = n_blocks * MOE_ROWS
    grid_spec = pltpu.PrefetchScalarGridSpec(
        num_scalar_prefetch=6,
        grid=(T // tm,),
        in_specs=[
            pl.BlockSpec((1, TOP_K, tm), lambda i, *_: (i, 0, 0)),
            pl.BlockSpec((tm, D), lambda i, *_: (i, 0)),
        ],
        out_specs=pl.BlockSpec(memory_space=pl.ANY),
        scratch_shapes=[pltpu.VMEM((2, slots, D), F32), pltpu.VMEM((MOE_ROWS, D), F32),
                        pltpu.SemaphoreType.DMA, pltpu.SemaphoreType.DMA((2,))],
    )
    return pl.pallas_call(
        functools.partial(_dispatch_kernel, n_exp=n_exp, slots=slots, n_blocks=n_blocks),
        grid_spec=grid_spec,
        out_shape=jax.ShapeDtypeStruct((n_rows, D), F32),
        compiler_params=pltpu.CompilerParams(
            dimension_semantics=("arbitrary",), vmem_limit_bytes=VMEM_LIMIT),
        name="moe_dispatch",
    )(soff, rlen, rdst, tail_dst, tail_len, n_active, slot_lane, xn)


def _moe_kernel(bexp_ref, nact_ref, x_ref, wup_ref, bup_ref, wdn_ref, bdn_ref,
                y_ref, wup_bf, wdn_bf, *, d_exp):
    i = pl.program_id(0)

    @pl.when(i < nact_ref[0])
    def _():
        new_expert = jnp.logical_or(i == 0, bexp_ref[i] != bexp_ref[jnp.maximum(i - 1, 0)])

        @pl.when(new_expert)
        def _():
            wup_bf[...] = wup_ref[0].astype(BF16)
            wdn_bf[...] = wdn_ref[0].astype(BF16)

        x = x_ref[...].astype(BF16)
        h_glu = _dot(x, wup_bf[:, :d_exp]) + bup_ref[0, :, :d_exp]
        h_lin = _dot(x, wup_bf[:, d_exp:]) + bup_ref[0, :, d_exp:]
        h_glu = jnp.minimum(h_glu, SWIGLU_LIMIT)
        h_lin = jnp.clip(h_lin, -SWIGLU_LIMIT, SWIGLU_LIMIT)
        act = h_glu * _sigmoid(SWIGLU_ALPHA * h_glu) * (h_lin + 1.0)
        y_ref[...] = _dot(act.astype(BF16), wdn_bf[...]) + bdn_ref[0]

    @pl.when(i >= nact_ref[0])
    def _():
        y_ref[...] = jnp.zeros_like(y_ref)


def _moe(blk_expert, n_active, xs, w_up, b_up, w_down, b_down):
    n_blocks = blk_expert.shape[0]
    n_exp, D, two_f = w_up.shape
    d_exp = two_f // 2
    blk = lambda i, na: jnp.minimum(i, na[0] - 1)
    grid_spec = pltpu.PrefetchScalarGridSpec(
        num_scalar_prefetch=2,
        grid=(n_blocks,),
        in_specs=[
            pl.BlockSpec((MOE_ROWS, D), lambda i, be, na: (blk(i, na), 0)),
            pl.BlockSpec((1, D, two_f), lambda i, be, na: (be[blk(i, na)], 0, 0)),
            pl.BlockSpec((1, 1, two_f), lambda i, be, na: (be[blk(i, na)], 0, 0)),
            pl.BlockSpec((1, d_exp, D), lambda i, be, na: (be[blk(i, na)], 0, 0)),
            pl.BlockSpec((1, 1, D), lambda i, be, na: (be[blk(i, na)], 0, 0)),
        ],
        out_specs=pl.BlockSpec((MOE_ROWS, D), lambda i, be, na: (i, 0)),
        scratch_shapes=[
            pltpu.VMEM((D, two_f), BF16),
            pltpu.VMEM((d_exp, D), BF16),
        ],
    )
    return pl.pallas_call(
        functools.partial(_moe_kernel, d_exp=d_exp),
        grid_spec=grid_spec,
        out_shape=jax.ShapeDtypeStruct((n_blocks * MOE_ROWS, D), F32),
        compiler_params=pltpu.CompilerParams(
            dimension_semantics=("arbitrary",), vmem_limit_bytes=VMEM_LIMIT),
        name="moe_ffn",
    )(blk_expert, n_active, xs,
      w_up, b_up.reshape(n_exp, 1, two_f), w_down, b_down.reshape(n_exp, 1, D))


def _combine_kernel(soff_ref, rlen_ref, rdst_ref, slot_ref, gate_ref, x2_ref, y_hbm, o_ref,
                    buf, sems, *, n_exp, slots):
    j = pl.program_id(0)
    tm = ROUTE_TILE

    cur = lax.rem(j, 2)

    def runs(action, tile, half):
        for e in range(n_exp):
            r = tile * n_exp + e
            _run_copies(action, rlen_ref[r], y_hbm, rdst_ref[r], buf.at[half], soff_ref[r],
                        sems.at[half])

    @pl.when(j == 0)
    def _():
        buf[...] = jnp.zeros_like(buf)
        runs("start", 0, 0)

    @pl.when(j + 1 < pl.num_programs(0))
    def _():
        runs("start", j + 1, 1 - cur)

    runs("wait", j, cur)

    slot = slot_ref[...]
    gate = gate_ref[...]
    acc = x2_ref[...]
    for sb in range(slots // GROUP):
        lio = lax.broadcasted_iota(jnp.int32, (tm, GROUP), 1) + sb * GROUP
        w = jnp.zeros((tm, GROUP), F32)
        for k in range(TOP_K):
            w = jnp.where(lio == slot[:, k: k + 1], gate[:, k: k + 1], w)
        acc = acc + _dot(w.astype(BF16), buf[cur, sb * GROUP: (sb + 1) * GROUP, :].astype(BF16))
    o_ref[...] = acc


def _combine(soff, rlen, rdst, slot_row, gates_row, x2, y_rows, slots):
    T, D = x2.shape
    tm = ROUTE_TILE
    n_exp = soff.shape[0] // (T // tm)
    grid_spec = pltpu.PrefetchScalarGridSpec(
        num_scalar_prefetch=3,
        grid=(T // tm,),
        in_specs=[
            pl.BlockSpec((tm, TOP_K), lambda i, *_: (i, 0)),
            pl.BlockSpec((tm, TOP_K), lambda i, *_: (i, 0)),
            pl.BlockSpec((tm, D), lambda i, *_: (i, 0)),
            pl.BlockSpec(memory_space=pl.ANY),
        ],
        out_specs=pl.BlockSpec((tm, D), lambda i, *_: (i, 0)),
        scratch_shapes=[pltpu.VMEM((2, slots, D), F32), pltpu.SemaphoreType.DMA((2,))],
    )
    return pl.pallas_call(
        functools.partial(_combine_kernel, n_exp=n_exp, slots=slots),
        grid_spec=grid_spec,
        out_shape=jax.ShapeDtypeStruct((T, D), F32),
        compiler_params=pltpu.CompilerParams(
            dimension_semantics=("arbitrary",), vmem_limit_bytes=VMEM_LIMIT),
        name="moe_combine",
    )(soff, rlen, rdst, slot_row, gates_row, x2, y_rows)


def _block_diag_ones(n, blk):
    i = jnp.arange(n)
    return ((i[:, None] // blk) == (i[None, :] // blk)).astype(BF16)


def _layer(x, norm1_g, w_in, tshift_mu, q_norm_g, k_norm_g, rel_bias, w0, w_decay_up, a0,
           w_aaa_up, w_gate_up, k_k, k_a, r_k, ln_x_g, ln_x_b, w_out, norm2_g, w_router,
           b_router, w_up, b_up, w_down, b_down):
    B, S, D = x.shape
    T = B * S
    att_heads = rel_bias.shape[0]
    att_w = att_heads * HEAD_DIM
    width = w0.shape[0]
    n_exp = w_router.shape[1]
    assert att_w % LANES == 0 and width % GROUP == 0 and S % ATT_QROWS == 0
    assert T % ROUTE_TILE == 0

    x2d = x.reshape(T, D)
    row = lambda a: a.reshape(1, -1).astype(F32)
    bd_att = _block_diag_ones(att_w, HEAD_DIM)
    bd_rwkv = _block_diag_ones(width, HEAD_DIM)

    qkv, pr = _inproj(
        x2d, row(norm1_g), w_in.astype(BF16),
        row(jnp.tile(q_norm_g, att_heads)), row(jnp.tile(k_norm_g, att_heads)), bd_att, att_w)

    att = _attention(qkv, _attention_bias(rel_bias), B, S, att_w)

    lora = jnp.zeros((DECAY_LORA + AAA_LORA, 2 * width), F32)
    lora = lora.at[:DECAY_LORA, :width].set(w_decay_up).at[DECAY_LORA:, width:].set(w_aaa_up)
    rwk = _rwkv(pr, row(tshift_mu), row(w0), row(a0), lora.astype(BF16),
                w_gate_up.astype(BF16), row(k_k), row(k_a), row(r_k), row(ln_x_g),
                row(ln_x_b), bd_rwkv, B, S, width)

    wrt = w_router.T.astype(F32)
    wrt_hi = wrt.astype(BF16)
    wrt_lo = (wrt - wrt_hi.astype(F32)).astype(BF16)
    x2, xn, eidx, rank, gates, counts = _outproj(
        att, rwk, x2d, w_out[:att_w].astype(BF16), w_out[att_w:].astype(BF16), row(norm2_g),
        jnp.stack([wrt_hi, wrt_lo]), b_router.reshape(n_exp, 1).astype(F32), n_exp)

    tm = ROUTE_TILE
    n_tiles = T // tm
    i32 = lambda a: a.astype(jnp.int32)
    run_n = counts.reshape(n_tiles, n_exp, LANES)[:, :, 0]
    run_len = (run_n + RUN_ALIGN - 1) // RUN_ALIGN * RUN_ALIGN
    run_off = jnp.cumsum(run_len, axis=1) - run_len
    exp_rows = jnp.sum(run_len, axis=0)
    exp_pad = (exp_rows + MOE_ROWS - 1) // MOE_ROWS * MOE_ROWS
    exp_end = jnp.cumsum(exp_pad)
    exp_start = exp_end - exp_pad
    run_dst = exp_start[None, :] + jnp.cumsum(run_len, axis=0) - run_len
    eidx_t = eidx.reshape(TOP_K, n_tiles, tm)
    experts = jnp.arange(n_exp, dtype=jnp.int32)[:, None, None, None]
    slot = rank.reshape(TOP_K, n_tiles, tm) + jnp.sum(
        jnp.where(eidx_t[None] == experts, run_off.T[:, None, :, None], 0), axis=0)
    slot_lane = i32(slot.transpose(1, 0, 2))
    slot_row = i32(slot.transpose(1, 2, 0).reshape(T, TOP_K))
    slots = -(-(tm * TOP_K + n_exp * (RUN_ALIGN - 1)) // GROUP) * GROUP
    n_blocks = -(-(T * TOP_K + n_tiles * n_exp * (RUN_ALIGN - 1)) // MOE_ROWS) + n_exp
    blk_start = jnp.arange(n_blocks, dtype=jnp.int32) * MOE_ROWS
    blk_expert = i32(jnp.minimum(
        jnp.sum(blk_start[:, None] >= exp_end[None, :], axis=1), n_exp - 1))
    n_active = i32(exp_end[-1:] // MOE_ROWS)
    soff, rlen, rdst = (i32(a.reshape(-1)) for a in (run_off, run_len, run_dst))

    xs = _dispatch(soff, rlen, rdst, i32(exp_start + exp_rows), i32(exp_pad - exp_rows),
                   n_active, slot_lane, xn, n_blocks, slots)
    y_rows = _moe(blk_expert, n_active, xs, w_up, b_up, w_down, b_down)
    out = _combine(soff, rlen, rdst, slot_row, gates.T, x2, y_rows, slots)
    return out.reshape(B, S, D)


def kernel(x, norm1_g, w_in, tshift_mu, q_norm_g, k_norm_g, rel_bias, w0, w_decay_up, a0,
           w_aaa_up, w_gate_up, k_k, k_a, r_k, ln_x_g, ln_x_b, w_out, norm2_g, w_router,
           b_router, w_up, b_up, w_down, b_down):
    params = (norm1_g, w_in, tshift_mu, q_norm_g, k_norm_g, rel_bias, w0, w_decay_up, a0,
              w_aaa_up, w_gate_up, k_k, k_a, r_k, ln_x_g, ln_x_b, w_out, norm2_g, w_router,
              b_router, w_up, b_up, w_down, b_down)
    for l in range(norm1_g.shape[0]):
        x = _layer(x, *[p[l] for p in params])
    return x
```

```python
import functools
import math

import jax
import jax.numpy as jnp
from jax import lax
from jax.experimental import pallas as pl
from jax.experimental.pallas import tpu as pltpu

F32 = jnp.float32
BF16 = jnp.bfloat16

CHUNK = 64
HEAD_DIM = 64
LEFT_CHUNKS = 8
MAX_LEFT_REL = 128
TOP_K = 4
SWIGLU_ALPHA = 1.702
SWIGLU_LIMIT = 7.0
NORM_EPS = 1e-6
LN_X_EPS = 64e-5
DECAY_LORA = 64
AAA_LORA = 64
GATE_LORA = 128

LANES = 128
GROUP = 256
NEG_BIG = -1e30

VMEM_LIMIT = 56 * 1024 * 1024

ATT_QROWS = 512
ATT_SUB = 256
ATT_KEYS = ATT_SUB + LEFT_CHUNKS * CHUNK
ATT_INTERLEAVE = 4

LOCAL_UNROLL = 2

MOE_ROWS = 512
ROUTE_TILE = 512
RUN_ALIGN = 8
RUN_SIZES = tuple(ROUTE_TILE >> s for s in range((ROUTE_TILE // RUN_ALIGN).bit_length()))
assert MOE_ROWS <= ROUTE_TILE and RUN_SIZES[-1] == RUN_ALIGN


def _dot(a, b):
    return jnp.dot(a, b, preferred_element_type=F32)


def _dot_nt(a, b):
    return lax.dot_general(a, b, (((1,), (1,)), ((), ())), preferred_element_type=F32)


def _head_sums(a, block_ones):
    return _dot(a.astype(BF16), block_ones)


def _sigmoid(x):
    return 0.5 * jnp.tanh(0.5 * x) + 0.5


def _inproj_kernel(x_ref, g_ref, w_ref, qg_ref, kg_ref, bd_ref, qkv_ref, pr_ref, *, att_w):
    x = x_ref[...]
    ms = jnp.mean(x * x, axis=-1, keepdims=True)
    h = (x * lax.rsqrt(ms + NORM_EPS) * g_ref[...]).astype(BF16)
    qkv = _dot(h, w_ref[:, : 3 * att_w])
    bd = bd_ref[...]

    def head_norm(t, g):
        ss = _head_sums(t * t, bd)
        return t * lax.rsqrt(ss * (1.0 / HEAD_DIM) + NORM_EPS) * g

    q = head_norm(qkv[:, :att_w], qg_ref[...]) * (HEAD_DIM ** -0.5)
    k = head_norm(qkv[:, att_w: 2 * att_w], kg_ref[...])
    qkv_ref[:, :att_w] = q.astype(BF16)
    qkv_ref[:, att_w: 2 * att_w] = k.astype(BF16)
    qkv_ref[:, 2 * att_w:] = qkv[:, 2 * att_w:].astype(BF16)
    pr_ref[...] = _dot(h, w_ref[:, 3 * att_w:]).astype(BF16)


def _inproj(x2d, g1, w_in_bf, qg, kg, bd, att_w, tm=512):
    T, D = x2d.shape
    in_cols = w_in_bf.shape[1]
    r_cols = in_cols - 3 * att_w
    const = lambda i: (0, 0)
    return pl.pallas_call(
        functools.partial(_inproj_kernel, att_w=att_w),
        grid=(T // tm,),
        in_specs=[
            pl.BlockSpec((tm, D), lambda i: (i, 0)),
            pl.BlockSpec((1, D), const),
            pl.BlockSpec((D, in_cols), const),
            pl.BlockSpec((1, att_w), const),
            pl.BlockSpec((1, att_w), const),
            pl.BlockSpec((att_w, att_w), const),
        ],
        out_specs=[
            pl.BlockSpec((tm, 3 * att_w), lambda i: (i, 0)),
            pl.BlockSpec((tm, r_cols), lambda i: (i, 0)),
        ],
        out_shape=[
            jax.ShapeDtypeStruct((T, 3 * att_w), BF16),
            jax.ShapeDtypeStruct((T, r_cols), BF16),
        ],
        compiler_params=pltpu.CompilerParams(
            dimension_semantics=("arbitrary",), vmem_limit_bytes=VMEM_LIMIT),
        name="inproj",
    )(x2d, g1, w_in_bf, qg, kg, bd)


def _attn_kernel(q_ref, kp_ref, kc_ref, vp_ref, vc_ref, bias_ref, o_ref, kbuf, vbuf):
    i = pl.program_id(2)
    kbuf[:ATT_QROWS, :] = kp_ref[...]
    kbuf[ATT_QROWS:, :] = kc_ref[...]
    lane_head = lax.broadcasted_iota(jnp.int32, (1, LANES), 1) >> (HEAD_DIM.bit_length() - 1)
    one = jnp.ones((), BF16)
    for h in range(2):
        vbuf[h, :ATT_QROWS, :] = jnp.where(lane_head == h, vp_ref[...], one)
        vbuf[h, ATT_QROWS:, :] = jnp.where(lane_head == h, vc_ref[...], one)
    tiles = range(ATT_QROWS // ATT_SUB)
    all_units = [(qs * ATT_SUB, h * HEAD_DIM, h, qs) for qs in tiles for h in range(2)]
    ext = []
    for u0 in range(0, len(all_units), ATT_INTERLEAVE):
        units = all_units[u0: u0 + ATT_INTERLEAVE]
        s = [_dot_nt(q_ref[r0: r0 + ATT_SUB, lo: lo + HEAD_DIM],
                     kbuf[r0: r0 + ATT_KEYS, lo: lo + HEAD_DIM])
             + bias_ref[h, jnp.where(i == 0, 1 + qs, 0)] for (r0, lo, h, qs) in units]
        m = [jnp.max(x, axis=-1, keepdims=True) for x in s]
        p = [jnp.exp(x - mx).astype(BF16) for x, mx in zip(s, m)]
        ext += [_dot(x, vbuf[h, r0: r0 + ATT_KEYS, :]) for x, (r0, _, h, _) in zip(p, units)]
    for qs in tiles:
        e0, e1 = ext[2 * qs], ext[2 * qs + 1]
        o_ref[qs * ATT_SUB: (qs + 1) * ATT_SUB, :] = jnp.where(
            lane_head == 0, e0 / pltpu.roll(e0, HEAD_DIM, axis=1),
            e1 / pltpu.roll(e1, HEAD_DIM, axis=1)).astype(BF16)


def _attention(qkv, bias, B, S, att_w):
    T = B * S
    n_i = S // ATT_QROWS
    n_hp = att_w // LANES
    heads_per = LANES // HEAD_DIM
    assert heads_per == 2

    def cur(col0):
        return lambda b, hp, i: (b * n_i + i, col0 + hp)

    def prev(col0):
        return lambda b, hp, i: (b * n_i + jnp.maximum(i - 1, 0), col0 + hp)

    blk = (ATT_QROWS, LANES)
    return pl.pallas_call(
        _attn_kernel,
        grid=(B, n_hp, n_i),
        in_specs=[
            pl.BlockSpec(blk, cur(0)),
            pl.BlockSpec(blk, prev(n_hp)),
            pl.BlockSpec(blk, cur(n_hp)),
            pl.BlockSpec(blk, prev(2 * n_hp)),
            pl.BlockSpec(blk, cur(2 * n_hp)),
            pl.BlockSpec((heads_per,) + bias.shape[1:], lambda b, hp, i: (hp, 0, 0, 0)),
        ],
        out_specs=pl.BlockSpec(blk, lambda b, hp, i: (b * n_i + i, hp)),
        out_shape=jax.ShapeDtypeStruct((T, att_w), BF16),
        scratch_shapes=[
            pltpu.VMEM((2 * ATT_QROWS, LANES), BF16),
            pltpu.VMEM((2, 2 * ATT_QROWS, LANES), BF16),
        ],
        compiler_params=pltpu.CompilerParams(
            dimension_semantics=("arbitrary", "arbitrary", "arbitrary"),
            vmem_limit_bytes=VMEM_LIMIT),
        name="band_attention",
    )(qkv, qkv, qkv, qkv, qkv, bias)


def _attention_bias(rel_bias):
    H = rel_bias.shape[0]
    table = rel_bias.astype(F32)
    far = LEFT_CHUNKS * CHUNK + ATT_SUB - 1
    n_left = far - MAX_LEFT_REL + 1
    n_ext = ATT_SUB + ATT_KEYS - 1
    ext = jnp.concatenate([
        jnp.broadcast_to(table[:, :1], (H, n_left)), table[:, 1:],
        jnp.broadcast_to(table[:, -1:], (H, n_ext - n_left - (table.shape[1] - 1)))], axis=1)
    flat = jnp.tile(ext, (1, ATT_SUB + 1))[:, ATT_SUB - 1: ATT_SUB - 1 + ATT_SUB * (n_ext - 1)]
    bias = flat.reshape(H, ATT_SUB, n_ext - 1)[:, :, :ATT_KEYS]
    r = jnp.arange(ATT_SUB)[:, None]
    c = jnp.arange(ATT_KEYS)[None, :]
    band = c // CHUNK - r // CHUNK
    ok = (band >= 0) & (band <= LEFT_CHUNKS)
    variants = [ok] + [ok & (c >= ATT_QROWS - qs * ATT_SUB) for qs in range(ATT_QROWS // ATT_SUB)]
    return jnp.where(jnp.stack(variants)[None], bias[:, None], NEG_BIG)


def _rwkv_kernel(p_ref, pprev_ref, mu_ref, w0_ref, a0_ref, lora_ref, wg_ref, kk_ref, ka_ref,
                 rk_ref, lng_ref, lnb_ref, bd_ref, tri_ref, o_ref,
                 rt_s, at_s, bt_s, kt_s, bh_s, kh_s, v_s, wl_s, y_s, z_s, g_s, y0_s, m_s, n_s,
                 *, width, tm):
    i = pl.program_id(1)
    n_groups = width // GROUP
    heads_per_group = GROUP // HEAD_DIM

    @pl.when(i == 0)
    def _():
        z_s[...] = jnp.zeros_like(z_s)

    p = p_ref[...].astype(F32)
    first = jnp.where(i > 0, 1.0, 0.0)
    prev_row = pprev_ref[7:8, :].astype(F32) * first
    row = lax.broadcasted_iota(jnp.int32, (tm, 1), 0)
    prev = jnp.where(row == 0, prev_row, pltpu.roll(p, 1, axis=0))
    ps = p + (prev - p) * mu_ref[...]

    r = ps[:, :width]
    k = ps[:, width: 2 * width]
    v = ps[:, 2 * width: 3 * width]
    wa = ps[:, 3 * width: 3 * width + DECAY_LORA + AAA_LORA]
    gd = ps[:, 3 * width + DECAY_LORA + AAA_LORA:]
    lane = lax.broadcasted_iota(jnp.int32, (1, DECAY_LORA + AAA_LORA), 1)
    wa = jnp.where(lane < DECAY_LORA, jnp.tanh(wa), wa)
    lw = _dot(wa.astype(BF16), lora_ref[...])
    logw = -math.exp(-0.5) * _sigmoid(w0_ref[...] + lw[:, :width])
    eta = _sigmoid(a0_ref[...] + lw[:, width:])
    g = _dot(_sigmoid(gd).astype(BF16), wg_ref[...])

    bd = bd_ref[...]
    kk = k * kk_ref[...]
    kk = kk * lax.rsqrt(jnp.maximum(_head_sums(kk * kk, bd), 1e-24))
    km = k * (1.0 + (eta - 1.0) * ka_ref[...])
    bb = kk * eta

    tri = tri_ref[...]
    cs_chunks = [_split_dot_left(tri, logw[c * CHUNK: (c + 1) * CHUNK]) for c in range(tm // CHUNK)]
    cs = jnp.concatenate(cs_chunks, axis=0)
    tot = jnp.concatenate(
        [jnp.broadcast_to(x[CHUNK - 1:], (CHUNK, width)) for x in cs_chunks], axis=0)
    e_pos = jnp.exp(cs)
    e_neg = jnp.exp(-cs)
    e_tail = jnp.exp(tot - cs)
    rt_s[...] = r * e_pos
    at_s[...] = -kk * jnp.exp(cs - logw)
    bt_s[...] = bb * e_neg
    kt_s[...] = km * e_neg
    bh_s[...] = bb * e_tail
    kh_s[...] = km * e_tail
    v_s[...] = v
    wl_s[...] = jnp.exp(tot)

    rr = lax.broadcasted_iota(jnp.int32, (GROUP, GROUP), 0)
    cc = lax.broadcasted_iota(jnp.int32, (GROUP, GROUP), 1)
    shift = CHUNK.bit_length() - 1
    same = (rr >> shift) == (cc >> shift)
    strict = same & (rr > cc)
    incl = same & (rr >= cc)
    eye = jnp.where(rr == cc, 1.0, 0.0)
    lane_head = lax.broadcasted_iota(jnp.int32, (1, GROUP), 1) >> shift

    def stack_masked(x):
        return jnp.concatenate(
            [jnp.where(lane_head == h, x, 0.0) for h in range(heads_per_group)], axis=0)

    def stack_plain(x):
        return jnp.concatenate([x] * heads_per_group, axis=0)

    def bdot(a, b):
        return _dot(a.astype(BF16), b.astype(BF16))

    def bdot_nt(a, b):
        return _dot_nt(a.astype(BF16), b.astype(BF16))

    def local_body(it, carry):
        units = []
        for u in range(LOCAL_UNROLL):
            c = it * LOCAL_UNROLL + u
            rows = pl.ds(pl.multiple_of(c * CHUNK, CHUNK), CHUNK)
            units += [(c, gi, rows, slice(gi * GROUP, (gi + 1) * GROUP)) for gi in range(n_groups)]

        def load(ref, stack):
            return [stack(ref[rows, cols]) for (_, _, rows, cols) in units]

        def each(f, *lists):
            return [f(*args) for args in zip(*lists)]

        b16 = lambda xs: [x.astype(BF16) for x in xs]
        rt4 = load(rt_s, stack_masked)
        at4, rt4b, v4 = b16(load(at_s, stack_masked)), b16(rt4), load(v_s, stack_masked)
        v4b, bh4 = b16(v4), b16(load(bh_s, stack_masked))
        kh4 = b16(load(kh_s, stack_masked))
        bt_rep, kt_rep = b16(load(bt_s, stack_plain)), b16(load(kt_s, stack_plain))
        a_ab = each(lambda a, b: jnp.where(strict, _dot_nt(a, b), 0.0), at4, bt_rep)
        a_ak = b16(each(lambda a, b: jnp.where(strict, _dot_nt(a, b), 0.0), at4, kt_rep))
        a_rb = b16(each(lambda a, b: jnp.where(incl, _dot_nt(a, b), 0.0), rt4b, bt_rep))
        a_rk = b16(each(lambda a, b: jnp.where(incl, _dot_nt(a, b), 0.0), rt4b, kt_rep))
        tinv = [eye + a for a in a_ab]
        pw = b16(a_ab)
        span = 2
        while span < CHUNK:
            pw = b16(each(_dot, pw, pw))
            tinv = each(lambda t, p: t + _dot(p, t.astype(BF16)), tinv, pw)
            span *= 2
        tinv = b16(tinv)
        p4 = each(_dot, tinv, at4)
        q4 = each(_dot, tinv, b16(each(_dot, a_ak, v4b)))
        p4b, q4b = b16(p4), b16(q4)
        g4 = each(lambda r, a, p: r + _dot(a, p), rt4, a_rb, p4b)
        y0 = each(lambda a, q, ak, v: _dot(a, q) + _dot(ak, v), a_rb, q4b, a_rk, v4b)
        m4 = each(lambda p, b: _dot(p.T.astype(BF16), b), p4, bh4)
        n4 = each(lambda q, b, v, k: _dot(q.T.astype(BF16), b) + _dot(v.T.astype(BF16), k),
                  q4, bh4, v4, kh4)
        for (c, gi, _, _), g, y, m, n in zip(units, g4, y0, m4, n4):
            g_s[c, gi] = g.astype(BF16)
            y0_s[c, gi] = y
            m_s[c, gi] = m.astype(BF16)
            n_s[c, gi] = n
        return carry

    lax.fori_loop(0, tm // CHUNK // LOCAL_UNROLL, local_body, 0)

    def carry_body(c, carry):
        r0 = pl.multiple_of(c * CHUNK, CHUNK)
        rows = pl.ds(r0, CHUNK)
        for gi in range(n_groups):
            cols = slice(gi * GROUP, (gi + 1) * GROUP)
            zt = z_s[gi]
            zb = zt.astype(BF16)
            y4 = _dot_nt(g_s[c, gi], zb) + y0_s[c, gi]
            z_s[gi] = zt * wl_s[pl.ds(r0, 1), cols] + _dot(zb, m_s[c, gi]) + n_s[c, gi]
            ytok = y4[:CHUNK]
            for h in range(1, heads_per_group):
                ytok = ytok + y4[h * CHUNK: (h + 1) * CHUNK]
            y_s[rows, cols] = ytok
        return carry

    lax.fori_loop(0, tm // CHUNK, carry_body, 0, unroll=True)

    y = y_s[...]
    inv_n = 1.0 / HEAD_DIM
    mean = _head_sums(y, bd) * inv_n
    d = y - mean
    var = _head_sums(d * d, bd) * inv_n
    yn = d * lax.rsqrt(var + LN_X_EPS) * lng_ref[...] + lnb_ref[...]
    bonus = _head_sums(r * km * rk_ref[...], bd) * v
    o_ref[...] = ((yn + bonus) * g).astype(BF16)


def _split_dot_left(a_exact, b, terms=2):
    acc = None
    rem = b
    for _ in range(terms):
        part = rem.astype(BF16)
        rem = rem - part.astype(F32)
        d = _dot(a_exact, part)
        acc = d if acc is None else acc + d
    return acc


def _rwkv(pr, mu, w0, a0, lora, wg, k_k, k_a, r_k, ln_g, ln_b, bd, B, S, width, tm=512):
    T = B * S
    n_t = S // tm
    cols = pr.shape[1]
    ti = jnp.arange(CHUNK)
    tri = (ti[:, None] >= ti[None, :]).astype(BF16)
    const = lambda b, i: (0, 0)
    vec = lambda n: pl.BlockSpec((1, n), const)
    scr = lambda: pltpu.VMEM((tm, width), F32)
    return pl.pallas_call(
        functools.partial(_rwkv_kernel, width=width, tm=tm),
        grid=(B, n_t),
        in_specs=[
            pl.BlockSpec((tm, cols), lambda b, i: (b * n_t + i, 0)),
            pl.BlockSpec((8, cols), lambda b, i: (jnp.maximum((b * n_t + i) * (tm // 8) - 1, 0), 0)),
            vec(cols), vec(width), vec(width),
            pl.BlockSpec(lora.shape, const),
            pl.BlockSpec(wg.shape, const),
            vec(width), vec(width), vec(width), vec(width), vec(width),
            pl.BlockSpec((width, width), const),
            pl.BlockSpec((CHUNK, CHUNK), const),
        ],
        out_specs=pl.BlockSpec((tm, width), lambda b, i: (b * n_t + i, 0)),
        out_shape=jax.ShapeDtypeStruct((T, width), BF16),
        scratch_shapes=[scr() for _ in range(9)]
        + [pltpu.VMEM((width // GROUP, GROUP, GROUP), F32)]
        + [pltpu.VMEM((tm // CHUNK, width // GROUP, GROUP, GROUP), dt)
           for dt in (BF16, F32, BF16, F32)],
        compiler_params=pltpu.CompilerParams(
            dimension_semantics=("arbitrary", "arbitrary"), vmem_limit_bytes=VMEM_LIMIT),
        name="rwkv7",
    )(pr, pr, mu, w0, a0, lora, wg, k_k, k_a, r_k, ln_g, ln_b, bd, tri)


def _outproj_kernel(att_ref, rwk_ref, x_ref, wa_ref, wr_ref, g_ref, wrt_ref, br_ref, upper_ref,
                    x2_ref, xn_ref, eidx_ref, rank_ref, gate_ref, cnt_ref, *, n_exp, tm):
    x2 = x_ref[...] + _dot(att_ref[...], wa_ref[...]) + _dot(rwk_ref[...], wr_ref[...])
    x2_ref[...] = x2
    ms = jnp.mean(x2 * x2, axis=-1, keepdims=True)
    xn = x2 * lax.rsqrt(ms + NORM_EPS) * g_ref[...]
    xn_ref[...] = xn.astype(BF16)

    x_hi = xn.astype(BF16)
    x_lo = (xn - x_hi.astype(F32)).astype(BF16)
    logits = (_dot_nt(wrt_ref[0], x_hi) + _dot_nt(wrt_ref[0], x_lo)
              + _dot_nt(wrt_ref[1], x_hi)) + br_ref[...]

    eio = lax.broadcasted_iota(jnp.int32, (n_exp, tm), 0)
    work = logits
    vals, onehots = [], []
    for k in range(TOP_K):
        m = jnp.max(work, axis=0, keepdims=True)
        idx = jnp.min(jnp.where(work == m, eio, n_exp), axis=0, keepdims=True)
        oh = eio == idx
        vals.append(m)
        onehots.append(oh)
        eidx_ref[k: k + 1, :] = idx
        work = jnp.where(oh, -jnp.inf, work)

    exps = [jnp.exp(vk - vals[0]) for vk in vals]
    denom = exps[0]
    for e in exps[1:]:
        denom = denom + e
    for k in range(TOP_K):
        gate_ref[k: k + 1, :] = exps[k] / denom

    sel = onehots[0]
    for oh in onehots[1:]:
        sel = sel | oh
    sel_f = jnp.where(sel, 1.0, 0.0)
    before = _dot(sel_f.astype(BF16), upper_ref[...])
    for k in range(TOP_K):
        rank_ref[k: k + 1, :] = jnp.sum(
            jnp.where(onehots[k], before, 0.0), axis=0, keepdims=True).astype(jnp.int32)
    cnt_ref[...] = jnp.broadcast_to(
        jnp.sum(sel_f, axis=1, keepdims=True), cnt_ref.shape).astype(jnp.int32)


def _outproj(att, rwk, x2d, w_att, w_rwk, g2, wrt, br, n_exp):
    T, D = x2d.shape
    tm = ROUTE_TILE
    upper = (jnp.arange(tm)[:, None] < jnp.arange(tm)[None, :]).astype(BF16)
    const = lambda i: (0, 0)
    rows = lambda n: pl.BlockSpec((tm, n), lambda i: (i, 0))
    lanes = pl.BlockSpec((TOP_K, tm), lambda i: (0, i))
    return pl.pallas_call(
        functools.partial(_outproj_kernel, n_exp=n_exp, tm=tm),
        grid=(T // tm,),
        in_specs=[
            rows(att.shape[1]), rows(rwk.shape[1]), rows(D),
            pl.BlockSpec(w_att.shape, const), pl.BlockSpec(w_rwk.shape, const),
            pl.BlockSpec((1, D), const),
            pl.BlockSpec(wrt.shape, lambda i: (0, 0, 0)),
            pl.BlockSpec((n_exp, 1), const),
            pl.BlockSpec((tm, tm), const),
        ],
        out_specs=[rows(D), rows(D), lanes, lanes, lanes,
                   pl.BlockSpec((n_exp, LANES), lambda i: (i, 0))],
        out_shape=[
            jax.ShapeDtypeStruct((T, D), F32),
            jax.ShapeDtypeStruct((T, D), BF16),
            jax.ShapeDtypeStruct((TOP_K, T), jnp.int32),
            jax.ShapeDtypeStruct((TOP_K, T), jnp.int32),
            jax.ShapeDtypeStruct((TOP_K, T), F32),
            jax.ShapeDtypeStruct((T // tm * n_exp, LANES), jnp.int32),
        ],
        compiler_params=pltpu.CompilerParams(
            dimension_semantics=("arbitrary",), vmem_limit_bytes=VMEM_LIMIT),
        name="outproj_router",
    )(att, rwk, x2d, w_att, w_rwk, g2, wrt, br, upper)


def _run_copies(action, length, src, src_off, dst, dst_off, sem):
    for size in RUN_SIZES:
        start = length & ~(2 * size - 1)

        @pl.when((length & size) != 0)
        def _():
            cp = pltpu.make_async_copy(
                src.at[pl.ds(pl.multiple_of(src_off + start, RUN_ALIGN), size)],
                dst.at[pl.ds(pl.multiple_of(dst_off + start, RUN_ALIGN), size)], sem)
            cp.start() if action == "start" else cp.wait()


def _dispatch_kernel(soff_ref, rlen_ref, rdst_ref, tail_dst_ref, tail_len_ref, nact_ref,
                     slot_ref, xn_ref, xs_hbm, buf, zbuf, sem, run_sems, *, n_exp, slots, n_blocks):
    j = pl.program_id(0)
    tm = ROUTE_TILE

    @pl.when(j == 0)
    def _():
        zbuf[...] = jnp.zeros_like(zbuf)

        def unused_block(action):
            def body(b, c):
                cp = pltpu.make_async_copy(
                    zbuf, xs_hbm.at[pl.ds(pl.multiple_of(b * MOE_ROWS, MOE_ROWS), MOE_ROWS)], sem)
                cp.start() if action == "start" else cp.wait()
                return c
            return body

        for action in ("start", "wait"):
            for e in range(n_exp):
                _run_copies(action, tail_len_ref[e], zbuf, 0, xs_hbm, tail_dst_ref[e], sem)
            lax.fori_loop(nact_ref[0], n_blocks, unused_block(action), 0)

    cur = lax.rem(j, 2)
    x = xn_ref[...]
    for sb in range(slots // GROUP):
        sio = lax.broadcasted_iota(jnp.int32, (GROUP, tm), 0) + sb * GROUP
        p = jnp.zeros((GROUP, tm), F32)
        for k in range(TOP_K):
            p = jnp.where(sio == slot_ref[0, k: k + 1, :], 1.0, p)
        buf[cur, sb * GROUP: (sb + 1) * GROUP, :] = _dot(p.astype(BF16), x)

    def runs(action, tile, half):
        for e in range(n_exp):
            r = tile * n_exp + e
            _run_copies(action, rlen_ref[r], buf.at[half], soff_ref[r], xs_hbm, rdst_ref[r],
                        run_sems.at[half])

    runs("start", j, cur)

    @pl.when(j > 0)
    def _():
        runs("wait", j - 1, 1 - cur)

    @pl.when(j == pl.num_programs(0) - 1)
    def _():
        runs("wait", j, cur)


def _dispatch(soff, rlen, rdst, tail_dst, tail_len, n_active, slot_lane, xn, n_blocks, slots):
    T, D = xn.shape
    tm = ROUTE_TILE
    n_exp = tail_dst.shape[0]
    n_rows = n_blocks * MOE_ROWS
    grid_spec = pltpu.PrefetchScalarGridSpec(
        num_scalar_prefetch=6,
        grid=(T // tm,),
        in_specs=[
            pl.BlockSpec((1, TOP_K, tm), lambda i, *_: (i, 0, 0)),
            pl.BlockSpec((tm, D), lambda i, *_: (i, 0)),
        ],
        out_specs=pl.BlockSpec(memory_space=pl.ANY),
        scratch_shapes=[pltpu.VMEM((2, slots, D), F32), pltpu.VMEM((MOE_ROWS, D), F32),
                        pltpu.SemaphoreType.DMA, pltpu.SemaphoreType.DMA((2,))],
    )
    return pl.pallas_call(
        functools.partial(_dispatch_kernel, n_exp=n_exp, slots=slots, n_blocks=n_blocks),
        grid_spec=grid_spec,
        out_shape=jax.ShapeDtypeStruct((n_rows, D), F32),
        compiler_params=pltpu.CompilerParams(
            dimension_semantics=("arbitrary",), vmem_limit_bytes=VMEM_LIMIT),
        name="moe_dispatch",
    )(soff, rlen, rdst, tail_dst, tail_len, n_active, slot_lane, xn)


def _moe_kernel(bexp_ref, nact_ref, x_ref, wup_ref, bup_ref, wdn_ref, bdn_ref,
                y_ref, wup_bf, wdn_bf, *, d_exp):
    i = pl.program_id(0)

    @pl.when(i < nact_ref[0])
    def _():
        new_expert = jnp.logical_or(i == 0, bexp_ref[i] != bexp_ref[jnp.maximum(i - 1, 0)])

        @pl.when(new_expert)
        def _():
            wup_bf[...] = wup_ref[0].astype(BF16)
            wdn_bf[...] = wdn_ref[0].astype(BF16)

        x = x_ref[...].astype(BF16)
        h_glu = _dot(x, wup_bf[:, :d_exp]) + bup_ref[0, :, :d_exp]
        h_lin = _dot(x, wup_bf[:, d_exp:]) + bup_ref[0, :, d_exp:]
        h_glu = jnp.minimum(h_glu, SWIGLU_LIMIT)
        h_lin = jnp.clip(h_lin, -SWIGLU_LIMIT, SWIGLU_LIMIT)
        act = h_glu * _sigmoid(SWIGLU_ALPHA * h_glu) * (h_lin + 1.0)
        y_ref[...] = _dot(act.astype(BF16), wdn_bf[...]) + bdn_ref[0]

    @pl.when(i >= nact_ref[0])
    def _():
        y_ref[...] = jnp.zeros_like(y_ref)


def _moe(blk_expert, n_active, xs, w_up, b_up, w_down, b_down):
    n_blocks = blk_expert.shape[0]
    n_exp, D, two_f = w_up.shape
    d_exp = two_f // 2
    blk = lambda i, na: jnp.minimum(i, na[0] - 1)
    grid_spec = pltpu.PrefetchScalarGridSpec(
        num_scalar_prefetch=2,
        grid=(n_blocks,),
        in_specs=[
            pl.BlockSpec((MOE_ROWS, D), lambda i, be, na: (blk(i, na), 0)),
            pl.BlockSpec((1, D, two_f), lambda i, be, na: (be[blk(i, na)], 0, 0)),
            pl.BlockSpec((1, 1, two_f), lambda i, be, na: (be[blk(i, na)], 0, 0)),
            pl.BlockSpec((1, d_exp, D), lambda i, be, na: (be[blk(i, na)], 0, 0)),
            pl.BlockSpec((1, 1, D), lambda i, be, na: (be[blk(i, na)], 0, 0)),
        ],
        out_specs=pl.BlockSpec((MOE_ROWS, D), lambda i, be, na: (i, 0)),
        scratch_shapes=[
            pltpu.VMEM((D, two_f), BF16),
            pltpu.VMEM((d_exp, D), BF16),
        ],
    )
    return pl.pallas_call(
        functools.partial(_moe_kernel, d_exp=d_exp),
        grid_spec=grid_spec,
        out_shape=jax.ShapeDtypeStruct((n_blocks * MOE_ROWS, D), F32),
        compiler_params=pltpu.CompilerParams(
            dimension_semantics=("arbitrary",), vmem_limit_bytes=VMEM_LIMIT),
        name="moe_ffn",
    )(blk_expert, n_active, xs,
      w_up, b_up.reshape(n_exp, 1, two_f), w_down, b_down.reshape(n_exp, 1, D))


def _combine_kernel(soff_ref, rlen_ref, rdst_ref, slot_ref, gate_ref, x2_ref, y_hbm, o_ref,
                    buf, sems, *, n_exp, slots):
    j = pl.program_id(0)
    tm = ROUTE_TILE

    cur = lax.rem(j, 2)

    def runs(action, tile, half):
        for e in range(n_exp):
            r = tile * n_exp + e
            _run_copies(action, rlen_ref[r], y_hbm, rdst_ref[r], buf.at[half], soff_ref[r],
                        sems.at[half])

    @pl.when(j == 0)
    def _():
        buf[...] = jnp.zeros_like(buf)
        runs("start", 0, 0)

    @pl.when(j + 1 < pl.num_programs(0))
    def _():
        runs("start", j + 1, 1 - cur)

    runs("wait", j, cur)

    slot = slot_ref[...]
    gate = gate_ref[...]
    acc = x2_ref[...]
    for sb in range(slots // GROUP):
        lio = lax.broadcasted_iota(jnp.int32, (tm, GROUP), 1) + sb * GROUP
        w = jnp.zeros((tm, GROUP), F32)
        for k in range(TOP_K):
            w = jnp.where(lio == slot[:, k: k + 1], gate[:, k: k + 1], w)
        acc = acc + _dot(w.astype(BF16), buf[cur, sb * GROUP: (sb + 1) * GROUP, :].astype(BF16))
    o_ref[...] = acc


def _combine(soff, rlen, rdst, slot_row, gates_row, x2, y_rows, slots):
    T, D = x2.shape
    tm = ROUTE_TILE
    n_exp = soff.shape[0] // (T // tm)
    grid_spec = pltpu.PrefetchScalarGridSpec(
        num_scalar_prefetch=3,
        grid=(T // tm,),
        in_specs=[
            pl.BlockSpec((tm, TOP_K), lambda i, *_: (i, 0)),
            pl.BlockSpec((tm, TOP_K), lambda i, *_: (i, 0)),
            pl.BlockSpec((tm, D), lambda i, *_: (i, 0)),
            pl.BlockSpec(memory_space=pl.ANY),
        ],
        out_specs=pl.BlockSpec((tm, D), lambda i, *_: (i, 0)),
        scratch_shapes=[pltpu.VMEM((2, slots, D), F32), pltpu.SemaphoreType.DMA((2,))],
    )
    return pl.pallas_call(
        functools.partial(_combine_kernel, n_exp=n_exp, slots=slots),
        grid_spec=grid_spec,
        out_shape=jax.ShapeDtypeStruct((T, D), F32),
        compiler_params=pltpu.CompilerParams(
            dimension_semantics=("arbitrary",), vmem_limit_bytes=VMEM_LIMIT),
        name="moe_combine",
    )(soff, rlen, rdst, slot_row, gates_row, x2, y_rows)


def _block_diag_ones(n, blk):
    i = jnp.arange(n)
    return ((i[:, None] // blk) == (i[None, :] // blk)).astype(BF16)


def _layer(x, norm1_g, w_in, tshift_mu, q_norm_g, k_norm_g, rel_bias, w0, w_decay_up, a0,
           w_aaa_up, w_gate_up, k_k, k_a, r_k, ln_x_g, ln_x_b, w_out, norm2_g, w_router,
           b_router, w_up, b_up, w_down, b_down):
    B, S, D = x.shape
    T = B * S
    att_heads = rel_bias.shape[0]
    att_w = att_heads * HEAD_DIM
    width = w0.shape[0]
    n_exp = w_router.shape[1]
    assert att_w % LANES == 0 and width % GROUP == 0 and S % ATT_QROWS == 0
    assert T % ROUTE_TILE == 0

    x2d = x.reshape(T, D)
    row = lambda a: a.reshape(1, -1).astype(F32)
    bd_att = _block_diag_ones(att_w, HEAD_DIM)
    bd_rwkv = _block_diag_ones(width, HEAD_DIM)

    qkv, pr = _inproj(
        x2d, row(norm1_g), w_in.astype(BF16),
        row(jnp.tile(q_norm_g, att_heads)), row(jnp.tile(k_norm_g, att_heads)), bd_att, att_w)

    att = _attention(qkv, _attention_bias(rel_bias), B, S, att_w)

    lora = jnp.zeros((DECAY_LORA + AAA_LORA, 2 * width), F32)
    lora = lora.at[:DECAY_LORA, :width].set(w_decay_up).at[DECAY_LORA:, width:].set(w_aaa_up)
    rwk = _rwkv(pr, row(tshift_mu), row(w0), row(a0), lora.astype(BF16),
                w_gate_up.astype(BF16), row(k_k), row(k_a), row(r_k), row(ln_x_g),
                row(ln_x_b), bd_rwkv, B, S, width)

    wrt = w_router.T.astype(F32)
    wrt_hi = wrt.astype(BF16)
    wrt_lo = (wrt - wrt_hi.astype(F32)).astype(BF16)
    x2, xn, eidx, rank, gates, counts = _outproj(
        att, rwk, x2d, w_out[:att_w].astype(BF16), w_out[att_w:].astype(BF16), row(norm2_g),
        jnp.stack([wrt_hi, wrt_lo]), b_router.reshape(n_exp, 1).astype(F32), n_exp)

    tm = ROUTE_TILE
    n_tiles = T // tm
    i32 = lambda a: a.astype(jnp.int32)
    run_n = counts.reshape(n_tiles, n_exp, LANES)[:, :, 0]
    run_len = (run_n + RUN_ALIGN - 1) // RUN_ALIGN * RUN_ALIGN
    run_off = jnp.cumsum(run_len, axis=1) - run_len
    exp_rows = jnp.sum(run_len, axis=0)
    exp_pad = (exp_rows + MOE_ROWS - 1) // MOE_ROWS * MOE_ROWS
    exp_end = jnp.cumsum(exp_pad)
    exp_start = exp_end - exp_pad
    run_dst = exp_start[None, :] + jnp.cumsum(run_len, axis=0) - run_len
    eidx_t = eidx.reshape(TOP_K, n_tiles, tm)
    experts = jnp.arange(n_exp, dtype=jnp.int32)[:, None, None, None]
    slot = rank.reshape(TOP_K, n_tiles, tm) + jnp.sum(
        jnp.where(eidx_t[None] == experts, run_off.T[:, None, :, None], 0), axis=0)
    slot_lane = i32(slot.transpose(1, 0, 2))
    slot_row = i32(slot.transpose(1, 2, 0).reshape(T, TOP_K))
    slots = -(-(tm * TOP_K + n_exp * (RUN_ALIGN - 1)) // GROUP) * GROUP
    n_blocks = -(-(T * TOP_K + n_tiles * n_exp * (RUN_ALIGN - 1)) // MOE_ROWS) + n_exp
    blk_start = jnp.arange(n_blocks, dtype=jnp.int32) * MOE_ROWS
    blk_expert = i32(jnp.minimum(
        jnp.sum(blk_start[:, None] >= exp_end[None, :], axis=1), n_exp - 1))
    n_active = i32(exp_end[-1:] // MOE_ROWS)
    soff, rlen, rdst = (i32(a.reshape(-1)) for a in (run_off, run_len, run_dst))

    xs = _dispatch(soff, rlen, rdst, i32(exp_start + exp_rows), i32(exp_pad - exp_rows),
                   n_active, slot_lane, xn, n_blocks, slots)
    y_rows = _moe(blk_expert, n_active, xs, w_up, b_up, w_down, b_down)
    out = _combine(soff, rlen, rdst, slot_row, gates.T, x2, y_rows, slots)
    return out.reshape(B, S, D)


def kernel(x, norm1_g, w_in, tshift_mu, q_norm_g, k_norm_g, rel_bias, w0, w_decay_up, a0,
           w_aaa_up, w_gate_up, k_k, k_a, r_k, ln_x_g, ln_x_b, w_out, norm2_g, w_router,
           b_router, w_up, b_up, w_down, b_down):
    params = (norm1_g, w_in, tshift_mu, q_norm_g, k_norm_g, rel_bias, w0, w_decay_up, a0,
              w_aaa_up, w_gate_up, k_k, k_a, r_k, ln_x_g, ln_x_b, w_out, norm2_g, w_router,
              b_router, w_up, b_up, w_down, b_down)
    for l in range(norm1_g.shape[0]):
        x = _layer(x, *[p[l] for p in params])
    return x
```

```python
import functools
import math

import jax
import jax.numpy as jnp
from jax import lax
from jax.experimental import pallas as pl
from jax.experimental.pallas import tpu as pltpu

F32 = jnp.float32
BF16 = jnp.bfloat16

CHUNK = 64
HEAD_DIM = 64
LEFT_CHUNKS = 8
MAX_LEFT_REL = 128
TOP_K = 4
SWIGLU_ALPHA = 1.702
SWIGLU_LIMIT = 7.0
NORM_EPS = 1e-6
LN_X_EPS = 64e-5
DECAY_LORA = 64
AAA_LORA = 64
GATE_LORA = 128

LANES = 128
GROUP = 256
NEG_BIG = -1e30

VMEM_LIMIT = 56 * 1024 * 1024

ATT_QROWS = 512
ATT_SUB = 256
ATT_KEYS = ATT_SUB + LEFT_CHUNKS * CHUNK
ATT_INTERLEAVE = 4

LOCAL_UNROLL = 2

MOE_ROWS = 512
ROUTE_TILE = 512
RUN_ALIGN = 8
RUN_SIZES = tuple(ROUTE_TILE >> s for s in range((ROUTE_TILE // RUN_ALIGN).bit_length()))
assert MOE_ROWS <= ROUTE_TILE and RUN_SIZES[-1] == RUN_ALIGN


def _dot(a, b):
    return jnp.dot(a, b, preferred_element_type=F32)


def _dot_nt(a, b):
    return lax.dot_general(a, b, (((1,), (1,)), ((), ())), preferred_element_type=F32)


def _head_sums(a, block_ones):
    return _dot(a.astype(BF16), block_ones)


def _sigmoid(x):
    return 0.5 * jnp.tanh(0.5 * x) + 0.5


def _inproj_kernel(x_ref, g_ref, w_ref, qg_ref, kg_ref, bd_ref, qkv_ref, pr_ref, *, att_w):
    x = x_ref[...]
    ms = jnp.mean(x * x, axis=-1, keepdims=True)
    h = (x * lax.rsqrt(ms + NORM_EPS) * g_ref[...]).astype(BF16)
    qkv = _dot(h, w_ref[:, : 3 * att_w])
    bd = bd_ref[...]

    def head_norm(t, g):
        ss = _head_sums(t * t, bd)
        return t * lax.rsqrt(ss * (1.0 / HEAD_DIM) + NORM_EPS) * g

    q = head_norm(qkv[:, :att_w], qg_ref[...]) * (HEAD_DIM ** -0.5)
    k = head_norm(qkv[:, att_w: 2 * att_w], kg_ref[...])
    qkv_ref[:, :att_w] = q.astype(BF16)
    qkv_ref[:, att_w: 2 * att_w] = k.astype(BF16)
    qkv_ref[:, 2 * att_w:] = qkv[:, 2 * att_w:].astype(BF16)
    pr_ref[...] = _dot(h, w_ref[:, 3 * att_w:]).astype(BF16)


def _inproj(x2d, g1, w_in_bf, qg, kg, bd, att_w, tm=512):
    T, D = x2d.shape
    in_cols = w_in_bf.shape[1]
    r_cols = in_cols - 3 * att_w
    const = lambda i: (0, 0)
    return pl.pallas_call(
        functools.partial(_inproj_kernel, att_w=att_w),
        grid=(T // tm,),
        in_specs=[
            pl.BlockSpec((tm, D), lambda i: (i, 0)),
            pl.BlockSpec((1, D), const),
            pl.BlockSpec((D, in_cols), const),
            pl.BlockSpec((1, att_w), const),
            pl.BlockSpec((1, att_w), const),
            pl.BlockSpec((att_w, att_w), const),
        ],
        out_specs=[
            pl.BlockSpec((tm, 3 * att_w), lambda i: (i, 0)),
            pl.BlockSpec((tm, r_cols), lambda i: (i, 0)),
        ],
        out_shape=[
            jax.ShapeDtypeStruct((T, 3 * att_w), BF16),
            jax.ShapeDtypeStruct((T, r_cols), BF16),
        ],
        compiler_params=pltpu.CompilerParams(
            dimension_semantics=("arbitrary",), vmem_limit_bytes=VMEM_LIMIT),
        name="inproj",
    )(x2d, g1, w_in_bf, qg, kg, bd)


def _attn_kernel(q_ref, kp_ref, kc_ref, vp_ref, vc_ref, bias_ref, o_ref, kbuf, vbuf):
    i = pl.program_id(2)
    kbuf[:ATT_QROWS, :] = kp_ref[...]
    kbuf[ATT_QROWS:, :] = kc_ref[...]
    lane_head = lax.broadcasted_iota(jnp.int32, (1, LANES), 1) >> (HEAD_DIM.bit_length() - 1)
    one = jnp.ones((), BF16)
    for h in range(2):
        vbuf[h, :ATT_QROWS, :] = jnp.where(lane_head == h, vp_ref[...], one)
        vbuf[h, ATT_QROWS:, :] = jnp.where(lane_head == h, vc_ref[...], one)
    tiles = range(ATT_QROWS // ATT_SUB)
    all_units = [(qs * ATT_SUB, h * HEAD_DIM, h, qs) for qs in tiles for h in range(2)]
    ext = []
    for u0 in range(0, len(all_units), ATT_INTERLEAVE):
        units = all_units[u0: u0 + ATT_INTERLEAVE]
        s = [_dot_nt(q_ref[r0: r0 + ATT_SUB, lo: lo + HEAD_DIM],
                     kbuf[r0: r0 + ATT_KEYS, lo: lo + HEAD_DIM])
             + bias_ref[h, jnp.where(i == 0, 1 + qs, 0)] for (r0, lo, h, qs) in units]
        m = [jnp.max(x, axis=-1, keepdims=True) for x in s]
        p = [jnp.exp(x - mx).astype(BF16) for x, mx in zip(s, m)]
        ext += [_dot(x, vbuf[h, r0: r0 + ATT_KEYS, :]) for x, (r0, _, h, _) in zip(p, units)]
    for qs in tiles:
        e0, e1 = ext[2 * qs], ext[2 * qs + 1]
        o_ref[qs * ATT_SUB: (qs + 1) * ATT_SUB, :] = jnp.where(
            lane_head == 0, e0 / pltpu.roll(e0, HEAD_DIM, axis=1),
            e1 / pltpu.roll(e1, HEAD_DIM, axis=1)).astype(BF16)


def _attention(qkv, bias, B, S, att_w):
    T = B * S
    n_i = S // ATT_QROWS
    n_hp = att_w // LANES
    heads_per = LANES // HEAD_DIM
    assert heads_per == 2

    def cur(col0):
        return lambda b, hp, i: (b * n_i + i, col0 + hp)

    def prev(col0):
        return lambda b, hp, i: (b * n_i + jnp.maximum(i - 1, 0), col0 + hp)

    blk = (ATT_QROWS, LANES)
    return pl.pallas_call(
        _attn_kernel,
        grid=(B, n_hp, n_i),
        in_specs=[
            pl.BlockSpec(blk, cur(0)),
            pl.BlockSpec(blk, prev(n_hp)),
            pl.BlockSpec(blk, cur(n_hp)),
            pl.BlockSpec(blk, prev(2 * n_hp)),
            pl.BlockSpec(blk, cur(2 * n_hp)),
            pl.BlockSpec((heads_per,) + bias.shape[1:], lambda b, hp, i: (hp, 0, 0, 0)),
        ],
        out_specs=pl.BlockSpec(blk, lambda b, hp, i: (b * n_i + i, hp)),
        out_shape=jax.ShapeDtypeStruct((T, att_w), BF16),
        scratch_shapes=[
            pltpu.VMEM((2 * ATT_QROWS, LANES), BF16),
            pltpu.VMEM((2, 2 * ATT_QROWS, LANES), BF16),
        ],
        compiler_params=pltpu.CompilerParams(
            dimension_semantics=("arbitrary", "arbitrary", "arbitrary"),
            vmem_limit_bytes=VMEM_LIMIT),
        name="band_attention",
    )(qkv, qkv, qkv, qkv, qkv, bias)


def _attention_bias(rel_bias):
    H = rel_bias.shape[0]
    table = rel_bias.astype(F32)
    far = LEFT_CHUNKS * CHUNK + ATT_SUB - 1
    n_left = far - MAX_LEFT_REL + 1
    n_ext = ATT_SUB + ATT_KEYS - 1
    ext = jnp.concatenate([
        jnp.broadcast_to(table[:, :1], (H, n_left)), table[:, 1:],
        jnp.broadcast_to(table[:, -1:], (H, n_ext - n_left - (table.shape[1] - 1)))], axis=1)
    flat = jnp.tile(ext, (1, ATT_SUB + 1))[:, ATT_SUB - 1: ATT_SUB - 1 + ATT_SUB * (n_ext - 1)]
    bias = flat.reshape(H, ATT_SUB, n_ext - 1)[:, :, :ATT_KEYS]
    r = jnp.arange(ATT_SUB)[:, None]
    c = jnp.arange(ATT_KEYS)[None, :]
    band = c // CHUNK - r // CHUNK
    ok = (band >= 0) & (band <= LEFT_CHUNKS)
    variants = [ok] + [ok & (c >= ATT_QROWS - qs * ATT_SUB) for qs in range(ATT_QROWS // ATT_SUB)]
    return jnp.where(jnp.stack(variants)[None], bias[:, None], NEG_BIG)


def _rwkv_kernel(p_ref, pprev_ref, mu_ref, w0_ref, a0_ref, lora_ref, wg_ref, kk_ref, ka_ref,
                 rk_ref, lng_ref, lnb_ref, bd_ref, tri_ref, o_ref,
                 rt_s, at_s, bt_s, kt_s, bh_s, kh_s, v_s, wl_s, y_s, z_s, g_s, y0_s, m_s, n_s,
                 *, width, tm):
    i = pl.program_id(1)
    n_groups = width // GROUP
    heads_per_group = GROUP // HEAD_DIM

    @pl.when(i == 0)
    def _():
        z_s[...] = jnp.zeros_like(z_s)

    p = p_ref[...].astype(F32)
    first = jnp.where(i > 0, 1.0, 0.0)
    prev_row = pprev_ref[7:8, :].astype(F32) * first
    row = lax.broadcasted_iota(jnp.int32, (tm, 1), 0)
    prev = jnp.where(row == 0, prev_row, pltpu.roll(p, 1, axis=0))
    ps = p + (prev - p) * mu_ref[...]

    r = ps[:, :width]
    k = ps[:, width: 2 * width]
    v = ps[:, 2 * width: 3 * width]
    wa = ps[:, 3 * width: 3 * width + DECAY_LORA + AAA_LORA]
    gd = ps[:, 3 * width + DECAY_LORA + AAA_LORA:]
    lane = lax.broadcasted_iota(jnp.int32, (1, DECAY_LORA + AAA_LORA), 1)
    wa = jnp.where(lane < DECAY_LORA, jnp.tanh(wa), wa)
    lw = _dot(wa.astype(BF16), lora_ref[...])
    logw = -math.exp(-0.5) * _sigmoid(w0_ref[...] + lw[:, :width])
    eta = _sigmoid(a0_ref[...] + lw[:, width:])
    g = _dot(_sigmoid(gd).astype(BF16), wg_ref[...])

    bd = bd_ref[...]
    kk = k * kk_ref[...]
    kk = kk * lax.rsqrt(jnp.maximum(_head_sums(kk * kk, bd), 1e-24))
    km = k * (1.0 + (eta - 1.0) * ka_ref[...])
    bb = kk * eta

    tri = tri_ref[...]
    cs_chunks = [_split_dot_left(tri, logw[c * CHUNK: (c + 1) * CHUNK]) for c in range(tm // CHUNK)]
    cs = jnp.concatenate(cs_chunks, axis=0)
    tot = jnp.concatenate(
        [jnp.broadcast_to(x[CHUNK - 1:], (CHUNK, width)) for x in cs_chunks], axis=0)
    e_pos = jnp.exp(cs)
    e_neg = jnp.exp(-cs)
    e_tail = jnp.exp(tot - cs)
    rt_s[...] = r * e_pos
    at_s[...] = -kk * jnp.exp(cs - logw)
    bt_s[...] = bb * e_neg
    kt_s[...] = km * e_neg
    bh_s[...] = bb * e_tail
    kh_s[...] = km * e_tail
    v_s[...] = v
    wl_s[...] = jnp.exp(tot)

    rr = lax.broadcasted_iota(jnp.int32, (GROUP, GROUP), 0)
    cc = lax.broadcasted_iota(jnp.int32, (GROUP, GROUP), 1)
    shift = CHUNK.bit_length() - 1
    same = (rr >> shift) == (cc >> shift)
    strict = same & (rr > cc)
    incl = same & (rr >= cc)
    eye = jnp.where(rr == cc, 1.0, 0.0)
    lane_head = lax.broadcasted_iota(jnp.int32, (1, GROUP), 1) >> shift

    def stack_masked(x):
        return jnp.concatenate(
            [jnp.where(lane_head == h, x, 0.0) for h in range(heads_per_group)], axis=0)

    def stack_plain(x):
        return jnp.concatenate([x] * heads_per_group, axis=0)

    def bdot(a, b):
        return _dot(a.astype(BF16), b.astype(BF16))

    def bdot_nt(a, b):
        return _dot_nt(a.astype(BF16), b.astype(BF16))

    def local_body(it, carry):
        units = []
        for u in range(LOCAL_UNROLL):
            c = it * LOCAL_UNROLL + u
            rows = pl.ds(pl.multiple_of(c * CHUNK, CHUNK), CHUNK)
            units += [(c, gi, rows, slice(gi * GROUP, (gi + 1) * GROUP)) for gi in range(n_groups)]

        def load(ref, stack):
            return [stack(ref[rows, cols]) for (_, _, rows, cols) in units]

        def each(f, *lists):
            return [f(*args) for args in zip(*lists)]

        b16 = lambda xs: [x.astype(BF16) for x in xs]
        rt4 = load(rt_s, stack_masked)
        at4, rt4b, v4 = b16(load(at_s, stack_masked)), b16(rt4), load(v_s, stack_masked)
        v4b, bh4 = b16(v4), b16(load(bh_s, stack_masked))
        kh4 = b16(load(kh_s, stack_masked))
        bt_rep, kt_rep = b16(load(bt_s, stack_plain)), b16(load(kt_s, stack_plain))
        a_ab = each(lambda a, b: jnp.where(strict, _dot_nt(a, b), 0.0), at4, bt_rep)
        a_ak = b16(each(lambda a, b: jnp.where(strict, _dot_nt(a, b), 0.0), at4, kt_rep))
        a_rb = b16(each(lambda a, b: jnp.where(incl, _dot_nt(a, b), 0.0), rt4b, bt_rep))
        a_rk = b16(each(lambda a, b: jnp.where(incl, _dot_nt(a, b), 0.0), rt4b, kt_rep))
        tinv = [eye + a for a in a_ab]
        pw = b16(a_ab)
        span = 2
        while span < CHUNK:
            pw = b16(each(_dot, pw, pw))
            tinv = each(lambda t, p: t + _dot(p, t.astype(BF16)), tinv, pw)
            span *= 2
        tinv = b16(tinv)
        p4 = each(_dot, tinv, at4)
        q4 = each(_dot, tinv, b16(each(_dot, a_ak, v4b)))
        p4b, q4b = b16(p4), b16(q4)
        g4 = each(lambda r, a, p: r + _dot(a, p), rt4, a_rb, p4b)
        y0 = each(lambda a, q, ak, v: _dot(a, q) + _dot(ak, v), a_rb, q4b, a_rk, v4b)
        m4 = each(lambda p, b: _dot(p.T.astype(BF16), b), p4, bh4)
        n4 = each(lambda q, b, v, k: _dot(q.T.astype(BF16), b) + _dot(v.T.astype(BF16), k),
                  q4, bh4, v4, kh4)
        for (c, gi, _, _), g, y, m, n in zip(units, g4, y0, m4, n4):
            g_s[c, gi] = g.astype(BF16)
            y0_s[c, gi] = y
            m_s[c, gi] = m.astype(BF16)
            n_s[c, gi] = n
        return carry

    lax.fori_loop(0, tm // CHUNK // LOCAL_UNROLL, local_body, 0)

    def carry_body(c, carry):
        r0 = pl.multiple_of(c * CHUNK, CHUNK)
        rows = pl.ds(r0, CHUNK)
        for gi in range(n_groups):
            cols = slice(gi * GROUP, (gi + 1) * GROUP)
            zt = z_s[gi]
            zb = zt.astype(BF16)
            y4 = _dot_nt(g_s[c, gi], zb) + y0_s[c, gi]
            z_s[gi] = zt * wl_s[pl.ds(r0, 1), cols] + _dot(zb, m_s[c, gi]) + n_s[c, gi]
            ytok = y4[:CHUNK]
            for h in range(1, heads_per_group):
                ytok = ytok + y4[h * CHUNK: (h + 1) * CHUNK]
            y_s[rows, cols] = ytok
        return carry

    lax.fori_loop(0, tm // CHUNK, carry_body, 0, unroll=True)

    y = y_s[...]
    inv_n = 1.0 / HEAD_DIM
    mean = _head_sums(y, bd) * inv_n
    d = y - mean
    var = _head_sums(d * d, bd) * inv_n
    yn = d * lax.rsqrt(var + LN_X_EPS) * lng_ref[...] + lnb_ref[...]
    bonus = _head_sums(r * km * rk_ref[...], bd) * v
    o_ref[...] = ((yn + bonus) * g).astype(BF16)


def _split_dot_left(a_exact, b, terms=2):
    acc = None
    rem = b
    for _ in range(terms):
        part = rem.astype(BF16)
        rem = rem - part.astype(F32)
        d = _dot(a_exact, part)
        acc = d if acc is None else acc + d
    return acc


def _rwkv(pr, mu, w0, a0, lora, wg, k_k, k_a, r_k, ln_g, ln_b, bd, B, S, width, tm=512):
    T = B * S
    n_t = S // tm
    cols = pr.shape[1]
    ti = jnp.arange(CHUNK)
    tri = (ti[:, None] >= ti[None, :]).astype(BF16)
    const = lambda b, i: (0, 0)
    vec = lambda n: pl.BlockSpec((1, n), const)
    scr = lambda: pltpu.VMEM((tm, width), F32)
    return pl.pallas_call(
        functools.partial(_rwkv_kernel, width=width, tm=tm),
        grid=(B, n_t),
        in_specs=[
            pl.BlockSpec((tm, cols), lambda b, i: (b * n_t + i, 0)),
            pl.BlockSpec((8, cols), lambda b, i: (jnp.maximum((b * n_t + i) * (tm // 8) - 1, 0), 0)),
            vec(cols), vec(width), vec(width),
            pl.BlockSpec(lora.shape, const),
            pl.BlockSpec(wg.shape, const),
            vec(width), vec(width), vec(width), vec(width), vec(width),
            pl.BlockSpec((width, width), const),
            pl.BlockSpec((CHUNK, CHUNK), const),
        ],
        out_specs=pl.BlockSpec((tm, width), lambda b, i: (b * n_t + i, 0)),
        out_shape=jax.ShapeDtypeStruct((T, width), BF16),
        scratch_shapes=[scr() for _ in range(9)]
        + [pltpu.VMEM((width // GROUP, GROUP, GROUP), F32)]
        + [pltpu.VMEM((tm // CHUNK, width // GROUP, GROUP, GROUP), dt)
           for dt in (BF16, F32, BF16, F32)],
        compiler_params=pltpu.CompilerParams(
            dimension_semantics=("arbitrary", "arbitrary"), vmem_limit_bytes=VMEM_LIMIT),
        name="rwkv7",
    )(pr, pr, mu, w0, a0, lora, wg, k_k, k_a, r_k, ln_g, ln_b, bd, tri)


def _outproj_kernel(att_ref, rwk_ref, x_ref, wa_ref, wr_ref, g_ref, wrt_ref, br_ref, upper_ref,
                    x2_ref, xn_ref, eidx_ref, rank_ref, gate_ref, cnt_ref, *, n_exp, tm):
    x2 = x_ref[...] + _dot(att_ref[...], wa_ref[...]) + _dot(rwk_ref[...], wr_ref[...])
    x2_ref[...] = x2
    ms = jnp.mean(x2 * x2, axis=-1, keepdims=True)
    xn = x2 * lax.rsqrt(ms + NORM_EPS) * g_ref[...]
    xn_ref[...] = xn.astype(BF16)

    x_hi = xn.astype(BF16)
    x_lo = (xn - x_hi.astype(F32)).astype(BF16)
    logits = (_dot_nt(wrt_ref[0], x_hi) + _dot_nt(wrt_ref[0], x_lo)
              + _dot_nt(wrt_ref[1], x_hi)) + br_ref[...]

    eio = lax.broadcasted_iota(jnp.int32, (n_exp, tm), 0)
    work = logits
    vals, onehots = [], []
    for k in range(TOP_K):
        m = jnp.max(work, axis=0, keepdims=True)
        idx = jnp.min(jnp.where(work == m, eio, n_exp), axis=0, keepdims=True)
        oh = eio == idx
        vals.append(m)
        onehots.append(oh)
        eidx_ref[k: k + 1, :] = idx
        work = jnp.where(oh, -jnp.inf, work)

    exps = [jnp.exp(vk - vals[0]) for vk in vals]
    denom = exps[0]
    for e in exps[1:]:
        denom = denom + e
    for k in range(TOP_K):
        gate_ref[k: k + 1, :] = exps[k] / denom

    sel = onehots[0]
    for oh in onehots[1:]:
        sel = sel | oh
    sel_f = jnp.where(sel, 1.0, 0.0)
    before = _dot(sel_f.astype(BF16), upper_ref[...])
    for k in range(TOP_K):
        rank_ref[k: k + 1, :] = jnp.sum(
            jnp.where(onehots[k], before, 0.0), axis=0, keepdims=True).astype(jnp.int32)
    cnt_ref[...] = jnp.broadcast_to(
        jnp.sum(sel_f, axis=1, keepdims=True), cnt_ref.shape).astype(jnp.int32)


def _outproj(att, rwk, x2d, w_att, w_rwk, g2, wrt, br, n_exp):
    T, D = x2d.shape
    tm = ROUTE_TILE
    upper = (jnp.arange(tm)[:, None] < jnp.arange(tm)[None, :]).astype(BF16)
    const = lambda i: (0, 0)
    rows = lambda n: pl.BlockSpec((tm, n), lambda i: (i, 0))
    lanes = pl.BlockSpec((TOP_K, tm), lambda i: (0, i))
    return pl.pallas_call(
        functools.partial(_outproj_kernel, n_exp=n_exp, tm=tm),
        grid=(T // tm,),
        in_specs=[
            rows(att.shape[1]), rows(rwk.shape[1]), rows(D),
            pl.BlockSpec(w_att.shape, const), pl.BlockSpec(w_rwk.shape, const),
            pl.BlockSpec((1, D), const),
            pl.BlockSpec(wrt.shape, lambda i: (0, 0, 0)),
            pl.BlockSpec((n_exp, 1), const),
            pl.BlockSpec((tm, tm), const),
        ],
        out_specs=[rows(D), rows(D), lanes, lanes, lanes,
                   pl.BlockSpec((n_exp, LANES), lambda i: (i, 0))],
        out_shape=[
            jax.ShapeDtypeStruct((T, D), F32),
            jax.ShapeDtypeStruct((T, D), BF16),
            jax.ShapeDtypeStruct((TOP_K, T), jnp.int32),
            jax.ShapeDtypeStruct((TOP_K, T), jnp.int32),
            jax.ShapeDtypeStruct((TOP_K, T), F32),
            jax.ShapeDtypeStruct((T // tm * n_exp, LANES), jnp.int32),
        ],
        compiler_params=pltpu.CompilerParams(
            dimension_semantics=("arbitrary",), vmem_limit_bytes=VMEM_LIMIT),
        name="outproj_router",
    )(att, rwk, x2d, w_att, w_rwk, g2, wrt, br, upper)


def _pow2_sizes(limit):
    top = 1 << (limit.bit_length() - 1)
    return tuple(top >> s for s in range((top // RUN_ALIGN).bit_length()))


def _run_copies(action, length, src, src_off, dst, dst_off, sem, sizes=RUN_SIZES):
    for size in sizes:
        start = length & ~(2 * size - 1)

        @pl.when((length & size) != 0)
        def _():
            cp = pltpu.make_async_copy(
                src.at[pl.ds(pl.multiple_of(src_off + start, RUN_ALIGN), size)],
                dst.at[pl.ds(pl.multiple_of(dst_off + start, RUN_ALIGN), size)], sem)
            cp.start() if action == "start" else cp.wait()


def _dispatch_kernel(soff_ref, rlen_ref, rdst_ref, tail_dst_ref, tail_len_ref, nact_ref,
                     slot_ref, xn_ref, xs_hbm, buf, zbuf, sem, run_sems, *, n_exp, slots, n_blocks):
    j = pl.program_id(0)
    tm = ROUTE_TILE

    @pl.when(j == 0)
    def _():
        zbuf[...] = jnp.zeros_like(zbuf)

        def unused_block(action):
            def body(b, c):
                cp = pltpu.make_async_copy(
                    zbuf, xs_hbm.at[pl.ds(pl.multiple_of(b * MOE_ROWS, MOE_ROWS), MOE_ROWS)], sem)
                cp.start() if action == "start" else cp.wait()
                return c
            return body

        for action in ("start", "wait"):
            for e in range(n_exp):
                _run_copies(action, tail_len_ref[e], zbuf, 0, xs_hbm, tail_dst_ref[e], sem)
            lax.fori_loop(nact_ref[0], n_blocks, unused_block(action), 0)

    cur = lax.rem(j, 2)
    x = xn_ref[...]
    for sb in range(slots // GROUP):
        sio = lax.broadcasted_iota(jnp.int32, (GROUP, tm), 0) + sb * GROUP
        p = jnp.zeros((GROUP, tm), F32)
        for k in range(TOP_K):
            p = jnp.where(sio == slot_ref[0, k: k + 1, :], 1.0, p)
        buf[cur, sb * GROUP: (sb + 1) * GROUP, :] = _dot(p.astype(BF16), x)

    def start_runs(tile, half):
        for e in range(n_exp):
            r = tile * n_exp + e
            _run_copies("start", rlen_ref[r], buf.at[half], soff_ref[r], xs_hbm, rdst_ref[r],
                        run_sems.at[half])

    def wait_runs(tile, half):
        last = tile * n_exp + n_exp - 1
        _run_copies("wait", soff_ref[last] + rlen_ref[last], buf.at[half], 0, xs_hbm, 0,
                    run_sems.at[half], sizes=_pow2_sizes(slots))

    start_runs(j, cur)

    @pl.when(j > 0)
    def _():
        wait_runs(j - 1, 1 - cur)

    @pl.when(j == pl.num_programs(0) - 1)
    def _():
        wait_runs(j, cur)


def _dispatch(soff, rlen, rdst, tail_dst, tail_len, n_active, slot_lane, xn, n_blocks, slots):
    T, D = xn.shape
    tm = ROUTE_TILE
    n_exp = tail_dst.shape[0]
    n_rows = n_blocks * MOE_ROWS
    grid_spec = pltpu.PrefetchScalarGridSpec(
        num_scalar_prefetch=6,
        grid=(T // tm,),
        in_specs=[
            pl.BlockSpec((1, TOP_K, tm), lambda i, *_: (i, 0, 0)),
            pl.BlockSpec((tm, D), lambda i, *_: (i, 0)),
        ],
        out_specs=pl.BlockSpec(memory_space=pl.ANY),
        scratch_shapes=[pltpu.VMEM((2, slots, D), F32), pltpu.VMEM((MOE_ROWS, D), F32),
                        pltpu.SemaphoreType.DMA, pltpu.SemaphoreType.DMA((2,))],
    )
    return pl.pallas_call(
        functools.partial(_dispatch_kernel, n_exp=n_exp, slots=slots, n_blocks=n_blocks),
        grid_spec=grid_spec,
        out_shape=jax.ShapeDtypeStruct((n_rows, D), F32),
        compiler_params=pltpu.CompilerParams(
            dimension_semantics=("arbitrary",), vmem_limit_bytes=VMEM_LIMIT),
        name="moe_dispatch",
    )(soff, rlen, rdst, tail_dst, tail_len, n_active, slot_lane, xn)


def _moe_kernel(bexp_ref, nact_ref, x_ref, wup_ref, bup_ref, wdn_ref, bdn_ref,
                y_ref, wup_bf, wdn_bf, *, d_exp):
    i = pl.program_id(0)

    @pl.when(i < nact_ref[0])
    def _():
        new_expert = jnp.logical_or(i == 0, bexp_ref[i] != bexp_ref[jnp.maximum(i - 1, 0)])

        @pl.when(new_expert)
        def _():
            wup_bf[...] = wup_ref[0].astype(BF16)
            wdn_bf[...] = wdn_ref[0].astype(BF16)

        x = x_ref[...].astype(BF16)

        def up(c):
            lo = c * GROUP
            glu = _dot(x, wup_bf[:, lo: lo + GROUP]) + bup_ref[0, :, lo: lo + GROUP]
            lin = (_dot(x, wup_bf[:, d_exp + lo: d_exp + lo + GROUP])
                   + bup_ref[0, :, d_exp + lo: d_exp + lo + GROUP])
            return glu, lin

        def down(c, glu, lin):
            glu = jnp.minimum(glu, SWIGLU_LIMIT)
            lin = jnp.clip(lin, -SWIGLU_LIMIT, SWIGLU_LIMIT)
            act = glu * _sigmoid(SWIGLU_ALPHA * glu) * (lin + 1.0)
            return _dot(act.astype(BF16), wdn_bf[c * GROUP: (c + 1) * GROUP, :])

        n_c = d_exp // GROUP
        y = bdn_ref[0]
        h = up(0)
        for c in range(n_c):
            h_next = up(c + 1) if c + 1 < n_c else None
            y = y + down(c, *h)
            h = h_next
        y_ref[...] = y

    @pl.when(i >= nact_ref[0])
    def _():
        y_ref[...] = jnp.zeros_like(y_ref)


def _moe(blk_expert, n_active, xs, w_up, b_up, w_down, b_down):
    n_blocks = blk_expert.shape[0]
    n_exp, D, two_f = w_up.shape
    d_exp = two_f // 2
    blk = lambda i, na: jnp.minimum(i, na[0] - 1)
    grid_spec = pltpu.PrefetchScalarGridSpec(
        num_scalar_prefetch=2,
        grid=(n_blocks,),
        in_specs=[
            pl.BlockSpec((MOE_ROWS, D), lambda i, be, na: (blk(i, na), 0)),
            pl.BlockSpec((1, D, two_f), lambda i, be, na: (be[blk(i, na)], 0, 0)),
            pl.BlockSpec((1, 1, two_f), lambda i, be, na: (be[blk(i, na)], 0, 0)),
            pl.BlockSpec((1, d_exp, D), lambda i, be, na: (be[blk(i, na)], 0, 0)),
            pl.BlockSpec((1, 1, D), lambda i, be, na: (be[blk(i, na)], 0, 0)),
        ],
        out_specs=pl.BlockSpec((MOE_ROWS, D), lambda i, be, na: (i, 0)),
        scratch_shapes=[
            pltpu.VMEM((D, two_f), BF16),
            pltpu.VMEM((d_exp, D), BF16),
        ],
    )
    return pl.pallas_call(
        functools.partial(_moe_kernel, d_exp=d_exp),
        grid_spec=grid_spec,
        out_shape=jax.ShapeDtypeStruct((n_blocks * MOE_ROWS, D), F32),
        compiler_params=pltpu.CompilerParams(
            dimension_semantics=("arbitrary",), vmem_limit_bytes=VMEM_LIMIT),
        name="moe_ffn",
    )(blk_expert, n_active, xs,
      w_up, b_up.reshape(n_exp, 1, two_f), w_down, b_down.reshape(n_exp, 1, D))


def _combine_kernel(soff_ref, rlen_ref, rdst_ref, slot_ref, gate_ref, x2_ref, y_hbm, o_ref,
                    buf, sems, *, n_exp, slots):
    j = pl.program_id(0)
    tm = ROUTE_TILE

    cur = lax.rem(j, 2)

    def start_runs(tile, half):
        for e in range(n_exp):
            r = tile * n_exp + e
            _run_copies("start", rlen_ref[r], y_hbm, rdst_ref[r], buf.at[half], soff_ref[r],
                        sems.at[half])

    @pl.when(j == 0)
    def _():
        buf[...] = jnp.zeros_like(buf)
        start_runs(0, 0)

    @pl.when(j + 1 < pl.num_programs(0))
    def _():
        start_runs(j + 1, 1 - cur)

    last = j * n_exp + n_exp - 1
    _run_copies("wait", soff_ref[last] + rlen_ref[last], y_hbm, 0, buf.at[cur], 0, sems.at[cur],
                sizes=_pow2_sizes(slots))

    slot = slot_ref[...]
    gate = gate_ref[...]
    acc = x2_ref[...]
    for sb in range(slots // GROUP):
        lio = lax.broadcasted_iota(jnp.int32, (tm, GROUP), 1) + sb * GROUP
        w = jnp.zeros((tm, GROUP), F32)
        for k in range(TOP_K):
            w = jnp.where(lio == slot[:, k: k + 1], gate[:, k: k + 1], w)
        acc = acc + _dot(w.astype(BF16), buf[cur, sb * GROUP: (sb + 1) * GROUP, :].astype(BF16))
    o_ref[...] = acc


def _combine(soff, rlen, rdst, slot_row, gates_row, x2, y_rows, slots):
    T, D = x2.shape
    tm = ROUTE_TILE
    n_exp = soff.shape[0] // (T // tm)
    grid_spec = pltpu.PrefetchScalarGridSpec(
        num_scalar_prefetch=3,
        grid=(T // tm,),
        in_specs=[
            pl.BlockSpec((tm, TOP_K), lambda i, *_: (i, 0)),
            pl.BlockSpec((tm, TOP_K), lambda i, *_: (i, 0)),
            pl.BlockSpec((tm, D), lambda i, *_: (i, 0)),
            pl.BlockSpec(memory_space=pl.ANY),
        ],
        out_specs=pl.BlockSpec((tm, D), lambda i, *_: (i, 0)),
        scratch_shapes=[pltpu.VMEM((2, slots, D), F32), pltpu.SemaphoreType.DMA((2,))],
    )
    return pl.pallas_call(
        functools.partial(_combine_kernel, n_exp=n_exp, slots=slots),
        grid_spec=grid_spec,
        out_shape=jax.ShapeDtypeStruct((T, D), F32),
        compiler_params=pltpu.CompilerParams(
            dimension_semantics=("arbitrary",), vmem_limit_bytes=VMEM_LIMIT),
        name="moe_combine",
    )(soff, rlen, rdst, slot_row, gates_row, x2, y_rows)


def _block_diag_ones(n, blk):
    i = jnp.arange(n)
    return ((i[:, None] // blk) == (i[None, :] // blk)).astype(BF16)


def _layer(x, norm1_g, w_in, tshift_mu, q_norm_g, k_norm_g, rel_bias, w0, w_decay_up, a0,
           w_aaa_up, w_gate_up, k_k, k_a, r_k, ln_x_g, ln_x_b, w_out, norm2_g, w_router,
           b_router, w_up, b_up, w_down, b_down):
    B, S, D = x.shape
    T = B * S
    att_heads = rel_bias.shape[0]
    att_w = att_heads * HEAD_DIM
    width = w0.shape[0]
    n_exp = w_router.shape[1]
    assert att_w % LANES == 0 and width % GROUP == 0 and S % ATT_QROWS == 0
    assert T % ROUTE_TILE == 0

    x2d = x.reshape(T, D)
    row = lambda a: a.reshape(1, -1).astype(F32)
    bd_att = _block_diag_ones(att_w, HEAD_DIM)
    bd_rwkv = _block_diag_ones(width, HEAD_DIM)

    qkv, pr = _inproj(
        x2d, row(norm1_g), w_in.astype(BF16),
        row(jnp.tile(q_norm_g, att_heads)), row(jnp.tile(k_norm_g, att_heads)), bd_att, att_w)

    att = _attention(qkv, _attention_bias(rel_bias), B, S, att_w)

    lora = jnp.zeros((DECAY_LORA + AAA_LORA, 2 * width), F32)
    lora = lora.at[:DECAY_LORA, :width].set(w_decay_up).at[DECAY_LORA:, width:].set(w_aaa_up)
    rwk = _rwkv(pr, row(tshift_mu), row(w0), row(a0), lora.astype(BF16),
                w_gate_up.astype(BF16), row(k_k), row(k_a), row(r_k), row(ln_x_g),
                row(ln_x_b), bd_rwkv, B, S, width)

    wrt = w_router.T.astype(F32)
    wrt_hi = wrt.astype(BF16)
    wrt_lo = (wrt - wrt_hi.astype(F32)).astype(BF16)
    x2, xn, eidx, rank, gates, counts = _outproj(
        att, rwk, x2d, w_out[:att_w].astype(BF16), w_out[att_w:].astype(BF16), row(norm2_g),
        jnp.stack([wrt_hi, wrt_lo]), b_router.reshape(n_exp, 1).astype(F32), n_exp)

    tm = ROUTE_TILE
    n_tiles = T // tm
    i32 = lambda a: a.astype(jnp.int32)
    run_n = counts.reshape(n_tiles, n_exp, LANES)[:, :, 0]
    run_len = (run_n + RUN_ALIGN - 1) // RUN_ALIGN * RUN_ALIGN
    run_off = jnp.cumsum(run_len, axis=1) - run_len
    exp_rows = jnp.sum(run_len, axis=0)
    exp_pad = (exp_rows + MOE_ROWS - 1) // MOE_ROWS * MOE_ROWS
    exp_end = jnp.cumsum(exp_pad)
    exp_start = exp_end - exp_pad
    run_dst = exp_start[None, :] + jnp.cumsum(run_len, axis=0) - run_len
    eidx_t = eidx.reshape(TOP_K, n_tiles, tm)
    experts = jnp.arange(n_exp, dtype=jnp.int32)[:, None, None, None]
    slot = rank.reshape(TOP_K, n_tiles, tm) + jnp.sum(
        jnp.where(eidx_t[None] == experts, run_off.T[:, None, :, None], 0), axis=0)
    slot_lane = i32(slot.transpose(1, 0, 2))
    slot_row = i32(slot.transpose(1, 2, 0).reshape(T, TOP_K))
    slots = -(-(tm * TOP_K + n_exp * (RUN_ALIGN - 1)) // GROUP) * GROUP
    n_blocks = -(-(T * TOP_K + n_tiles * n_exp * (RUN_ALIGN - 1)) // MOE_ROWS) + n_exp
    blk_start = jnp.arange(n_blocks, dtype=jnp.int32) * MOE_ROWS
    blk_expert = i32(jnp.minimum(
        jnp.sum(blk_start[:, None] >= exp_end[None, :], axis=1), n_exp - 1))
    n_active = i32(exp_end[-1:] // MOE_ROWS)
    soff, rlen, rdst = (i32(a.reshape(-1)) for a in (run_off, run_len, run_dst))

    xs = _dispatch(soff, rlen, rdst, i32(exp_start + exp_rows), i32(exp_pad - exp_rows),
                   n_active, slot_lane, xn, n_blocks, slots)
    y_rows = _moe(blk_expert, n_active, xs, w_up, b_up, w_down, b_down)
    out = _combine(soff, rlen, rdst, slot_row, gates.T, x2, y_rows, slots)
    return out.reshape(B, S, D)


def kernel(x, norm1_g, w_in, tshift_mu, q_norm_g, k_norm_g, rel_bias, w0, w_decay_up, a0,
           w_aaa_up, w_gate_up, k_k, k_a, r_k, ln_x_g, ln_x_b, w_out, norm2_g, w_router,
           b_router, w_up, b_up, w_down, b_down):
    params = (norm1_g, w_in, tshift_mu, q_norm_g, k_norm_g, rel_bias, w0, w_decay_up, a0,
              w_aaa_up, w_gate_up, k_k, k_a, r_k, ln_x_g, ln_x_b, w_out, norm2_g, w_router,
              b_router, w_up, b_up, w_down, b_down)
    for l in range(norm1_g.shape[0]):
        x = _layer(x, *[p[l] for p in params])
    return x
```

```python
import functools
import math

import jax
import jax.numpy as jnp
from jax import lax
from jax.experimental import pallas as pl
from jax.experimental.pallas import tpu as pltpu

F32 = jnp.float32
BF16 = jnp.bfloat16

CHUNK = 64
HEAD_DIM = 64
LEFT_CHUNKS = 8
MAX_LEFT_REL = 128
TOP_K = 4
SWIGLU_ALPHA = 1.702
SWIGLU_LIMIT = 7.0
NORM_EPS = 1e-6
LN_X_EPS = 64e-5
DECAY_LORA = 64
AAA_LORA = 64
GATE_LORA = 128

LANES = 128
GROUP = 256
NEG_BIG = -1e30

VMEM_LIMIT = 56 * 1024 * 1024

ATT_QROWS = 512
ATT_SUB = 256
ATT_KEYS = ATT_SUB + LEFT_CHUNKS * CHUNK
ATT_INTERLEAVE = 4

LOCAL_UNROLL = 2

MOE_ROWS = 512
ROUTE_TILE = 512
RUN_ALIGN = 8
RUN_SIZES = tuple(ROUTE_TILE >> s for s in range((ROUTE_TILE // RUN_ALIGN).bit_length()))
assert MOE_ROWS <= ROUTE_TILE and RUN_SIZES[-1] == RUN_ALIGN


def _dot(a, b):
    return jnp.dot(a, b, preferred_element_type=F32)


def _dot_nt(a, b):
    return lax.dot_general(a, b, (((1,), (1,)), ((), ())), preferred_element_type=F32)


def _head_sums(a, block_ones):
    return _dot(a.astype(BF16), block_ones)


def _pack_bf16_pairs(x):
    n = x.shape[1] // 2
    bits = pltpu.bitcast(x.astype(BF16).astype(F32), jnp.uint32)
    return bits[:, :n] | (bits[:, n:] >> 16)


def _unpack_bf16_pairs(w):
    hi = pltpu.bitcast(w & jnp.uint32(0xFFFF0000), F32)
    lo = pltpu.bitcast(w << 16, F32)
    return jnp.concatenate([hi, lo], axis=1).astype(BF16)


def _sigmoid(x):
    return 0.5 * jnp.tanh(0.5 * x) + 0.5


def _inproj_kernel(x_ref, g_ref, w_ref, qg_ref, kg_ref, bd_ref, qkv_ref, pr_ref, *, att_w):
    x = x_ref[...]
    ms = jnp.mean(x * x, axis=-1, keepdims=True)
    h = (x * lax.rsqrt(ms + NORM_EPS) * g_ref[...]).astype(BF16)
    qkv = _dot(h, w_ref[:, : 3 * att_w])
    bd = bd_ref[...]

    def head_norm(t, g):
        ss = _head_sums(t * t, bd)
        return t * lax.rsqrt(ss * (1.0 / HEAD_DIM) + NORM_EPS) * g

    q = head_norm(qkv[:, :att_w], qg_ref[...]) * (HEAD_DIM ** -0.5)
    k = head_norm(qkv[:, att_w: 2 * att_w], kg_ref[...])
    qkv_ref[:, :att_w] = q.astype(BF16)
    qkv_ref[:, att_w: 2 * att_w] = k.astype(BF16)
    qkv_ref[:, 2 * att_w:] = qkv[:, 2 * att_w:].astype(BF16)
    pr_ref[...] = _dot(h, w_ref[:, 3 * att_w:]).astype(BF16)


def _inproj(x2d, g1, w_in_bf, qg, kg, bd, att_w, tm=512):
    T, D = x2d.shape
    in_cols = w_in_bf.shape[1]
    r_cols = in_cols - 3 * att_w
    const = lambda i: (0, 0)
    return pl.pallas_call(
        functools.partial(_inproj_kernel, att_w=att_w),
        grid=(T // tm,),
        in_specs=[
            pl.BlockSpec((tm, D), lambda i: (i, 0)),
            pl.BlockSpec((1, D), const),
            pl.BlockSpec((D, in_cols), const),
            pl.BlockSpec((1, att_w), const),
            pl.BlockSpec((1, att_w), const),
            pl.BlockSpec((att_w, att_w), const),
        ],
        out_specs=[
            pl.BlockSpec((tm, 3 * att_w), lambda i: (i, 0)),
            pl.BlockSpec((tm, r_cols), lambda i: (i, 0)),
        ],
        out_shape=[
            jax.ShapeDtypeStruct((T, 3 * att_w), BF16),
            jax.ShapeDtypeStruct((T, r_cols), BF16),
        ],
        compiler_params=pltpu.CompilerParams(
            dimension_semantics=("arbitrary",), vmem_limit_bytes=VMEM_LIMIT),
        name="inproj",
    )(x2d, g1, w_in_bf, qg, kg, bd)


def _attn_kernel(q_ref, kp_ref, kc_ref, vp_ref, vc_ref, bias_ref, o_ref, kbuf, vbuf):
    i = pl.program_id(2)
    kbuf[:ATT_QROWS, :] = kp_ref[...]
    kbuf[ATT_QROWS:, :] = kc_ref[...]
    lane_head = lax.broadcasted_iota(jnp.int32, (1, LANES), 1) >> (HEAD_DIM.bit_length() - 1)
    one = jnp.ones((), BF16)
    for h in range(2):
        vbuf[h, :ATT_QROWS, :] = jnp.where(lane_head == h, vp_ref[...], one)
        vbuf[h, ATT_QROWS:, :] = jnp.where(lane_head == h, vc_ref[...], one)
    tiles = range(ATT_QROWS // ATT_SUB)
    all_units = [(qs * ATT_SUB, h * HEAD_DIM, h, qs) for qs in tiles for h in range(2)]
    ext = []
    for u0 in range(0, len(all_units), ATT_INTERLEAVE):
        units = all_units[u0: u0 + ATT_INTERLEAVE]
        s = [_dot_nt(q_ref[r0: r0 + ATT_SUB, lo: lo + HEAD_DIM],
                     kbuf[r0: r0 + ATT_KEYS, lo: lo + HEAD_DIM])
             + bias_ref[h, jnp.where(i == 0, 1 + qs, 0)] for (r0, lo, h, qs) in units]
        m = [jnp.max(x, axis=-1, keepdims=True) for x in s]
        p = [jnp.exp(x - mx).astype(BF16) for x, mx in zip(s, m)]
        ext += [_dot(x, vbuf[h, r0: r0 + ATT_KEYS, :]) for x, (r0, _, h, _) in zip(p, units)]
    for qs in tiles:
        e0, e1 = ext[2 * qs], ext[2 * qs + 1]
        o_ref[qs * ATT_SUB: (qs + 1) * ATT_SUB, :] = jnp.where(
            lane_head == 0, e0 / pltpu.roll(e0, HEAD_DIM, axis=1),
            e1 / pltpu.roll(e1, HEAD_DIM, axis=1)).astype(BF16)


def _attention(qkv, bias, B, S, att_w):
    T = B * S
    n_i = S // ATT_QROWS
    n_hp = att_w // LANES
    heads_per = LANES // HEAD_DIM
    assert heads_per == 2

    def cur(col0):
        return lambda b, hp, i: (b * n_i + i, col0 + hp)

    def prev(col0):
        return lambda b, hp, i: (b * n_i + jnp.maximum(i - 1, 0), col0 + hp)

    blk = (ATT_QROWS, LANES)
    return pl.pallas_call(
        _attn_kernel,
        grid=(B, n_hp, n_i),
        in_specs=[
            pl.BlockSpec(blk, cur(0)),
            pl.BlockSpec(blk, prev(n_hp)),
            pl.BlockSpec(blk, cur(n_hp)),
            pl.BlockSpec(blk, prev(2 * n_hp)),
            pl.BlockSpec(blk, cur(2 * n_hp)),
            pl.BlockSpec((heads_per,) + bias.shape[1:], lambda b, hp, i: (hp, 0, 0, 0)),
        ],
        out_specs=pl.BlockSpec(blk, lambda b, hp, i: (b * n_i + i, hp)),
        out_shape=jax.ShapeDtypeStruct((T, att_w), BF16),
        scratch_shapes=[
            pltpu.VMEM((2 * ATT_QROWS, LANES), BF16),
            pltpu.VMEM((2, 2 * ATT_QROWS, LANES), BF16),
        ],
        compiler_params=pltpu.CompilerParams(
            dimension_semantics=("arbitrary", "arbitrary", "arbitrary"),
            vmem_limit_bytes=VMEM_LIMIT),
        name="band_attention",
    )(qkv, qkv, qkv, qkv, qkv, bias)


def _attention_bias(rel_bias):
    H = rel_bias.shape[0]
    table = rel_bias.astype(F32)
    far = LEFT_CHUNKS * CHUNK + ATT_SUB - 1
    n_left = far - MAX_LEFT_REL + 1
    n_ext = ATT_SUB + ATT_KEYS - 1
    ext = jnp.concatenate([
        jnp.broadcast_to(table[:, :1], (H, n_left)), table[:, 1:],
        jnp.broadcast_to(table[:, -1:], (H, n_ext - n_left - (table.shape[1] - 1)))], axis=1)
    flat = jnp.tile(ext, (1, ATT_SUB + 1))[:, ATT_SUB - 1: ATT_SUB - 1 + ATT_SUB * (n_ext - 1)]
    bias = flat.reshape(H, ATT_SUB, n_ext - 1)[:, :, :ATT_KEYS]
    r = jnp.arange(ATT_SUB)[:, None]
    c = jnp.arange(ATT_KEYS)[None, :]
    band = c // CHUNK - r // CHUNK
    ok = (band >= 0) & (band <= LEFT_CHUNKS)
    variants = [ok] + [ok & (c >= ATT_QROWS - qs * ATT_SUB) for qs in range(ATT_QROWS // ATT_SUB)]
    return jnp.where(jnp.stack(variants)[None], bias[:, None], NEG_BIG)


def _rwkv_kernel(p_ref, pprev_ref, mu_ref, w0_ref, a0_ref, lora_ref, wg_ref, kk_ref, ka_ref,
                 rk_ref, lng_ref, lnb_ref, bd_ref, tri_ref, o_ref,
                 rt_s, at_s, bt_s, kt_s, bh_s, kh_s, v_s, wl_s, y_s, z_s, g_s, y0_s, m_s, n_s,
                 *, width, tm):
    i = pl.program_id(1)
    n_groups = width // GROUP
    heads_per_group = GROUP // HEAD_DIM

    @pl.when(i == 0)
    def _():
        z_s[...] = jnp.zeros_like(z_s)

    p = p_ref[...].astype(F32)
    first = jnp.where(i > 0, 1.0, 0.0)
    prev_row = pprev_ref[7:8, :].astype(F32) * first
    row = lax.broadcasted_iota(jnp.int32, (tm, 1), 0)
    prev = jnp.where(row == 0, prev_row, pltpu.roll(p, 1, axis=0))
    ps = p + (prev - p) * mu_ref[...]

    r = ps[:, :width]
    k = ps[:, width: 2 * width]
    v = ps[:, 2 * width: 3 * width]
    wa = ps[:, 3 * width: 3 * width + DECAY_LORA + AAA_LORA]
    gd = ps[:, 3 * width + DECAY_LORA + AAA_LORA:]
    lane = lax.broadcasted_iota(jnp.int32, (1, DECAY_LORA + AAA_LORA), 1)
    wa = jnp.where(lane < DECAY_LORA, jnp.tanh(wa), wa)
    lw = _dot(wa.astype(BF16), lora_ref[...])
    logw = -math.exp(-0.5) * _sigmoid(w0_ref[...] + lw[:, :width])
    eta = _sigmoid(a0_ref[...] + lw[:, width:])
    g = _dot(_sigmoid(gd).astype(BF16), wg_ref[...])

    bd = bd_ref[...]
    kk = k * kk_ref[...]
    kk = kk * lax.rsqrt(jnp.maximum(_head_sums(kk * kk, bd), 1e-24))
    km = k * (1.0 + (eta - 1.0) * ka_ref[...])
    bb = kk * eta

    tri = tri_ref[...]
    cs_chunks = [_split_dot_left(tri, logw[c * CHUNK: (c + 1) * CHUNK]) for c in range(tm // CHUNK)]
    cs = jnp.concatenate(cs_chunks, axis=0)
    tot = jnp.concatenate(
        [jnp.broadcast_to(x[CHUNK - 1:], (CHUNK, width)) for x in cs_chunks], axis=0)
    e_pos = jnp.exp(cs)
    e_neg = jnp.exp(-cs)
    e_tail = jnp.exp(tot - cs)
    rt_s[...] = r * e_pos
    at_s[...] = -kk * jnp.exp(cs - logw)
    bt_s[...] = bb * e_neg
    kt_s[...] = km * e_neg
    bh_s[...] = bb * e_tail
    kh_s[...] = km * e_tail
    v_s[...] = v
    wl_s[...] = jnp.exp(tot)

    rr = lax.broadcasted_iota(jnp.int32, (GROUP, GROUP), 0)
    cc = lax.broadcasted_iota(jnp.int32, (GROUP, GROUP), 1)
    shift = CHUNK.bit_length() - 1
    same = (rr >> shift) == (cc >> shift)
    strict = same & (rr > cc)
    incl = same & (rr >= cc)
    eye = jnp.where(rr == cc, 1.0, 0.0)
    lane_head = lax.broadcasted_iota(jnp.int32, (1, GROUP), 1) >> shift

    def stack_masked(x):
        return jnp.concatenate(
            [jnp.where(lane_head == h, x, 0.0) for h in range(heads_per_group)], axis=0)

    def stack_plain(x):
        return jnp.concatenate([x] * heads_per_group, axis=0)

    def bdot(a, b):
        return _dot(a.astype(BF16), b.astype(BF16))

    def bdot_nt(a, b):
        return _dot_nt(a.astype(BF16), b.astype(BF16))

    def local_body(it, carry):
        units = []
        for u in range(LOCAL_UNROLL):
            c = it * LOCAL_UNROLL + u
            rows = pl.ds(pl.multiple_of(c * CHUNK, CHUNK), CHUNK)
            units += [(c, gi, rows, slice(gi * GROUP, (gi + 1) * GROUP)) for gi in range(n_groups)]

        def load(ref, stack):
            return [stack(ref[rows, cols]) for (_, _, rows, cols) in units]

        def each(f, *lists):
            return [f(*args) for args in zip(*lists)]

        b16 = lambda xs: [x.astype(BF16) for x in xs]
        rt4 = load(rt_s, stack_masked)
        at4, rt4b, v4 = b16(load(at_s, stack_masked)), b16(rt4), load(v_s, stack_masked)
        v4b, bh4 = b16(v4), b16(load(bh_s, stack_masked))
        kh4 = b16(load(kh_s, stack_masked))
        bt_rep, kt_rep = b16(load(bt_s, stack_plain)), b16(load(kt_s, stack_plain))
        a_ab = each(lambda a, b: jnp.where(strict, _dot_nt(a, b), 0.0), at4, bt_rep)
        a_ak = b16(each(lambda a, b: jnp.where(strict, _dot_nt(a, b), 0.0), at4, kt_rep))
        a_rb = b16(each(lambda a, b: jnp.where(incl, _dot_nt(a, b), 0.0), rt4b, bt_rep))
        a_rk = b16(each(lambda a, b: jnp.where(incl, _dot_nt(a, b), 0.0), rt4b, kt_rep))
        tinv = [eye + a for a in a_ab]
        pw = b16(a_ab)
        span = 2
        while span < CHUNK:
            pw = b16(each(_dot, pw, pw))
            tinv = each(lambda t, p: t + _dot(p, t.astype(BF16)), tinv, pw)
            span *= 2
        tinv = b16(tinv)
        p4 = each(_dot, tinv, at4)
        q4 = each(_dot, tinv, b16(each(_dot, a_ak, v4b)))
        p4b, q4b = b16(p4), b16(q4)
        g4 = each(lambda r, a, p: r + _dot(a, p), rt4, a_rb, p4b)
        y0 = each(lambda a, q, ak, v: _dot(a, q) + _dot(ak, v), a_rb, q4b, a_rk, v4b)
        m4 = each(lambda p, b: _dot(p.T.astype(BF16), b), p4, bh4)
        n4 = each(lambda q, b, v, k: _dot(q.T.astype(BF16), b) + _dot(v.T.astype(BF16), k),
                  q4, bh4, v4, kh4)
        for (c, gi, _, _), g, y, m, n in zip(units, g4, y0, m4, n4):
            g_s[c, gi] = g.astype(BF16)
            y0_s[c, gi] = y
            m_s[c, gi] = m.astype(BF16)
            n_s[c, gi] = n
        return carry

    lax.fori_loop(0, tm // CHUNK // LOCAL_UNROLL, local_body, 0)

    def carry_body(c, carry):
        r0 = pl.multiple_of(c * CHUNK, CHUNK)
        rows = pl.ds(r0, CHUNK)
        for gi in range(n_groups):
            cols = slice(gi * GROUP, (gi + 1) * GROUP)
            zt = z_s[gi]
            zb = zt.astype(BF16)
            y4 = _dot_nt(g_s[c, gi], zb) + y0_s[c, gi]
            z_s[gi] = zt * wl_s[pl.ds(r0, 1), cols] + _dot(zb, m_s[c, gi]) + n_s[c, gi]
            ytok = y4[:CHUNK]
            for h in range(1, heads_per_group):
                ytok = ytok + y4[h * CHUNK: (h + 1) * CHUNK]
            y_s[rows, cols] = ytok
        return carry

    lax.fori_loop(0, tm // CHUNK, carry_body, 0, unroll=True)

    y = y_s[...]
    inv_n = 1.0 / HEAD_DIM
    mean = _head_sums(y, bd) * inv_n
    d = y - mean
    var = _head_sums(d * d, bd) * inv_n
    yn = d * lax.rsqrt(var + LN_X_EPS) * lng_ref[...] + lnb_ref[...]
    bonus = _head_sums(r * km * rk_ref[...], bd) * v
    o_ref[...] = ((yn + bonus) * g).astype(BF16)


def _split_dot_left(a_exact, b, terms=2):
    acc = None
    rem = b
    for _ in range(terms):
        part = rem.astype(BF16)
        rem = rem - part.astype(F32)
        d = _dot(a_exact, part)
        acc = d if acc is None else acc + d
    return acc


def _rwkv(pr, mu, w0, a0, lora, wg, k_k, k_a, r_k, ln_g, ln_b, bd, B, S, width, tm=512):
    T = B * S
    n_t = S // tm
    cols = pr.shape[1]
    ti = jnp.arange(CHUNK)
    tri = (ti[:, None] >= ti[None, :]).astype(BF16)
    const = lambda b, i: (0, 0)
    vec = lambda n: pl.BlockSpec((1, n), const)
    scr = lambda: pltpu.VMEM((tm, width), F32)
    return pl.pallas_call(
        functools.partial(_rwkv_kernel, width=width, tm=tm),
        grid=(B, n_t),
        in_specs=[
            pl.BlockSpec((tm, cols), lambda b, i: (b * n_t + i, 0)),
            pl.BlockSpec((8, cols), lambda b, i: (jnp.maximum((b * n_t + i) * (tm // 8) - 1, 0), 0)),
            vec(cols), vec(width), vec(width),
            pl.BlockSpec(lora.shape, const),
            pl.BlockSpec(wg.shape, const),
            vec(width), vec(width), vec(width), vec(width), vec(width),
            pl.BlockSpec((width, width), const),
            pl.BlockSpec((CHUNK, CHUNK), const),
        ],
        out_specs=pl.BlockSpec((tm, width), lambda b, i: (b * n_t + i, 0)),
        out_shape=jax.ShapeDtypeStruct((T, width), BF16),
        scratch_shapes=[scr() for _ in range(9)]
        + [pltpu.VMEM((width // GROUP, GROUP, GROUP), F32)]
        + [pltpu.VMEM((tm // CHUNK, width // GROUP, GROUP, GROUP), dt)
           for dt in (BF16, F32, BF16, F32)],
        compiler_params=pltpu.CompilerParams(
            dimension_semantics=("arbitrary", "arbitrary"), vmem_limit_bytes=VMEM_LIMIT),
        name="rwkv7",
    )(pr, pr, mu, w0, a0, lora, wg, k_k, k_a, r_k, ln_g, ln_b, bd, tri)


def _outproj_kernel(att_ref, rwk_ref, x_ref, wa_ref, wr_ref, g_ref, wrt_ref, br_ref, upper_ref,
                    x2_ref, xn_ref, eidx_ref, rank_ref, gate_ref, cnt_ref, *, n_exp, tm):
    x2 = x_ref[...] + _dot(att_ref[...], wa_ref[...]) + _dot(rwk_ref[...], wr_ref[...])
    x2_ref[...] = x2
    ms = jnp.mean(x2 * x2, axis=-1, keepdims=True)
    xn = x2 * lax.rsqrt(ms + NORM_EPS) * g_ref[...]
    xn_ref[...] = xn.astype(BF16)

    x_hi = xn.astype(BF16)
    x_lo = (xn - x_hi.astype(F32)).astype(BF16)
    logits = (_dot_nt(wrt_ref[0], x_hi) + _dot_nt(wrt_ref[0], x_lo)
              + _dot_nt(wrt_ref[1], x_hi)) + br_ref[...]

    eio = lax.broadcasted_iota(jnp.int32, (n_exp, tm), 0)
    work = logits
    vals, onehots = [], []
    for k in range(TOP_K):
        m = jnp.max(work, axis=0, keepdims=True)
        idx = jnp.min(jnp.where(work == m, eio, n_exp), axis=0, keepdims=True)
        oh = eio == idx
        vals.append(m)
        onehots.append(oh)
        eidx_ref[k: k + 1, :] = idx
        work = jnp.where(oh, -jnp.inf, work)

    exps = [jnp.exp(vk - vals[0]) for vk in vals]
    denom = exps[0]
    for e in exps[1:]:
        denom = denom + e
    for k in range(TOP_K):
        gate_ref[k: k + 1, :] = exps[k] / denom

    sel = onehots[0]
    for oh in onehots[1:]:
        sel = sel | oh
    sel_f = jnp.where(sel, 1.0, 0.0)
    before = _dot(sel_f.astype(BF16), upper_ref[...])
    for k in range(TOP_K):
        rank_ref[k: k + 1, :] = jnp.sum(
            jnp.where(onehots[k], before, 0.0), axis=0, keepdims=True).astype(jnp.int32)
    cnt_ref[...] = jnp.broadcast_to(
        jnp.sum(sel_f, axis=1, keepdims=True), cnt_ref.shape).astype(jnp.int32)


def _outproj(att, rwk, x2d, w_att, w_rwk, g2, wrt, br, n_exp):
    T, D = x2d.shape
    tm = ROUTE_TILE
    upper = (jnp.arange(tm)[:, None] < jnp.arange(tm)[None, :]).astype(BF16)
    const = lambda i: (0, 0)
    rows = lambda n: pl.BlockSpec((tm, n), lambda i: (i, 0))
    lanes = pl.BlockSpec((TOP_K, tm), lambda i: (0, i))
    return pl.pallas_call(
        functools.partial(_outproj_kernel, n_exp=n_exp, tm=tm),
        grid=(T // tm,),
        in_specs=[
            rows(att.shape[1]), rows(rwk.shape[1]), rows(D),
            pl.BlockSpec(w_att.shape, const), pl.BlockSpec(w_rwk.shape, const),
            pl.BlockSpec((1, D), const),
            pl.BlockSpec(wrt.shape, lambda i: (0, 0, 0)),
            pl.BlockSpec((n_exp, 1), const),
            pl.BlockSpec((tm, tm), const),
        ],
        out_specs=[rows(D), rows(D), lanes, lanes, lanes,
                   pl.BlockSpec((n_exp, LANES), lambda i: (i, 0))],
        out_shape=[
            jax.ShapeDtypeStruct((T, D), F32),
            jax.ShapeDtypeStruct((T, D), BF16),
            jax.ShapeDtypeStruct((TOP_K, T), jnp.int32),
            jax.ShapeDtypeStruct((TOP_K, T), jnp.int32),
            jax.ShapeDtypeStruct((TOP_K, T), F32),
            jax.ShapeDtypeStruct((T // tm * n_exp, LANES), jnp.int32),
        ],
        compiler_params=pltpu.CompilerParams(
            dimension_semantics=("arbitrary",), vmem_limit_bytes=VMEM_LIMIT),
        name="outproj_router",
    )(att, rwk, x2d, w_att, w_rwk, g2, wrt, br, upper)


def _pow2_sizes(limit):
    top = 1 << (limit.bit_length() - 1)
    return tuple(top >> s for s in range((top // RUN_ALIGN).bit_length()))


def _run_copies(action, length, src, src_off, dst, dst_off, sem, sizes=RUN_SIZES):
    for size in sizes:
        start = length & ~(2 * size - 1)

        @pl.when((length & size) != 0)
        def _():
            cp = pltpu.make_async_copy(
                src.at[pl.ds(pl.multiple_of(src_off + start, RUN_ALIGN), size)],
                dst.at[pl.ds(pl.multiple_of(dst_off + start, RUN_ALIGN), size)], sem)
            cp.start() if action == "start" else cp.wait()


def _dispatch_kernel(soff_ref, rlen_ref, rdst_ref, tail_dst_ref, tail_len_ref, nact_ref,
                     slot_ref, xn_ref, xs_hbm, buf, zbuf, sem, run_sems, *, n_exp, slots, n_blocks):
    j = pl.program_id(0)
    tm = ROUTE_TILE

    @pl.when(j == 0)
    def _():
        zbuf[...] = jnp.zeros_like(zbuf)

        def unused_block(action):
            def body(b, c):
                cp = pltpu.make_async_copy(
                    zbuf, xs_hbm.at[pl.ds(pl.multiple_of(b * MOE_ROWS, MOE_ROWS), MOE_ROWS)], sem)
                cp.start() if action == "start" else cp.wait()
                return c
            return body

        for action in ("start", "wait"):
            for e in range(n_exp):
                _run_copies(action, tail_len_ref[e], zbuf, 0, xs_hbm, tail_dst_ref[e], sem)
            lax.fori_loop(nact_ref[0], n_blocks, unused_block(action), 0)

    cur = lax.rem(j, 2)
    x = xn_ref[...]
    for sb in range(slots // GROUP):
        sio = lax.broadcasted_iota(jnp.int32, (GROUP, tm), 0) + sb * GROUP
        p = jnp.zeros((GROUP, tm), F32)
        for k in range(TOP_K):
            p = jnp.where(sio == slot_ref[0, k: k + 1, :], 1.0, p)
        buf[cur, sb * GROUP: (sb + 1) * GROUP, :] = _pack_bf16_pairs(_dot(p.astype(BF16), x))

    def start_runs(tile, half):
        for e in range(n_exp):
            r = tile * n_exp + e
            _run_copies("start", rlen_ref[r], buf.at[half], soff_ref[r], xs_hbm, rdst_ref[r],
                        run_sems.at[half])

    def wait_runs(tile, half):
        last = tile * n_exp + n_exp - 1
        _run_copies("wait", soff_ref[last] + rlen_ref[last], buf.at[half], 0, xs_hbm, 0,
                    run_sems.at[half], sizes=_pow2_sizes(slots))

    start_runs(j, cur)

    @pl.when(j > 0)
    def _():
        wait_runs(j - 1, 1 - cur)

    @pl.when(j == pl.num_programs(0) - 1)
    def _():
        wait_runs(j, cur)


def _dispatch(soff, rlen, rdst, tail_dst, tail_len, n_active, slot_lane, xn, n_blocks, slots):
    T, D = xn.shape
    tm = ROUTE_TILE
    n_exp = tail_dst.shape[0]
    n_rows = n_blocks * MOE_ROWS
    grid_spec = pltpu.PrefetchScalarGridSpec(
        num_scalar_prefetch=6,
        grid=(T // tm,),
        in_specs=[
            pl.BlockSpec((1, TOP_K, tm), lambda i, *_: (i, 0, 0)),
            pl.BlockSpec((tm, D), lambda i, *_: (i, 0)),
        ],
        out_specs=pl.BlockSpec(memory_space=pl.ANY),
        scratch_shapes=[pltpu.VMEM((2, slots, D // 2), jnp.uint32),
                        pltpu.VMEM((MOE_ROWS, D // 2), jnp.uint32),
                        pltpu.SemaphoreType.DMA, pltpu.SemaphoreType.DMA((2,))],
    )
    return pl.pallas_call(
        functools.partial(_dispatch_kernel, n_exp=n_exp, slots=slots, n_blocks=n_blocks),
        grid_spec=grid_spec,
        out_shape=jax.ShapeDtypeStruct((n_rows, D // 2), jnp.uint32),
        compiler_params=pltpu.CompilerParams(
            dimension_semantics=("arbitrary",), vmem_limit_bytes=VMEM_LIMIT),
        name="moe_dispatch",
    )(soff, rlen, rdst, tail_dst, tail_len, n_active, slot_lane, xn)


def _moe_kernel(bexp_ref, nact_ref, x_ref, wup_ref, bup_ref, wdn_ref, bdn_ref,
                y_ref, wup_bf, wdn_bf, *, d_exp):
    i = pl.program_id(0)

    @pl.when(i < nact_ref[0])
    def _():
        new_expert = jnp.logical_or(i == 0, bexp_ref[i] != bexp_ref[jnp.maximum(i - 1, 0)])

        @pl.when(new_expert)
        def _():
            wup_bf[...] = wup_ref[0].astype(BF16)
            wdn_bf[...] = wdn_ref[0].astype(BF16)

        x = _unpack_bf16_pairs(x_ref[...])
        h_glu = _dot(x, wup_bf[:, :d_exp]) + bup_ref[0, :, :d_exp]
        h_lin = _dot(x, wup_bf[:, d_exp:]) + bup_ref[0, :, d_exp:]
        h_glu = jnp.minimum(h_glu, SWIGLU_LIMIT)
        h_lin = jnp.clip(h_lin, -SWIGLU_LIMIT, SWIGLU_LIMIT)
        act = h_glu * _sigmoid(SWIGLU_ALPHA * h_glu) * (h_lin + 1.0)
        y_ref[...] = _pack_bf16_pairs(_dot(act.astype(BF16), wdn_bf[...]) + bdn_ref[0])

    @pl.when(i >= nact_ref[0])
    def _():
        y_ref[...] = jnp.zeros_like(y_ref)


def _moe(blk_expert, n_active, xs, w_up, b_up, w_down, b_down):
    n_blocks = blk_expert.shape[0]
    n_exp, D, two_f = w_up.shape
    d_exp = two_f // 2
    blk = lambda i, na: jnp.minimum(i, na[0] - 1)
    grid_spec = pltpu.PrefetchScalarGridSpec(
        num_scalar_prefetch=2,
        grid=(n_blocks,),
        in_specs=[
            pl.BlockSpec((MOE_ROWS, D // 2), lambda i, be, na: (blk(i, na), 0)),
            pl.BlockSpec((1, D, two_f), lambda i, be, na: (be[blk(i, na)], 0, 0)),
            pl.BlockSpec((1, 1, two_f), lambda i, be, na: (be[blk(i, na)], 0, 0)),
            pl.BlockSpec((1, d_exp, D), lambda i, be, na: (be[blk(i, na)], 0, 0)),
            pl.BlockSpec((1, 1, D), lambda i, be, na: (be[blk(i, na)], 0, 0)),
        ],
        out_specs=pl.BlockSpec((MOE_ROWS, D // 2), lambda i, be, na: (i, 0)),
        scratch_shapes=[
            pltpu.VMEM((D, two_f), BF16),
            pltpu.VMEM((d_exp, D), BF16),
        ],
    )
    return pl.pallas_call(
        functools.partial(_moe_kernel, d_exp=d_exp),
        grid_spec=grid_spec,
        out_shape=jax.ShapeDtypeStruct((n_blocks * MOE_ROWS, D // 2), jnp.uint32),
        compiler_params=pltpu.CompilerParams(
            dimension_semantics=("arbitrary",), vmem_limit_bytes=VMEM_LIMIT),
        name="moe_ffn",
    )(blk_expert, n_active, xs,
      w_up, b_up.reshape(n_exp, 1, two_f), w_down, b_down.reshape(n_exp, 1, D))


def _combine_kernel(soff_ref, rlen_ref, rdst_ref, slot_ref, gate_ref, x2_ref, y_hbm, o_ref,
                    buf, sems, *, n_exp, slots):
    j = pl.program_id(0)
    tm = ROUTE_TILE

    cur = lax.rem(j, 2)

    def start_runs(tile, half):
        for e in range(n_exp):
            r = tile * n_exp + e
            _run_copies("start", rlen_ref[r], y_hbm, rdst_ref[r], buf.at[half], soff_ref[r],
                        sems.at[half])

    @pl.when(j == 0)
    def _():
        buf[...] = jnp.zeros_like(buf)
        start_runs(0, 0)

    @pl.when(j + 1 < pl.num_programs(0))
    def _():
        start_runs(j + 1, 1 - cur)

    last = j * n_exp + n_exp - 1
    _run_copies("wait", soff_ref[last] + rlen_ref[last], y_hbm, 0, buf.at[cur], 0, sems.at[cur],
                sizes=_pow2_sizes(slots))

    slot = slot_ref[...]
    gate = gate_ref[...]
    acc = x2_ref[...]
    for sb in range(slots // GROUP):
        lio = lax.broadcasted_iota(jnp.int32, (tm, GROUP), 1) + sb * GROUP
        w = jnp.zeros((tm, GROUP), F32)
        for k in range(TOP_K):
            w = jnp.where(lio == slot[:, k: k + 1], gate[:, k: k + 1], w)
        acc = acc + _dot(w.astype(BF16),
                         _unpack_bf16_pairs(buf[cur, sb * GROUP: (sb + 1) * GROUP, :]))
    o_ref[...] = acc


def _combine(soff, rlen, rdst, slot_row, gates_row, x2, y_rows, slots):
    T, D = x2.shape
    tm = ROUTE_TILE
    n_exp = soff.shape[0] // (T // tm)
    grid_spec = pltpu.PrefetchScalarGridSpec(
        num_scalar_prefetch=3,
        grid=(T // tm,),
        in_specs=[
            pl.BlockSpec((tm, TOP_K), lambda i, *_: (i, 0)),
            pl.BlockSpec((tm, TOP_K), lambda i, *_: (i, 0)),
            pl.BlockSpec((tm, D), lambda i, *_: (i, 0)),
            pl.BlockSpec(memory_space=pl.ANY),
        ],
        out_specs=pl.BlockSpec((tm, D), lambda i, *_: (i, 0)),
        scratch_shapes=[pltpu.VMEM((2, slots, D // 2), jnp.uint32),
                        pltpu.SemaphoreType.DMA((2,))],
    )
    return pl.pallas_call(
        functools.partial(_combine_kernel, n_exp=n_exp, slots=slots),
        grid_spec=grid_spec,
        out_shape=jax.ShapeDtypeStruct((T, D), F32),
        compiler_params=pltpu.CompilerParams(
            dimension_semantics=("arbitrary",), vmem_limit_bytes=VMEM_LIMIT),
        name="moe_combine",
    )(soff, rlen, rdst, slot_row, gates_row, x2, y_rows)


def _block_diag_ones(n, blk):
    i = jnp.arange(n)
    return ((i[:, None] // blk) == (i[None, :] // blk)).astype(BF16)


def _layer(x, norm1_g, w_in, tshift_mu, q_norm_g, k_norm_g, rel_bias, w0, w_decay_up, a0,
           w_aaa_up, w_gate_up, k_k, k_a, r_k, ln_x_g, ln_x_b, w_out, norm2_g, w_router,
           b_router, w_up, b_up, w_down, b_down):
    B, S, D = x.shape
    T = B * S
    att_heads = rel_bias.shape[0]
    att_w = att_heads * HEAD_DIM
    width = w0.shape[0]
    n_exp = w_router.shape[1]
    assert att_w % LANES == 0 and width % GROUP == 0 and S % ATT_QROWS == 0
    assert T % ROUTE_TILE == 0

    x2d = x.reshape(T, D)
    row = lambda a: a.reshape(1, -1).astype(F32)
    bd_att = _block_diag_ones(att_w, HEAD_DIM)
    bd_rwkv = _block_diag_ones(width, HEAD_DIM)

    qkv, pr = _inproj(
        x2d, row(norm1_g), w_in.astype(BF16),
        row(jnp.tile(q_norm_g, att_heads)), row(jnp.tile(k_norm_g, att_heads)), bd_att, att_w)

    att = _attention(qkv, _attention_bias(rel_bias), B, S, att_w)

    lora = jnp.zeros((DECAY_LORA + AAA_LORA, 2 * width), F32)
    lora = lora.at[:DECAY_LORA, :width].set(w_decay_up).at[DECAY_LORA:, width:].set(w_aaa_up)
    rwk = _rwkv(pr, row(tshift_mu), row(w0), row(a0), lora.astype(BF16),
                w_gate_up.astype(BF16), row(k_k), row(k_a), row(r_k), row(ln_x_g),
                row(ln_x_b), bd_rwkv, B, S, width)

    wrt = w_router.T.astype(F32)
    wrt_hi = wrt.astype(BF16)
    wrt_lo = (wrt - wrt_hi.astype(F32)).astype(BF16)
    x2, xn, eidx, rank, gates, counts = _outproj(
        att, rwk, x2d, w_out[:att_w].astype(BF16), w_out[att_w:].astype(BF16), row(norm2_g),
        jnp.stack([wrt_hi, wrt_lo]), b_router.reshape(n_exp, 1).astype(F32), n_exp)

    tm = ROUTE_TILE
    n_tiles = T // tm
    i32 = lambda a: a.astype(jnp.int32)
    run_n = counts.reshape(n_tiles, n_exp, LANES)[:, :, 0]
    run_len = (run_n + RUN_ALIGN - 1) // RUN_ALIGN * RUN_ALIGN
    run_off = jnp.cumsum(run_len, axis=1) - run_len
    exp_rows = jnp.sum(run_len, axis=0)
    exp_pad = (exp_rows + MOE_ROWS - 1) // MOE_ROWS * MOE_ROWS
    exp_end = jnp.cumsum(exp_pad)
    exp_start = exp_end - exp_pad
    run_dst = exp_start[None, :] + jnp.cumsum(run_len, axis=0) - run_len
    eidx_t = eidx.reshape(TOP_K, n_tiles, tm)
    experts = jnp.arange(n_exp, dtype=jnp.int32)[:, None, None, None]
    slot = rank.reshape(TOP_K, n_tiles, tm) + jnp.sum(
        jnp.where(eidx_t[None] == experts, run_off.T[:, None, :, None], 0), axis=0)
    slot_lane = i32(slot.transpose(1, 0, 2))
    slot_row = i32(slot.transpose(1, 2, 0).reshape(T, TOP_K))
    slots = -(-(tm * TOP_K + n_exp * (RUN_ALIGN - 1)) // GROUP) * GROUP
    n_blocks = -(-(T * TOP_K + n_tiles * n_exp * (RUN_ALIGN - 1)) // MOE_ROWS) + n_exp
    blk_start = jnp.arange(n_blocks, dtype=jnp.int32) * MOE_ROWS
    blk_expert = i32(jnp.minimum(
        jnp.sum(blk_start[:, None] >= exp_end[None, :], axis=1), n_exp - 1))
    n_active = i32(exp_end[-1:] // MOE_ROWS)
    soff, rlen, rdst = (i32(a.reshape(-1)) for a in (run_off, run_len, run_dst))

    xs = _dispatch(soff, rlen, rdst, i32(exp_start + exp_rows), i32(exp_pad - exp_rows),
                   n_active, slot_lane, xn, n_blocks, slots)
    y_rows = _moe(blk_expert, n_active, xs, w_up, b_up, w_down, b_down)
    out = _combine(soff, rlen, rdst, slot_row, gates.T, x2, y_rows, slots)
    return out.reshape(B, S, D)


def kernel(x, norm1_g, w_in, tshift_mu, q_norm_g, k_norm_g, rel_bias, w0, w_decay_up, a0,
           w_aaa_up, w_gate_up, k_k, k_a, r_k, ln_x_g, ln_x_b, w_out, norm2_g, w_router,
           b_router, w_up, b_up, w_down, b_down):
    params = (norm1_g, w_in, tshift_mu, q_norm_g, k_norm_g, rel_bias, w0, w_decay_up, a0,
              w_aaa_up, w_gate_up, k_k, k_a, r_k, ln_x_g, ln_x_b, w_out, norm2_g, w_router,
              b_router, w_up, b_up, w_down, b_down)
    for l in range(norm1_g.shape[0]):
        x = _layer(x, *[p[l] for p in params])
    return x
```

```python
import functools
import math

import jax
import jax.numpy as jnp
from jax import lax
from jax.experimental import pallas as pl
from jax.experimental.pallas import tpu as pltpu

F32 = jnp.float32
BF16 = jnp.bfloat16

CHUNK = 64
HEAD_DIM = 64
LEFT_CHUNKS = 8
MAX_LEFT_REL = 128
TOP_K = 4
SWIGLU_ALPHA = 1.702
SWIGLU_LIMIT = 7.0
NORM_EPS = 1e-6
LN_X_EPS = 64e-5
DECAY_LORA = 64
AAA_LORA = 64
GATE_LORA = 128

LANES = 128
GROUP = 256
NEG_BIG = -1e30

VMEM_LIMIT = 56 * 1024 * 1024

ATT_QROWS = 1024
ATT_SUB = 256
ATT_HALO = LEFT_CHUNKS * CHUNK
ATT_KEYS = ATT_SUB + ATT_HALO
ATT_MASKED_TILES = ATT_HALO // ATT_SUB
ATT_INTERLEAVE = 4

LOCAL_UNROLL = 2

MOE_ROWS = 1024
ROUTE_TILE = 512
RUN_ALIGN = 8
RUN_SIZES = tuple(ROUTE_TILE >> s for s in range((ROUTE_TILE // RUN_ALIGN).bit_length()))
assert MOE_ROWS <= 2 * RUN_SIZES[0] and RUN_SIZES[-1] == RUN_ALIGN


def _dot(a, b):
    return jnp.dot(a, b, preferred_element_type=F32)


def _dot_nt(a, b):
    return lax.dot_general(a, b, (((1,), (1,)), ((), ())), preferred_element_type=F32)


def _head_sums(a, block_ones):
    return _dot(a.astype(BF16), block_ones)


def _pack_bf16_pairs(x):
    n = x.shape[1] // 2
    bits = pltpu.bitcast(x.astype(BF16).astype(F32), jnp.uint32)
    return bits[:, :n] | (bits[:, n:] >> 16)


def _unpack_bf16_pairs(w):
    hi = pltpu.bitcast(w & jnp.uint32(0xFFFF0000), F32)
    lo = pltpu.bitcast(w << 16, F32)
    return jnp.concatenate([hi, lo], axis=1).astype(BF16)


def _sigmoid(x):
    return 0.5 * jnp.tanh(0.5 * x) + 0.5


def _inproj_kernel(x_ref, g_ref, w_ref, qg_ref, kg_ref, bd_ref, qkv_ref, pr_ref, *, att_w):
    x = x_ref[...]
    ms = jnp.mean(x * x, axis=-1, keepdims=True)
    h = (x * lax.rsqrt(ms + NORM_EPS) * g_ref[...]).astype(BF16)
    qkv = _dot(h, w_ref[:, : 3 * att_w])
    bd = bd_ref[...]

    def head_norm(t, g):
        ss = _head_sums(t * t, bd)
        return t * lax.rsqrt(ss * (1.0 / HEAD_DIM) + NORM_EPS) * g

    q = head_norm(qkv[:, :att_w], qg_ref[...]) * (HEAD_DIM ** -0.5)
    k = head_norm(qkv[:, att_w: 2 * att_w], kg_ref[...])
    qkv_ref[:, :att_w] = q.astype(BF16)
    qkv_ref[:, att_w: 2 * att_w] = k.astype(BF16)
    qkv_ref[:, 2 * att_w:] = qkv[:, 2 * att_w:].astype(BF16)
    pr_ref[...] = _dot(h, w_ref[:, 3 * att_w:]).astype(BF16)


def _inproj(x2d, g1, w_in_bf, qg, kg, bd, att_w, tm=512):
    T, D = x2d.shape
    in_cols = w_in_bf.shape[1]
    r_cols = in_cols - 3 * att_w
    const = lambda i: (0, 0)
    return pl.pallas_call(
        functools.partial(_inproj_kernel, att_w=att_w),
        grid=(T // tm,),
        in_specs=[
            pl.BlockSpec((tm, D), lambda i: (i, 0)),
            pl.BlockSpec((1, D), const),
            pl.BlockSpec((D, in_cols), const),
            pl.BlockSpec((1, att_w), const),
            pl.BlockSpec((1, att_w), const),
            pl.BlockSpec((att_w, att_w), const),
        ],
        out_specs=[
            pl.BlockSpec((tm, 3 * att_w), lambda i: (i, 0)),
            pl.BlockSpec((tm, r_cols), lambda i: (i, 0)),
        ],
        out_shape=[
            jax.ShapeDtypeStruct((T, 3 * att_w), BF16),
            jax.ShapeDtypeStruct((T, r_cols), BF16),
        ],
        compiler_params=pltpu.CompilerParams(
            dimension_semantics=("arbitrary",), vmem_limit_bytes=VMEM_LIMIT),
        name="inproj",
    )(x2d, g1, w_in_bf, qg, kg, bd)


def _attn_kernel(q_ref, kp_ref, kc_ref, vp_ref, vc_ref, bias_ref, o_ref, kbuf, vbuf):
    i = pl.program_id(2)
    kbuf[:ATT_HALO, :] = kp_ref[...]
    kbuf[ATT_HALO:, :] = kc_ref[...]
    lane_head = lax.broadcasted_iota(jnp.int32, (1, LANES), 1) >> (HEAD_DIM.bit_length() - 1)
    one = jnp.ones((), BF16)
    for h in range(2):
        vbuf[h, :ATT_HALO, :] = jnp.where(lane_head == h, vp_ref[...], one)
        vbuf[h, ATT_HALO:, :] = jnp.where(lane_head == h, vc_ref[...], one)
    tiles = range(ATT_QROWS // ATT_SUB)
    all_units = [(qs * ATT_SUB, h * HEAD_DIM, h, qs) for qs in tiles for h in range(2)]
    ext = []
    for u0 in range(0, len(all_units), ATT_INTERLEAVE):
        units = all_units[u0: u0 + ATT_INTERLEAVE]
        s = [_dot_nt(q_ref[r0: r0 + ATT_SUB, lo: lo + HEAD_DIM],
                     kbuf[r0: r0 + ATT_KEYS, lo: lo + HEAD_DIM])
             + bias_ref[h, jnp.where(i == 0, 1 + qs, 0) if qs < ATT_MASKED_TILES else 0]
             for (r0, lo, h, qs) in units]
        m = [jnp.max(x, axis=-1, keepdims=True) for x in s]
        p = [jnp.exp(x - mx).astype(BF16) for x, mx in zip(s, m)]
        ext += [_dot(x, vbuf[h, r0: r0 + ATT_KEYS, :]) for x, (r0, _, h, _) in zip(p, units)]
    for qs in tiles:
        e0, e1 = ext[2 * qs], ext[2 * qs + 1]
        o_ref[qs * ATT_SUB: (qs + 1) * ATT_SUB, :] = jnp.where(
            lane_head == 0, e0 / pltpu.roll(e0, HEAD_DIM, axis=1),
            e1 / pltpu.roll(e1, HEAD_DIM, axis=1)).astype(BF16)


def _attention(qkv, bias, B, S, att_w):
    T = B * S
    n_i = S // ATT_QROWS
    n_hp = att_w // LANES
    heads_per = LANES // HEAD_DIM
    assert heads_per == 2

    def cur(col0):
        return lambda b, hp, i: (b * n_i + i, col0 + hp)

    halo_per_blk = ATT_QROWS // ATT_HALO

    def prev(col0):
        return lambda b, hp, i: (
            b * n_i * halo_per_blk + jnp.maximum(i * halo_per_blk - 1, 0), col0 + hp)

    blk = (ATT_QROWS, LANES)
    halo = (ATT_HALO, LANES)
    return pl.pallas_call(
        _attn_kernel,
        grid=(B, n_hp, n_i),
        in_specs=[
            pl.BlockSpec(blk, cur(0)),
            pl.BlockSpec(halo, prev(n_hp)),
            pl.BlockSpec(blk, cur(n_hp)),
            pl.BlockSpec(halo, prev(2 * n_hp)),
            pl.BlockSpec(blk, cur(2 * n_hp)),
            pl.BlockSpec((heads_per,) + bias.shape[1:], lambda b, hp, i: (hp, 0, 0, 0)),
        ],
        out_specs=pl.BlockSpec(blk, lambda b, hp, i: (b * n_i + i, hp)),
        out_shape=jax.ShapeDtypeStruct((T, att_w), BF16),
        scratch_shapes=[
            pltpu.VMEM((ATT_HALO + ATT_QROWS, LANES), BF16),
            pltpu.VMEM((2, ATT_HALO + ATT_QROWS, LANES), BF16),
        ],
        compiler_params=pltpu.CompilerParams(
            dimension_semantics=("arbitrary", "arbitrary", "arbitrary"),
            vmem_limit_bytes=VMEM_LIMIT),
        name="band_attention",
    )(qkv, qkv, qkv, qkv, qkv, bias)


def _attention_bias(rel_bias):
    H = rel_bias.shape[0]
    table = rel_bias.astype(F32)
    far = LEFT_CHUNKS * CHUNK + ATT_SUB - 1
    n_left = far - MAX_LEFT_REL + 1
    n_ext = ATT_SUB + ATT_KEYS - 1
    ext = jnp.concatenate([
        jnp.broadcast_to(table[:, :1], (H, n_left)), table[:, 1:],
        jnp.broadcast_to(table[:, -1:], (H, n_ext - n_left - (table.shape[1] - 1)))], axis=1)
    flat = jnp.tile(ext, (1, ATT_SUB + 1))[:, ATT_SUB - 1: ATT_SUB - 1 + ATT_SUB * (n_ext - 1)]
    bias = flat.reshape(H, ATT_SUB, n_ext - 1)[:, :, :ATT_KEYS]
    r = jnp.arange(ATT_SUB)[:, None]
    c = jnp.arange(ATT_KEYS)[None, :]
    band = c // CHUNK - r // CHUNK
    ok = (band >= 0) & (band <= LEFT_CHUNKS)
    variants = [ok] + [ok & (c >= ATT_HALO - qs * ATT_SUB) for qs in range(ATT_MASKED_TILES)]
    return jnp.where(jnp.stack(variants)[None], bias[:, None], NEG_BIG)


def _rwkv_kernel(p_ref, pprev_ref, mu_ref, w0_ref, a0_ref, lora_ref, wg_ref, kk_ref, ka_ref,
                 rk_ref, lng_ref, lnb_ref, bd_ref, tri_ref, o_ref,
                 rt_s, at_s, bt_s, kt_s, bh_s, kh_s, v_s, wl_s, y_s, z_s, g_s, y0_s, m_s, n_s,
                 *, width, tm):
    i = pl.program_id(1)
    n_groups = width // GROUP
    heads_per_group = GROUP // HEAD_DIM

    @pl.when(i == 0)
    def _():
        z_s[...] = jnp.zeros_like(z_s)

    p = p_ref[...].astype(F32)
    first = jnp.where(i > 0, 1.0, 0.0)
    prev_row = pprev_ref[7:8, :].astype(F32) * first
    row = lax.broadcasted_iota(jnp.int32, (tm, 1), 0)
    prev = jnp.where(row == 0, prev_row, pltpu.roll(p, 1, axis=0))
    ps = p + (prev - p) * mu_ref[...]

    r = ps[:, :width]
    k = ps[:, width: 2 * width]
    v = ps[:, 2 * width: 3 * width]
    wa = ps[:, 3 * width: 3 * width + DECAY_LORA + AAA_LORA]
    gd = ps[:, 3 * width + DECAY_LORA + AAA_LORA:]
    lane = lax.broadcasted_iota(jnp.int32, (1, DECAY_LORA + AAA_LORA), 1)
    wa = jnp.where(lane < DECAY_LORA, jnp.tanh(wa), wa)
    lw = _dot(wa.astype(BF16), lora_ref[...])
    logw = -math.exp(-0.5) * _sigmoid(w0_ref[...] + lw[:, :width])
    eta = _sigmoid(a0_ref[...] + lw[:, width:])
    g = _dot(_sigmoid(gd).astype(BF16), wg_ref[...])

    bd = bd_ref[...]
    kk = k * kk_ref[...]
    kk = kk * lax.rsqrt(jnp.maximum(_head_sums(kk * kk, bd), 1e-24))
    km = k * (1.0 + (eta - 1.0) * ka_ref[...])
    bb = kk * eta

    tri = tri_ref[...]
    cs_chunks = [_split_dot_left(tri, logw[c * CHUNK: (c + 1) * CHUNK]) for c in range(tm // CHUNK)]
    cs = jnp.concatenate(cs_chunks, axis=0)
    tot = jnp.concatenate(
        [jnp.broadcast_to(x[CHUNK - 1:], (CHUNK, width)) for x in cs_chunks], axis=0)
    e_pos = jnp.exp(cs)
    e_neg = jnp.exp(-cs)
    e_tail = jnp.exp(tot - cs)
    rt_s[...] = r * e_pos
    at_s[...] = -kk * jnp.exp(cs - logw)
    bt_s[...] = bb * e_neg
    kt_s[...] = km * e_neg
    bh_s[...] = bb * e_tail
    kh_s[...] = km * e_tail
    v_s[...] = v
    wl_s[...] = jnp.exp(tot)

    rr = lax.broadcasted_iota(jnp.int32, (GROUP, GROUP), 0)
    cc = lax.broadcasted_iota(jnp.int32, (GROUP, GROUP), 1)
    shift = CHUNK.bit_length() - 1
    same = (rr >> shift) == (cc >> shift)
    strict = same & (rr > cc)
    incl = same & (rr >= cc)
    eye = jnp.where(rr == cc, 1.0, 0.0)
    lane_head = lax.broadcasted_iota(jnp.int32, (1, GROUP), 1) >> shift

    def stack_masked(x):
        return jnp.concatenate(
            [jnp.where(lane_head == h, x, 0.0) for h in range(heads_per_group)], axis=0)

    def stack_plain(x):
        return jnp.concatenate([x] * heads_per_group, axis=0)

    def bdot(a, b):
        return _dot(a.astype(BF16), b.astype(BF16))

    def bdot_nt(a, b):
        return _dot_nt(a.astype(BF16), b.astype(BF16))

    def local_body(it, carry):
        units = []
        for u in range(LOCAL_UNROLL):
            c = it * LOCAL_UNROLL + u
            rows = pl.ds(pl.multiple_of(c * CHUNK, CHUNK), CHUNK)
            units += [(c, gi, rows, slice(gi * GROUP, (gi + 1) * GROUP)) for gi in range(n_groups)]

        def load(ref, stack):
            return [stack(ref[rows, cols]) for (_, _, rows, cols) in units]

        def each(f, *lists):
            return [f(*args) for args in zip(*lists)]

        b16 = lambda xs: [x.astype(BF16) for x in xs]
        rt4 = load(rt_s, stack_masked)
        at4, rt4b, v4 = b16(load(at_s, stack_masked)), b16(rt4), load(v_s, stack_masked)
        v4b, bh4 = b16(v4), b16(load(bh_s, stack_masked))
        kh4 = b16(load(kh_s, stack_masked))
        bt_rep, kt_rep = b16(load(bt_s, stack_plain)), b16(load(kt_s, stack_plain))
        a_ab = each(lambda a, b: jnp.where(strict, _dot_nt(a, b), 0.0), at4, bt_rep)
        a_ak = b16(each(lambda a, b: jnp.where(strict, _dot_nt(a, b), 0.0), at4, kt_rep))
        a_rb = b16(each(lambda a, b: jnp.where(incl, _dot_nt(a, b), 0.0), rt4b, bt_rep))
        a_rk = b16(each(lambda a, b: jnp.where(incl, _dot_nt(a, b), 0.0), rt4b, kt_rep))
        tinv = [eye + a for a in a_ab]
        pw = b16(a_ab)
        span = 2
        while span < CHUNK:
            pw = b16(each(_dot, pw, pw))
            tinv = each(lambda t, p: t + _dot(p, t.astype(BF16)), tinv, pw)
            span *= 2
        tinv = b16(tinv)
        p4 = each(_dot, tinv, at4)
        q4 = each(_dot, tinv, b16(each(_dot, a_ak, v4b)))
        p4b, q4b = b16(p4), b16(q4)
        g4 = each(lambda r, a, p: r + _dot(a, p), rt4, a_rb, p4b)
        y0 = each(lambda a, q, ak, v: _dot(a, q) + _dot(ak, v), a_rb, q4b, a_rk, v4b)
        m4 = each(lambda p, b: _dot(p.T.astype(BF16), b), p4, bh4)
        n4 = each(lambda q, b, v, k: _dot(q.T.astype(BF16), b) + _dot(v.T.astype(BF16), k),
                  q4, bh4, v4, kh4)
        for (c, gi, _, _), g, y, m, n in zip(units, g4, y0, m4, n4):
            g_s[c, gi] = g.astype(BF16)
            y0_s[c, gi] = y
            m_s[c, gi] = m.astype(BF16)
            n_s[c, gi] = n
        return carry

    lax.fori_loop(0, tm // CHUNK // LOCAL_UNROLL, local_body, 0)

    def carry_body(c, carry):
        r0 = pl.multiple_of(c * CHUNK, CHUNK)
        rows = pl.ds(r0, CHUNK)
        for gi in range(n_groups):
            cols = slice(gi * GROUP, (gi + 1) * GROUP)
            zt = z_s[gi]
            zb = zt.astype(BF16)
            y4 = _dot_nt(g_s[c, gi], zb) + y0_s[c, gi]
            z_s[gi] = zt * wl_s[pl.ds(r0, 1), cols] + _dot(zb, m_s[c, gi]) + n_s[c, gi]
            ytok = y4[:CHUNK]
            for h in range(1, heads_per_group):
                ytok = ytok + y4[h * CHUNK: (h + 1) * CHUNK]
            y_s[rows, cols] = ytok
        return carry

    lax.fori_loop(0, tm // CHUNK, carry_body, 0, unroll=True)

    y = y_s[...]
    inv_n = 1.0 / HEAD_DIM
    mean = _head_sums(y, bd) * inv_n
    d = y - mean
    var = _head_sums(d * d, bd) * inv_n
    yn = d * lax.rsqrt(var + LN_X_EPS) * lng_ref[...] + lnb_ref[...]
    bonus = _head_sums(r * km * rk_ref[...], bd) * v
    o_ref[...] = ((yn + bonus) * g).astype(BF16)


def _split_dot_left(a_exact, b, terms=2):
    acc = None
    rem = b
    for _ in range(terms):
        part = rem.astype(BF16)
        rem = rem - part.astype(F32)
        d = _dot(a_exact, part)
        acc = d if acc is None else acc + d
    return acc


def _rwkv(pr, mu, w0, a0, lora, wg, k_k, k_a, r_k, ln_g, ln_b, bd, B, S, width, tm=512):
    T = B * S
    n_t = S // tm
    cols = pr.shape[1]
    ti = jnp.arange(CHUNK)
    tri = (ti[:, None] >= ti[None, :]).astype(BF16)
    const = lambda b, i: (0, 0)
    vec = lambda n: pl.BlockSpec((1, n), const)
    scr = lambda: pltpu.VMEM((tm, width), F32)
    return pl.pallas_call(
        functools.partial(_rwkv_kernel, width=width, tm=tm),
        grid=(B, n_t),
        in_specs=[
            pl.BlockSpec((tm, cols), lambda b, i: (b * n_t + i, 0)),
            pl.BlockSpec((8, cols), lambda b, i: (jnp.maximum((b * n_t + i) * (tm // 8) - 1, 0), 0)),
            vec(cols), vec(width), vec(width),
            pl.BlockSpec(lora.shape, const),
            pl.BlockSpec(wg.shape, const),
            vec(width), vec(width), vec(width), vec(width), vec(width),
            pl.BlockSpec((width, width), const),
            pl.BlockSpec((CHUNK, CHUNK), const),
        ],
        out_specs=pl.BlockSpec((tm, width), lambda b, i: (b * n_t + i, 0)),
        out_shape=jax.ShapeDtypeStruct((T, width), BF16),
        scratch_shapes=[scr() for _ in range(9)]
        + [pltpu.VMEM((width // GROUP, GROUP, GROUP), F32)]
        + [pltpu.VMEM((tm // CHUNK, width // GROUP, GROUP, GROUP), dt)
           for dt in (BF16, F32, BF16, F32)],
        compiler_params=pltpu.CompilerParams(
            dimension_semantics=("arbitrary", "arbitrary"), vmem_limit_bytes=VMEM_LIMIT),
        name="rwkv7",
    )(pr, pr, mu, w0, a0, lora, wg, k_k, k_a, r_k, ln_g, ln_b, bd, tri)


def _outproj_kernel(att_ref, rwk_ref, x_ref, wa_ref, wr_ref, g_ref, wrt_ref, br_ref, upper_ref,
                    x2_ref, xn_ref, eidx_ref, rank_ref, gate_ref, cnt_ref, *, n_exp, tm):
    x2 = x_ref[...] + _dot(att_ref[...], wa_ref[...]) + _dot(rwk_ref[...], wr_ref[...])
    x2_ref[...] = x2
    ms = jnp.mean(x2 * x2, axis=-1, keepdims=True)
    xn = x2 * lax.rsqrt(ms + NORM_EPS) * g_ref[...]
    xn_ref[...] = xn.astype(BF16)

    x_hi = xn.astype(BF16)
    x_lo = (xn - x_hi.astype(F32)).astype(BF16)
    logits = (_dot_nt(wrt_ref[0], x_hi) + _dot_nt(wrt_ref[0], x_lo)
              + _dot_nt(wrt_ref[1], x_hi)) + br_ref[...]

    eio = lax.broadcasted_iota(jnp.int32, (n_exp, tm), 0)
    work = logits
    vals, onehots = [], []
    for k in range(TOP_K):
        m = jnp.max(work, axis=0, keepdims=True)
        idx = jnp.min(jnp.where(work == m, eio, n_exp), axis=0, keepdims=True)
        oh = eio == idx
        vals.append(m)
        onehots.append(oh)
        eidx_ref[k: k + 1, :] = idx
        work = jnp.where(oh, -jnp.inf, work)

    exps = [jnp.exp(vk - vals[0]) for vk in vals]
    denom = exps[0]
    for e in exps[1:]:
        denom = denom + e
    for k in range(TOP_K):
        gate_ref[k: k + 1, :] = exps[k] / denom

    sel = onehots[0]
    for oh in onehots[1:]:
        sel = sel | oh
    sel_f = jnp.where(sel, 1.0, 0.0)
    before = _dot(sel_f.astype(BF16), upper_ref[...])
    for k in range(TOP_K):
        rank_ref[k: k + 1, :] = jnp.sum(
            jnp.where(onehots[k], before, 0.0), axis=0, keepdims=True).astype(jnp.int32)
    cnt_ref[...] = jnp.broadcast_to(
        jnp.sum(sel_f, axis=1, keepdims=True), cnt_ref.shape).astype(jnp.int32)


def _outproj(att, rwk, x2d, w_att, w_rwk, g2, wrt, br, n_exp):
    T, D = x2d.shape
    tm = ROUTE_TILE
    upper = (jnp.arange(tm)[:, None] < jnp.arange(tm)[None, :]).astype(BF16)
    const = lambda i: (0, 0)
    rows = lambda n: pl.BlockSpec((tm, n), lambda i: (i, 0))
    lanes = pl.BlockSpec((TOP_K, tm), lambda i: (0, i))
    return pl.pallas_call(
        functools.partial(_outproj_kernel, n_exp=n_exp, tm=tm),
        grid=(T // tm,),
        in_specs=[
            rows(att.shape[1]), rows(rwk.shape[1]), rows(D),
            pl.BlockSpec(w_att.shape, const), pl.BlockSpec(w_rwk.shape, const),
            pl.BlockSpec((1, D), const),
            pl.BlockSpec(wrt.shape, lambda i: (0, 0, 0)),
            pl.BlockSpec((n_exp, 1), const),
            pl.BlockSpec((tm, tm), const),
        ],
        out_specs=[rows(D), rows(D), lanes, lanes, lanes,
                   pl.BlockSpec((n_exp, LANES), lambda i: (i, 0))],
        out_shape=[
            jax.ShapeDtypeStruct((T, D), F32),
            jax.ShapeDtypeStruct((T, D), BF16),
            jax.ShapeDtypeStruct((TOP_K, T), jnp.int32),
            jax.ShapeDtypeStruct((TOP_K, T), jnp.int32),
            jax.ShapeDtypeStruct((TOP_K, T), F32),
            jax.ShapeDtypeStruct((T // tm * n_exp, LANES), jnp.int32),
        ],
        compiler_params=pltpu.CompilerParams(
            dimension_semantics=("arbitrary",), vmem_limit_bytes=VMEM_LIMIT),
        name="outproj_router",
    )(att, rwk, x2d, w_att, w_rwk, g2, wrt, br, upper)


def _pow2_sizes(limit):
    top = 1 << (limit.bit_length() - 1)
    return tuple(top >> s for s in range((top // RUN_ALIGN).bit_length()))


def _run_copies(action, length, src, src_off, dst, dst_off, sem, sizes=RUN_SIZES):
    for size in sizes:
        start = length & ~(2 * size - 1)

        @pl.when((length & size) != 0)
        def _():
            cp = pltpu.make_async_copy(
                src.at[pl.ds(pl.multiple_of(src_off + start, RUN_ALIGN), size)],
                dst.at[pl.ds(pl.multiple_of(dst_off + start, RUN_ALIGN), size)], sem)
            cp.start() if action == "start" else cp.wait()


def _dispatch_kernel(soff_ref, rlen_ref, rdst_ref, tail_dst_ref, tail_len_ref, nact_ref,
                     slot_ref, xn_ref, xs_hbm, buf, zbuf, sem, run_sems, *, n_exp, slots, n_blocks):
    j = pl.program_id(0)
    tm = ROUTE_TILE

    @pl.when(j == 0)
    def _():
        zbuf[...] = jnp.zeros_like(zbuf)

        def unused_block(action):
            def body(b, c):
                cp = pltpu.make_async_copy(
                    zbuf, xs_hbm.at[pl.ds(pl.multiple_of(b * MOE_ROWS, MOE_ROWS), MOE_ROWS)], sem)
                cp.start() if action == "start" else cp.wait()
                return c
            return body

        for action in ("start", "wait"):
            for e in range(n_exp):
                _run_copies(action, tail_len_ref[e], zbuf, 0, xs_hbm, tail_dst_ref[e], sem)
            lax.fori_loop(nact_ref[0], n_blocks, unused_block(action), 0)

    cur = lax.rem(j, 2)
    x = xn_ref[...]
    for sb in range(slots // GROUP):
        sio = lax.broadcasted_iota(jnp.int32, (GROUP, tm), 0) + sb * GROUP
        p = jnp.zeros((GROUP, tm), F32)
        for k in range(TOP_K):
            p = jnp.where(sio == slot_ref[0, k: k + 1, :], 1.0, p)
        buf[cur, sb * GROUP: (sb + 1) * GROUP, :] = _pack_bf16_pairs(_dot(p.astype(BF16), x))

    def start_runs(tile, half):
        for e in range(n_exp):
            r = tile * n_exp + e
            _run_copies("start", rlen_ref[r], buf.at[half], soff_ref[r], xs_hbm, rdst_ref[r],
                        run_sems.at[half])

    def wait_runs(tile, half):
        last = tile * n_exp + n_exp - 1
        _run_copies("wait", soff_ref[last] + rlen_ref[last], buf.at[half], 0, xs_hbm, 0,
                    run_sems.at[half], sizes=_pow2_sizes(slots))

    start_runs(j, cur)

    @pl.when(j > 0)
    def _():
        wait_runs(j - 1, 1 - cur)

    @pl.when(j == pl.num_programs(0) - 1)
    def _():
        wait_runs(j, cur)


def _dispatch(soff, rlen, rdst, tail_dst, tail_len, n_active, slot_lane, xn, n_blocks, slots):
    T, D = xn.shape
    tm = ROUTE_TILE
    n_exp = tail_dst.shape[0]
    n_rows = n_blocks * MOE_ROWS
    grid_spec = pltpu.PrefetchScalarGridSpec(
        num_scalar_prefetch=6,
        grid=(T // tm,),
        in_specs=[
            pl.BlockSpec((1, TOP_K, tm), lambda i, *_: (i, 0, 0)),
            pl.BlockSpec((tm, D), lambda i, *_: (i, 0)),
        ],
        out_specs=pl.BlockSpec(memory_space=pl.ANY),
        scratch_shapes=[pltpu.VMEM((2, slots, D // 2), jnp.uint32),
                        pltpu.VMEM((MOE_ROWS, D // 2), jnp.uint32),
                        pltpu.SemaphoreType.DMA, pltpu.SemaphoreType.DMA((2,))],
    )
    return pl.pallas_call(
        functools.partial(_dispatch_kernel, n_exp=n_exp, slots=slots, n_blocks=n_blocks),
        grid_spec=grid_spec,
        out_shape=jax.ShapeDtypeStruct((n_rows, D // 2), jnp.uint32),
        compiler_params=pltpu.CompilerParams(
            dimension_semantics=("arbitrary",), vmem_limit_bytes=VMEM_LIMIT),
        name="moe_dispatch",
    )(soff, rlen, rdst, tail_dst, tail_len, n_active, slot_lane, xn)


def _moe_kernel(bexp_ref, nact_ref, x_ref, wup_ref, bup_ref, wdn_ref, bdn_ref,
                y_ref, wup_bf, wdn_bf, *, d_exp):
    i = pl.program_id(0)

    @pl.when(i < nact_ref[0])
    def _():
        new_expert = jnp.logical_or(i == 0, bexp_ref[i] != bexp_ref[jnp.maximum(i - 1, 0)])

        @pl.when(new_expert)
        def _():
            wup_bf[...] = wup_ref[0].astype(BF16)
            wdn_bf[...] = wdn_ref[0].astype(BF16)

        x = _unpack_bf16_pairs(x_ref[...])
        h_glu = _dot(x, wup_bf[:, :d_exp]) + bup_ref[0, :, :d_exp]
        h_lin = _dot(x, wup_bf[:, d_exp:]) + bup_ref[0, :, d_exp:]
        h_glu = jnp.minimum(h_glu, SWIGLU_LIMIT)
        h_lin = jnp.clip(h_lin, -SWIGLU_LIMIT, SWIGLU_LIMIT)
        act = h_glu * _sigmoid(SWIGLU_ALPHA * h_glu) * (h_lin + 1.0)
        y_ref[...] = _pack_bf16_pairs(_dot(act.astype(BF16), wdn_bf[...]) + bdn_ref[0])

    @pl.when(i >= nact_ref[0])
    def _():
        y_ref[...] = jnp.zeros_like(y_ref)


def _moe(blk_expert, n_active, xs, w_up, b_up, w_down, b_down):
    n_blocks = blk_expert.shape[0]
    n_exp, D, two_f = w_up.shape
    d_exp = two_f // 2
    blk = lambda i, na: jnp.minimum(i, na[0] - 1)
    grid_spec = pltpu.PrefetchScalarGridSpec(
        num_scalar_prefetch=2,
        grid=(n_blocks,),
        in_specs=[
            pl.BlockSpec((MOE_ROWS, D // 2), lambda i, be, na: (blk(i, na), 0)),
            pl.BlockSpec((1, D, two_f), lambda i, be, na: (be[blk(i, na)], 0, 0)),
            pl.BlockSpec((1, 1, two_f), lambda i, be, na: (be[blk(i, na)], 0, 0)),
            pl.BlockSpec((1, d_exp, D), lambda i, be, na: (be[blk(i, na)], 0, 0)),
            pl.BlockSpec((1, 1, D), lambda i, be, na: (be[blk(i, na)], 0, 0)),
        ],
        out_specs=pl.BlockSpec((MOE_ROWS, D // 2), lambda i, be, na: (i, 0)),
        scratch_shapes=[
            pltpu.VMEM((D, two_f), BF16),
            pltpu.VMEM((d_exp, D), BF16),
        ],
    )
    return pl.pallas_call(
        functools.partial(_moe_kernel, d_exp=d_exp),
        grid_spec=grid_spec,
        out_shape=jax.ShapeDtypeStruct((n_blocks * MOE_ROWS, D // 2), jnp.uint32),
        compiler_params=pltpu.CompilerParams(
            dimension_semantics=("arbitrary",), vmem_limit_bytes=VMEM_LIMIT),
        name="moe_ffn",
    )(blk_expert, n_active, xs,
      w_up, b_up.reshape(n_exp, 1, two_f), w_down, b_down.reshape(n_exp, 1, D))


def _combine_kernel(soff_ref, rlen_ref, rdst_ref, slot_ref, gate_ref, x2_ref, y_hbm, o_ref,
                    buf, sems, *, n_exp, slots):
    j = pl.program_id(0)
    tm = ROUTE_TILE

    cur = lax.rem(j, 2)

    def start_runs(tile, half):
        for e in range(n_exp):
            r = tile * n_exp + e
            _run_copies("start", rlen_ref[r], y_hbm, rdst_ref[r], buf.at[half], soff_ref[r],
                        sems.at[half])

    @pl.when(j == 0)
    def _():
        buf[...] = jnp.zeros_like(buf)
        start_runs(0, 0)

    @pl.when(j + 1 < pl.num_programs(0))
    def _():
        start_runs(j + 1, 1 - cur)

    last = j * n_exp + n_exp - 1
    _run_copies("wait", soff_ref[last] + rlen_ref[last], y_hbm, 0, buf.at[cur], 0, sems.at[cur],
                sizes=_pow2_sizes(slots))

    slot = slot_ref[...]
    gate = gate_ref[...]
    acc = x2_ref[...]
    for sb in range(slots // GROUP):
        lio = lax.broadcasted_iota(jnp.int32, (tm, GROUP), 1) + sb * GROUP
        w = jnp.zeros((tm, GROUP), F32)
        for k in range(TOP_K):
            w = jnp.where(lio == slot[:, k: k + 1], gate[:, k: k + 1], w)
        acc = acc + _dot(w.astype(BF16),
                         _unpack_bf16_pairs(buf[cur, sb * GROUP: (sb + 1) * GROUP, :]))
    o_ref[...] = acc


def _combine(soff, rlen, rdst, slot_row, gates_row, x2, y_rows, slots):
    T, D = x2.shape
    tm = ROUTE_TILE
    n_exp = soff.shape[0] // (T // tm)
    grid_spec = pltpu.PrefetchScalarGridSpec(
        num_scalar_prefetch=3,
        grid=(T // tm,),
        in_specs=[
            pl.BlockSpec((tm, TOP_K), lambda i, *_: (i, 0)),
            pl.BlockSpec((tm, TOP_K), lambda i, *_: (i, 0)),
            pl.BlockSpec((tm, D), lambda i, *_: (i, 0)),
            pl.BlockSpec(memory_space=pl.ANY),
        ],
        out_specs=pl.BlockSpec((tm, D), lambda i, *_: (i, 0)),
        scratch_shapes=[pltpu.VMEM((2, slots, D // 2), jnp.uint32),
                        pltpu.SemaphoreType.DMA((2,))],
    )
    return pl.pallas_call(
        functools.partial(_combine_kernel, n_exp=n_exp, slots=slots),
        grid_spec=grid_spec,
        out_shape=jax.ShapeDtypeStruct((T, D), F32),
        compiler_params=pltpu.CompilerParams(
            dimension_semantics=("arbitrary",), vmem_limit_bytes=VMEM_LIMIT),
        name="moe_combine",
    )(soff, rlen, rdst, slot_row, gates_row, x2, y_rows)


def _block_diag_ones(n, blk):
    i = jnp.arange(n)
    return ((i[:, None] // blk) == (i[None, :] // blk)).astype(BF16)


def _layer(x, norm1_g, w_in, tshift_mu, q_norm_g, k_norm_g, rel_bias, w0, w_decay_up, a0,
           w_aaa_up, w_gate_up, k_k, k_a, r_k, ln_x_g, ln_x_b, w_out, norm2_g, w_router,
           b_router, w_up, b_up, w_down, b_down):
    B, S, D = x.shape
    T = B * S
    att_heads = rel_bias.shape[0]
    att_w = att_heads * HEAD_DIM
    width = w0.shape[0]
    n_exp = w_router.shape[1]
    assert att_w % LANES == 0 and width % GROUP == 0
    assert S % ATT_QROWS == 0 and ATT_QROWS % ATT_HALO == 0 and ATT_HALO % ATT_SUB == 0
    assert T % ROUTE_TILE == 0

    x2d = x.reshape(T, D)
    row = lambda a: a.reshape(1, -1).astype(F32)
    bd_att = _block_diag_ones(att_w, HEAD_DIM)
    bd_rwkv = _block_diag_ones(width, HEAD_DIM)

    qkv, pr = _inproj(
        x2d, row(norm1_g), w_in.astype(BF16),
        row(jnp.tile(q_norm_g, att_heads)), row(jnp.tile(k_norm_g, att_heads)), bd_att, att_w)

    att = _attention(qkv, _attention_bias(rel_bias), B, S, att_w)

    lora = jnp.zeros((DECAY_LORA + AAA_LORA, 2 * width), F32)
    lora = lora.at[:DECAY_LORA, :width].set(w_decay_up).at[DECAY_LORA:, width:].set(w_aaa_up)
    rwk = _rwkv(pr, row(tshift_mu), row(w0), row(a0), lora.astype(BF16),
                w_gate_up.astype(BF16), row(k_k), row(k_a), row(r_k), row(ln_x_g),
                row(ln_x_b), bd_rwkv, B, S, width)

    wrt = w_router.T.astype(F32)
    wrt_hi = wrt.astype(BF16)
    wrt_lo = (wrt - wrt_hi.astype(F32)).astype(BF16)
    x2, xn, eidx, rank, gates, counts = _outproj(
        att, rwk, x2d, w_out[:att_w].astype(BF16), w_out[att_w:].astype(BF16), row(norm2_g),
        jnp.stack([wrt_hi, wrt_lo]), b_router.reshape(n_exp, 1).astype(F32), n_exp)

    tm = ROUTE_TILE
    n_tiles = T // tm
    i32 = lambda a: a.astype(jnp.int32)
    run_n = counts.reshape(n_tiles, n_exp, LANES)[:, :, 0]
    run_len = (run_n + RUN_ALIGN - 1) // RUN_ALIGN * RUN_ALIGN
    run_off = jnp.cumsum(run_len, axis=1) - run_len
    exp_rows = jnp.sum(run_len, axis=0)
    exp_pad = (exp_rows + MOE_ROWS - 1) // MOE_ROWS * MOE_ROWS
    exp_end = jnp.cumsum(exp_pad)
    exp_start = exp_end - exp_pad
    run_dst = exp_start[None, :] + jnp.cumsum(run_len, axis=0) - run_len
    eidx_t = eidx.reshape(TOP_K, n_tiles, tm)
    experts = jnp.arange(n_exp, dtype=jnp.int32)[:, None, None, None]
    slot = rank.reshape(TOP_K, n_tiles, tm) + jnp.sum(
        jnp.where(eidx_t[None] == experts, run_off.T[:, None, :, None], 0), axis=0)
    slot_lane = i32(slot.transpose(1, 0, 2))
    slot_row = i32(slot.transpose(1, 2, 0).reshape(T, TOP_K))
    slots = -(-(tm * TOP_K + n_exp * (RUN_ALIGN - 1)) // GROUP) * GROUP
    n_blocks = -(-(T * TOP_K + n_tiles * n_exp * (RUN_ALIGN - 1)) // MOE_ROWS) + n_exp
    blk_start = jnp.arange(n_blocks, dtype=jnp.int32) * MOE_ROWS
    blk_expert = i32(jnp.minimum(
        jnp.sum(blk_start[:, None] >= exp_end[None, :], axis=1), n_exp - 1))
    n_active = i32(exp_end[-1:] // MOE_ROWS)
    soff, rlen, rdst = (i32(a.reshape(-1)) for a in (run_off, run_len, run_dst))

    xs = _dispatch(soff, rlen, rdst, i32(exp_start + exp_rows), i32(exp_pad - exp_rows),
                   n_active, slot_lane, xn, n_blocks, slots)
    y_rows = _moe(blk_expert, n_active, xs, w_up, b_up, w_down, b_down)
    out = _combine(soff, rlen, rdst, slot_row, gates.T, x2, y_rows, slots)
    return out.reshape(B, S, D)


def kernel(x, norm1_g, w_in, tshift_mu, q_norm_g, k_norm_g, rel_bias, w0, w_decay_up, a0,
           w_aaa_up, w_gate_up, k_k, k_a, r_k, ln_x_g, ln_x_b, w_out, norm2_g, w_router,
           b_router, w_up, b_up, w_down, b_down):
    params = (norm1_g, w_in, tshift_mu, q_norm_g, k_norm_g, rel_bias, w0, w_decay_up, a0,
              w_aaa_up, w_gate_up, k_k, k_a, r_k, ln_x_g, ln_x_b, w_out, norm2_g, w_router,
              b_router, w_up, b_up, w_down, b_down)
    for l in range(norm1_g.shape[0]):
        x = _layer(x, *[p[l] for p in params])
    return x
```

```python
import functools
import math

import jax
import jax.numpy as jnp
from jax import lax
from jax.experimental import pallas as pl
from jax.experimental.pallas import tpu as pltpu

F32 = jnp.float32
BF16 = jnp.bfloat16

CHUNK = 64
HEAD_DIM = 64
LEFT_CHUNKS = 8
MAX_LEFT_REL = 128
TOP_K = 4
SWIGLU_ALPHA = 1.702
SWIGLU_LIMIT = 7.0
NORM_EPS = 1e-6
LN_X_EPS = 64e-5
DECAY_LORA = 64
AAA_LORA = 64
GATE_LORA = 128

LANES = 128
GROUP = 256
NEG_BIG = -1e30

VMEM_LIMIT = 56 * 1024 * 1024

ATT_QROWS = 2048
ATT_SUB = 256
ATT_HALO = LEFT_CHUNKS * CHUNK
ATT_KEYS = ATT_SUB + ATT_HALO
ATT_MASKED_TILES = ATT_HALO // ATT_SUB
ATT_INTERLEAVE = 4

LOCAL_UNROLL = 2

MOE_ROWS = 512
ROUTE_TILE = 512
RUN_ALIGN = 8
RUN_SIZES = tuple(ROUTE_TILE >> s for s in range((ROUTE_TILE // RUN_ALIGN).bit_length()))
assert MOE_ROWS <= 2 * RUN_SIZES[0] and RUN_SIZES[-1] == RUN_ALIGN


def _dot(a, b):
    return jnp.dot(a, b, preferred_element_type=F32)


def _dot_nt(a, b):
    return lax.dot_general(a, b, (((1,), (1,)), ((), ())), preferred_element_type=F32)


def _head_sums(a, block_ones):
    return _dot(a.astype(BF16), block_ones)


def _pack_bf16_pairs(x):
    n = x.shape[1] // 2
    bits = pltpu.bitcast(x.astype(BF16).astype(F32), jnp.uint32)
    return bits[:, :n] | (bits[:, n:] >> 16)


def _unpack_bf16_pairs(w):
    hi = pltpu.bitcast(w & jnp.uint32(0xFFFF0000), F32)
    lo = pltpu.bitcast(w << 16, F32)
    return jnp.concatenate([hi, lo], axis=1).astype(BF16)


def _sigmoid(x):
    return 0.5 * jnp.tanh(0.5 * x) + 0.5


def _inproj_kernel(x_ref, g_ref, w_ref, qg_ref, kg_ref, bd_ref, qkv_ref, pr_ref, *, att_w):
    x = x_ref[...]
    ms = jnp.mean(x * x, axis=-1, keepdims=True)
    h = (x * lax.rsqrt(ms + NORM_EPS) * g_ref[...]).astype(BF16)
    qkv = _dot(h, w_ref[:, : 3 * att_w])
    bd = bd_ref[...]

    def head_norm(t, g):
        ss = _head_sums(t * t, bd)
        return t * lax.rsqrt(ss * (1.0 / HEAD_DIM) + NORM_EPS) * g

    q = head_norm(qkv[:, :att_w], qg_ref[...]) * (HEAD_DIM ** -0.5)
    k = head_norm(qkv[:, att_w: 2 * att_w], kg_ref[...])
    qkv_ref[:, :att_w] = q.astype(BF16)
    qkv_ref[:, att_w: 2 * att_w] = k.astype(BF16)
    qkv_ref[:, 2 * att_w:] = qkv[:, 2 * att_w:].astype(BF16)
    pr_ref[...] = _dot(h, w_ref[:, 3 * att_w:]).astype(BF16)


def _inproj(x2d, g1, w_in_bf, qg, kg, bd, att_w, tm=512):
    T, D = x2d.shape
    in_cols = w_in_bf.shape[1]
    r_cols = in_cols - 3 * att_w
    const = lambda i: (0, 0)
    return pl.pallas_call(
        functools.partial(_inproj_kernel, att_w=att_w),
        grid=(T // tm,),
        in_specs=[
            pl.BlockSpec((tm, D), lambda i: (i, 0)),
            pl.BlockSpec((1, D), const),
            pl.BlockSpec((D, in_cols), const),
            pl.BlockSpec((1, att_w), const),
            pl.BlockSpec((1, att_w), const),
            pl.BlockSpec((att_w, att_w), const),
        ],
        out_specs=[
            pl.BlockSpec((tm, 3 * att_w), lambda i: (i, 0)),
            pl.BlockSpec((tm, r_cols), lambda i: (i, 0)),
        ],
        out_shape=[
            jax.ShapeDtypeStruct((T, 3 * att_w), BF16),
            jax.ShapeDtypeStruct((T, r_cols), BF16),
        ],
        compiler_params=pltpu.CompilerParams(
            dimension_semantics=("arbitrary",), vmem_limit_bytes=VMEM_LIMIT),
        name="inproj",
    )(x2d, g1, w_in_bf, qg, kg, bd)


def _attn_kernel(q_ref, kp_ref, kc_ref, vp_ref, vc_ref, bias_ref, o_ref, kbuf, vbuf):
    i = pl.program_id(2)
    kbuf[:ATT_HALO, :] = kp_ref[...]
    kbuf[ATT_HALO:, :] = kc_ref[...]
    lane_head = lax.broadcasted_iota(jnp.int32, (1, LANES), 1) >> (HEAD_DIM.bit_length() - 1)
    one = jnp.ones((), BF16)
    for h in range(2):
        vbuf[h, :ATT_HALO, :] = jnp.where(lane_head == h, vp_ref[...], one)
        vbuf[h, ATT_HALO:, :] = jnp.where(lane_head == h, vc_ref[...], one)
    tiles = range(ATT_QROWS // ATT_SUB)
    col = lax.broadcasted_iota(jnp.int32, (1, ATT_KEYS), 1)
    all_units =[(qs * ATT_SUB, h * HEAD_DIM, h, qs) for qs in tiles for h in range(2)]
    ext = []
    for u0 in range(0, len(all_units), ATT_INTERLEAVE):
        units = all_units[u0: u0 + ATT_INTERLEAVE]
        s = [_dot_nt(q_ref[r0: r0 + ATT_SUB, lo: lo + HEAD_DIM],
                     kbuf[r0: r0 + ATT_KEYS, lo: lo + HEAD_DIM]) + bias_ref[h]
             for (r0, lo, h, qs) in units]
        s = [x + jnp.where((col < ATT_HALO - qs * ATT_SUB) & (i == 0), NEG_BIG, 0.0)
             if qs < ATT_MASKED_TILES else x for x, (_, _, _, qs) in zip(s, units)]
        m = [jnp.max(x, axis=-1, keepdims=True) for x in s]
        p = [jnp.exp(x - mx).astype(BF16) for x, mx in zip(s, m)]
        ext += [_dot(x, vbuf[h, r0: r0 + ATT_KEYS, :]) for x, (r0, _, h, _) in zip(p, units)]
    for qs in tiles:
        e0, e1 = ext[2 * qs], ext[2 * qs + 1]
        o_ref[qs * ATT_SUB: (qs + 1) * ATT_SUB, :] = jnp.where(
            lane_head == 0, e0 / pltpu.roll(e0, HEAD_DIM, axis=1),
            e1 / pltpu.roll(e1, HEAD_DIM, axis=1)).astype(BF16)


def _attention(qkv, bias, B, S, att_w):
    T = B * S
    n_i = S // ATT_QROWS
    n_hp = att_w // LANES
    heads_per = LANES // HEAD_DIM
    assert heads_per == 2

    def cur(col0):
        return lambda b, hp, i: (b * n_i + i, col0 + hp)

    halo_per_blk = ATT_QROWS // ATT_HALO

    def prev(col0):
        return lambda b, hp, i: (
            b * n_i * halo_per_blk + jnp.maximum(i * halo_per_blk - 1, 0), col0 + hp)

    blk = (ATT_QROWS, LANES)
    halo = (ATT_HALO, LANES)
    return pl.pallas_call(
        _attn_kernel,
        grid=(B, n_hp, n_i),
        in_specs=[
            pl.BlockSpec(blk, cur(0)),
            pl.BlockSpec(halo, prev(n_hp)),
            pl.BlockSpec(blk, cur(n_hp)),
            pl.BlockSpec(halo, prev(2 * n_hp)),
            pl.BlockSpec(blk, cur(2 * n_hp)),
            pl.BlockSpec((heads_per,) + bias.shape[1:], lambda b, hp, i: (hp, 0, 0)),
        ],
        out_specs=pl.BlockSpec(blk, lambda b, hp, i: (b * n_i + i, hp)),
        out_shape=jax.ShapeDtypeStruct((T, att_w), BF16),
        scratch_shapes=[
            pltpu.VMEM((ATT_HALO + ATT_QROWS, LANES), BF16),
            pltpu.VMEM((2, ATT_HALO + ATT_QROWS, LANES), BF16),
        ],
        compiler_params=pltpu.CompilerParams(
            dimension_semantics=("arbitrary", "arbitrary", "arbitrary"),
            vmem_limit_bytes=VMEM_LIMIT),
        name="band_attention",
    )(qkv, qkv, qkv, qkv, qkv, bias)


def _attention_bias(rel_bias):
    H = rel_bias.shape[0]
    table = rel_bias.astype(F32)
    far = LEFT_CHUNKS * CHUNK + ATT_SUB - 1
    n_left = far - MAX_LEFT_REL + 1
    n_ext = ATT_SUB + ATT_KEYS - 1
    ext = jnp.concatenate([
        jnp.broadcast_to(table[:, :1], (H, n_left)), table[:, 1:],
        jnp.broadcast_to(table[:, -1:], (H, n_ext - n_left - (table.shape[1] - 1)))], axis=1)
    flat = jnp.tile(ext, (1, ATT_SUB + 1))[:, ATT_SUB - 1: ATT_SUB - 1 + ATT_SUB * (n_ext - 1)]
    bias = flat.reshape(H, ATT_SUB, n_ext - 1)[:, :, :ATT_KEYS]
    r = jnp.arange(ATT_SUB)[:, None]
    c = jnp.arange(ATT_KEYS)[None, :]
    band = c // CHUNK - r // CHUNK
    ok = (band >= 0) & (band <= LEFT_CHUNKS)
    return jnp.where(ok[None], bias, NEG_BIG)


def _rwkv_kernel(p_ref, pprev_ref, mu_ref, w0_ref, a0_ref, lora_ref, wg_ref, kk_ref, ka_ref,
                 rk_ref, lng_ref, lnb_ref, bd_ref, tri_ref, o_ref,
                 rt_s, at_s, bt_s, kt_s, bh_s, kh_s, v_s, wl_s, y_s, z_s, g_s, y0_s, m_s, n_s,
                 *, width, tm):
    i = pl.program_id(1)
    n_groups = width // GROUP
    heads_per_group = GROUP // HEAD_DIM

    @pl.when(i == 0)
    def _():
        z_s[...] = jnp.zeros_like(z_s)

    p = p_ref[...].astype(F32)
    first = jnp.where(i > 0, 1.0, 0.0)
    prev_row = pprev_ref[7:8, :].astype(F32) * first
    row = lax.broadcasted_iota(jnp.int32, (tm, 1), 0)
    prev = jnp.where(row == 0, prev_row, pltpu.roll(p, 1, axis=0))
    ps = p + (prev - p) * mu_ref[...]

    r = ps[:, :width]
    k = ps[:, width: 2 * width]
    v = ps[:, 2 * width: 3 * width]
    wa = ps[:, 3 * width: 3 * width + DECAY_LORA + AAA_LORA]
    gd = ps[:, 3 * width + DECAY_LORA + AAA_LORA:]
    lane = lax.broadcasted_iota(jnp.int32, (1, DECAY_LORA + AAA_LORA), 1)
    wa = jnp.where(lane < DECAY_LORA, jnp.tanh(wa), wa)
    lw = _dot(wa.astype(BF16), lora_ref[...])
    logw = -math.exp(-0.5) * _sigmoid(w0_ref[...] + lw[:, :width])
    eta = _sigmoid(a0_ref[...] + lw[:, width:])
    g = _dot(_sigmoid(gd).astype(BF16), wg_ref[...])

    bd = bd_ref[...]
    kk = k * kk_ref[...]
    kk = kk * lax.rsqrt(jnp.maximum(_head_sums(kk * kk, bd), 1e-24))
    km = k * (1.0 + (eta - 1.0) * ka_ref[...])
    bb = kk * eta

    tri = tri_ref[...]
    cs_chunks = [_split_dot_left(tri, logw[c * CHUNK: (c + 1) * CHUNK]) for c in range(tm // CHUNK)]
    cs = jnp.concatenate(cs_chunks, axis=0)
    tot = jnp.concatenate(
        [jnp.broadcast_to(x[CHUNK - 1:], (CHUNK, width)) for x in cs_chunks], axis=0)
    e_pos = jnp.exp(cs)
    e_neg = jnp.exp(-cs)
    e_tail = jnp.exp(tot - cs)
    rt_s[...] = r * e_pos
    at_s[...] = -kk * jnp.exp(cs - logw)
    bt_s[...] = bb * e_neg
    kt_s[...] = km * e_neg
    bh_s[...] = bb * e_tail
    kh_s[...] = km * e_tail
    v_s[...] = v
    wl_s[...] = jnp.exp(tot)

    rr = lax.broadcasted_iota(jnp.int32, (GROUP, GROUP), 0)
    cc = lax.broadcasted_iota(jnp.int32, (GROUP, GROUP), 1)
    shift = CHUNK.bit_length() - 1
    same = (rr >> shift) == (cc >> shift)
    strict = same & (rr > cc)
    incl = same & (rr >= cc)
    eye = jnp.where(rr == cc, 1.0, 0.0)
    lane_head = lax.broadcasted_iota(jnp.int32, (1, GROUP), 1) >> shift

    def stack_masked(x):
        return jnp.concatenate(
            [jnp.where(lane_head == h, x, 0.0) for h in range(heads_per_group)], axis=0)

    def stack_plain(x):
        return jnp.concatenate([x] * heads_per_group, axis=0)

    def bdot(a, b):
        return _dot(a.astype(BF16), b.astype(BF16))

    def bdot_nt(a, b):
        return _dot_nt(a.astype(BF16), b.astype(BF16))

    def local_body(it, carry):
        units = []
        for u in range(LOCAL_UNROLL):
            c = it * LOCAL_UNROLL + u
            rows = pl.ds(pl.multiple_of(c * CHUNK, CHUNK), CHUNK)
            units += [(c, gi, rows, slice(gi * GROUP, (gi + 1) * GROUP)) for gi in range(n_groups)]

        def load(ref, stack):
            return [stack(ref[rows, cols]) for (_, _, rows, cols) in units]

        def each(f, *lists):
            return [f(*args) for args in zip(*lists)]

        b16 = lambda xs: [x.astype(BF16) for x in xs]
        rt4 = load(rt_s, stack_masked)
        at4, rt4b, v4 = b16(load(at_s, stack_masked)), b16(rt4), load(v_s, stack_masked)
        v4b, bh4 = b16(v4), b16(load(bh_s, stack_masked))
        kh4 = b16(load(kh_s, stack_masked))
        bt_rep, kt_rep = b16(load(bt_s, stack_plain)), b16(load(kt_s, stack_plain))
        a_ab = each(lambda a, b: jnp.where(strict, _dot_nt(a, b), 0.0), at4, bt_rep)
        a_ak = b16(each(lambda a, b: jnp.where(strict, _dot_nt(a, b), 0.0), at4, kt_rep))
        a_rb = b16(each(lambda a, b: jnp.where(incl, _dot_nt(a, b), 0.0), rt4b, bt_rep))
        a_rk = b16(each(lambda a, b: jnp.where(incl, _dot_nt(a, b), 0.0), rt4b, kt_rep))
        tinv = [eye + a for a in a_ab]
        pw = b16(a_ab)
        span = 2
        while span < CHUNK:
            pw = b16(each(_dot, pw, pw))
            tinv = each(lambda t, p: t + _dot(p, t.astype(BF16)), tinv, pw)
            span *= 2
        tinv = b16(tinv)
        p4 = each(_dot, tinv, at4)
        q4 = each(_dot, tinv, b16(each(_dot, a_ak, v4b)))
        p4b, q4b = b16(p4), b16(q4)
        g4 = each(lambda r, a, p: r + _dot(a, p), rt4, a_rb, p4b)
        y0 = each(lambda a, q, ak, v: _dot(a, q) + _dot(ak, v), a_rb, q4b, a_rk, v4b)
        m4 = each(lambda p, b: _dot(p.T.astype(BF16), b), p4, bh4)
        n4 = each(lambda q, b, v, k: _dot(q.T.astype(BF16), b) + _dot(v.T.astype(BF16), k),
                  q4, bh4, v4, kh4)
        for (c, gi, _, _), g, y, m, n in zip(units, g4, y0, m4, n4):
            g_s[c, gi] = g.astype(BF16)
            y0_s[c, gi] = y
            m_s[c, gi] = m.astype(BF16)
            n_s[c, gi] = n
        return carry

    lax.fori_loop(0, tm // CHUNK // LOCAL_UNROLL, local_body, 0)

    def carry_body(c, carry):
        r0 = pl.multiple_of(c * CHUNK, CHUNK)
        rows = pl.ds(r0, CHUNK)
        for gi in range(n_groups):
            cols = slice(gi * GROUP, (gi + 1) * GROUP)
            zt = z_s[gi]
            zb = zt.astype(BF16)
            y4 = _dot_nt(g_s[c, gi], zb) + y0_s[c, gi]
            z_s[gi] = zt * wl_s[pl.ds(r0, 1), cols] + _dot(zb, m_s[c, gi]) + n_s[c, gi]
            ytok = y4[:CHUNK]
            for h in range(1, heads_per_group):
                ytok = ytok + y4[h * CHUNK: (h + 1) * CHUNK]
            y_s[rows, cols] = ytok
        return carry

    lax.fori_loop(0, tm // CHUNK, carry_body, 0, unroll=True)

    y = y_s[...]
    inv_n = 1.0 / HEAD_DIM
    mean = _head_sums(y, bd) * inv_n
    d = y - mean
    var = _head_sums(d * d, bd) * inv_n
    yn = d * lax.rsqrt(var + LN_X_EPS) * lng_ref[...] + lnb_ref[...]
    bonus = _head_sums(r * km * rk_ref[...], bd) * v
    o_ref[...] = ((yn + bonus) * g).astype(BF16)


def _split_dot_left(a_exact, b, terms=2):
    acc = None
    rem = b
    for _ in range(terms):
        part = rem.astype(BF16)
        rem = rem - part.astype(F32)
        d = _dot(a_exact, part)
        acc = d if acc is None else acc + d
    return acc


def _rwkv(pr, mu, w0, a0, lora, wg, k_k, k_a, r_k, ln_g, ln_b, bd, B, S, width, tm=512):
    T = B * S
    n_t = S // tm
    cols = pr.shape[1]
    ti = jnp.arange(CHUNK)
    tri = (ti[:, None] >= ti[None, :]).astype(BF16)
    const = lambda b, i: (0, 0)
    vec = lambda n: pl.BlockSpec((1, n), const)
    scr = lambda: pltpu.VMEM((tm, width), F32)
    return pl.pallas_call(
        functools.partial(_rwkv_kernel, width=width, tm=tm),
        grid=(B, n_t),
        in_specs=[
            pl.BlockSpec((tm, cols), lambda b, i: (b * n_t + i, 0)),
            pl.BlockSpec((8, cols), lambda b, i: (jnp.maximum((b * n_t + i) * (tm // 8) - 1, 0), 0)),
            vec(cols), vec(width), vec(width),
            pl.BlockSpec(lora.shape, const),
            pl.BlockSpec(wg.shape, const),
            vec(width), vec(width), vec(width), vec(width), vec(width),
            pl.BlockSpec((width, width), const),
            pl.BlockSpec((CHUNK, CHUNK), const),
        ],
        out_specs=pl.BlockSpec((tm, width), lambda b, i: (b * n_t + i, 0)),
        out_shape=jax.ShapeDtypeStruct((T, width), BF16),
        scratch_shapes=[scr() for _ in range(9)]
        + [pltpu.VMEM((width // GROUP, GROUP, GROUP), F32)]
        + [pltpu.VMEM((tm // CHUNK, width // GROUP, GROUP, GROUP), dt)
           for dt in (BF16, F32, BF16, F32)],
        compiler_params=pltpu.CompilerParams(
            dimension_semantics=("arbitrary", "arbitrary"), vmem_limit_bytes=VMEM_LIMIT),
        name="rwkv7",
    )(pr, pr, mu, w0, a0, lora, wg, k_k, k_a, r_k, ln_g, ln_b, bd, tri)


def _outproj_kernel(att_ref, rwk_ref, x_ref, wa_ref, wr_ref, g_ref, wrt_ref, br_ref, upper_ref,
                    x2_ref, xn_ref, eidx_ref, rank_ref, gate_ref, cnt_ref, *, n_exp, tm):
    x2 = x_ref[...] + _dot(att_ref[...], wa_ref[...]) + _dot(rwk_ref[...], wr_ref[...])
    x2_ref[...] = x2
    ms = jnp.mean(x2 * x2, axis=-1, keepdims=True)
    xn = x2 * lax.rsqrt(ms + NORM_EPS) * g_ref[...]
    xn_ref[...] = xn.astype(BF16)

    x_hi = xn.astype(BF16)
    x_lo = (xn - x_hi.astype(F32)).astype(BF16)
    logits = (_dot_nt(wrt_ref[0], x_hi) + _dot_nt(wrt_ref[0], x_lo)
              + _dot_nt(wrt_ref[1], x_hi)) + br_ref[...]

    eio = lax.broadcasted_iota(jnp.int32, (n_exp, tm), 0)
    work = logits
    vals, onehots = [], []
    for k in range(TOP_K):
        m = jnp.max(work, axis=0, keepdims=True)
        idx = jnp.min(jnp.where(work == m, eio, n_exp), axis=0, keepdims=True)
        oh = eio == idx
        vals.append(m)
        onehots.append(oh)
        eidx_ref[k: k + 1, :] = idx
        work = jnp.where(oh, -jnp.inf, work)

    exps = [jnp.exp(vk - vals[0]) for vk in vals]
    denom = exps[0]
    for e in exps[1:]:
        denom = denom + e
    for k in range(TOP_K):
        gate_ref[k: k + 1, :] = exps[k] / denom

    sel = onehots[0]
    for oh in onehots[1:]:
        sel = sel | oh
    sel_f = jnp.where(sel, 1.0, 0.0)
    before = _dot(sel_f.astype(BF16), upper_ref[...])
    for k in range(TOP_K):
        rank_ref[k: k + 1, :] = jnp.sum(
            jnp.where(onehots[k], before, 0.0), axis=0, keepdims=True).astype(jnp.int32)
    cnt_ref[...] = jnp.broadcast_to(
        jnp.sum(sel_f, axis=1, keepdims=True), cnt_ref.shape).astype(jnp.int32)


def _outproj(att, rwk, x2d, w_att, w_rwk, g2, wrt, br, n_exp):
    T, D = x2d.shape
    tm = ROUTE_TILE
    upper = (jnp.arange(tm)[:, None] < jnp.arange(tm)[None, :]).astype(BF16)
    const = lambda i: (0, 0)
    rows = lambda n: pl.BlockSpec((tm, n), lambda i: (i, 0))
    lanes = pl.BlockSpec((TOP_K, tm), lambda i: (0, i))
    return pl.pallas_call(
        functools.partial(_outproj_kernel, n_exp=n_exp, tm=tm),
        grid=(T // tm,),
        in_specs=[
            rows(att.shape[1]), rows(rwk.shape[1]), rows(D),
            pl.BlockSpec(w_att.shape, const), pl.BlockSpec(w_rwk.shape, const),
            pl.BlockSpec((1, D), const),
            pl.BlockSpec(wrt.shape, lambda i: (0, 0, 0)),
            pl.BlockSpec((n_exp, 1), const),
            pl.BlockSpec((tm, tm), const),
        ],
        out_specs=[rows(D), rows(D), lanes, lanes, lanes,
                   pl.BlockSpec((n_exp, LANES), lambda i: (i, 0))],
        out_shape=[
            jax.ShapeDtypeStruct((T, D), F32),
            jax.ShapeDtypeStruct((T, D), BF16),
            jax.ShapeDtypeStruct((TOP_K, T), jnp.int32),
            jax.ShapeDtypeStruct((TOP_K, T), jnp.int32),
            jax.ShapeDtypeStruct((TOP_K, T), F32),
            jax.ShapeDtypeStruct((T // tm * n_exp, LANES), jnp.int32),
        ],
        compiler_params=pltpu.CompilerParams(
            dimension_semantics=("arbitrary",), vmem_limit_bytes=VMEM_LIMIT),
        name="outproj_router",
    )(att, rwk, x2d, w_att, w_rwk, g2, wrt, br, upper)


def _pow2_sizes(limit):
    top = 1 << (limit.bit_length() - 1)
    return tuple(top >> s for s in range((top // RUN_ALIGN).bit_length()))


def _run_copies(action, length, src, src_off, dst, dst_off, sem, sizes=RUN_SIZES):
    for size in sizes:
        start = length & ~(2 * size - 1)

        @pl.when((length & size) != 0)
        def _():
            cp = pltpu.make_async_copy(
                src.at[pl.ds(pl.multiple_of(src_off + start, RUN_ALIGN), size)],
                dst.at[pl.ds(pl.multiple_of(dst_off + start, RUN_ALIGN), size)], sem)
            cp.start() if action == "start" else cp.wait()


def _dispatch_kernel(soff_ref, rlen_ref, rdst_ref, tail_dst_ref, tail_len_ref, nact_ref,
                     slot_ref, xn_ref, xs_hbm, buf, zbuf, sem, run_sems, *, n_exp, slots, n_blocks):
    j = pl.program_id(0)
    tm = ROUTE_TILE

    @pl.when(j == 0)
    def _():
        zbuf[...] = jnp.zeros_like(zbuf)

        def unused_block(action):
            def body(b, c):
                cp = pltpu.make_async_copy(
                    zbuf, xs_hbm.at[pl.ds(pl.multiple_of(b * MOE_ROWS, MOE_ROWS), MOE_ROWS)], sem)
                cp.start() if action == "start" else cp.wait()
                return c
            return body

        for action in ("start", "wait"):
            for e in range(n_exp):
                _run_copies(action, tail_len_ref[e], zbuf, 0, xs_hbm, tail_dst_ref[e], sem)
            lax.fori_loop(nact_ref[0], n_blocks, unused_block(action), 0)

    cur = lax.rem(j, 2)
    x = xn_ref[...]
    for sb in range(slots // GROUP):
        sio = lax.broadcasted_iota(jnp.int32, (GROUP, tm), 0) + sb * GROUP
        p = jnp.zeros((GROUP, tm), F32)
        for k in range(TOP_K):
            p = jnp.where(sio == slot_ref[0, k: k + 1, :], 1.0, p)
        buf[cur, sb * GROUP: (sb + 1) * GROUP, :] = _pack_bf16_pairs(_dot(p.astype(BF16), x))

    def start_runs(tile, half):
        for e in range(n_exp):
            r = tile * n_exp + e
            _run_copies("start", rlen_ref[r], buf.at[half], soff_ref[r], xs_hbm, rdst_ref[r],
                        run_sems.at[half])

    def wait_runs(tile, half):
        last = tile * n_exp + n_exp - 1
        _run_copies("wait", soff_ref[last] + rlen_ref[last], buf.at[half], 0, xs_hbm, 0,
                    run_sems.at[half], sizes=_pow2_sizes(slots))

    start_runs(j, cur)

    @pl.when(j > 0)
    def _():
        wait_runs(j - 1, 1 - cur)

    @pl.when(j == pl.num_programs(0) - 1)
    def _():
        wait_runs(j, cur)


def _dispatch(soff, rlen, rdst, tail_dst, tail_len, n_active, slot_lane, xn, n_blocks, slots):
    T, D = xn.shape
    tm = ROUTE_TILE
    n_exp = tail_dst.shape[0]
    n_rows = n_blocks * MOE_ROWS
    grid_spec = pltpu.PrefetchScalarGridSpec(
        num_scalar_prefetch=6,
        grid=(T // tm,),
        in_specs=[
            pl.BlockSpec((1, TOP_K, tm), lambda i, *_: (i, 0, 0)),
            pl.BlockSpec((tm, D), lambda i, *_: (i, 0)),
        ],
        out_specs=pl.BlockSpec(memory_space=pl.ANY),
        scratch_shapes=[pltpu.VMEM((2, slots, D // 2), jnp.uint32),
                        pltpu.VMEM((MOE_ROWS, D // 2), jnp.uint32),
                        pltpu.SemaphoreType.DMA, pltpu.SemaphoreType.DMA((2,))],
    )
    return pl.pallas_call(
        functools.partial(_dispatch_kernel, n_exp=n_exp, slots=slots, n_blocks=n_blocks),
        grid_spec=grid_spec,
        out_shape=jax.ShapeDtypeStruct((n_rows, D // 2), jnp.uint32),
        compiler_params=pltpu.CompilerParams(
            dimension_semantics=("arbitrary",), vmem_limit_bytes=VMEM_LIMIT),
        name="moe_dispatch",
    )(soff, rlen, rdst, tail_dst, tail_len, n_active, slot_lane, xn)


def _moe_kernel(bexp_ref, nact_ref, x_ref, wup_ref, bup_ref, wdn_ref, bdn_ref,
                y_ref, wup_bf, wdn_bf, *, d_exp):
    i = pl.program_id(0)

    @pl.when(i < nact_ref[0])
    def _():
        new_expert = jnp.logical_or(i == 0, bexp_ref[i] != bexp_ref[jnp.maximum(i - 1, 0)])

        @pl.when(new_expert)
        def _():
            wup_bf[...] = wup_ref[0].astype(BF16)
            wdn_bf[...] = wdn_ref[0].astype(BF16)

        x = _unpack_bf16_pairs(x_ref[...])
        h_glu = _dot(x, wup_bf[:, :d_exp]) + bup_ref[0, :, :d_exp]
        h_lin = _dot(x, wup_bf[:, d_exp:]) + bup_ref[0, :, d_exp:]
        h_glu = jnp.minimum(h_glu, SWIGLU_LIMIT)
        h_lin = jnp.clip(h_lin, -SWIGLU_LIMIT, SWIGLU_LIMIT)
        act = h_glu * _sigmoid(SWIGLU_ALPHA * h_glu) * (h_lin + 1.0)
        y_ref[...] = _pack_bf16_pairs(_dot(act.astype(BF16), wdn_bf[...]) + bdn_ref[0])

    @pl.when(i >= nact_ref[0])
    def _():
        y_ref[...] = jnp.zeros_like(y_ref)


def _moe(blk_expert, n_active, xs, w_up, b_up, w_down, b_down):
    n_blocks = blk_expert.shape[0]
    n_exp, D, two_f = w_up.shape
    d_exp = two_f // 2
    blk = lambda i, na: jnp.minimum(i, na[0] - 1)
    grid_spec = pltpu.PrefetchScalarGridSpec(
        num_scalar_prefetch=2,
        grid=(n_blocks,),
        in_specs=[
            pl.BlockSpec((MOE_ROWS, D // 2), lambda i, be, na: (blk(i, na), 0)),
            pl.BlockSpec((1, D, two_f), lambda i, be, na: (be[blk(i, na)], 0, 0)),
            pl.BlockSpec((1, 1, two_f), lambda i, be, na: (be[blk(i, na)], 0, 0)),
            pl.BlockSpec((1, d_exp, D), lambda i, be, na: (be[blk(i, na)], 0, 0)),
            pl.BlockSpec((1, 1, D), lambda i, be, na: (be[blk(i, na)], 0, 0)),
        ],
        out_specs=pl.BlockSpec((MOE_ROWS, D // 2), lambda i, be, na: (i, 0)),
        scratch_shapes=[
            pltpu.VMEM((D, two_f), BF16),
            pltpu.VMEM((d_exp, D), BF16),
        ],
    )
    return pl.pallas_call(
        functools.partial(_moe_kernel, d_exp=d_exp),
        grid_spec=grid_spec,
        out_shape=jax.ShapeDtypeStruct((n_blocks * MOE_ROWS, D // 2), jnp.uint32),
        compiler_params=pltpu.CompilerParams(
            dimension_semantics=("arbitrary",), vmem_limit_bytes=VMEM_LIMIT),
        name="moe_ffn",
    )(blk_expert, n_active, xs,
      w_up, b_up.reshape(n_exp, 1, two_f), w_down, b_down.reshape(n_exp, 1, D))


def _combine_kernel(soff_ref, rlen_ref, rdst_ref, slot_ref, gate_ref, x2_ref, y_hbm, o_ref,
                    buf, sems, *, n_exp, slots):
    j = pl.program_id(0)
    tm = ROUTE_TILE

    cur = lax.rem(j, 2)

    def start_runs(tile, half):
        for e in range(n_exp):
            r = tile * n_exp + e
            _run_copies("start", rlen_ref[r], y_hbm, rdst_ref[r], buf.at[half], soff_ref[r],
                        sems.at[half])

    @pl.when(j == 0)
    def _():
        buf[...] = jnp.zeros_like(buf)
        start_runs(0, 0)

    @pl.when(j + 1 < pl.num_programs(0))
    def _():
        start_runs(j + 1, 1 - cur)

    last = j * n_exp + n_exp - 1
    _run_copies("wait", soff_ref[last] + rlen_ref[last], y_hbm, 0, buf.at[cur], 0, sems.at[cur],
                sizes=_pow2_sizes(slots))

    slot = slot_ref[...]
    gate = gate_ref[...]
    acc = x2_ref[...]
    for sb in range(slots // GROUP):
        lio = lax.broadcasted_iota(jnp.int32, (tm, GROUP), 1) + sb * GROUP
        w = jnp.zeros((tm, GROUP), F32)
        for k in range(TOP_K):
            w = jnp.where(lio == slot[:, k: k + 1], gate[:, k: k + 1], w)
        acc = acc + _dot(w.astype(BF16),
                         _unpack_bf16_pairs(buf[cur, sb * GROUP: (sb + 1) * GROUP, :]))
    o_ref[...] = acc


def _combine(soff, rlen, rdst, slot_row, gates_row, x2, y_rows, slots):
    T, D = x2.shape
    tm = ROUTE_TILE
    n_exp = soff.shape[0] // (T // tm)
    grid_spec = pltpu.PrefetchScalarGridSpec(
        num_scalar_prefetch=3,
        grid=(T // tm,),
        in_specs=[
            pl.BlockSpec((tm, TOP_K), lambda i, *_: (i, 0)),
            pl.BlockSpec((tm, TOP_K), lambda i, *_: (i, 0)),
            pl.BlockSpec((tm, D), lambda i, *_: (i, 0)),
            pl.BlockSpec(memory_space=pl.ANY),
        ],
        out_specs=pl.BlockSpec((tm, D), lambda i, *_: (i, 0)),
        scratch_shapes=[pltpu.VMEM((2, slots, D // 2), jnp.uint32),
                        pltpu.SemaphoreType.DMA((2,))],
    )
    return pl.pallas_call(
        functools.partial(_combine_kernel, n_exp=n_exp, slots=slots),
        grid_spec=grid_spec,
        out_shape=jax.ShapeDtypeStruct((T, D), F32),
        compiler_params=pltpu.CompilerParams(
            dimension_semantics=("arbitrary",), vmem_limit_bytes=VMEM_LIMIT),
        name="moe_combine",
    )(soff, rlen, rdst, slot_row, gates_row, x2, y_rows)


def _block_diag_ones(n, blk):
    i = jnp.arange(n)
    return ((i[:, None] // blk) == (i[None, :] // blk)).astype(BF16)


def _layer(x, norm1_g, w_in, tshift_mu, q_norm_g, k_norm_g, rel_bias, w0, w_decay_up, a0,
           w_aaa_up, w_gate_up, k_k, k_a, r_k, ln_x_g, ln_x_b, w_out, norm2_g, w_router,
           b_router, w_up, b_up, w_down, b_down):
    B, S, D = x.shape
    T = B * S
    att_heads = rel_bias.shape[0]
    att_w = att_heads * HEAD_DIM
    width = w0.shape[0]
    n_exp = w_router.shape[1]
    assert att_w % LANES == 0 and width % GROUP == 0
    assert S % ATT_QROWS == 0 and ATT_QROWS % ATT_HALO == 0 and ATT_HALO % ATT_SUB == 0
    assert T % ROUTE_TILE == 0

    x2d = x.reshape(T, D)
    row = lambda a: a.reshape(1, -1).astype(F32)
    bd_att = _block_diag_ones(att_w, HEAD_DIM)
    bd_rwkv = _block_diag_ones(width, HEAD_DIM)

    qkv, pr = _inproj(
        x2d, row(norm1_g), w_in.astype(BF16),
        row(jnp.tile(q_norm_g, att_heads)), row(jnp.tile(k_norm_g, att_heads)), bd_att, att_w)

    att = _attention(qkv, _attention_bias(rel_bias), B, S, att_w)

    lora = jnp.zeros((DECAY_LORA + AAA_LORA, 2 * width), F32)
    lora = lora.at[:DECAY_LORA, :width].set(w_decay_up).at[DECAY_LORA:, width:].set(w_aaa_up)
    rwk = _rwkv(pr, row(tshift_mu), row(w0), row(a0), lora.astype(BF16),
                w_gate_up.astype(BF16), row(k_k), row(k_a), row(r_k), row(ln_x_g),
                row(ln_x_b), bd_rwkv, B, S, width)

    wrt = w_router.T.astype(F32)
    wrt_hi = wrt.astype(BF16)
    wrt_lo = (wrt - wrt_hi.astype(F32)).astype(BF16)
    x2, xn, eidx, rank, gates, counts = _outproj(
        att, rwk, x2d, w_out[:att_w].astype(BF16), w_out[att_w:].astype(BF16), row(norm2_g),
        jnp.stack([wrt_hi, wrt_lo]), b_router.reshape(n_exp, 1).astype(F32), n_exp)

    tm = ROUTE_TILE
    n_tiles = T // tm
    i32 = lambda a: a.astype(jnp.int32)
    run_n = counts.reshape(n_tiles, n_exp, LANES)[:, :, 0]
    run_len = (run_n + RUN_ALIGN - 1) // RUN_ALIGN * RUN_ALIGN
    run_off = jnp.cumsum(run_len, axis=1) - run_len
    exp_rows = jnp.sum(run_len, axis=0)
    exp_pad = (exp_rows + MOE_ROWS - 1) // MOE_ROWS * MOE_ROWS
    exp_end = jnp.cumsum(exp_pad)
    exp_start = exp_end - exp_pad
    run_dst = exp_start[None, :] + jnp.cumsum(run_len, axis=0) - run_len
    eidx_t = eidx.reshape(TOP_K, n_tiles, tm)
    experts = jnp.arange(n_exp, dtype=jnp.int32)[:, None, None, None]
    slot = rank.reshape(TOP_K, n_tiles, tm) + jnp.sum(
        jnp.where(eidx_t[None] == experts, run_off.T[:, None, :, None], 0), axis=0)
    slot_lane = i32(slot.transpose(1, 0, 2))
    slot_row = i32(slot.transpose(1, 2, 0).reshape(T, TOP_K))
    slots = -(-(tm * TOP_K + n_exp * (RUN_ALIGN - 1)) // GROUP) * GROUP
    n_blocks = -(-(T * TOP_K + n_tiles * n_exp * (RUN_ALIGN - 1)) // MOE_ROWS) + n_exp
    blk_start = jnp.arange(n_blocks, dtype=jnp.int32) * MOE_ROWS
    blk_expert = i32(jnp.minimum(
        jnp.sum(blk_start[:, None] >= exp_end[None, :], axis=1), n_exp - 1))
    n_active = i32(exp_end[-1:] // MOE_ROWS)
    soff, rlen, rdst = (i32(a.reshape(-1)) for a in (run_off, run_len, run_dst))

    xs = _dispatch(soff, rlen, rdst, i32(exp_start + exp_rows), i32(exp_pad - exp_rows),
                   n_active, slot_lane, xn, n_blocks, slots)
    y_rows = _moe(blk_expert, n_active, xs, w_up, b_up, w_down, b_down)
    out = _combine(soff, rlen, rdst, slot_row, gates.T, x2, y_rows, slots)
    return out.reshape(B, S, D)


def kernel(x, norm1_g, w_in, tshift_mu, q_norm_g, k_norm_g, rel_bias, w0, w_decay_up, a0,
           w_aaa_up, w_gate_up, k_k, k_a, r_k, ln_x_g, ln_x_b, w_out, norm2_g, w_router,
           b_router, w_up, b_up, w_down, b_down):
    params = (norm1_g, w_in, tshift_mu, q_norm_g, k_norm_g, rel_bias, w0, w_decay_up, a0,
              w_aaa_up, w_gate_up, k_k, k_a, r_k, ln_x_g, ln_x_b, w_out, norm2_g, w_router,
              b_router, w_up, b_up, w_down, b_down)
    for l in range(norm1_g.shape[0]):
        x = _layer(x, *[p[l] for p in params])
    return x
```

```python
import functools
import math

import jax
import jax.numpy as jnp
from jax import lax
from jax.experimental import pallas as pl
from jax.experimental.pallas import tpu as pltpu

F32 = jnp.float32
BF16 = jnp.bfloat16

CHUNK = 64
HEAD_DIM = 64
LEFT_CHUNKS = 8
MAX_LEFT_REL = 128
TOP_K = 4
SWIGLU_ALPHA = 1.702
SWIGLU_LIMIT = 7.0
NORM_EPS = 1e-6
LN_X_EPS = 64e-5
DECAY_LORA = 64
AAA_LORA = 64
GATE_LORA = 128

LANES = 128
GROUP = 256
NEG_BIG = -1e30

VMEM_LIMIT = 56 * 1024 * 1024

ATT_QROWS = 2048
ATT_SUB = 256
ATT_HALO = LEFT_CHUNKS * CHUNK
ATT_KEYS = ATT_SUB + ATT_HALO
ATT_MASKED_TILES = ATT_HALO // ATT_SUB
ATT_INTERLEAVE = 4

LOCAL_UNROLL = 4

MOE_ROWS = 512
ROUTE_TILE = 512
RUN_ALIGN = 8
RUN_SIZES = tuple(ROUTE_TILE >> s for s in range((ROUTE_TILE // RUN_ALIGN).bit_length()))
assert MOE_ROWS <= 2 * RUN_SIZES[0] and RUN_SIZES[-1] == RUN_ALIGN


def _dot(a, b):
    return jnp.dot(a, b, preferred_element_type=F32)


def _dot_nt(a, b):
    return lax.dot_general(a, b, (((1,), (1,)), ((), ())), preferred_element_type=F32)


def _head_sums(a, block_ones):
    return _dot(a.astype(BF16), block_ones)


def _pack_bf16_pairs(x):
    n = x.shape[1] // 2
    bits = pltpu.bitcast(x.astype(BF16).astype(F32), jnp.uint32)
    return bits[:, :n] | (bits[:, n:] >> 16)


def _unpack_bf16_pairs(w):
    hi = pltpu.bitcast(w & jnp.uint32(0xFFFF0000), F32)
    lo = pltpu.bitcast(w << 16, F32)
    return jnp.concatenate([hi, lo], axis=1).astype(BF16)


def _sigmoid(x):
    return 0.5 * jnp.tanh(0.5 * x) + 0.5


def _inproj_kernel(x_ref, g_ref, w_ref, qg_ref, kg_ref, bd_ref, qkv_ref, pr_ref, *, att_w):
    x = x_ref[...]
    ms = jnp.mean(x * x, axis=-1, keepdims=True)
    h = (x * lax.rsqrt(ms + NORM_EPS) * g_ref[...]).astype(BF16)
    qkv = _dot(h, w_ref[:, : 3 * att_w])
    bd = bd_ref[...]

    def head_norm(t, g):
        ss = _head_sums(t * t, bd)
        return t * lax.rsqrt(ss * (1.0 / HEAD_DIM) + NORM_EPS) * g

    q = head_norm(qkv[:, :att_w], qg_ref[...]) * (HEAD_DIM ** -0.5)
    k = head_norm(qkv[:, att_w: 2 * att_w], kg_ref[...])
    qkv_ref[:, :att_w] = q.astype(BF16)
    qkv_ref[:, att_w: 2 * att_w] = k.astype(BF16)
    qkv_ref[:, 2 * att_w:] = qkv[:, 2 * att_w:].astype(BF16)
    pr_ref[...] = _dot(h, w_ref[:, 3 * att_w:]).astype(BF16)


def _inproj(x2d, g1, w_in_bf, qg, kg, bd, att_w, tm=512):
    T, D = x2d.shape
    in_cols = w_in_bf.shape[1]
    r_cols = in_cols - 3 * att_w
    const = lambda i: (0, 0)
    return pl.pallas_call(
        functools.partial(_inproj_kernel, att_w=att_w),
        grid=(T // tm,),
        in_specs=[
            pl.BlockSpec((tm, D), lambda i: (i, 0)),
            pl.BlockSpec((1, D), const),
            pl.BlockSpec((D, in_cols), const),
            pl.BlockSpec((1, att_w), const),
            pl.BlockSpec((1, att_w), const),
            pl.BlockSpec((att_w, att_w), const),
        ],
        out_specs=[
            pl.BlockSpec((tm, 3 * att_w), lambda i: (i, 0)),
            pl.BlockSpec((tm, r_cols), lambda i: (i, 0)),
        ],
        out_shape=[
            jax.ShapeDtypeStruct((T, 3 * att_w), BF16),
            jax.ShapeDtypeStruct((T, r_cols), BF16),
        ],
        compiler_params=pltpu.CompilerParams(
            dimension_semantics=("arbitrary",), vmem_limit_bytes=VMEM_LIMIT),
        name="inproj",
    )(x2d, g1, w_in_bf, qg, kg, bd)


def _attn_kernel(q_ref, kp_ref, kc_ref, vp_ref, vc_ref, bias_ref, o_ref, kbuf, vbuf):
    i = pl.program_id(2)
    kbuf[:ATT_HALO, :] = kp_ref[...]
    kbuf[ATT_HALO:, :] = kc_ref[...]
    lane_head = lax.broadcasted_iota(jnp.int32, (1, LANES), 1) >> (HEAD_DIM.bit_length() - 1)
    one = jnp.ones((), BF16)
    for h in range(2):
        vbuf[h, :ATT_HALO, :] = jnp.where(lane_head == h, vp_ref[...], one)
        vbuf[h, ATT_HALO:, :] = jnp.where(lane_head == h, vc_ref[...], one)
    tiles = range(ATT_QROWS // ATT_SUB)
    col = lax.broadcasted_iota(jnp.int32, (1, ATT_KEYS), 1)
    all_units =[(qs * ATT_SUB, h * HEAD_DIM, h, qs) for qs in tiles for h in range(2)]
    ext = []
    for u0 in range(0, len(all_units), ATT_INTERLEAVE):
        units = all_units[u0: u0 + ATT_INTERLEAVE]
        s = [_dot_nt(q_ref[r0: r0 + ATT_SUB, lo: lo + HEAD_DIM],
                     kbuf[r0: r0 + ATT_KEYS, lo: lo + HEAD_DIM]) + bias_ref[h]
             for (r0, lo, h, qs) in units]
        s = [x + jnp.where((col < ATT_HALO - qs * ATT_SUB) & (i == 0), NEG_BIG, 0.0)
             if qs < ATT_MASKED_TILES else x for x, (_, _, _, qs) in zip(s, units)]
        m = [jnp.max(x, axis=-1, keepdims=True) for x in s]
        p = [jnp.exp(x - mx).astype(BF16) for x, mx in zip(s, m)]
        ext += [_dot(x, vbuf[h, r0: r0 + ATT_KEYS, :]) for x, (r0, _, h, _) in zip(p, units)]
    for qs in tiles:
        e0, e1 = ext[2 * qs], ext[2 * qs + 1]
        o_ref[qs * ATT_SUB: (qs + 1) * ATT_SUB, :] = jnp.where(
            lane_head == 0, e0 / pltpu.roll(e0, HEAD_DIM, axis=1),
            e1 / pltpu.roll(e1, HEAD_DIM, axis=1)).astype(BF16)


def _attention(qkv, bias, B, S, att_w):
    T = B * S
    n_i = S // ATT_QROWS
    n_hp = att_w // LANES
    heads_per = LANES // HEAD_DIM
    assert heads_per == 2

    def cur(col0):
        return lambda b, hp, i: (b * n_i + i, col0 + hp)

    halo_per_blk = ATT_QROWS // ATT_HALO

    def prev(col0):
        return lambda b, hp, i: (
            b * n_i * halo_per_blk + jnp.maximum(i * halo_per_blk - 1, 0), col0 + hp)

    blk = (ATT_QROWS, LANES)
    halo = (ATT_HALO, LANES)
    return pl.pallas_call(
        _attn_kernel,
        grid=(B, n_hp, n_i),
        in_specs=[
            pl.BlockSpec(blk, cur(0)),
            pl.BlockSpec(halo, prev(n_hp)),
            pl.BlockSpec(blk, cur(n_hp)),
            pl.BlockSpec(halo, prev(2 * n_hp)),
            pl.BlockSpec(blk, cur(2 * n_hp)),
            pl.BlockSpec((heads_per,) + bias.shape[1:], lambda b, hp, i: (hp, 0, 0)),
        ],
        out_specs=pl.BlockSpec(blk, lambda b, hp, i: (b * n_i + i, hp)),
        out_shape=jax.ShapeDtypeStruct((T, att_w), BF16),
        scratch_shapes=[
            pltpu.VMEM((ATT_HALO + ATT_QROWS, LANES), BF16),
            pltpu.VMEM((2, ATT_HALO + ATT_QROWS, LANES), BF16),
        ],
        compiler_params=pltpu.CompilerParams(
            dimension_semantics=("arbitrary", "arbitrary", "arbitrary"),
            vmem_limit_bytes=VMEM_LIMIT),
        name="band_attention",
    )(qkv, qkv, qkv, qkv, qkv, bias)


def _attention_bias(rel_bias):
    H = rel_bias.shape[0]
    table = rel_bias.astype(F32)
    far = LEFT_CHUNKS * CHUNK + ATT_SUB - 1
    n_left = far - MAX_LEFT_REL + 1
    period = ATT_SUB + ATT_KEYS + 1
    ext = jnp.concatenate([
        jnp.broadcast_to(table[:, :1], (H, n_left)), table[:, 1:],
        jnp.broadcast_to(table[:, -1:], (H, period - n_left - (table.shape[1] - 1)))], axis=1)
    flat = jnp.tile(ext, (1, ATT_SUB + 1))[:, ATT_SUB - 1: ATT_SUB - 1 + ATT_SUB * (period - 1)]
    bias = flat.reshape(H, ATT_SUB, period - 1)[:, :, :ATT_KEYS]
    r = jnp.arange(ATT_SUB)[:, None]
    c = jnp.arange(ATT_KEYS)[None, :]
    band = c // CHUNK - r // CHUNK
    ok = (band >= 0) & (band <= LEFT_CHUNKS)
    return jnp.where(ok[None], bias, NEG_BIG)


def _rwkv_kernel(p_ref, pprev_ref, mu_ref, w0_ref, a0_ref, lora_ref, wg_ref, kk_ref, ka_ref,
                 rk_ref, lng_ref, lnb_ref, bd_ref, tri_ref, o_ref,
                 rt_s, at_s, bt_s, kt_s, bh_s, kh_s, v_s, wl_s, y_s, z_s, g_s, y0_s, m_s, n_s,
                 *, width, tm):
    i = pl.program_id(1)
    n_groups = width // GROUP
    heads_per_group = GROUP // HEAD_DIM

    @pl.when(i == 0)
    def _():
        z_s[...] = jnp.zeros_like(z_s)

    p = p_ref[...].astype(F32)
    has_prev = jnp.where(i > 0, 1.0, 0.0)
    prev_row = pprev_ref[7:8, :].astype(F32) * has_prev
    row = lax.broadcasted_iota(jnp.int32, (tm, 1), 0)
    prev = jnp.where(row == 0, prev_row, pltpu.roll(p, 1, axis=0))
    ps = p + (prev - p) * mu_ref[...]

    r = ps[:, :width]
    k = ps[:, width: 2 * width]
    v = ps[:, 2 * width: 3 * width]
    wa = ps[:, 3 * width: 3 * width + DECAY_LORA + AAA_LORA]
    gd = ps[:, 3 * width + DECAY_LORA + AAA_LORA:]
    lane = lax.broadcasted_iota(jnp.int32, (1, DECAY_LORA + AAA_LORA), 1)
    wa = jnp.where(lane < DECAY_LORA, jnp.tanh(wa), wa)
    lw = _dot(wa.astype(BF16), lora_ref[...])
    logw = -math.exp(-0.5) * _sigmoid(w0_ref[...] + lw[:, :width])
    eta = _sigmoid(a0_ref[...] + lw[:, width:])
    g = _dot(_sigmoid(gd).astype(BF16), wg_ref[...])

    bd = bd_ref[...]
    kk = k * kk_ref[...]
    kk = kk * lax.rsqrt(jnp.maximum(_head_sums(kk * kk, bd), 1e-24))
    km = k * (1.0 + (eta - 1.0) * ka_ref[...])
    bb = kk * eta

    tri = tri_ref[...]
    cs_chunks = [_split_dot_left(tri, logw[c * CHUNK: (c + 1) * CHUNK]) for c in range(tm // CHUNK)]
    cs = jnp.concatenate(cs_chunks, axis=0)
    tot = jnp.concatenate(
        [jnp.broadcast_to(x[CHUNK - 1:], (CHUNK, width)) for x in cs_chunks], axis=0)
    e_pos = jnp.exp(cs)
    e_neg = jnp.exp(-cs)
    e_tail = jnp.exp(tot - cs)
    rt_s[...] = r * e_pos
    at_s[...] = -kk * jnp.exp(cs - logw)
    bt_s[...] = bb * e_neg
    kt_s[...] = km * e_neg
    bh_s[...] = bb * e_tail
    kh_s[...] = km * e_tail
    v_s[...] = v
    wl_s[...] = jnp.exp(tot)

    rr = lax.broadcasted_iota(jnp.int32, (GROUP, GROUP), 0)
    cc = lax.broadcasted_iota(jnp.int32, (GROUP, GROUP), 1)
    shift = CHUNK.bit_length() - 1
    same = (rr >> shift) == (cc >> shift)
    strict = same & (rr > cc)
    incl = same & (rr >= cc)
    eye = jnp.where(rr == cc, 1.0, 0.0)
    lane_head = lax.broadcasted_iota(jnp.int32, (1, GROUP), 1) >> shift

    def stack_masked(x):
        return jnp.concatenate(
            [jnp.where(lane_head == h, x, 0.0) for h in range(heads_per_group)], axis=0)

    def stack_plain(x):
        return jnp.concatenate([x] * heads_per_group, axis=0)

    def local_body(it, carry):
        units = []
        for u in range(LOCAL_UNROLL):
            c = it * LOCAL_UNROLL + u
            rows = pl.ds(pl.multiple_of(c * CHUNK, CHUNK), CHUNK)
            units += [(c, gi, rows, slice(gi * GROUP, (gi + 1) * GROUP)) for gi in range(n_groups)]

        def load(ref, stack):
            return [stack(ref[rows, cols]) for (_, _, rows, cols) in units]

        def each(f, *lists):
            return [f(*args) for args in zip(*lists)]

        b16 = lambda xs: [x.astype(BF16) for x in xs]
        rt4 = load(rt_s, stack_masked)
        at4, rt4b, v4 = b16(load(at_s, stack_masked)), b16(rt4), load(v_s, stack_masked)
        v4b, bh4 = b16(v4), b16(load(bh_s, stack_masked))
        kh4 = b16(load(kh_s, stack_masked))
        bt_rep, kt_rep = b16(load(bt_s, stack_plain)), b16(load(kt_s, stack_plain))
        a_ab = each(lambda a, b: jnp.where(strict, _dot_nt(a, b), 0.0), at4, bt_rep)
        a_ak = b16(each(lambda a, b: jnp.where(strict, _dot_nt(a, b), 0.0), at4, kt_rep))
        a_rb = b16(each(lambda a, b: jnp.where(incl, _dot_nt(a, b), 0.0), rt4b, bt_rep))
        a_rk = b16(each(lambda a, b: jnp.where(incl, _dot_nt(a, b), 0.0), rt4b, kt_rep))
        tinv = [eye + a for a in a_ab]
        pw = b16(a_ab)
        span = 2
        while span < CHUNK:
            pw = b16(each(_dot, pw, pw))
            tinv = each(lambda t, p: t + _dot(p, t.astype(BF16)), tinv, pw)
            span *= 2
        tinv = b16(tinv)
        p4 = each(_dot, tinv, at4)
        q4 = each(_dot, tinv, b16(each(_dot, a_ak, v4b)))
        p4b, q4b = b16(p4), b16(q4)
        g4 = each(lambda r, a, p: r + _dot(a, p), rt4, a_rb, p4b)
        y0 = each(lambda a, q, ak, v: _dot(a, q) + _dot(ak, v), a_rb, q4b, a_rk, v4b)
        m4 = each(lambda p, b: _dot(p.T.astype(BF16), b), p4, bh4)
        n4 = each(lambda q, b, v, k: _dot(q.T.astype(BF16), b) + _dot(v.T.astype(BF16), k),
                  q4, bh4, v4, kh4)
        for (c, gi, _, _), g, y, m, n in zip(units, g4, y0, m4, n4):
            g_s[c, gi] = g.astype(BF16)
            y0_s[c, gi] = y
            m_s[c, gi] = m.astype(BF16)
            n_s[c, gi] = n
        return carry

    lax.fori_loop(0, tm // CHUNK // LOCAL_UNROLL, local_body, 0)

    def carry_body(c, carry):
        r0 = pl.multiple_of(c * CHUNK, CHUNK)
        rows = pl.ds(r0, CHUNK)
        for gi in range(n_groups):
            cols = slice(gi * GROUP, (gi + 1) * GROUP)
            zt = z_s[gi]
            zb = zt.astype(BF16)
            y4 = _dot_nt(g_s[c, gi], zb) + y0_s[c, gi]
            z_s[gi] = zt * wl_s[pl.ds(r0, 1), cols] + _dot(zb, m_s[c, gi]) + n_s[c, gi]
            ytok = y4[:CHUNK]
            for h in range(1, heads_per_group):
                ytok = ytok + y4[h * CHUNK: (h + 1) * CHUNK]
            y_s[rows, cols] = ytok
        return carry

    lax.fori_loop(0, tm // CHUNK, carry_body, 0, unroll=True)

    y = y_s[...]
    inv_n = 1.0 / HEAD_DIM
    mean = _head_sums(y, bd) * inv_n
    d = y - mean
    var = _head_sums(d * d, bd) * inv_n
    yn = d * lax.rsqrt(var + LN_X_EPS) * lng_ref[...] + lnb_ref[...]
    bonus = _head_sums(r * km * rk_ref[...], bd) * v
    o_ref[...] = ((yn + bonus) * g).astype(BF16)


def _split_dot_left(a_exact, b, terms=2):
    acc = None
    rem = b
    for _ in range(terms):
        part = rem.astype(BF16)
        rem = rem - part.astype(F32)
        d = _dot(a_exact, part)
        acc = d if acc is None else acc + d
    return acc


def _rwkv(pr, mu, w0, a0, lora, wg, k_k, k_a, r_k, ln_g, ln_b, bd, B, S, width, tm=512):
    T = B * S
    n_t = S // tm
    cols = pr.shape[1]
    ti = jnp.arange(CHUNK)
    tri = (ti[:, None] >= ti[None, :]).astype(BF16)
    const = lambda b, i: (0, 0)
    vec = lambda n: pl.BlockSpec((1, n), const)
    scr = lambda: pltpu.VMEM((tm, width), F32)
    return pl.pallas_call(
        functools.partial(_rwkv_kernel, width=width, tm=tm),
        grid=(B, n_t),
        in_specs=[
            pl.BlockSpec((tm, cols), lambda b, i: (b * n_t + i, 0)),
            pl.BlockSpec((8, cols), lambda b, i: (jnp.maximum((b * n_t + i) * (tm // 8) - 1, 0), 0)),
            vec(cols), vec(width), vec(width),
            pl.BlockSpec(lora.shape, const),
            pl.BlockSpec(wg.shape, const),
            vec(width), vec(width), vec(width), vec(width), vec(width),
            pl.BlockSpec((width, width), const),
            pl.BlockSpec((CHUNK, CHUNK), const),
        ],
        out_specs=pl.BlockSpec((tm, width), lambda b, i: (b * n_t + i, 0)),
        out_shape=jax.ShapeDtypeStruct((T, width), BF16),
        scratch_shapes=[scr() for _ in range(9)]
        + [pltpu.VMEM((width // GROUP, GROUP, GROUP), F32)]
        + [pltpu.VMEM((tm // CHUNK, width // GROUP, GROUP, GROUP), dt)
           for dt in (BF16, F32, BF16, F32)],
        compiler_params=pltpu.CompilerParams(
            dimension_semantics=("arbitrary", "arbitrary"), vmem_limit_bytes=VMEM_LIMIT),
        name="rwkv7",
    )(pr, pr, mu, w0, a0, lora, wg, k_k, k_a, r_k, ln_g, ln_b, bd, tri)


def _outproj_kernel(att_ref, rwk_ref, x_ref, wa_ref, wr_ref, g_ref, wrt_ref, br_ref, upper_ref,
                    x2_ref, xn_ref, eidx_ref, rank_ref, gate_ref, cnt_ref, *, n_exp, tm):
    x2 = x_ref[...] + _dot(att_ref[...], wa_ref[...]) + _dot(rwk_ref[...], wr_ref[...])
    x2_ref[...] = x2
    ms = jnp.mean(x2 * x2, axis=-1, keepdims=True)
    xn = x2 * lax.rsqrt(ms + NORM_EPS) * g_ref[...]
    xn_ref[...] = xn.astype(BF16)

    x_hi = xn.astype(BF16)
    x_lo = (xn - x_hi.astype(F32)).astype(BF16)
    logits = (_dot_nt(wrt_ref[0], x_hi) + _dot_nt(wrt_ref[0], x_lo)
              + _dot_nt(wrt_ref[1], x_hi)) + br_ref[...]

    eio = lax.broadcasted_iota(jnp.int32, (n_exp, tm), 0)
    work = logits
    vals, onehots = [], []
    for k in range(TOP_K):
        m = jnp.max(work, axis=0, keepdims=True)
        idx = jnp.min(jnp.where(work == m, eio, n_exp), axis=0, keepdims=True)
        oh = eio == idx
        vals.append(m)
        onehots.append(oh)
        eidx_ref[k: k + 1, :] = idx
        work = jnp.where(oh, -jnp.inf, work)

    exps = [jnp.exp(vk - vals[0]) for vk in vals]
    denom = exps[0]
    for e in exps[1:]:
        denom = denom + e
    for k in range(TOP_K):
        gate_ref[k: k + 1, :] = exps[k] / denom

    sel = onehots[0]
    for oh in onehots[1:]:
        sel = sel | oh
    sel_f = jnp.where(sel, 1.0, 0.0)
    before = _dot(sel_f.astype(BF16), upper_ref[...])
    for k in range(TOP_K):
        rank_ref[k: k + 1, :] = jnp.sum(
            jnp.where(onehots[k], before, 0.0), axis=0, keepdims=True).astype(jnp.int32)
    cnt_ref[...] = jnp.broadcast_to(
        jnp.sum(sel_f, axis=1, keepdims=True), cnt_ref.shape).astype(jnp.int32)


def _outproj(att, rwk, x2d, w_att, w_rwk, g2, wrt, br, n_exp):
    T, D = x2d.shape
    tm = ROUTE_TILE
    upper = (jnp.arange(tm)[:, None] < jnp.arange(tm)[None, :]).astype(BF16)
    const = lambda i: (0, 0)
    rows = lambda n: pl.BlockSpec((tm, n), lambda i: (i, 0))
    lanes = pl.BlockSpec((TOP_K, tm), lambda i: (0, i))
    return pl.pallas_call(
        functools.partial(_outproj_kernel, n_exp=n_exp, tm=tm),
        grid=(T // tm,),
        in_specs=[
            rows(att.shape[1]), rows(rwk.shape[1]), rows(D),
            pl.BlockSpec(w_att.shape, const), pl.BlockSpec(w_rwk.shape, const),
            pl.BlockSpec((1, D), const),
            pl.BlockSpec(wrt.shape, lambda i: (0, 0, 0)),
            pl.BlockSpec((n_exp, 1), const),
            pl.BlockSpec((tm, tm), const),
        ],
        out_specs=[rows(D), rows(D), lanes, lanes, lanes,
                   pl.BlockSpec((n_exp, LANES), lambda i: (i, 0))],
        out_shape=[
            jax.ShapeDtypeStruct((T, D), F32),
            jax.ShapeDtypeStruct((T, D), BF16),
            jax.ShapeDtypeStruct((TOP_K, T), jnp.int32),
            jax.ShapeDtypeStruct((TOP_K, T), jnp.int32),
            jax.ShapeDtypeStruct((TOP_K, T), F32),
            jax.ShapeDtypeStruct((T // tm * n_exp, LANES), jnp.int32),
        ],
        compiler_params=pltpu.CompilerParams(
            dimension_semantics=("arbitrary",), vmem_limit_bytes=VMEM_LIMIT),
        name="outproj_router",
    )(att, rwk, x2d, w_att, w_rwk, g2, wrt, br, upper)


def _pow2_sizes(limit):
    top = 1 << (limit.bit_length() - 1)
    return tuple(top >> s for s in range((top // RUN_ALIGN).bit_length()))


def _run_copies(action, length, src, src_off, dst, dst_off, sem, sizes=RUN_SIZES):
    for size in sizes:
        start = length & ~(2 * size - 1)

        @pl.when((length & size) != 0)
        def _():
            cp = pltpu.make_async_copy(
                src.at[pl.ds(pl.multiple_of(src_off + start, RUN_ALIGN), size)],
                dst.at[pl.ds(pl.multiple_of(dst_off + start, RUN_ALIGN), size)], sem)
            cp.start() if action == "start" else cp.wait()


def _dispatch_kernel(soff_ref, rlen_ref, rdst_ref, tail_dst_ref, tail_len_ref, nact_ref,
                     slot_ref, xn_ref, xs_hbm, buf, zbuf, sem, run_sems, *, n_exp, slots, n_blocks):
    j = pl.program_id(0)
    tm = ROUTE_TILE

    @pl.when(j == 0)
    def _():
        zbuf[...] = jnp.zeros_like(zbuf)

        def unused_block(action):
            def body(b, c):
                cp = pltpu.make_async_copy(
                    zbuf, xs_hbm.at[pl.ds(pl.multiple_of(b * MOE_ROWS, MOE_ROWS), MOE_ROWS)], sem)
                cp.start() if action == "start" else cp.wait()
                return c
            return body

        for action in ("start", "wait"):
            for e in range(n_exp):
                _run_copies(action, tail_len_ref[e], zbuf, 0, xs_hbm, tail_dst_ref[e], sem)
            lax.fori_loop(nact_ref[0], n_blocks, unused_block(action), 0)

    cur = lax.rem(j, 2)
    x = xn_ref[...]
    for sb in range(slots // GROUP):
        sio = lax.broadcasted_iota(jnp.int32, (GROUP, tm), 0) + sb * GROUP
        p = jnp.zeros((GROUP, tm), F32)
        for k in range(TOP_K):
            p = jnp.where(sio == slot_ref[0, k: k + 1, :], 1.0, p)
        buf[cur, sb * GROUP: (sb + 1) * GROUP, :] = _pack_bf16_pairs(_dot(p.astype(BF16), x))

    def start_runs(tile, half):
        for e in range(n_exp):
            r = tile * n_exp + e
            _run_copies("start", rlen_ref[r], buf.at[half], soff_ref[r], xs_hbm, rdst_ref[r],
                        run_sems.at[half])

    def wait_runs(tile, half):
        last = tile * n_exp + n_exp - 1
        _run_copies("wait", soff_ref[last] + rlen_ref[last], buf.at[half], 0, xs_hbm, 0,
                    run_sems.at[half], sizes=_pow2_sizes(slots))

    start_runs(j, cur)

    @pl.when(j > 0)
    def _():
        wait_runs(j - 1, 1 - cur)

    @pl.when(j == pl.num_programs(0) - 1)
    def _():
        wait_runs(j, cur)


def _dispatch(soff, rlen, rdst, tail_dst, tail_len, n_active, slot_lane, xn, n_blocks, slots):
    T, D = xn.shape
    tm = ROUTE_TILE
    n_exp = tail_dst.shape[0]
    n_rows = n_blocks * MOE_ROWS
    grid_spec = pltpu.PrefetchScalarGridSpec(
        num_scalar_prefetch=6,
        grid=(T // tm,),
        in_specs=[
            pl.BlockSpec((1, TOP_K, tm), lambda i, *_: (i, 0, 0)),
            pl.BlockSpec((tm, D), lambda i, *_: (i, 0)),
        ],
        out_specs=pl.BlockSpec(memory_space=pl.ANY),
        scratch_shapes=[pltpu.VMEM((2, slots, D // 2), jnp.uint32),
                        pltpu.VMEM((MOE_ROWS, D // 2), jnp.uint32),
                        pltpu.SemaphoreType.DMA, pltpu.SemaphoreType.DMA((2,))],
    )
    return pl.pallas_call(
        functools.partial(_dispatch_kernel, n_exp=n_exp, slots=slots, n_blocks=n_blocks),
        grid_spec=grid_spec,
        out_shape=jax.ShapeDtypeStruct((n_rows, D // 2), jnp.uint32),
        compiler_params=pltpu.CompilerParams(
            dimension_semantics=("arbitrary",), vmem_limit_bytes=VMEM_LIMIT),
        name="moe_dispatch",
    )(soff, rlen, rdst, tail_dst, tail_len, n_active, slot_lane, xn)


def _moe_kernel(bexp_ref, nact_ref, x_ref, wup_ref, bup_ref, wdn_ref, bdn_ref,
                y_ref, wup_bf, wdn_bf, *, d_exp):
    i = pl.program_id(0)

    @pl.when(i < nact_ref[0])
    def _():
        new_expert = jnp.logical_or(i == 0, bexp_ref[i] != bexp_ref[jnp.maximum(i - 1, 0)])

        @pl.when(new_expert)
        def _():
            wup_bf[...] = wup_ref[0].astype(BF16)
            wdn_bf[...] = wdn_ref[0].astype(BF16)

        x = _unpack_bf16_pairs(x_ref[...])
        h_glu = _dot(x, wup_bf[:, :d_exp]) + bup_ref[0, :, :d_exp]
        h_lin = _dot(x, wup_bf[:, d_exp:]) + bup_ref[0, :, d_exp:]
        h_glu = jnp.minimum(h_glu, SWIGLU_LIMIT)
        h_lin = jnp.clip(h_lin, -SWIGLU_LIMIT, SWIGLU_LIMIT)
        act = h_glu * _sigmoid(SWIGLU_ALPHA * h_glu) * (h_lin + 1.0)
        y_ref[...] = _pack_bf16_pairs(_dot(act.astype(BF16), wdn_bf[...]) + bdn_ref[0])

    @pl.when(i >= nact_ref[0])
    def _():
        y_ref[...] = jnp.zeros_like(y_ref)


def _moe(blk_expert, n_active, xs, w_up, b_up, w_down, b_down):
    n_blocks = blk_expert.shape[0]
    n_exp, D, two_f = w_up.shape
    d_exp = two_f // 2
    blk = lambda i, na: jnp.minimum(i, na[0] - 1)
    grid_spec = pltpu.PrefetchScalarGridSpec(
        num_scalar_prefetch=2,
        grid=(n_blocks,),
        in_specs=[
            pl.BlockSpec((MOE_ROWS, D // 2), lambda i, be, na: (blk(i, na), 0)),
            pl.BlockSpec((1, D, two_f), lambda i, be, na: (be[blk(i, na)], 0, 0)),
            pl.BlockSpec((1, 1, two_f), lambda i, be, na: (be[blk(i, na)], 0, 0)),
            pl.BlockSpec((1, d_exp, D), lambda i, be, na: (be[blk(i, na)], 0, 0)),
            pl.BlockSpec((1, 1, D), lambda i, be, na: (be[blk(i, na)], 0, 0)),
        ],
        out_specs=pl.BlockSpec((MOE_ROWS, D // 2), lambda i, be, na: (i, 0)),
        scratch_shapes=[
            pltpu.VMEM((D, two_f), BF16),
            pltpu.VMEM((d_exp, D), BF16),
        ],
    )
    return pl.pallas_call(
        functools.partial(_moe_kernel, d_exp=d_exp),
        grid_spec=grid_spec,
        out_shape=jax.ShapeDtypeStruct((n_blocks * MOE_ROWS, D // 2), jnp.uint32),
        compiler_params=pltpu.CompilerParams(
            dimension_semantics=("arbitrary",), vmem_limit_bytes=VMEM_LIMIT),
        name="moe_ffn",
    )(blk_expert, n_active, xs,
      w_up, b_up.reshape(n_exp, 1, two_f), w_down, b_down.reshape(n_exp, 1, D))


def _combine_kernel(soff_ref, rlen_ref, rdst_ref, slot_ref, gate_ref, x2_ref, y_hbm, o_ref,
                    buf, sems, *, n_exp, slots):
    j = pl.program_id(0)
    tm = ROUTE_TILE

    cur = lax.rem(j, 2)

    def start_runs(tile, half):
        for e in range(n_exp):
            r = tile * n_exp + e
            _run_copies("start", rlen_ref[r], y_hbm, rdst_ref[r], buf.at[half], soff_ref[r],
                        sems.at[half])

    @pl.when(j == 0)
    def _():
        buf[...] = jnp.zeros_like(buf)
        start_runs(0, 0)

    @pl.when(j + 1 < pl.num_programs(0))
    def _():
        start_runs(j + 1, 1 - cur)

    last = j * n_exp + n_exp - 1
    _run_copies("wait", soff_ref[last] + rlen_ref[last], y_hbm, 0, buf.at[cur], 0, sems.at[cur],
                sizes=_pow2_sizes(slots))

    slot = slot_ref[...]
    gate = gate_ref[...]
    acc = x2_ref[...]
    for sb in range(slots // GROUP):
        lio = lax.broadcasted_iota(jnp.int32, (tm, GROUP), 1) + sb * GROUP
        w = jnp.zeros((tm, GROUP), F32)
        for k in range(TOP_K):
            w = jnp.where(lio == slot[:, k: k + 1], gate[:, k: k + 1], w)
        acc = acc + _dot(w.astype(BF16),
                         _unpack_bf16_pairs(buf[cur, sb * GROUP: (sb + 1) * GROUP, :]))
    o_ref[...] = acc


def _combine(soff, rlen, rdst, slot_row, gates_row, x2, y_rows, slots):
    T, D = x2.shape
    tm = ROUTE_TILE
    n_exp = soff.shape[0] // (T // tm)
    grid_spec = pltpu.PrefetchScalarGridSpec(
        num_scalar_prefetch=3,
        grid=(T // tm,),
        in_specs=[
            pl.BlockSpec((tm, TOP_K), lambda i, *_: (i, 0)),
            pl.BlockSpec((tm, TOP_K), lambda i, *_: (i, 0)),
            pl.BlockSpec((tm, D), lambda i, *_: (i, 0)),
            pl.BlockSpec(memory_space=pl.ANY),
        ],
        out_specs=pl.BlockSpec((tm, D), lambda i, *_: (i, 0)),
        scratch_shapes=[pltpu.VMEM((2, slots, D // 2), jnp.uint32),
                        pltpu.SemaphoreType.DMA((2,))],
    )
    return pl.pallas_call(
        functools.partial(_combine_kernel, n_exp=n_exp, slots=slots),
        grid_spec=grid_spec,
        out_shape=jax.ShapeDtypeStruct((T, D), F32),
        compiler_params=pltpu.CompilerParams(
            dimension_semantics=("arbitrary",), vmem_limit_bytes=VMEM_LIMIT),
        name="moe_combine",
    )(soff, rlen, rdst, slot_row, gates_row, x2, y_rows)


def _block_diag_ones(n, blk):
    i = jnp.arange(n)
    return ((i[:, None] // blk) == (i[None, :] // blk)).astype(BF16)


def _layer(x, norm1_g, w_in, tshift_mu, q_norm_g, k_norm_g, rel_bias, w0, w_decay_up, a0,
           w_aaa_up, w_gate_up, k_k, k_a, r_k, ln_x_g, ln_x_b, w_out, norm2_g, w_router,
           b_router, w_up, b_up, w_down, b_down):
    B, S, D = x.shape
    T = B * S
    att_heads = rel_bias.shape[0]
    att_w = att_heads * HEAD_DIM
    width = w0.shape[0]
    n_exp = w_router.shape[1]
    assert att_w % LANES == 0 and width % GROUP == 0
    assert S % ATT_QROWS == 0 and ATT_QROWS % ATT_HALO == 0 and ATT_HALO % ATT_SUB == 0
    assert T % ROUTE_TILE == 0

    x2d = x.reshape(T, D)
    row = lambda a: a.reshape(1, -1).astype(F32)
    bd_att = _block_diag_ones(att_w, HEAD_DIM)
    bd_rwkv = _block_diag_ones(width, HEAD_DIM)

    qkv, pr = _inproj(
        x2d, row(norm1_g), w_in.astype(BF16),
        row(jnp.tile(q_norm_g, att_heads)), row(jnp.tile(k_norm_g, att_heads)), bd_att, att_w)

    att = _attention(qkv, _attention_bias(rel_bias), B, S, att_w)

    lora = jnp.zeros((DECAY_LORA + AAA_LORA, 2 * width), F32)
    lora = lora.at[:DECAY_LORA, :width].set(w_decay_up).at[DECAY_LORA:, width:].set(w_aaa_up)
    rwk = _rwkv(pr, row(tshift_mu), row(w0), row(a0), lora.astype(BF16),
                w_gate_up.astype(BF16), row(k_k), row(k_a), row(r_k), row(ln_x_g),
                row(ln_x_b), bd_rwkv, B, S, width)

    wrt = w_router.T.astype(F32)
    wrt_hi = wrt.astype(BF16)
    wrt_lo = (wrt - wrt_hi.astype(F32)).astype(BF16)
    x2, xn, eidx, rank, gates, counts = _outproj(
        att, rwk, x2d, w_out[:att_w].astype(BF16), w_out[att_w:].astype(BF16), row(norm2_g),
        jnp.stack([wrt_hi, wrt_lo]), b_router.reshape(n_exp, 1).astype(F32), n_exp)

    tm = ROUTE_TILE
    n_tiles = T // tm
    i32 = lambda a: a.astype(jnp.int32)
    run_n = counts.reshape(n_tiles, n_exp, LANES)[:, :, 0]
    run_len = (run_n + RUN_ALIGN - 1) // RUN_ALIGN * RUN_ALIGN
    run_off = jnp.cumsum(run_len, axis=1) - run_len
    exp_rows = jnp.sum(run_len, axis=0)
    exp_pad = (exp_rows + MOE_ROWS - 1) // MOE_ROWS * MOE_ROWS
    exp_end = jnp.cumsum(exp_pad)
    exp_start = exp_end - exp_pad
    run_dst = exp_start[None, :] + jnp.cumsum(run_len, axis=0) - run_len
    eidx_t = eidx.reshape(TOP_K, n_tiles, tm)
    experts = jnp.arange(n_exp, dtype=jnp.int32)[:, None, None, None]
    slot = rank.reshape(TOP_K, n_tiles, tm) + jnp.sum(
        jnp.where(eidx_t[None] == experts, run_off.T[:, None, :, None], 0), axis=0)
    slot_lane = i32(slot.transpose(1, 0, 2))
    slot_row = i32(slot.transpose(1, 2, 0).reshape(T, TOP_K))
    slots = -(-(tm * TOP_K + n_exp * (RUN_ALIGN - 1)) // GROUP) * GROUP
    n_blocks = -(-(T * TOP_K + n_tiles * n_exp * (RUN_ALIGN - 1)) // MOE_ROWS) + n_exp
    blk_start = jnp.arange(n_blocks, dtype=jnp.int32) * MOE_ROWS
    blk_expert = i32(jnp.minimum(
        jnp.sum(blk_start[:, None] >= exp_end[None, :], axis=1), n_exp - 1))
    n_active = i32(exp_end[-1:] // MOE_ROWS)
    soff, rlen, rdst = (i32(a.reshape(-1)) for a in (run_off, run_len, run_dst))

    xs = _dispatch(soff, rlen, rdst, i32(exp_start + exp_rows), i32(exp_pad - exp_rows),
                   n_active, slot_lane, xn, n_blocks, slots)
    y_rows = _moe(blk_expert, n_active, xs, w_up, b_up, w_down, b_down)
    out = _combine(soff, rlen, rdst, slot_row, gates.T, x2, y_rows, slots)
    return out.reshape(B, S, D)


def kernel(x, norm1_g, w_in, tshift_mu, q_norm_g, k_norm_g, rel_bias, w0, w_decay_up, a0,
           w_aaa_up, w_gate_up, k_k, k_a, r_k, ln_x_g, ln_x_b, w_out, norm2_g, w_router,
           b_router, w_up, b_up, w_down, b_down):
    params = (norm1_g, w_in, tshift_mu, q_norm_g, k_norm_g, rel_bias, w0, w_decay_up, a0,
              w_aaa_up, w_gate_up, k_k, k_a, r_k, ln_x_g, ln_x_b, w_out, norm2_g, w_router,
              b_router, w_up, b_up, w_down, b_down)
    for l in range(norm1_g.shape[0]):
        x = _layer(x, *[p[l] for p in params])
    return x
```

```python
import functools
import math

import jax
import jax.numpy as jnp
from jax import lax
from jax.experimental import pallas as pl
from jax.experimental.pallas import tpu as pltpu

F32 = jnp.float32
BF16 = jnp.bfloat16

CHUNK = 64
HEAD_DIM = 64
LEFT_CHUNKS = 8
MAX_LEFT_REL = 128
TOP_K = 4
SWIGLU_ALPHA = 1.702
SWIGLU_LIMIT = 7.0
NORM_EPS = 1e-6
LN_X_EPS = 64e-5
DECAY_LORA = 64
AAA_LORA = 64
GATE_LORA = 128

LANES = 128
GROUP = 256
NEG_BIG = -1e30

VMEM_LIMIT = 56 * 1024 * 1024

ATT_QROWS = 2048
ATT_SUB = 256
ATT_HALO = LEFT_CHUNKS * CHUNK
ATT_KEYS = ATT_SUB + ATT_HALO
ATT_MASKED_TILES = ATT_HALO // ATT_SUB
ATT_INTERLEAVE = 4

LOCAL_UNROLL = 4

MOE_ROWS = 512
ROUTE_TILE = 512
RUN_ALIGN = 8
RUN_BIG = 128
RUN_SIZES = tuple(ROUTE_TILE >> s for s in range((ROUTE_TILE // RUN_ALIGN).bit_length()))
assert MOE_ROWS <= 2 * RUN_SIZES[0] and RUN_SIZES[-1] == RUN_ALIGN


def _dot(a, b):
    return jnp.dot(a, b, preferred_element_type=F32)


def _dot_nt(a, b):
    return lax.dot_general(a, b, (((1,), (1,)), ((), ())), preferred_element_type=F32)


def _head_sums(a, block_ones):
    a = a.astype(BF16)
    return jnp.concatenate(
        [_dot(a[:, j: j + GROUP], block_ones) for j in range(0, a.shape[1], GROUP)], axis=1)


def _pack_bf16_pairs(x):
    n = x.shape[1] // 2
    bits = pltpu.bitcast(x.astype(BF16).astype(F32), jnp.uint32)
    return bits[:, :n] | (bits[:, n:] >> 16)


def _unpack_bf16_pairs(w):
    hi = pltpu.bitcast(w & jnp.uint32(0xFFFF0000), F32)
    lo = pltpu.bitcast(w << 16, F32)
    return jnp.concatenate([hi, lo], axis=1).astype(BF16)


def _sigmoid(x):
    return 0.5 * jnp.tanh(0.5 * x) + 0.5


def _inproj_kernel(x_ref, g_ref, w_ref, qg_ref, kg_ref, bd_ref, qkv_ref, pr_ref, *, att_w):
    x = x_ref[...]
    ms = jnp.mean(x * x, axis=-1, keepdims=True)
    h = (x * lax.rsqrt(ms + NORM_EPS) * g_ref[...]).astype(BF16)
    qkv = _dot(h, w_ref[:, : 3 * att_w])
    bd = bd_ref[...]

    def head_norm(t, g):
        ss = _head_sums(t * t, bd)
        return t * lax.rsqrt(ss * (1.0 / HEAD_DIM) + NORM_EPS) * g

    q = head_norm(qkv[:, :att_w], qg_ref[...]) * (HEAD_DIM ** -0.5)
    k = head_norm(qkv[:, att_w: 2 * att_w], kg_ref[...])
    qkv_ref[:, :att_w] = q.astype(BF16)
    qkv_ref[:, att_w: 2 * att_w] = k.astype(BF16)
    qkv_ref[:, 2 * att_w:] = qkv[:, 2 * att_w:].astype(BF16)
    pr_ref[...] = _dot(h, w_ref[:, 3 * att_w:]).astype(BF16)


def _inproj(x2d, g1, w_in_bf, qg, kg, bd, att_w, tm=512):
    T, D = x2d.shape
    in_cols = w_in_bf.shape[1]
    r_cols = in_cols - 3 * att_w
    const = lambda i: (0, 0)
    return pl.pallas_call(
        functools.partial(_inproj_kernel, att_w=att_w),
        grid=(T // tm,),
        in_specs=[
            pl.BlockSpec((tm, D), lambda i: (i, 0)),
            pl.BlockSpec((1, D), const),
            pl.BlockSpec((D, in_cols), const),
            pl.BlockSpec((1, att_w), const),
            pl.BlockSpec((1, att_w), const),
            pl.BlockSpec((GROUP, GROUP), const),
        ],
        out_specs=[
            pl.BlockSpec((tm, 3 * att_w), lambda i: (i, 0)),
            pl.BlockSpec((tm, r_cols), lambda i: (i, 0)),
        ],
        out_shape=[
            jax.ShapeDtypeStruct((T, 3 * att_w), BF16),
            jax.ShapeDtypeStruct((T, r_cols), BF16),
        ],
        compiler_params=pltpu.CompilerParams(
            dimension_semantics=("arbitrary",), vmem_limit_bytes=VMEM_LIMIT),
        name="inproj",
    )(x2d, g1, w_in_bf, qg, kg, bd)


def _attn_kernel(q_ref, kp_ref, kc_ref, vp_ref, vc_ref, bias_ref, o_ref, kbuf, vbuf):
    i = pl.program_id(2)
    kbuf[:ATT_HALO, :] = kp_ref[...]
    kbuf[ATT_HALO:, :] = kc_ref[...]
    lane_head = lax.broadcasted_iota(jnp.int32, (1, LANES), 1) >> (HEAD_DIM.bit_length() - 1)
    one = jnp.ones((), BF16)
    for h in range(2):
        vbuf[h, :ATT_HALO, :] = jnp.where(lane_head == h, vp_ref[...], one)
        vbuf[h, ATT_HALO:, :] = jnp.where(lane_head == h, vc_ref[...], one)
    tiles = range(ATT_QROWS // ATT_SUB)
    col = lax.broadcasted_iota(jnp.int32, (1, ATT_KEYS), 1)
    all_units =[(qs * ATT_SUB, h * HEAD_DIM, h, qs) for qs in tiles for h in range(2)]
    ext = []
    for u0 in range(0, len(all_units), ATT_INTERLEAVE):
        units = all_units[u0: u0 + ATT_INTERLEAVE]
        s = [_dot_nt(q_ref[r0: r0 + ATT_SUB, lo: lo + HEAD_DIM],
                     kbuf[r0: r0 + ATT_KEYS, lo: lo + HEAD_DIM]) + bias_ref[h]
             for (r0, lo, h, qs) in units]
        s = [x + jnp.where((col < ATT_HALO - qs * ATT_SUB) & (i == 0), NEG_BIG, 0.0)
             if qs < ATT_MASKED_TILES else x for x, (_, _, _, qs) in zip(s, units)]
        m = [jnp.max(x, axis=-1, keepdims=True) for x in s]
        p = [jnp.exp(x - mx).astype(BF16) for x, mx in zip(s, m)]
        ext += [_dot(x, vbuf[h, r0: r0 + ATT_KEYS, :]) for x, (r0, _, h, _) in zip(p, units)]
    for qs in tiles:
        e0, e1 = ext[2 * qs], ext[2 * qs + 1]
        o_ref[qs * ATT_SUB: (qs + 1) * ATT_SUB, :] = jnp.where(
            lane_head == 0, e0 / pltpu.roll(e0, HEAD_DIM, axis=1),
            e1 / pltpu.roll(e1, HEAD_DIM, axis=1)).astype(BF16)


def _attention(qkv, bias, B, S, att_w):
    T = B * S
    n_i = S // ATT_QROWS
    n_hp = att_w // LANES
    heads_per = LANES // HEAD_DIM
    assert heads_per == 2

    def cur(col0):
        return lambda b, hp, i: (b * n_i + i, col0 + hp)

    halo_per_blk = ATT_QROWS // ATT_HALO

    def prev(col0):
        return lambda b, hp, i: (
            b * n_i * halo_per_blk + jnp.maximum(i * halo_per_blk - 1, 0), col0 + hp)

    blk = (ATT_QROWS, LANES)
    halo = (ATT_HALO, LANES)
    return pl.pallas_call(
        _attn_kernel,
        grid=(B, n_hp, n_i),
        in_specs=[
            pl.BlockSpec(blk, cur(0)),
            pl.BlockSpec(halo, prev(n_hp)),
            pl.BlockSpec(blk, cur(n_hp)),
            pl.BlockSpec(halo, prev(2 * n_hp)),
            pl.BlockSpec(blk, cur(2 * n_hp)),
            pl.BlockSpec((heads_per,) + bias.shape[1:], lambda b, hp, i: (hp, 0, 0)),
        ],
        out_specs=pl.BlockSpec(blk, lambda b, hp, i: (b * n_i + i, hp)),
        out_shape=jax.ShapeDtypeStruct((T, att_w), BF16),
        scratch_shapes=[
            pltpu.VMEM((ATT_HALO + ATT_QROWS, LANES), BF16),
            pltpu.VMEM((2, ATT_HALO + ATT_QROWS, LANES), BF16),
        ],
        compiler_params=pltpu.CompilerParams(
            dimension_semantics=("arbitrary", "arbitrary", "arbitrary"),
            vmem_limit_bytes=VMEM_LIMIT),
        name="band_attention",
    )(qkv, qkv, qkv, qkv, qkv, bias)


def _attention_bias(rel_bias):
    H = rel_bias.shape[0]
    table = rel_bias.astype(F32)
    far = LEFT_CHUNKS * CHUNK + ATT_SUB - 1
    n_left = far - MAX_LEFT_REL + 1
    period = ATT_SUB + ATT_KEYS + 1
    ext = jnp.concatenate([
        jnp.broadcast_to(table[:, :1], (H, n_left)), table[:, 1:],
        jnp.broadcast_to(table[:, -1:], (H, period - n_left - (table.shape[1] - 1)))], axis=1)
    flat = jnp.tile(ext, (1, ATT_SUB + 1))[:, ATT_SUB - 1: ATT_SUB - 1 + ATT_SUB * (period - 1)]
    bias = flat.reshape(H, ATT_SUB, period - 1)[:, :, :ATT_KEYS]
    r = jnp.arange(ATT_SUB)[:, None]
    c = jnp.arange(ATT_KEYS)[None, :]
    band = c // CHUNK - r // CHUNK
    ok = (band >= 0) & (band <= LEFT_CHUNKS)
    return jnp.where(ok[None], bias, NEG_BIG)


def _rwkv_kernel(p_ref, pprev_ref, mu_ref, w0_ref, a0_ref, lora_ref, wg_ref, kk_ref, ka_ref,
                 rk_ref, lng_ref, lnb_ref, bd_ref, tri_ref, o_ref,
                 rt_s, at_s, bt_s, kt_s, bh_s, kh_s, v_s, wl_s, y_s, gate_s, bonus_s,
                 z_s, g_s, y0_s, m_s, n_s, *, width, tm):
    i = pl.program_id(1)
    n_groups = width // GROUP
    heads_per_group = GROUP // HEAD_DIM

    @pl.when(i == 0)
    def _():
        z_s[...] = jnp.zeros_like(z_s)

    part_rows = LOCAL_UNROLL * CHUNK
    bd = bd_ref[...]

    def prep(part):
        r0 = part * part_rows
        rows = slice(r0, r0 + part_rows)
        p = p_ref[rows, :].astype(F32)
        if part == 0:
            has_prev = jnp.where(i > 0, 1.0, 0.0)
            prev_row = pprev_ref[7:8, :].astype(F32) * has_prev
        else:
            prev_row = p_ref[r0 - 16: r0, :].astype(F32)[15:16]
        row = lax.broadcasted_iota(jnp.int32, (part_rows, 1), 0)
        prev = jnp.where(row == 0, prev_row, pltpu.roll(p, 1, axis=0))
        ps = p + (prev - p) * mu_ref[...]

        r = ps[:, :width]
        k = ps[:, width: 2 * width]
        v = ps[:, 2 * width: 3 * width]
        wa = ps[:, 3 * width: 3 * width + DECAY_LORA + AAA_LORA]
        gd = ps[:, 3 * width + DECAY_LORA + AAA_LORA:]
        lane = lax.broadcasted_iota(jnp.int32, (1, DECAY_LORA + AAA_LORA), 1)
        wa = jnp.where(lane < DECAY_LORA, jnp.tanh(wa), wa)
        lw = _dot(wa.astype(BF16), lora_ref[...])
        logw = -math.exp(-0.5) * _sigmoid(w0_ref[...] + lw[:, :width])
        eta = _sigmoid(a0_ref[...] + lw[:, width:])
        gate_s[rows, :] = _dot(_sigmoid(gd).astype(BF16), wg_ref[...])

        kk = k * kk_ref[...]
        kk = kk * lax.rsqrt(jnp.maximum(_head_sums(kk * kk, bd), 1e-24))
        km = k * (1.0 + (eta - 1.0) * ka_ref[...])
        bb = kk * eta
        bonus_s[rows, :] = _head_sums(r * km * rk_ref[...], bd) * v

        tri = tri_ref[...]
        cs_chunks = [_split_dot_left(tri, logw[c * CHUNK: (c + 1) * CHUNK])
                     for c in range(LOCAL_UNROLL)]
        cs = jnp.concatenate(cs_chunks, axis=0)
        tot = jnp.concatenate(
            [jnp.broadcast_to(x[CHUNK - 1:], (CHUNK, width)) for x in cs_chunks], axis=0)
        e_pos = jnp.exp(cs)
        e_neg = jnp.exp(-cs)
        e_tail = jnp.exp(tot - cs)
        rt_s[rows, :] = r * e_pos
        at_s[rows, :] = -kk * jnp.exp(cs - logw)
        bt_s[rows, :] = bb * e_neg
        kt_s[rows, :] = km * e_neg
        bh_s[rows, :] = bb * e_tail
        kh_s[rows, :] = km * e_tail
        v_s[rows, :] = v
        wl_s[rows, :] = jnp.exp(tot)

    rr = lax.broadcasted_iota(jnp.int32, (GROUP, GROUP), 0)
    cc = lax.broadcasted_iota(jnp.int32, (GROUP, GROUP), 1)
    shift = CHUNK.bit_length() - 1
    same = (rr >> shift) == (cc >> shift)
    strict = same & (rr > cc)
    incl = same & (rr >= cc)
    eye = jnp.where(rr == cc, 1.0, 0.0)
    lane_head = lax.broadcasted_iota(jnp.int32, (1, GROUP), 1) >> shift

    def stack_masked(x):
        return jnp.concatenate(
            [jnp.where(lane_head == h, x, 0.0) for h in range(heads_per_group)], axis=0)

    def stack_plain(x):
        return jnp.concatenate([x] * heads_per_group, axis=0)

    def local(part):
        units = []
        for u in range(LOCAL_UNROLL):
            c = part * LOCAL_UNROLL + u
            rows = slice(c * CHUNK, (c + 1) * CHUNK)
            units += [(c, gi, rows, slice(gi * GROUP, (gi + 1) * GROUP)) for gi in range(n_groups)]

        def load(ref, stack):
            return [stack(ref[rows, cols]) for (_, _, rows, cols) in units]

        def each(f, *lists):
            return [f(*args) for args in zip(*lists)]

        b16 = lambda xs: [x.astype(BF16) for x in xs]
        rt4 = load(rt_s, stack_masked)
        at4, rt4b, v4 = b16(load(at_s, stack_masked)), b16(rt4), load(v_s, stack_masked)
        v4b, bh4 = b16(v4), b16(load(bh_s, stack_masked))
        kh4 = b16(load(kh_s, stack_masked))
        bt_rep, kt_rep = b16(load(bt_s, stack_plain)), b16(load(kt_s, stack_plain))
        a_ab = each(lambda a, b: jnp.where(strict, _dot_nt(a, b), 0.0), at4, bt_rep)
        a_ak = b16(each(lambda a, b: jnp.where(strict, _dot_nt(a, b), 0.0), at4, kt_rep))
        a_rb = b16(each(lambda a, b: jnp.where(incl, _dot_nt(a, b), 0.0), rt4b, bt_rep))
        a_rk = b16(each(lambda a, b: jnp.where(incl, _dot_nt(a, b), 0.0), rt4b, kt_rep))
        tinv = [eye + a for a in a_ab]
        pw = b16(a_ab)
        span = 2
        while span < CHUNK:
            pw = b16(each(_dot, pw, pw))
            tinv = each(lambda t, p: t + _dot(p, t.astype(BF16)), tinv, pw)
            span *= 2
        tinv = b16(tinv)
        p4 = each(_dot, tinv, at4)
        q4 = each(_dot, tinv, b16(each(_dot, a_ak, v4b)))
        p4b, q4b = b16(p4), b16(q4)
        g4 = each(lambda r, a, p: r + _dot(a, p), rt4, a_rb, p4b)
        y0 = each(lambda a, q, ak, v: _dot(a, q) + _dot(ak, v), a_rb, q4b, a_rk, v4b)
        m4 = each(lambda p, b: _dot(p.T.astype(BF16), b), p4, bh4)
        n4 = each(lambda q, b, v, k: _dot(q.T.astype(BF16), b) + _dot(v.T.astype(BF16), k),
                  q4, bh4, v4, kh4)
        for (c, gi, _, _), g, y, m, n in zip(units, g4, y0, m4, n4):
            g_s[c, gi] = g.astype(BF16)
            y0_s[c, gi] = y
            m_s[c, gi] = m.astype(BF16)
            n_s[c, gi] = n

    def carry(part):
        for c in range(part * LOCAL_UNROLL, (part + 1) * LOCAL_UNROLL):
            rows = slice(c * CHUNK, (c + 1) * CHUNK)
            for gi in range(n_groups):
                cols = slice(gi * GROUP, (gi + 1) * GROUP)
                zt = z_s[gi]
                zb = zt.astype(BF16)
                y4 = _dot_nt(g_s[c, gi], zb) + y0_s[c, gi]
                z_s[gi] = (zt * wl_s[c * CHUNK: c * CHUNK + 1, cols]
                           + _dot(zb, m_s[c, gi]) + n_s[c, gi])
                ytok = y4[:CHUNK]
                for h in range(1, heads_per_group):
                    ytok = ytok + y4[h * CHUNK: (h + 1) * CHUNK]
                y_s[rows, cols] = ytok
        rows = slice(part * part_rows, (part + 1) * part_rows)
        y = y_s[rows, :]
        inv_n = 1.0 / HEAD_DIM
        mean = _head_sums(y, bd) * inv_n
        d = y - mean
        var = _head_sums(d * d, bd) * inv_n
        yn = d * lax.rsqrt(var + LN_X_EPS) * lng_ref[...] + lnb_ref[...]
        o_ref[rows, :] = ((yn + bonus_s[rows, :]) * gate_s[rows, :]).astype(BF16)

    n_parts = tm // part_rows
    prep(0)
    for part in range(n_parts):
        if part + 1 < n_parts:
            prep(part + 1)
        local(part)
        if part > 0:
            carry(part - 1)
    carry(n_parts - 1)


def _split_dot_left(a_exact, b, terms=2):
    acc = None
    rem = b
    for _ in range(terms):
        part = rem.astype(BF16)
        rem = rem - part.astype(F32)
        d = _dot(a_exact, part)
        acc = d if acc is None else acc + d
    return acc


def _rwkv(pr, mu, w0, a0, lora, wg, k_k, k_a, r_k, ln_g, ln_b, bd, B, S, width, tm=512):
    T = B * S
    n_t = S // tm
    cols = pr.shape[1]
    ti = jnp.arange(CHUNK)
    tri = (ti[:, None] >= ti[None, :]).astype(BF16)
    const = lambda b, i: (0, 0)
    vec = lambda n: pl.BlockSpec((1, n), const)
    scr = lambda: pltpu.VMEM((tm, width), F32)
    return pl.pallas_call(
        functools.partial(_rwkv_kernel, width=width, tm=tm),
        grid=(B, n_t),
        in_specs=[
            pl.BlockSpec((tm, cols), lambda b, i: (b * n_t + i, 0)),
            pl.BlockSpec((8, cols), lambda b, i: (jnp.maximum((b * n_t + i) * (tm // 8) - 1, 0), 0)),
            vec(cols), vec(width), vec(width),
            pl.BlockSpec(lora.shape, const),
            pl.BlockSpec(wg.shape, const),
            vec(width), vec(width), vec(width), vec(width), vec(width),
            pl.BlockSpec((GROUP, GROUP), const),
            pl.BlockSpec((CHUNK, CHUNK), const),
        ],
        out_specs=pl.BlockSpec((tm, width), lambda b, i: (b * n_t + i, 0)),
        out_shape=jax.ShapeDtypeStruct((T, width), BF16),
        scratch_shapes=[scr() for _ in range(11)]
        + [pltpu.VMEM((width // GROUP, GROUP, GROUP), F32)]
        + [pltpu.VMEM((tm // CHUNK, width // GROUP, GROUP, GROUP), dt)
           for dt in (BF16, F32, BF16, F32)],
        compiler_params=pltpu.CompilerParams(
            dimension_semantics=("arbitrary", "arbitrary"), vmem_limit_bytes=VMEM_LIMIT),
        name="rwkv7",
    )(pr, pr, mu, w0, a0, lora, wg, k_k, k_a, r_k, ln_g, ln_b, bd, tri)


def _outproj_kernel(att_ref, rwk_ref, x_ref, wa_ref, wr_ref, g_ref, wrt_ref, br_ref, upper_ref,
                    x2_ref, xn_ref, eidx_ref, rank_ref, gate_ref, cnt_ref, *, n_exp, tm):
    x2 = x_ref[...] + _dot(att_ref[...], wa_ref[...]) + _dot(rwk_ref[...], wr_ref[...])
    x2_ref[...] = x2
    ms = jnp.mean(x2 * x2, axis=-1, keepdims=True)
    xn = x2 * lax.rsqrt(ms + NORM_EPS) * g_ref[...]
    xn_ref[...] = xn.astype(BF16)

    x_hi = xn.astype(BF16)
    x_lo = (xn - x_hi.astype(F32)).astype(BF16)
    logits = (_dot_nt(wrt_ref[0], x_hi) + _dot_nt(wrt_ref[0], x_lo)
              + _dot_nt(wrt_ref[1], x_hi)) + br_ref[...]

    eio = lax.broadcasted_iota(jnp.int32, (n_exp, tm), 0)
    work = logits
    vals, onehots = [], []
    for k in range(TOP_K):
        m = jnp.max(work, axis=0, keepdims=True)
        idx = jnp.min(jnp.where(work == m, eio, n_exp), axis=0, keepdims=True)
        oh = eio == idx
        vals.append(m)
        onehots.append(oh)
        eidx_ref[k: k + 1, :] = idx
        work = jnp.where(oh, -jnp.inf, work)

    exps = [jnp.exp(vk - vals[0]) for vk in vals]
    denom = exps[0]
    for e in exps[1:]:
        denom = denom + e
    for k in range(TOP_K):
        gate_ref[k: k + 1, :] = exps[k] / denom

    sel = onehots[0]
    for oh in onehots[1:]:
        sel = sel | oh
    sel_f = jnp.where(sel, 1.0, 0.0)
    before = _dot(sel_f.astype(BF16), upper_ref[...])
    for k in range(TOP_K):
        rank_ref[k: k + 1, :] = jnp.sum(
            jnp.where(onehots[k], before, 0.0), axis=0, keepdims=True).astype(jnp.int32)
    cnt_ref[...] = jnp.broadcast_to(
        jnp.sum(sel_f, axis=1, keepdims=True), cnt_ref.shape).astype(jnp.int32)


def _outproj(att, rwk, x2d, w_att, w_rwk, g2, wrt, br, n_exp):
    T, D = x2d.shape
    tm = ROUTE_TILE
    upper = (jnp.arange(tm)[:, None] < jnp.arange(tm)[None, :]).astype(BF16)
    const = lambda i: (0, 0)
    rows = lambda n: pl.BlockSpec((tm, n), lambda i: (i, 0))
    lanes = pl.BlockSpec((TOP_K, tm), lambda i: (0, i))
    return pl.pallas_call(
        functools.partial(_outproj_kernel, n_exp=n_exp, tm=tm),
        grid=(T // tm,),
        in_specs=[
            rows(att.shape[1]), rows(rwk.shape[1]), rows(D),
            pl.BlockSpec(w_att.shape, const), pl.BlockSpec(w_rwk.shape, const),
            pl.BlockSpec((1, D), const),
            pl.BlockSpec(wrt.shape, lambda i: (0, 0, 0)),
            pl.BlockSpec((n_exp, 1), const),
            pl.BlockSpec((tm, tm), const),
        ],
        out_specs=[rows(D), rows(D), lanes, lanes, lanes,
                   pl.BlockSpec((n_exp, LANES), lambda i: (i, 0))],
        out_shape=[
            jax.ShapeDtypeStruct((T, D), F32),
            jax.ShapeDtypeStruct((T, D), BF16),
            jax.ShapeDtypeStruct((TOP_K, T), jnp.int32),
            jax.ShapeDtypeStruct((TOP_K, T), jnp.int32),
            jax.ShapeDtypeStruct((TOP_K, T), F32),
            jax.ShapeDtypeStruct((T // tm * n_exp, LANES), jnp.int32),
        ],
        compiler_params=pltpu.CompilerParams(
            dimension_semantics=("arbitrary",), vmem_limit_bytes=VMEM_LIMIT),
        name="outproj_router",
    )(att, rwk, x2d, w_att, w_rwk, g2, wrt, br, upper)


def _pow2_sizes(limit):
    top = 1 << (limit.bit_length() - 1)
    return tuple(top >> s for s in range((top // RUN_ALIGN).bit_length()))


def _run_copies(action, length, src, src_off, dst, dst_off, sem, sizes=RUN_SIZES):
    def copies(some_sizes):
        for size in some_sizes:
            start = length & ~(2 * size - 1)

            @pl.when((length & size) != 0)
            def _():
                cp = pltpu.make_async_copy(
                    src.at[pl.ds(pl.multiple_of(src_off + start, RUN_ALIGN), size)],
                    dst.at[pl.ds(pl.multiple_of(dst_off + start, RUN_ALIGN), size)], sem)
                cp.start() if action == "start" else cp.wait()

    big = tuple(sz for sz in sizes if sz >= RUN_BIG)
    if len(big) > 1:
        pl.when(length >= RUN_BIG)(lambda: copies(big))
        copies(tuple(sz for sz in sizes if sz < RUN_BIG))
    else:
        copies(sizes)


def _dispatch_kernel(soff_ref, rlen_ref, rdst_ref, tail_dst_ref, tail_len_ref, nact_ref,
                     slot_ref, xn_ref, xs_hbm, buf, zbuf, sem, run_sems, *, n_exp, slots, n_blocks):
    j = pl.program_id(0)
    tm = ROUTE_TILE

    @pl.when(j == 0)
    def _():
        zbuf[...] = jnp.zeros_like(zbuf)

        def unused_block(action):
            def body(b, c):
                cp = pltpu.make_async_copy(
                    zbuf, xs_hbm.at[pl.ds(pl.multiple_of(b * MOE_ROWS, MOE_ROWS), MOE_ROWS)], sem)
                cp.start() if action == "start" else cp.wait()
                return c
            return body

        for action in ("start", "wait"):
            for e in range(n_exp):
                _run_copies(action, tail_len_ref[e], zbuf, 0, xs_hbm, tail_dst_ref[e], sem)
            lax.fori_loop(nact_ref[0], n_blocks, unused_block(action), 0)

    cur = lax.rem(j, 2)
    x = xn_ref[...]
    for sb in range(slots // GROUP):
        sio = lax.broadcasted_iota(jnp.int32, (GROUP, tm), 0) + sb * GROUP
        p = jnp.zeros((GROUP, tm), F32)
        for k in range(TOP_K):
            p = jnp.where(sio == slot_ref[0, k: k + 1, :], 1.0, p)
        buf[cur, sb * GROUP: (sb + 1) * GROUP, :] = _pack_bf16_pairs(_dot(p.astype(BF16), x))

    def start_runs(tile, half):
        for e in range(n_exp):
            r = tile * n_exp + e
            _run_copies("start", rlen_ref[r], buf.at[half], soff_ref[r], xs_hbm, rdst_ref[r],
                        run_sems.at[half])

    def wait_runs(tile, half):
        last = tile * n_exp + n_exp - 1
        _run_copies("wait", soff_ref[last] + rlen_ref[last], buf.at[half], 0, xs_hbm, 0,
                    run_sems.at[half], sizes=_pow2_sizes(slots))

    start_runs(j, cur)

    @pl.when(j > 0)
    def _():
        wait_runs(j - 1, 1 - cur)

    @pl.when(j == pl.num_programs(0) - 1)
    def _():
        wait_runs(j, cur)


def _dispatch(soff, rlen, rdst, tail_dst, tail_len, n_active, slot_lane, xn, n_blocks, slots):
    T, D = xn.shape
    tm = ROUTE_TILE
    n_exp = tail_dst.shape[0]
    n_rows = n_blocks * MOE_ROWS
    grid_spec = pltpu.PrefetchScalarGridSpec(
        num_scalar_prefetch=6,
        grid=(T // tm,),
        in_specs=[
            pl.BlockSpec((1, TOP_K, tm), lambda i, *_: (i, 0, 0)),
            pl.BlockSpec((tm, D), lambda i, *_: (i, 0)),
        ],
        out_specs=pl.BlockSpec(memory_space=pl.ANY),
        scratch_shapes=[pltpu.VMEM((2, slots, D // 2), jnp.uint32),
                        pltpu.VMEM((MOE_ROWS, D // 2), jnp.uint32),
                        pltpu.SemaphoreType.DMA, pltpu.SemaphoreType.DMA((2,))],
    )
    return pl.pallas_call(
        functools.partial(_dispatch_kernel, n_exp=n_exp, slots=slots, n_blocks=n_blocks),
        grid_spec=grid_spec,
        out_shape=jax.ShapeDtypeStruct((n_rows, D // 2), jnp.uint32),
        compiler_params=pltpu.CompilerParams(
            dimension_semantics=("arbitrary",), vmem_limit_bytes=VMEM_LIMIT),
        name="moe_dispatch",
    )(soff, rlen, rdst, tail_dst, tail_len, n_active, slot_lane, xn)


def _moe_kernel(bexp_ref, nact_ref, x_ref, wup_ref, bup_ref, wdn_ref, bdn_ref,
                y_ref, wup_bf, wdn_bf, *, d_exp):
    i = pl.program_id(0)

    @pl.when(i < nact_ref[0])
    def _():
        new_expert = jnp.logical_or(i == 0, bexp_ref[i] != bexp_ref[jnp.maximum(i - 1, 0)])

        @pl.when(new_expert)
        def _():
            wup_bf[...] = wup_ref[0].astype(BF16)
            wdn_bf[...] = wdn_ref[0].astype(BF16)

        x = _unpack_bf16_pairs(x_ref[...])
        h_glu = _dot(x, wup_bf[:, :d_exp]) + bup_ref[0, :, :d_exp]
        h_lin = _dot(x, wup_bf[:, d_exp:]) + bup_ref[0, :, d_exp:]
        h_glu = jnp.minimum(h_glu, SWIGLU_LIMIT)
        h_lin = jnp.clip(h_lin, -SWIGLU_LIMIT, SWIGLU_LIMIT)
        act = h_glu * _sigmoid(SWIGLU_ALPHA * h_glu) * (h_lin + 1.0)
        y_ref[...] = _pack_bf16_pairs(_dot(act.astype(BF16), wdn_bf[...]) + bdn_ref[0])

    @pl.when(i >= nact_ref[0])
    def _():
        y_ref[...] = jnp.zeros_like(y_ref)


def _moe(blk_expert, n_active, xs, w_up, b_up, w_down, b_down):
    n_blocks = blk_expert.shape[0]
    n_exp, D, two_f = w_up.shape
    d_exp = two_f // 2
    blk = lambda i, na: jnp.minimum(i, na[0] - 1)
    grid_spec = pltpu.PrefetchScalarGridSpec(
        num_scalar_prefetch=2,
        grid=(n_blocks,),
        in_specs=[
            pl.BlockSpec((MOE_ROWS, D // 2), lambda i, be, na: (blk(i, na), 0)),
            pl.BlockSpec((1, D, two_f), lambda i, be, na: (be[blk(i, na)], 0, 0)),
            pl.BlockSpec((1, 1, two_f), lambda i, be, na: (be[blk(i, na)], 0, 0)),
            pl.BlockSpec((1, d_exp, D), lambda i, be, na: (be[blk(i, na)], 0, 0)),
            pl.BlockSpec((1, 1, D), lambda i, be, na: (be[blk(i, na)], 0, 0)),
        ],
        out_specs=pl.BlockSpec((MOE_ROWS, D // 2), lambda i, be, na: (i, 0)),
        scratch_shapes=[
            pltpu.VMEM((D, two_f), BF16),
            pltpu.VMEM((d_exp, D), BF16),
        ],
    )
    return pl.pallas_call(
        functools.partial(_moe_kernel, d_exp=d_exp),
        grid_spec=grid_spec,
        out_shape=jax.ShapeDtypeStruct((n_blocks * MOE_ROWS, D // 2), jnp.uint32),
        compiler_params=pltpu.CompilerParams(
            dimension_semantics=("arbitrary",), vmem_limit_bytes=VMEM_LIMIT),
        name="moe_ffn",
    )(blk_expert, n_active, xs,
      w_up, b_up.reshape(n_exp, 1, two_f), w_down, b_down.reshape(n_exp, 1, D))


def _combine_kernel(soff_ref, rlen_ref, rdst_ref, slot_ref, gate_ref, x2_ref, y_hbm, o_ref,
                    buf, sems, *, n_exp, slots):
    j = pl.program_id(0)
    tm = ROUTE_TILE

    cur = lax.rem(j, 2)

    def start_runs(tile, half):
        for e in range(n_exp):
            r = tile * n_exp + e
            _run_copies("start", rlen_ref[r], y_hbm, rdst_ref[r], buf.at[half], soff_ref[r],
                        sems.at[half])

    @pl.when(j == 0)
    def _():
        buf[...] = jnp.zeros_like(buf)
        start_runs(0, 0)

    @pl.when(j + 1 < pl.num_programs(0))
    def _():
        start_runs(j + 1, 1 - cur)

    last = j * n_exp + n_exp - 1
    _run_copies("wait", soff_ref[last] + rlen_ref[last], y_hbm, 0, buf.at[cur], 0, sems.at[cur],
                sizes=_pow2_sizes(slots))

    slot = slot_ref[...]
    gate = gate_ref[...]
    acc = x2_ref[...]
    for sb in range(slots // GROUP):
        lio = lax.broadcasted_iota(jnp.int32, (tm, GROUP), 1) + sb * GROUP
        w = jnp.zeros((tm, GROUP), F32)
        for k in range(TOP_K):
            w = jnp.where(lio == slot[:, k: k + 1], gate[:, k: k + 1], w)
        acc = acc + _dot(w.astype(BF16),
                         _unpack_bf16_pairs(buf[cur, sb * GROUP: (sb + 1) * GROUP, :]))
    o_ref[...] = acc


def _combine(soff, rlen, rdst, slot_row, gates_row, x2, y_rows, slots):
    T, D = x2.shape
    tm = ROUTE_TILE
    n_exp = soff.shape[0] // (T // tm)
    grid_spec = pltpu.PrefetchScalarGridSpec(
        num_scalar_prefetch=3,
        grid=(T // tm,),
        in_specs=[
            pl.BlockSpec((tm, TOP_K), lambda i, *_: (i, 0)),
            pl.BlockSpec((tm, TOP_K), lambda i, *_: (i, 0)),
            pl.BlockSpec((tm, D), lambda i, *_: (i, 0)),
            pl.BlockSpec(memory_space=pl.ANY),
        ],
        out_specs=pl.BlockSpec((tm, D), lambda i, *_: (i, 0)),
        scratch_shapes=[pltpu.VMEM((2, slots, D // 2), jnp.uint32),
                        pltpu.SemaphoreType.DMA((2,))],
    )
    return pl.pallas_call(
        functools.partial(_combine_kernel, n_exp=n_exp, slots=slots),
        grid_spec=grid_spec,
        out_shape=jax.ShapeDtypeStruct((T, D), F32),
        compiler_params=pltpu.CompilerParams(
            dimension_semantics=("arbitrary",), vmem_limit_bytes=VMEM_LIMIT),
        name="moe_combine",
    )(soff, rlen, rdst, slot_row, gates_row, x2, y_rows)


def _block_diag_ones(n, blk):
    i = jnp.arange(n)
    return ((i[:, None] // blk) == (i[None, :] // blk)).astype(BF16)


def _layer(x, norm1_g, w_in, tshift_mu, q_norm_g, k_norm_g, rel_bias, w0, w_decay_up, a0,
           w_aaa_up, w_gate_up, k_k, k_a, r_k, ln_x_g, ln_x_b, w_out, norm2_g, w_router,
           b_router, w_up, b_up, w_down, b_down):
    B, S, D = x.shape
    T = B * S
    att_heads = rel_bias.shape[0]
    att_w = att_heads * HEAD_DIM
    width = w0.shape[0]
    n_exp = w_router.shape[1]
    assert att_w % LANES == 0 and width % GROUP == 0
    assert S % ATT_QROWS == 0 and ATT_QROWS % ATT_HALO == 0 and ATT_HALO % ATT_SUB == 0
    assert T % ROUTE_TILE == 0

    x2d = x.reshape(T, D)
    row = lambda a: a.reshape(1, -1).astype(F32)
    head_ones = _block_diag_ones(GROUP, HEAD_DIM)

    qkv, pr = _inproj(
        x2d, row(norm1_g), w_in.astype(BF16),
        row(jnp.tile(q_norm_g, att_heads)), row(jnp.tile(k_norm_g, att_heads)), head_ones, att_w)

    att = _attention(qkv, _attention_bias(rel_bias), B, S, att_w)

    lora = jnp.zeros((DECAY_LORA + AAA_LORA, 2 * width), F32)
    lora = lora.at[:DECAY_LORA, :width].set(w_decay_up).at[DECAY_LORA:, width:].set(w_aaa_up)
    rwk = _rwkv(pr, row(tshift_mu), row(w0), row(a0), lora.astype(BF16),
                w_gate_up.astype(BF16), row(k_k), row(k_a), row(r_k), row(ln_x_g),
                row(ln_x_b), head_ones, B, S, width)

    wrt = w_router.T.astype(F32)
    wrt_hi = wrt.astype(BF16)
    wrt_lo = (wrt - wrt_hi.astype(F32)).astype(BF16)
    x2, xn, eidx, rank, gates, counts = _outproj(
        att, rwk, x2d, w_out[:att_w].astype(BF16), w_out[att_w:].astype(BF16), row(norm2_g),
        jnp.stack([wrt_hi, wrt_lo]), b_router.reshape(n_exp, 1).astype(F32), n_exp)

    tm = ROUTE_TILE
    n_tiles = T // tm
    i32 = lambda a: a.astype(jnp.int32)
    run_n = counts.reshape(n_tiles, n_exp, LANES)[:, :, 0]
    run_len = (run_n + RUN_ALIGN - 1) // RUN_ALIGN * RUN_ALIGN
    run_off = jnp.cumsum(run_len, axis=1) - run_len
    exp_rows = jnp.sum(run_len, axis=0)
    exp_pad = (exp_rows + MOE_ROWS - 1) // MOE_ROWS * MOE_ROWS
    exp_end = jnp.cumsum(exp_pad)
    exp_start = exp_end - exp_pad
    run_dst = exp_start[None, :] + jnp.cumsum(run_len, axis=0) - run_len
    eidx_t = eidx.reshape(TOP_K, n_tiles, tm)
    experts = jnp.arange(n_exp, dtype=jnp.int32)[:, None, None, None]
    slot = rank.reshape(TOP_K, n_tiles, tm) + jnp.sum(
        jnp.where(eidx_t[None] == experts, run_off.T[:, None, :, None], 0), axis=0)
    slot_lane = i32(slot.transpose(1, 0, 2))
    slot_row = i32(slot.transpose(1, 2, 0).reshape(T, TOP_K))
    slots = -(-(tm * TOP_K + n_exp * (RUN_ALIGN - 1)) // GROUP) * GROUP
    n_blocks = -(-(T * TOP_K + n_tiles * n_exp * (RUN_ALIGN - 1)) // MOE_ROWS) + n_exp
    blk_start = jnp.arange(n_blocks, dtype=jnp.int32) * MOE_ROWS
    blk_expert = i32(jnp.minimum(
        jnp.sum(blk_start[:, None] >= exp_end[None, :], axis=1), n_exp - 1))
    n_active = i32(exp_end[-1:] // MOE_ROWS)
    soff, rlen, rdst = (i32(a.reshape(-1)) for a in (run_off, run_len, run_dst))

    xs = _dispatch(soff, rlen, rdst, i32(exp_start + exp_rows), i32(exp_pad - exp_rows),
                   n_active, slot_lane, xn, n_blocks, slots)
    y_rows = _moe(blk_expert, n_active, xs, w_up, b_up, w_down, b_down)
    out = _combine(soff, rlen, rdst, slot_row, gates.T, x2, y_rows, slots)
    return out.reshape(B, S, D)


def kernel(x, norm1_g, w_in, tshift_mu, q_norm_g, k_norm_g, rel_bias, w0, w_decay_up, a0,
           w_aaa_up, w_gate_up, k_k, k_a, r_k, ln_x_g, ln_x_b, w_out, norm2_g, w_router,
           b_router, w_up, b_up, w_down, b_down):
    params = (norm1_g, w_in, tshift_mu, q_norm_g, k_norm_g, rel_bias, w0, w_decay_up, a0,
              w_aaa_up, w_gate_up, k_k, k_a, r_k, ln_x_g, ln_x_b, w_out, norm2_g, w_router,
              b_router, w_up, b_up, w_down, b_down)
    for l in range(norm1_g.shape[0]):
        x = _layer(x, *[p[l] for p in params])
    return x
```

```python
import functools
import math

import jax
import jax.numpy as jnp
from jax import lax
from jax.experimental import pallas as pl
from jax.experimental.pallas import tpu as pltpu

F32 = jnp.float32
BF16 = jnp.bfloat16

CHUNK = 64
HEAD_DIM = 64
LEFT_CHUNKS = 8
MAX_LEFT_REL = 128
TOP_K = 4
SWIGLU_ALPHA = 1.702
SWIGLU_LIMIT = 7.0
NORM_EPS = 1e-6
LN_X_EPS = 64e-5
DECAY_LORA = 64
AAA_LORA = 64
GATE_LORA = 128

LANES = 128
GROUP = 256
NEG_BIG = -1e30

VMEM_LIMIT = 56 * 1024 * 1024

ATT_QROWS = 2048
ATT_SUB = 256
ATT_HALO = LEFT_CHUNKS * CHUNK
ATT_KEYS = ATT_SUB + ATT_HALO
ATT_MASKED_TILES = ATT_HALO // ATT_SUB
ATT_INTERLEAVE = 4

LOCAL_UNROLL = 4

MOE_ROWS = 512
ROUTE_TILE = 512
RUN_ALIGN = 8
RUN_SIZES = tuple(ROUTE_TILE >> s for s in range((ROUTE_TILE // RUN_ALIGN).bit_length()))
assert MOE_ROWS <= 2 * RUN_SIZES[0] and RUN_SIZES[-1] == RUN_ALIGN


def _dot(a, b):
    return jnp.dot(a, b, preferred_element_type=F32)


def _dot_nt(a, b):
    return lax.dot_general(a, b, (((1,), (1,)), ((), ())), preferred_element_type=F32)


def _head_sums(a, block_ones):
    a = a.astype(BF16)
    return jnp.concatenate(
        [_dot(a[:, j: j + GROUP], block_ones) for j in range(0, a.shape[1], GROUP)], axis=1)


def _pack_bf16_pairs(x):
    n = x.shape[1] // 2
    bits = pltpu.bitcast(x.astype(BF16).astype(F32), jnp.uint32)
    return bits[:, :n] | (bits[:, n:] >> 16)


def _unpack_bf16_pairs(w):
    hi = pltpu.bitcast(w & jnp.uint32(0xFFFF0000), F32)
    lo = pltpu.bitcast(w << 16, F32)
    return jnp.concatenate([hi, lo], axis=1).astype(BF16)


def _sigmoid(x):
    return 0.5 * jnp.tanh(0.5 * x) + 0.5


def _inproj_kernel(x_ref, g_ref, w_ref, qg_ref, kg_ref, bd_ref, qkv_ref, pr_ref, *, att_w):
    x = x_ref[...]
    ms = jnp.mean(x * x, axis=-1, keepdims=True)
    h = (x * lax.rsqrt(ms + NORM_EPS) * g_ref[...]).astype(BF16)
    qkv = _dot(h, w_ref[:, : 3 * att_w])
    bd = bd_ref[...]

    def head_norm(t, g):
        ss = _head_sums(t * t, bd)
        return t * lax.rsqrt(ss * (1.0 / HEAD_DIM) + NORM_EPS) * g

    q = head_norm(qkv[:, :att_w], qg_ref[...]) * (HEAD_DIM ** -0.5)
    k = head_norm(qkv[:, att_w: 2 * att_w], kg_ref[...])
    qkv_ref[:, :att_w] = q.astype(BF16)
    qkv_ref[:, att_w: 2 * att_w] = k.astype(BF16)
    qkv_ref[:, 2 * att_w:] = qkv[:, 2 * att_w:].astype(BF16)
    pr_ref[...] = _dot(h, w_ref[:, 3 * att_w:]).astype(BF16)


def _inproj(x2d, g1, w_in_bf, qg, kg, bd, att_w, tm=512):
    T, D = x2d.shape
    in_cols = w_in_bf.shape[1]
    r_cols = in_cols - 3 * att_w
    const = lambda i: (0, 0)
    return pl.pallas_call(
        functools.partial(_inproj_kernel, att_w=att_w),
        grid=(T // tm,),
        in_specs=[
            pl.BlockSpec((tm, D), lambda i: (i, 0)),
            pl.BlockSpec((1, D), const),
            pl.BlockSpec((D, in_cols), const),
            pl.BlockSpec((1, att_w), const),
            pl.BlockSpec((1, att_w), const),
            pl.BlockSpec((GROUP, GROUP), const),
        ],
        out_specs=[
            pl.BlockSpec((tm, 3 * att_w), lambda i: (i, 0)),
            pl.BlockSpec((tm, r_cols), lambda i: (i, 0)),
        ],
        out_shape=[
            jax.ShapeDtypeStruct((T, 3 * att_w), BF16),
            jax.ShapeDtypeStruct((T, r_cols), BF16),
        ],
        compiler_params=pltpu.CompilerParams(
            dimension_semantics=("arbitrary",), vmem_limit_bytes=VMEM_LIMIT),
        name="inproj",
    )(x2d, g1, w_in_bf, qg, kg, bd)


def _attn_kernel(q_ref, kp_ref, kc_ref, vp_ref, vc_ref, bias_ref, o_ref, kbuf, vbuf):
    i = pl.program_id(2)
    kbuf[:ATT_HALO, :] = kp_ref[...]
    kbuf[ATT_HALO:, :] = kc_ref[...]
    lane_head = lax.broadcasted_iota(jnp.int32, (1, LANES), 1) >> (HEAD_DIM.bit_length() - 1)
    one = jnp.ones((), BF16)
    for h in range(2):
        vbuf[h, :ATT_HALO, :] = jnp.where(lane_head == h, vp_ref[...], one)
        vbuf[h, ATT_HALO:, :] = jnp.where(lane_head == h, vc_ref[...], one)
    tiles = range(ATT_QROWS // ATT_SUB)
    col = lax.broadcasted_iota(jnp.int32, (1, ATT_KEYS), 1)
    all_units =[(qs * ATT_SUB, h * HEAD_DIM, h, qs) for qs in tiles for h in range(2)]
    ext = []
    for u0 in range(0, len(all_units), ATT_INTERLEAVE):
        units = all_units[u0: u0 + ATT_INTERLEAVE]
        s = [_dot_nt(q_ref[r0: r0 + ATT_SUB, lo: lo + HEAD_DIM],
                     kbuf[r0: r0 + ATT_KEYS, lo: lo + HEAD_DIM]) + bias_ref[h]
             for (r0, lo, h, qs) in units]
        s = [x + jnp.where((col < ATT_HALO - qs * ATT_SUB) & (i == 0), NEG_BIG, 0.0)
             if qs < ATT_MASKED_TILES else x for x, (_, _, _, qs) in zip(s, units)]
        m = [jnp.max(x, axis=-1, keepdims=True) for x in s]
        p = [jnp.exp(x - mx).astype(BF16) for x, mx in zip(s, m)]
        ext += [_dot(x, vbuf[h, r0: r0 + ATT_KEYS, :]) for x, (r0, _, h, _) in zip(p, units)]
    for qs in tiles:
        e0, e1 = ext[2 * qs], ext[2 * qs + 1]
        o_ref[qs * ATT_SUB: (qs + 1) * ATT_SUB, :] = jnp.where(
            lane_head == 0, e0 / pltpu.roll(e0, HEAD_DIM, axis=1),
            e1 / pltpu.roll(e1, HEAD_DIM, axis=1)).astype(BF16)


def _attention(qkv, bias, B, S, att_w):
    T = B * S
    n_i = S // ATT_QROWS
    n_hp = att_w // LANES
    heads_per = LANES // HEAD_DIM
    assert heads_per == 2

    def cur(col0):
        return lambda b, hp, i: (b * n_i + i, col0 + hp)

    halo_per_blk = ATT_QROWS // ATT_HALO

    def prev(col0):
        return lambda b, hp, i: (
            b * n_i * halo_per_blk + jnp.maximum(i * halo_per_blk - 1, 0), col0 + hp)

    blk = (ATT_QROWS, LANES)
    halo = (ATT_HALO, LANES)
    return pl.pallas_call(
        _attn_kernel,
        grid=(B, n_hp, n_i),
        in_specs=[
            pl.BlockSpec(blk, cur(0)),
            pl.BlockSpec(halo, prev(n_hp)),
            pl.BlockSpec(blk, cur(n_hp)),
            pl.BlockSpec(halo, prev(2 * n_hp)),
            pl.BlockSpec(blk, cur(2 * n_hp)),
            pl.BlockSpec((heads_per,) + bias.shape[1:], lambda b, hp, i: (hp, 0, 0)),
        ],
        out_specs=pl.BlockSpec(blk, lambda b, hp, i: (b * n_i + i, hp)),
        out_shape=jax.ShapeDtypeStruct((T, att_w), BF16),
        scratch_shapes=[
            pltpu.VMEM((ATT_HALO + ATT_QROWS, LANES), BF16),
            pltpu.VMEM((2, ATT_HALO + ATT_QROWS, LANES), BF16),
        ],
        compiler_params=pltpu.CompilerParams(
            dimension_semantics=("arbitrary", "arbitrary", "arbitrary"),
            vmem_limit_bytes=VMEM_LIMIT),
        name="band_attention",
    )(qkv, qkv, qkv, qkv, qkv, bias)


def _attention_bias(rel_bias):
    H = rel_bias.shape[0]
    table = rel_bias.astype(F32)
    far = LEFT_CHUNKS * CHUNK + ATT_SUB - 1
    n_left = far - MAX_LEFT_REL + 1
    period = ATT_SUB + ATT_KEYS + 1
    ext = jnp.concatenate([
        jnp.broadcast_to(table[:, :1], (H, n_left)), table[:, 1:],
        jnp.broadcast_to(table[:, -1:], (H, period - n_left - (table.shape[1] - 1)))], axis=1)
    flat = jnp.tile(ext, (1, ATT_SUB + 1))[:, ATT_SUB - 1: ATT_SUB - 1 + ATT_SUB * (period - 1)]
    bias = flat.reshape(H, ATT_SUB, period - 1)[:, :, :ATT_KEYS]
    r = jnp.arange(ATT_SUB)[:, None]
    c = jnp.arange(ATT_KEYS)[None, :]
    band = c // CHUNK - r // CHUNK
    ok = (band >= 0) & (band <= LEFT_CHUNKS)
    return jnp.where(ok[None], bias, NEG_BIG)


def _rwkv_kernel(p_ref, pprev_ref, mu_ref, w0_ref, a0_ref, lora_ref, wg_ref, kk_ref, ka_ref,
                 rk_ref, lng_ref, lnb_ref, bd_ref, tri_ref, o_ref,
                 rt_s, at_s, bt_s, kt_s, bh_s, kh_s, v_s, wl_s, y_s, gate_s, bonus_s,
                 z_s, g_s, y0_s, m_s, n_s, *, width, tm):
    i = pl.program_id(1)
    n_groups = width // GROUP
    heads_per_group = GROUP // HEAD_DIM

    @pl.when(i == 0)
    def _():
        z_s[...] = jnp.zeros_like(z_s)

    part_rows = LOCAL_UNROLL * CHUNK
    bd = bd_ref[...]

    def prep(part):
        r0 = part * part_rows
        rows = slice(r0, r0 + part_rows)
        p = p_ref[rows, :].astype(F32)
        if part == 0:
            has_prev = jnp.where(i > 0, 1.0, 0.0)
            prev_row = pprev_ref[7:8, :].astype(F32) * has_prev
        else:
            prev_row = p_ref[r0 - 16: r0, :].astype(F32)[15:16]
        row = lax.broadcasted_iota(jnp.int32, (part_rows, 1), 0)
        prev = jnp.where(row == 0, prev_row, pltpu.roll(p, 1, axis=0))
        ps = p + (prev - p) * mu_ref[...]

        r = ps[:, :width]
        k = ps[:, width: 2 * width]
        v = ps[:, 2 * width: 3 * width]
        wa = ps[:, 3 * width: 3 * width + DECAY_LORA + AAA_LORA]
        gd = ps[:, 3 * width + DECAY_LORA + AAA_LORA:]
        lane = lax.broadcasted_iota(jnp.int32, (1, DECAY_LORA + AAA_LORA), 1)
        wa = jnp.where(lane < DECAY_LORA, jnp.tanh(wa), wa)
        lw = _dot(wa.astype(BF16), lora_ref[...])
        logw = -math.exp(-0.5) * _sigmoid(w0_ref[...] + lw[:, :width])
        eta = _sigmoid(a0_ref[...] + lw[:, width:])
        gate_s[rows, :] = _dot(_sigmoid(gd).astype(BF16), wg_ref[...])

        kk = k * kk_ref[...]
        kk = kk * lax.rsqrt(jnp.maximum(_head_sums(kk * kk, bd), 1e-24))
        km = k * (1.0 + (eta - 1.0) * ka_ref[...])
        bb = kk * eta
        bonus_s[rows, :] = _head_sums(r * km * rk_ref[...], bd) * v

        tri = tri_ref[...]
        cs_chunks = [_split_dot_left(tri, logw[c * CHUNK: (c + 1) * CHUNK])
                     for c in range(LOCAL_UNROLL)]
        cs = jnp.concatenate(cs_chunks, axis=0)
        tot = jnp.concatenate(
            [jnp.broadcast_to(x[CHUNK - 1:], (CHUNK, width)) for x in cs_chunks], axis=0)
        e_pos = jnp.exp(cs)
        e_neg = jnp.exp(-cs)
        e_tail = jnp.exp(tot - cs)
        rt_s[rows, :] = r * e_pos
        at_s[rows, :] = -kk * jnp.exp(cs - logw)
        bt_s[rows, :] = bb * e_neg
        kt_s[rows, :] = km * e_neg
        bh_s[rows, :] = bb * e_tail
        kh_s[rows, :] = km * e_tail
        v_s[rows, :] = v
        wl_s[rows, :] = jnp.exp(tot)

    rr = lax.broadcasted_iota(jnp.int32, (GROUP, GROUP), 0)
    cc = lax.broadcasted_iota(jnp.int32, (GROUP, GROUP), 1)
    shift = CHUNK.bit_length() - 1
    same = (rr >> shift) == (cc >> shift)
    strict = same & (rr > cc)
    incl = same & (rr >= cc)
    eye = jnp.where(rr == cc, 1.0, 0.0)
    lane_head = lax.broadcasted_iota(jnp.int32, (1, GROUP), 1) >> shift

    def stack_masked(x):
        return jnp.concatenate(
            [jnp.where(lane_head == h, x, 0.0) for h in range(heads_per_group)], axis=0)

    def stack_plain(x):
        return jnp.concatenate([x] * heads_per_group, axis=0)

    def local(part):
        units = []
        for u in range(LOCAL_UNROLL):
            c = part * LOCAL_UNROLL + u
            rows = slice(c * CHUNK, (c + 1) * CHUNK)
            units += [(c, gi, rows, slice(gi * GROUP, (gi + 1) * GROUP)) for gi in range(n_groups)]

        def load(ref, stack):
            return [stack(ref[rows, cols]) for (_, _, rows, cols) in units]

        def each(f, *lists):
            return [f(*args) for args in zip(*lists)]

        b16 = lambda xs: [x.astype(BF16) for x in xs]
        rt4 = load(rt_s, stack_masked)
        at4, rt4b, v4 = b16(load(at_s, stack_masked)), b16(rt4), load(v_s, stack_masked)
        v4b, bh4 = b16(v4), b16(load(bh_s, stack_masked))
        kh4 = b16(load(kh_s, stack_masked))
        bt_rep, kt_rep = b16(load(bt_s, stack_plain)), b16(load(kt_s, stack_plain))
        a_ab = each(lambda a, b: jnp.where(strict, _dot_nt(a, b), 0.0), at4, bt_rep)
        a_ak = b16(each(lambda a, b: jnp.where(strict, _dot_nt(a, b), 0.0), at4, kt_rep))
        a_rb = b16(each(lambda a, b: jnp.where(incl, _dot_nt(a, b), 0.0), rt4b, bt_rep))
        a_rk = b16(each(lambda a, b: jnp.where(incl, _dot_nt(a, b), 0.0), rt4b, kt_rep))
        tinv = [eye + a for a in a_ab]
        pw = b16(a_ab)
        span = 2
        while span < CHUNK:
            pw = b16(each(_dot, pw, pw))
            tinv = each(lambda t, p: t + _dot(p, t.astype(BF16)), tinv, pw)
            span *= 2
        tinv = b16(tinv)
        p4 = each(_dot, tinv, at4)
        q4 = each(_dot, tinv, b16(each(_dot, a_ak, v4b)))
        p4b, q4b = b16(p4), b16(q4)
        g4 = each(lambda r, a, p: r + _dot(a, p), rt4, a_rb, p4b)
        y0 = each(lambda a, q, ak, v: _dot(a, q) + _dot(ak, v), a_rb, q4b, a_rk, v4b)
        m4 = each(lambda p, b: _dot(p.T.astype(BF16), b), p4, bh4)
        n4 = each(lambda q, b, v, k: _dot(q.T.astype(BF16), b) + _dot(v.T.astype(BF16), k),
                  q4, bh4, v4, kh4)
        for (c, gi, _, _), g, y, m, n in zip(units, g4, y0, m4, n4):
            g_s[c, gi] = g.astype(BF16)
            y0_s[c, gi] = y
            m_s[c, gi] = m.astype(BF16)
            n_s[c, gi] = n

    def carry(part):
        for c in range(part * LOCAL_UNROLL, (part + 1) * LOCAL_UNROLL):
            rows = slice(c * CHUNK, (c + 1) * CHUNK)
            for gi in range(n_groups):
                cols = slice(gi * GROUP, (gi + 1) * GROUP)
                zt = z_s[gi]
                zb = zt.astype(BF16)
                y4 = _dot_nt(g_s[c, gi], zb) + y0_s[c, gi]
                z_s[gi] = (zt * wl_s[c * CHUNK: c * CHUNK + 1, cols]
                           + _dot(zb, m_s[c, gi]) + n_s[c, gi])
                ytok = y4[:CHUNK]
                for h in range(1, heads_per_group):
                    ytok = ytok + y4[h * CHUNK: (h + 1) * CHUNK]
                y_s[rows, cols] = ytok
        rows = slice(part * part_rows, (part + 1) * part_rows)
        y = y_s[rows, :]
        inv_n = 1.0 / HEAD_DIM
        mean = _head_sums(y, bd) * inv_n
        d = y - mean
        var = _head_sums(d * d, bd) * inv_n
        yn = d * lax.rsqrt(var + LN_X_EPS) * lng_ref[...] + lnb_ref[...]
        o_ref[rows, :] = ((yn + bonus_s[rows, :]) * gate_s[rows, :]).astype(BF16)

    n_parts = tm // part_rows
    prep(0)
    for part in range(n_parts):
        if part + 1 < n_parts:
            prep(part + 1)
        local(part)
        if part > 0:
            carry(part - 1)
    carry(n_parts - 1)


def _split_dot_left(a_exact, b, terms=2):
    acc = None
    rem = b
    for _ in range(terms):
        part = rem.astype(BF16)
        rem = rem - part.astype(F32)
        d = _dot(a_exact, part)
        acc = d if acc is None else acc + d
    return acc


def _rwkv(pr, mu, w0, a0, lora, wg, k_k, k_a, r_k, ln_g, ln_b, bd, B, S, width, tm=512):
    T = B * S
    n_t = S // tm
    cols = pr.shape[1]
    ti = jnp.arange(CHUNK)
    tri = (ti[:, None] >= ti[None, :]).astype(BF16)
    const = lambda b, i: (0, 0)
    vec = lambda n: pl.BlockSpec((1, n), const)
    scr = lambda: pltpu.VMEM((tm, width), F32)
    return pl.pallas_call(
        functools.partial(_rwkv_kernel, width=width, tm=tm),
        grid=(B, n_t),
        in_specs=[
            pl.BlockSpec((tm, cols), lambda b, i: (b * n_t + i, 0)),
            pl.BlockSpec((8, cols), lambda b, i: (jnp.maximum((b * n_t + i) * (tm // 8) - 1, 0), 0)),
            vec(cols), vec(width), vec(width),
            pl.BlockSpec(lora.shape, const),
            pl.BlockSpec(wg.shape, const),
            vec(width), vec(width), vec(width), vec(width), vec(width),
            pl.BlockSpec((GROUP, GROUP), const),
            pl.BlockSpec((CHUNK, CHUNK), const),
        ],
        out_specs=pl.BlockSpec((tm, width), lambda b, i: (b * n_t + i, 0)),
        out_shape=jax.ShapeDtypeStruct((T, width), BF16),
        scratch_shapes=[scr() for _ in range(11)]
        + [pltpu.VMEM((width // GROUP, GROUP, GROUP), F32)]
        + [pltpu.VMEM((tm // CHUNK, width // GROUP, GROUP, GROUP), dt)
           for dt in (BF16, F32, BF16, F32)],
        compiler_params=pltpu.CompilerParams(
            dimension_semantics=("arbitrary", "arbitrary"), vmem_limit_bytes=VMEM_LIMIT),
        name="rwkv7",
    )(pr, pr, mu, w0, a0, lora, wg, k_k, k_a, r_k, ln_g, ln_b, bd, tri)


def _outproj_kernel(att_ref, rwk_ref, x_ref, wa_ref, wr_ref, g_ref, wrt_ref, br_ref, upper_ref,
                    x2_ref, xn_ref, eidx_ref, rank_ref, gate_ref, cnt_ref, *, n_exp, tm):
    x2 = x_ref[...] + _dot(att_ref[...], wa_ref[...]) + _dot(rwk_ref[...], wr_ref[...])
    x2_ref[...] = x2
    ms = jnp.mean(x2 * x2, axis=-1, keepdims=True)
    xn = x2 * lax.rsqrt(ms + NORM_EPS) * g_ref[...]
    xn_ref[...] = xn.astype(BF16)

    x_hi = xn.astype(BF16)
    x_lo = (xn - x_hi.astype(F32)).astype(BF16)
    logits = (_dot_nt(wrt_ref[0], x_hi) + _dot_nt(wrt_ref[0], x_lo)
              + _dot_nt(wrt_ref[1], x_hi)) + br_ref[...]

    eio = lax.broadcasted_iota(jnp.int32, (n_exp, tm), 0)
    work = logits
    vals, onehots = [], []
    for k in range(TOP_K):
        m = jnp.max(work, axis=0, keepdims=True)
        idx = jnp.min(jnp.where(work == m, eio, n_exp), axis=0, keepdims=True)
        oh = eio == idx
        vals.append(m)
        onehots.append(oh)
        eidx_ref[k: k + 1, :] = idx
        work = jnp.where(oh, -jnp.inf, work)

    exps = [jnp.exp(vk - vals[0]) for vk in vals]
    denom = exps[0]
    for e in exps[1:]:
        denom = denom + e
    for k in range(TOP_K):
        gate_ref[k: k + 1, :] = exps[k] / denom

    sel = onehots[0]
    for oh in onehots[1:]:
        sel = sel | oh
    sel_f = jnp.where(sel, 1.0, 0.0)
    before = _dot(sel_f.astype(BF16), upper_ref[...])
    for k in range(TOP_K):
        rank_ref[k: k + 1, :] = jnp.sum(
            jnp.where(onehots[k], before, 0.0), axis=0, keepdims=True).astype(jnp.int32)
    cnt_ref[...] = jnp.broadcast_to(
        jnp.sum(sel_f, axis=1, keepdims=True), cnt_ref.shape).astype(jnp.int32)


def _outproj(att, rwk, x2d, w_att, w_rwk, g2, wrt, br, n_exp):
    T, D = x2d.shape
    tm = ROUTE_TILE
    upper = (jnp.arange(tm)[:, None] < jnp.arange(tm)[None, :]).astype(BF16)
    const = lambda i: (0, 0)
    rows = lambda n: pl.BlockSpec((tm, n), lambda i: (i, 0))
    lanes = pl.BlockSpec((TOP_K, tm), lambda i: (0, i))
    return pl.pallas_call(
        functools.partial(_outproj_kernel, n_exp=n_exp, tm=tm),
        grid=(T // tm,),
        in_specs=[
            rows(att.shape[1]), rows(rwk.shape[1]), rows(D),
            pl.BlockSpec(w_att.shape, const), pl.BlockSpec(w_rwk.shape, const),
            pl.BlockSpec((1, D), const),
            pl.BlockSpec(wrt.shape, lambda i: (0, 0, 0)),
            pl.BlockSpec((n_exp, 1), const),
            pl.BlockSpec((tm, tm), const),
        ],
        out_specs=[rows(D), rows(D), lanes, lanes, lanes,
                   pl.BlockSpec((n_exp, LANES), lambda i: (i, 0))],
        out_shape=[
            jax.ShapeDtypeStruct((T, D), F32),
            jax.ShapeDtypeStruct((T, D), BF16),
            jax.ShapeDtypeStruct((TOP_K, T), jnp.int32),
            jax.ShapeDtypeStruct((TOP_K, T), jnp.int32),
            jax.ShapeDtypeStruct((TOP_K, T), F32),
            jax.ShapeDtypeStruct((T // tm * n_exp, LANES), jnp.int32),
        ],
        compiler_params=pltpu.CompilerParams(
            dimension_semantics=("arbitrary",), vmem_limit_bytes=VMEM_LIMIT),
        name="outproj_router",
    )(att, rwk, x2d, w_att, w_rwk, g2, wrt, br, upper)


def _pow2_sizes(limit):
    top = 1 << (limit.bit_length() - 1)
    return tuple(top >> s for s in range((top // RUN_ALIGN).bit_length()))


def _run_copies(action, length, src, src_off, dst, dst_off, sem, sizes=RUN_SIZES):
    for size in sizes:
        start = length & ~(2 * size - 1)

        @pl.when((length & size) != 0)
        def _():
            cp = pltpu.make_async_copy(
                src.at[pl.ds(pl.multiple_of(src_off + start, RUN_ALIGN), size)],
                dst.at[pl.ds(pl.multiple_of(dst_off + start, RUN_ALIGN), size)], sem)
            cp.start() if action == "start" else cp.wait()


def _dispatch_kernel(soff_ref, rlen_ref, rdst_ref, tail_dst_ref, tail_len_ref, nact_ref,
                     slot_ref, xn_ref, xs_hbm, buf, zbuf, sem, run_sems, *, n_exp, slots, n_blocks):
    j = pl.program_id(0)
    tm = ROUTE_TILE

    @pl.when(j == 0)
    def _():
        zbuf[...] = jnp.zeros_like(zbuf)

        def unused_block(action):
            def body(b, c):
                cp = pltpu.make_async_copy(
                    zbuf, xs_hbm.at[pl.ds(pl.multiple_of(b * MOE_ROWS, MOE_ROWS), MOE_ROWS)], sem)
                cp.start() if action == "start" else cp.wait()
                return c
            return body

        for action in ("start", "wait"):
            for e in range(n_exp):
                _run_copies(action, tail_len_ref[e], zbuf, 0, xs_hbm, tail_dst_ref[e], sem)
            lax.fori_loop(nact_ref[0], n_blocks, unused_block(action), 0)

    cur = lax.rem(j, 2)
    x = xn_ref[...]
    for sb in range(slots // GROUP):
        sio = lax.broadcasted_iota(jnp.int32, (GROUP, tm), 0) + sb * GROUP
        p = jnp.zeros((GROUP, tm), F32)
        for k in range(TOP_K):
            p = jnp.where(sio == slot_ref[0, k: k + 1, :], 1.0, p)
        buf[cur, sb * GROUP: (sb + 1) * GROUP, :] = _pack_bf16_pairs(_dot(p.astype(BF16), x))

    def start_runs(tile, half):
        for e in range(n_exp):
            r = tile * n_exp + e
            _run_copies("start", rlen_ref[r], buf.at[half], soff_ref[r], xs_hbm, rdst_ref[r],
                        run_sems.at[half])

    def wait_runs(tile, half):
        last = tile * n_exp + n_exp - 1
        _run_copies("wait", soff_ref[last] + rlen_ref[last], buf.at[half], 0, xs_hbm, 0,
                    run_sems.at[half], sizes=_pow2_sizes(slots))

    start_runs(j, cur)

    @pl.when(j > 0)
    def _():
        wait_runs(j - 1, 1 - cur)

    @pl.when(j == pl.num_programs(0) - 1)
    def _():
        wait_runs(j, cur)


def _dispatch(soff, rlen, rdst, tail_dst, tail_len, n_active, slot_lane, xn, n_blocks, slots):
    T, D = xn.shape
    tm = ROUTE_TILE
    n_exp = tail_dst.shape[0]
    n_rows = n_blocks * MOE_ROWS
    grid_spec = pltpu.PrefetchScalarGridSpec(
        num_scalar_prefetch=6,
        grid=(T // tm,),
        in_specs=[
            pl.BlockSpec((1, TOP_K, tm), lambda i, *_: (i, 0, 0)),
            pl.BlockSpec((tm, D), lambda i, *_: (i, 0)),
        ],
        out_specs=pl.BlockSpec(memory_space=pl.ANY),
        scratch_shapes=[pltpu.VMEM((2, slots, D // 2), jnp.uint32),
                        pltpu.VMEM((MOE_ROWS, D // 2), jnp.uint32),
                        pltpu.SemaphoreType.DMA, pltpu.SemaphoreType.DMA((2,))],
    )
    return pl.pallas_call(
        functools.partial(_dispatch_kernel, n_exp=n_exp, slots=slots, n_blocks=n_blocks),
        grid_spec=grid_spec,
        out_shape=jax.ShapeDtypeStruct((n_rows, D // 2), jnp.uint32),
        compiler_params=pltpu.CompilerParams(
            dimension_semantics=("arbitrary",), vmem_limit_bytes=VMEM_LIMIT),
        name="moe_dispatch",
    )(soff, rlen, rdst, tail_dst, tail_len, n_active, slot_lane, xn)


def _moe_kernel(bexp_ref, nact_ref, x_ref, wup_ref, bup_ref, wdn_ref, bdn_ref,
                y_ref, wup_bf, wdn_bf, *, d_exp):
    i = pl.program_id(0)

    @pl.when(i < nact_ref[0])
    def _():
        new_expert = jnp.logical_or(i == 0, bexp_ref[i] != bexp_ref[jnp.maximum(i - 1, 0)])

        @pl.when(new_expert)
        def _():
            wup_bf[...] = wup_ref[0].astype(BF16)
            wdn_bf[...] = wdn_ref[0].astype(BF16)

        x = _unpack_bf16_pairs(x_ref[...])
        h_glu = _dot(x, wup_bf[:, :d_exp]) + bup_ref[0, :, :d_exp]
        h_lin = _dot(x, wup_bf[:, d_exp:]) + bup_ref[0, :, d_exp:]
        h_glu = jnp.minimum(h_glu, SWIGLU_LIMIT)
        h_lin = jnp.clip(h_lin, -SWIGLU_LIMIT, SWIGLU_LIMIT)
        act = h_glu * _sigmoid(SWIGLU_ALPHA * h_glu) * (h_lin + 1.0)
        y_ref[...] = _pack_bf16_pairs(_dot(act.astype(BF16), wdn_bf[...]) + bdn_ref[0])

    @pl.when(i >= nact_ref[0])
    def _():
        y_ref[...] = jnp.zeros_like(y_ref)


def _moe(blk_expert, n_active, xs, w_up, b_up, w_down, b_down):
    n_blocks = blk_expert.shape[0]
    n_exp, D, two_f = w_up.shape
    d_exp = two_f // 2
    blk = lambda i, na: jnp.minimum(i, na[0] - 1)
    grid_spec = pltpu.PrefetchScalarGridSpec(
        num_scalar_prefetch=2,
        grid=(n_blocks,),
        in_specs=[
            pl.BlockSpec((MOE_ROWS, D // 2), lambda i, be, na: (blk(i, na), 0)),
            pl.BlockSpec((1, D, two_f), lambda i, be, na: (be[blk(i, na)], 0, 0)),
            pl.BlockSpec((1, 1, two_f), lambda i, be, na: (be[blk(i, na)], 0, 0)),
            pl.BlockSpec((1, d_exp, D), lambda i, be, na: (be[blk(i, na)], 0, 0)),
            pl.BlockSpec((1, 1, D), lambda i, be, na: (be[blk(i, na)], 0, 0)),
        ],
        out_specs=pl.BlockSpec((MOE_ROWS, D // 2), lambda i, be, na: (i, 0)),
        scratch_shapes=[
            pltpu.VMEM((D, two_f), BF16),
            pltpu.VMEM((d_exp, D), BF16),
        ],
    )
    return pl.pallas_call(
        functools.partial(_moe_kernel, d_exp=d_exp),
        grid_spec=grid_spec,
        out_shape=jax.ShapeDtypeStruct((n_blocks * MOE_ROWS, D // 2), jnp.uint32),
        compiler_params=pltpu.CompilerParams(
            dimension_semantics=("arbitrary",), vmem_limit_bytes=VMEM_LIMIT),
        name="moe_ffn",
    )(blk_expert, n_active, xs,
      w_up, b_up.reshape(n_exp, 1, two_f), w_down, b_down.reshape(n_exp, 1, D))


def _combine_kernel(soff_ref, rlen_ref, rdst_ref, slot_ref, gate_ref, x2_ref, y_hbm, o_ref,
                    buf, sems, *, n_exp, slots):
    j = pl.program_id(0)
    tm = ROUTE_TILE

    cur = lax.rem(j, 2)

    def start_runs(tile, half):
        for e in range(n_exp):
            r = tile * n_exp + e
            _run_copies("start", rlen_ref[r], y_hbm, rdst_ref[r], buf.at[half], soff_ref[r],
                        sems.at[half])

    @pl.when(j == 0)
    def _():
        buf[...] = jnp.zeros_like(buf)
        start_runs(0, 0)

    @pl.when(j + 1 < pl.num_programs(0))
    def _():
        start_runs(j + 1, 1 - cur)

    last = j * n_exp + n_exp - 1
    _run_copies("wait", soff_ref[last] + rlen_ref[last], y_hbm, 0, buf.at[cur], 0, sems.at[cur],
                sizes=_pow2_sizes(slots))

    slot = slot_ref[...]
    gate = gate_ref[...]
    acc = x2_ref[...]
    for sb in range(slots // GROUP):
        lio = lax.broadcasted_iota(jnp.int32, (tm, GROUP), 1) + sb * GROUP
        w = jnp.zeros((tm, GROUP), F32)
        for k in range(TOP_K):
            w = jnp.where(lio == slot[:, k: k + 1], gate[:, k: k + 1], w)
        acc = acc + _dot(w.astype(BF16),
                         _unpack_bf16_pairs(buf[cur, sb * GROUP: (sb + 1) * GROUP, :]))
    o_ref[...] = acc


def _combine(soff, rlen, rdst, slot_row, gates_row, x2, y_rows, slots):
    T, D = x2.shape
    tm = ROUTE_TILE
    n_exp = soff.shape[0] // (T // tm)
    grid_spec = pltpu.PrefetchScalarGridSpec(
        num_scalar_prefetch=3,
        grid=(T // tm,),
        in_specs=[
            pl.BlockSpec((tm, TOP_K), lambda i, *_: (i, 0)),
            pl.BlockSpec((tm, TOP_K), lambda i, *_: (i, 0)),
            pl.BlockSpec((tm, D), lambda i, *_: (i, 0)),
            pl.BlockSpec(memory_space=pl.ANY),
        ],
        out_specs=pl.BlockSpec((tm, D), lambda i, *_: (i, 0)),
        scratch_shapes=[pltpu.VMEM((2, slots, D // 2), jnp.uint32),
                        pltpu.SemaphoreType.DMA((2,))],
    )
    return pl.pallas_call(
        functools.partial(_combine_kernel, n_exp=n_exp, slots=slots),
        grid_spec=grid_spec,
        out_shape=jax.ShapeDtypeStruct((T, D), F32),
        compiler_params=pltpu.CompilerParams(
            dimension_semantics=("arbitrary",), vmem_limit_bytes=VMEM_LIMIT),
        name="moe_combine",
    )(soff, rlen, rdst, slot_row, gates_row, x2, y_rows)


def _block_diag_ones(n, blk):
    i = jnp.arange(n)
    return ((i[:, None] // blk) == (i[None, :] // blk)).astype(BF16)


def _layer(x, norm1_g, w_in, tshift_mu, q_norm_g, k_norm_g, rel_bias, w0, w_decay_up, a0,
           w_aaa_up, w_gate_up, k_k, k_a, r_k, ln_x_g, ln_x_b, w_out, norm2_g, w_router,
           b_router, w_up, b_up, w_down, b_down):
    B, S, D = x.shape
    T = B * S
    att_heads = rel_bias.shape[0]
    att_w = att_heads * HEAD_DIM
    width = w0.shape[0]
    n_exp = w_router.shape[1]
    assert att_w % LANES == 0 and width % GROUP == 0
    assert S % ATT_QROWS == 0 and ATT_QROWS % ATT_HALO == 0 and ATT_HALO % ATT_SUB == 0
    assert T % ROUTE_TILE == 0

    x2d = x.reshape(T, D)
    row = lambda a: a.reshape(1, -1).astype(F32)
    head_ones = _block_diag_ones(GROUP, HEAD_DIM)

    qkv, pr = _inproj(
        x2d, row(norm1_g), w_in.astype(BF16),
        row(jnp.tile(q_norm_g, att_heads)), row(jnp.tile(k_norm_g, att_heads)), head_ones, att_w)

    att = _attention(qkv, _attention_bias(rel_bias), B, S, att_w)

    lora = jnp.zeros((DECAY_LORA + AAA_LORA, 2 * width), F32)
    lora = lora.at[:DECAY_LORA, :width].set(w_decay_up).at[DECAY_LORA:, width:].set(w_aaa_up)
    rwk = _rwkv(pr, row(tshift_mu), row(w0), row(a0), lora.astype(BF16),
                w_gate_up.astype(BF16), row(k_k), row(k_a), row(r_k), row(ln_x_g),
                row(ln_x_b), head_ones, B, S, width)

    wrt = w_router.T.astype(F32)
    wrt_hi = wrt.astype(BF16)
    wrt_lo = (wrt - wrt_hi.astype(F32)).astype(BF16)
    x2, xn, eidx, rank, gates, counts = _outproj(
        att, rwk, x2d, w_out[:att_w].astype(BF16), w_out[att_w:].astype(BF16), row(norm2_g),
        jnp.stack([wrt_hi, wrt_lo]), b_router.reshape(n_exp, 1).astype(F32), n_exp)

    tm = ROUTE_TILE
    n_tiles = T // tm
    i32 = lambda a: a.astype(jnp.int32)
    run_n = counts.reshape(n_tiles, n_exp, LANES)[:, :, 0]
    run_len = (run_n + RUN_ALIGN - 1) // RUN_ALIGN * RUN_ALIGN
    run_off = jnp.cumsum(run_len, axis=1) - run_len
    exp_rows = jnp.sum(run_len, axis=0)
    exp_pad = (exp_rows + MOE_ROWS - 1) // MOE_ROWS * MOE_ROWS
    exp_end = jnp.cumsum(exp_pad)
    exp_start = exp_end - exp_pad
    run_dst = exp_start[None, :] + jnp.cumsum(run_len, axis=0) - run_len
    eidx_t = eidx.reshape(TOP_K, n_tiles, tm)
    experts = jnp.arange(n_exp, dtype=jnp.int32)[:, None, None, None]
    slot = rank.reshape(TOP_K, n_tiles, tm) + jnp.sum(
        jnp.where(eidx_t[None] == experts, run_off.T[:, None, :, None], 0), axis=0)
    slot_lane = i32(slot.transpose(1, 0, 2))
    slot_row = i32(slot.transpose(1, 2, 0).reshape(T, TOP_K))
    slots = -(-(tm * TOP_K + n_exp * (RUN_ALIGN - 1)) // GROUP) * GROUP
    n_blocks = -(-(T * TOP_K + n_tiles * n_exp * (RUN_ALIGN - 1)) // MOE_ROWS) + n_exp
    blk_start = jnp.arange(n_blocks, dtype=jnp.int32) * MOE_ROWS
    blk_expert = i32(jnp.minimum(
        jnp.sum(blk_start[:, None] >= exp_end[None, :], axis=1), n_exp - 1))
    n_active = i32(exp_end[-1:] // MOE_ROWS)
    soff, rlen, rdst = (i32(a.reshape(-1)) for a in (run_off, run_len, run_dst))

    xs = _dispatch(soff, rlen, rdst, i32(exp_start + exp_rows), i32(exp_pad - exp_rows),
                   n_active, slot_lane, xn, n_blocks, slots)
    y_rows = _moe(blk_expert, n_active, xs, w_up, b_up, w_down, b_down)
    out = _combine(soff, rlen, rdst, slot_row, gates.T, x2, y_rows, slots)
    return out.reshape(B, S, D)


def kernel(x, norm1_g, w_in, tshift_mu, q_norm_g, k_norm_g, rel_bias, w0, w_decay_up, a0,
           w_aaa_up, w_gate_up, k_k, k_a, r_k, ln_x_g, ln_x_b, w_out, norm2_g, w_router,
           b_router, w_up, b_up, w_down, b_down):
    params = (norm1_g, w_in, tshift_mu, q_norm_g, k_norm_g, rel_bias, w0, w_decay_up, a0,
              w_aaa_up, w_gate_up, k_k, k_a, r_k, ln_x_g, ln_x_b, w_out, norm2_g, w_router,
              b_router, w_up, b_up, w_down, b_down)
    for l in range(norm1_g.shape[0]):
        x = _layer(x, *[p[l] for p in params])
    return x
```

```python
import functools
import math

import jax
import jax.numpy as jnp
from jax import lax
from jax.experimental import pallas as pl
from jax.experimental.pallas import tpu as pltpu

F32 = jnp.float32
BF16 = jnp.bfloat16

CHUNK = 64
HEAD_DIM = 64
LEFT_CHUNKS = 8
MAX_LEFT_REL = 128
TOP_K = 4
SWIGLU_ALPHA = 1.702
SWIGLU_LIMIT = 7.0
NORM_EPS = 1e-6
LN_X_EPS = 64e-5
DECAY_LORA = 64
AAA_LORA = 64
GATE_LORA = 128

LANES = 128
GROUP = 256
NEG_BIG = -1e30

VMEM_LIMIT = 56 * 1024 * 1024

ATT_QROWS = 4096
ATT_SUB = 256
ATT_HALO = LEFT_CHUNKS * CHUNK
ATT_KEYS = ATT_SUB + ATT_HALO
ATT_MASKED_TILES = ATT_HALO // ATT_SUB
ATT_INTERLEAVE = 4

LOCAL_UNROLL = 2

MOE_ROWS = 512
ROUTE_TILE = 512
RUN_ALIGN = 8
RUN_SIZES = tuple(ROUTE_TILE >> s for s in range((ROUTE_TILE // RUN_ALIGN).bit_length()))
assert MOE_ROWS <= 2 * RUN_SIZES[0] and RUN_SIZES[-1] == RUN_ALIGN


def _dot(a, b):
    return jnp.dot(a, b, preferred_element_type=F32)


def _dot_nt(a, b):
    return lax.dot_general(a, b, (((1,), (1,)), ((), ())), preferred_element_type=F32)


def _head_sums(a, block_ones):
    a = a.astype(BF16)
    return jnp.concatenate(
        [_dot(a[:, j: j + GROUP], block_ones) for j in range(0, a.shape[1], GROUP)], axis=1)


def _pack_bf16_pairs(x):
    n = x.shape[1] // 2
    bits = pltpu.bitcast(x.astype(BF16).astype(F32), jnp.uint32)
    return bits[:, :n] | (bits[:, n:] >> 16)


def _unpack_bf16_pairs(w):
    hi = pltpu.bitcast(w & jnp.uint32(0xFFFF0000), F32)
    lo = pltpu.bitcast(w << 16, F32)
    return jnp.concatenate([hi, lo], axis=1).astype(BF16)


def _sigmoid(x):
    return 0.5 * jnp.tanh(0.5 * x) + 0.5


def _inproj_kernel(x_ref, g_ref, w_ref, qg_ref, kg_ref, bd_ref, qkv_ref, pr_ref, *, att_w):
    x = x_ref[...]
    ms = jnp.mean(x * x, axis=-1, keepdims=True)
    h = (x * lax.rsqrt(ms + NORM_EPS) * g_ref[...]).astype(BF16)
    qkv = _dot(h, w_ref[:, : 3 * att_w])
    bd = bd_ref[...]

    def head_norm(t, g):
        ss = _head_sums(t * t, bd)
        return t * lax.rsqrt(ss * (1.0 / HEAD_DIM) + NORM_EPS) * g

    q = head_norm(qkv[:, :att_w], qg_ref[...]) * (HEAD_DIM ** -0.5)
    k = head_norm(qkv[:, att_w: 2 * att_w], kg_ref[...])
    qkv_ref[:, :att_w] = q.astype(BF16)
    qkv_ref[:, att_w: 2 * att_w] = k.astype(BF16)
    qkv_ref[:, 2 * att_w:] = qkv[:, 2 * att_w:].astype(BF16)
    pr_ref[...] = _dot(h, w_ref[:, 3 * att_w:]).astype(BF16)


def _inproj(x2d, g1, w_in_bf, qg, kg, bd, att_w, tm=1024):
    T, D = x2d.shape
    in_cols = w_in_bf.shape[1]
    r_cols = in_cols - 3 * att_w
    const = lambda i: (0, 0)
    return pl.pallas_call(
        functools.partial(_inproj_kernel, att_w=att_w),
        grid=(T // tm,),
        in_specs=[
            pl.BlockSpec((tm, D), lambda i: (i, 0)),
            pl.BlockSpec((1, D), const),
            pl.BlockSpec((D, in_cols), const),
            pl.BlockSpec((1, att_w), const),
            pl.BlockSpec((1, att_w), const),
            pl.BlockSpec((GROUP, GROUP), const),
        ],
        out_specs=[
            pl.BlockSpec((tm, 3 * att_w), lambda i: (i, 0)),
            pl.BlockSpec((tm, r_cols), lambda i: (i, 0)),
        ],
        out_shape=[
            jax.ShapeDtypeStruct((T, 3 * att_w), BF16),
            jax.ShapeDtypeStruct((T, r_cols), BF16),
        ],
        compiler_params=pltpu.CompilerParams(
            dimension_semantics=("arbitrary",), vmem_limit_bytes=VMEM_LIMIT),
        name="inproj",
    )(x2d, g1, w_in_bf, qg, kg, bd)


def _attn_kernel(q_ref, kp_ref, kc_ref, vp_ref, vc_ref, bias_ref, o_ref, kbuf, vbuf):
    i = pl.program_id(2)
    kbuf[:ATT_HALO, :] = kp_ref[...]
    kbuf[ATT_HALO:, :] = kc_ref[...]
    lane_head = lax.broadcasted_iota(jnp.int32, (1, LANES), 1) >> (HEAD_DIM.bit_length() - 1)
    one = jnp.ones((), BF16)
    for h in range(2):
        vbuf[h, :ATT_HALO, :] = jnp.where(lane_head == h, vp_ref[...], one)
        vbuf[h, ATT_HALO:, :] = jnp.where(lane_head == h, vc_ref[...], one)
    tiles = range(ATT_QROWS // ATT_SUB)
    col = lax.broadcasted_iota(jnp.int32, (1, ATT_KEYS), 1)
    all_units =[(qs * ATT_SUB, h * HEAD_DIM, h, qs) for qs in tiles for h in range(2)]
    ext = []
    for u0 in range(0, len(all_units), ATT_INTERLEAVE):
        units = all_units[u0: u0 + ATT_INTERLEAVE]
        s = [_dot_nt(q_ref[r0: r0 + ATT_SUB, lo: lo + HEAD_DIM],
                     kbuf[r0: r0 + ATT_KEYS, lo: lo + HEAD_DIM]) + bias_ref[h]
             for (r0, lo, h, qs) in units]
        s = [x + jnp.where((col < ATT_HALO - qs * ATT_SUB) & (i == 0), NEG_BIG, 0.0)
             if qs < ATT_MASKED_TILES else x for x, (_, _, _, qs) in zip(s, units)]
        m = [jnp.max(x, axis=-1, keepdims=True) for x in s]
        p = [jnp.exp(x - mx).astype(BF16) for x, mx in zip(s, m)]
        ext += [_dot(x, vbuf[h, r0: r0 + ATT_KEYS, :]) for x, (r0, _, h, _) in zip(p, units)]
    for qs in tiles:
        e0, e1 = ext[2 * qs], ext[2 * qs + 1]
        o_ref[qs * ATT_SUB: (qs + 1) * ATT_SUB, :] = jnp.where(
            lane_head == 0, e0 / pltpu.roll(e0, HEAD_DIM, axis=1),
            e1 / pltpu.roll(e1, HEAD_DIM, axis=1)).astype(BF16)


def _attention(qkv, bias, B, S, att_w):
    T = B * S
    n_i = S // ATT_QROWS
    n_hp = att_w // LANES
    heads_per = LANES // HEAD_DIM
    assert heads_per == 2

    def cur(col0):
        return lambda b, hp, i: (b * n_i + i, col0 + hp)

    halo_per_blk = ATT_QROWS // ATT_HALO

    def prev(col0):
        return lambda b, hp, i: (
            b * n_i * halo_per_blk + jnp.maximum(i * halo_per_blk - 1, 0), col0 + hp)

    blk = (ATT_QROWS, LANES)
    halo = (ATT_HALO, LANES)
    return pl.pallas_call(
        _attn_kernel,
        grid=(B, n_hp, n_i),
        in_specs=[
            pl.BlockSpec(blk, cur(0)),
            pl.BlockSpec(halo, prev(n_hp)),
            pl.BlockSpec(blk, cur(n_hp)),
            pl.BlockSpec(halo, prev(2 * n_hp)),
            pl.BlockSpec(blk, cur(2 * n_hp)),
            pl.BlockSpec((heads_per,) + bias.shape[1:], lambda b, hp, i: (hp, 0, 0)),
        ],
        out_specs=pl.BlockSpec(blk, lambda b, hp, i: (b * n_i + i, hp)),
        out_shape=jax.ShapeDtypeStruct((T, att_w), BF16),
        scratch_shapes=[
            pltpu.VMEM((ATT_HALO + ATT_QROWS, LANES), BF16),
            pltpu.VMEM((2, ATT_HALO + ATT_QROWS, LANES), BF16),
        ],
        compiler_params=pltpu.CompilerParams(
            dimension_semantics=("arbitrary", "arbitrary", "arbitrary"),
            vmem_limit_bytes=VMEM_LIMIT),
        name="band_attention",
    )(qkv, qkv, qkv, qkv, qkv, bias)


def _attention_bias(rel_bias):
    H = rel_bias.shape[0]
    table = rel_bias.astype(F32)
    far = LEFT_CHUNKS * CHUNK + ATT_SUB - 1
    n_left = far - MAX_LEFT_REL + 1
    period = ATT_SUB + ATT_KEYS + 1
    ext = jnp.concatenate([
        jnp.broadcast_to(table[:, :1], (H, n_left)), table[:, 1:],
        jnp.broadcast_to(table[:, -1:], (H, period - n_left - (table.shape[1] - 1)))], axis=1)
    flat = jnp.tile(ext, (1, ATT_SUB + 1))[:, ATT_SUB - 1: ATT_SUB - 1 + ATT_SUB * (period - 1)]
    bias = flat.reshape(H, ATT_SUB, period - 1)[:, :, :ATT_KEYS]
    r = jnp.arange(ATT_SUB)[:, None]
    c = jnp.arange(ATT_KEYS)[None, :]
    band = c // CHUNK - r // CHUNK
    ok = (band >= 0) & (band <= LEFT_CHUNKS)
    return jnp.where(ok[None], bias, NEG_BIG)


def _rwkv_kernel(p_ref, pprev_ref, mu_ref, w0_ref, a0_ref, lora_ref, wg_ref, kk_ref, ka_ref,
                 rk_ref, lng_ref, lnb_ref, bd_ref, tri_ref, o_ref,
                 rt_s, at_s, bt_s, kt_s, bh_s, kh_s, v_s, wl_s, y_s, gate_s, bonus_s,
                 z_s, g_s, y0_s, m_s, n_s, *, width, tm):
    i = pl.program_id(1)
    n_groups = width // GROUP
    heads_per_group = GROUP // HEAD_DIM

    @pl.when(i == 0)
    def _():
        z_s[...] = jnp.zeros_like(z_s)

    part_rows = LOCAL_UNROLL * CHUNK
    bd = bd_ref[...]

    def prep(part):
        r0 = part * part_rows
        rows = slice(r0, r0 + part_rows)
        p = p_ref[rows, :].astype(F32)
        if part == 0:
            has_prev = jnp.where(i > 0, 1.0, 0.0)
            prev_row = pprev_ref[7:8, :].astype(F32) * has_prev
        else:
            prev_row = p_ref[r0 - 16: r0, :].astype(F32)[15:16]
        row = lax.broadcasted_iota(jnp.int32, (part_rows, 1), 0)
        prev = jnp.where(row == 0, prev_row, pltpu.roll(p, 1, axis=0))
        ps = p + (prev - p) * mu_ref[...]

        r = ps[:, :width]
        k = ps[:, width: 2 * width]
        v = ps[:, 2 * width: 3 * width]
        wa = ps[:, 3 * width: 3 * width + DECAY_LORA + AAA_LORA]
        gd = ps[:, 3 * width + DECAY_LORA + AAA_LORA:]
        lane = lax.broadcasted_iota(jnp.int32, (1, DECAY_LORA + AAA_LORA), 1)
        wa = jnp.where(lane < DECAY_LORA, jnp.tanh(wa), wa)
        lw = _dot(wa.astype(BF16), lora_ref[...])
        logw = -math.exp(-0.5) * _sigmoid(w0_ref[...] + lw[:, :width])
        eta = _sigmoid(a0_ref[...] + lw[:, width:])
        gate_s[rows, :] = _dot(_sigmoid(gd).astype(BF16), wg_ref[...])

        kk = k * kk_ref[...]
        kk = kk * lax.rsqrt(jnp.maximum(_head_sums(kk * kk, bd), 1e-24))
        km = k * (1.0 + (eta - 1.0) * ka_ref[...])
        bb = kk * eta
        bonus_s[rows, :] = _head_sums(r * km * rk_ref[...], bd) * v

        tri = tri_ref[...]
        cs_chunks = [_split_dot_left(tri, logw[c * CHUNK: (c + 1) * CHUNK])
                     for c in range(LOCAL_UNROLL)]
        cs = jnp.concatenate(cs_chunks, axis=0)
        tot = jnp.concatenate(
            [jnp.broadcast_to(x[CHUNK - 1:], (CHUNK, width)) for x in cs_chunks], axis=0)
        e_pos = jnp.exp(cs)
        e_neg = jnp.exp(-cs)
        e_tail = jnp.exp(tot - cs)
        rt_s[rows, :] = r * e_pos
        at_s[rows, :] = -kk * jnp.exp(cs - logw)
        bt_s[rows, :] = bb * e_neg
        kt_s[rows, :] = km * e_neg
        bh_s[rows, :] = bb * e_tail
        kh_s[rows, :] = km * e_tail
        v_s[rows, :] = v
        wl_s[rows, :] = jnp.exp(tot)

    rr = lax.broadcasted_iota(jnp.int32, (GROUP, GROUP), 0)
    cc = lax.broadcasted_iota(jnp.int32, (GROUP, GROUP), 1)
    shift = CHUNK.bit_length() - 1
    same = (rr >> shift) == (cc >> shift)
    strict = same & (rr > cc)
    incl = same & (rr >= cc)
    eye = jnp.where(rr == cc, 1.0, 0.0)
    lane_head = lax.broadcasted_iota(jnp.int32, (1, GROUP), 1) >> shift

    def stack_masked(x):
        return jnp.concatenate(
            [jnp.where(lane_head == h, x, 0.0) for h in range(heads_per_group)], axis=0)

    def stack_plain(x):
        return jnp.concatenate([x] * heads_per_group, axis=0)

    def local(part):
        units = []
        for u in range(LOCAL_UNROLL):
            c = part * LOCAL_UNROLL + u
            rows = slice(c * CHUNK, (c + 1) * CHUNK)
            units += [(c, gi, rows, slice(gi * GROUP, (gi + 1) * GROUP)) for gi in range(n_groups)]

        def load(ref, stack):
            return [stack(ref[rows, cols]) for (_, _, rows, cols) in units]

        def each(f, *lists):
            return [f(*args) for args in zip(*lists)]

        b16 = lambda xs: [x.astype(BF16) for x in xs]
        rt4 = load(rt_s, stack_masked)
        at4, rt4b, v4 = b16(load(at_s, stack_masked)), b16(rt4), load(v_s, stack_masked)
        v4b, bh4 = b16(v4), b16(load(bh_s, stack_masked))
        kh4 = b16(load(kh_s, stack_masked))
        bt_rep, kt_rep = b16(load(bt_s, stack_plain)), b16(load(kt_s, stack_plain))
        a_ab = each(lambda a, b: jnp.where(strict, _dot_nt(a, b), 0.0), at4, bt_rep)
        a_ak = b16(each(lambda a, b: jnp.where(strict, _dot_nt(a, b), 0.0), at4, kt_rep))
        a_rb = b16(each(lambda a, b: jnp.where(incl, _dot_nt(a, b), 0.0), rt4b, bt_rep))
        a_rk = b16(each(lambda a, b: jnp.where(incl, _dot_nt(a, b), 0.0), rt4b, kt_rep))
        tinv = [eye + a for a in a_ab]
        pw = b16(a_ab)
        span = 2
        while span < CHUNK:
            pw = b16(each(_dot, pw, pw))
            tinv = each(lambda t, p: t + _dot(p, t.astype(BF16)), tinv, pw)
            span *= 2
        tinv = b16(tinv)
        p4 = each(_dot, tinv, at4)
        q4 = each(_dot, tinv, b16(each(_dot, a_ak, v4b)))
        p4b, q4b = b16(p4), b16(q4)
        g4 = each(lambda r, a, p: r + _dot(a, p), rt4, a_rb, p4b)
        y0 = each(lambda a, q, ak, v: _dot(a, q) + _dot(ak, v), a_rb, q4b, a_rk, v4b)
        m4 = each(lambda p, b: _dot(p.T.astype(BF16), b), p4, bh4)
        n4 = each(lambda q, b, v, k: _dot(q.T.astype(BF16), b) + _dot(v.T.astype(BF16), k),
                  q4, bh4, v4, kh4)
        for (c, gi, _, _), g, y, m, n in zip(units, g4, y0, m4, n4):
            g_s[c, gi] = g.astype(BF16)
            y0_s[c, gi] = y
            m_s[c, gi] = m.astype(BF16)
            n_s[c, gi] = n

    def carry(part):
        for c in range(part * LOCAL_UNROLL, (part + 1) * LOCAL_UNROLL):
            rows = slice(c * CHUNK, (c + 1) * CHUNK)
            for gi in range(n_groups):
                cols = slice(gi * GROUP, (gi + 1) * GROUP)
                zt = z_s[gi]
                zb = zt.astype(BF16)
                y4 = _dot_nt(g_s[c, gi], zb) + y0_s[c, gi]
                z_s[gi] = (zt * wl_s[c * CHUNK: c * CHUNK + 1, cols]
                           + _dot(zb, m_s[c, gi]) + n_s[c, gi])
                ytok = y4[:CHUNK]
                for h in range(1, heads_per_group):
                    ytok = ytok + y4[h * CHUNK: (h + 1) * CHUNK]
                y_s[rows, cols] = ytok
        rows = slice(part * part_rows, (part + 1) * part_rows)
        y = y_s[rows, :]
        inv_n = 1.0 / HEAD_DIM
        mean = _head_sums(y, bd) * inv_n
        d = y - mean
        var = _head_sums(d * d, bd) * inv_n
        yn = d * lax.rsqrt(var + LN_X_EPS) * lng_ref[...] + lnb_ref[...]
        o_ref[rows, :] = ((yn + bonus_s[rows, :]) * gate_s[rows, :]).astype(BF16)

    n_parts = tm // part_rows
    prep(0)
    for part in range(n_parts):
        if part + 1 < n_parts:
            prep(part + 1)
        local(part)
        if part > 0:
            carry(part - 1)
    carry(n_parts - 1)


def _split_dot_left(a_exact, b, terms=2):
    acc = None
    rem = b
    for _ in range(terms):
        part = rem.astype(BF16)
        rem = rem - part.astype(F32)
        d = _dot(a_exact, part)
        acc = d if acc is None else acc + d
    return acc


def _rwkv(pr, mu, w0, a0, lora, wg, k_k, k_a, r_k, ln_g, ln_b, bd, B, S, width, tm=512):
    T = B * S
    n_t = S // tm
    cols = pr.shape[1]
    ti = jnp.arange(CHUNK)
    tri = (ti[:, None] >= ti[None, :]).astype(BF16)
    const = lambda b, i: (0, 0)
    vec = lambda n: pl.BlockSpec((1, n), const)
    scr = lambda: pltpu.VMEM((tm, width), F32)
    return pl.pallas_call(
        functools.partial(_rwkv_kernel, width=width, tm=tm),
        grid=(B, n_t),
        in_specs=[
            pl.BlockSpec((tm, cols), lambda b, i: (b * n_t + i, 0)),
            pl.BlockSpec((8, cols), lambda b, i: (jnp.maximum((b * n_t + i) * (tm // 8) - 1, 0), 0)),
            vec(cols), vec(width), vec(width),
            pl.BlockSpec(lora.shape, const),
            pl.BlockSpec(wg.shape, const),
            vec(width), vec(width), vec(width), vec(width), vec(width),
            pl.BlockSpec((GROUP, GROUP), const),
            pl.BlockSpec((CHUNK, CHUNK), const),
        ],
        out_specs=pl.BlockSpec((tm, width), lambda b, i: (b * n_t + i, 0)),
        out_shape=jax.ShapeDtypeStruct((T, width), BF16),
        scratch_shapes=[scr() for _ in range(11)]
        + [pltpu.VMEM((width // GROUP, GROUP, GROUP), F32)]
        + [pltpu.VMEM((tm // CHUNK, width // GROUP, GROUP, GROUP), dt)
           for dt in (BF16, F32, BF16, F32)],
        compiler_params=pltpu.CompilerParams(
            dimension_semantics=("arbitrary", "arbitrary"), vmem_limit_bytes=VMEM_LIMIT),
        name="rwkv7",
    )(pr, pr, mu, w0, a0, lora, wg, k_k, k_a, r_k, ln_g, ln_b, bd, tri)


def _outproj_kernel(att_ref, rwk_ref, x_ref, wa_ref, wr_ref, g_ref, wrt_ref, br_ref, upper_ref,
                    x2_ref, xn_ref, eidx_ref, rank_ref, gate_ref, cnt_ref, *, n_exp, tm):
    x2 = x_ref[...] + _dot(att_ref[...], wa_ref[...]) + _dot(rwk_ref[...], wr_ref[...])
    x2_ref[...] = x2
    ms = jnp.mean(x2 * x2, axis=-1, keepdims=True)
    xn = x2 * lax.rsqrt(ms + NORM_EPS) * g_ref[...]
    xn_ref[...] = xn.astype(BF16)

    x_hi = xn.astype(BF16)
    x_lo = (xn - x_hi.astype(F32)).astype(BF16)
    logits = (_dot_nt(wrt_ref[0], x_hi) + _dot_nt(wrt_ref[0], x_lo)
              + _dot_nt(wrt_ref[1], x_hi)) + br_ref[...]

    eio = lax.broadcasted_iota(jnp.int32, (n_exp, tm), 0)
    work = logits
    vals, onehots = [], []
    for k in range(TOP_K):
        m = jnp.max(work, axis=0, keepdims=True)
        idx = jnp.min(jnp.where(work == m, eio, n_exp), axis=0, keepdims=True)
        oh = eio == idx
        vals.append(m)
        onehots.append(oh)
        eidx_ref[k: k + 1, :] = idx
        work = jnp.where(oh, -jnp.inf, work)

    exps = [jnp.exp(vk - vals[0]) for vk in vals]
    denom = exps[0]
    for e in exps[1:]:
        denom = denom + e
    for k in range(TOP_K):
        gate_ref[k: k + 1, :] = exps[k] / denom

    sel = onehots[0]
    for oh in onehots[1:]:
        sel = sel | oh
    sel_f = jnp.where(sel, 1.0, 0.0)
    before = _dot(sel_f.astype(BF16), upper_ref[...])
    for k in range(TOP_K):
        rank_ref[k: k + 1, :] = jnp.sum(
            jnp.where(onehots[k], before, 0.0), axis=0, keepdims=True).astype(jnp.int32)
    cnt_ref[...] = jnp.broadcast_to(
        jnp.sum(sel_f, axis=1, keepdims=True), cnt_ref.shape).astype(jnp.int32)


def _outproj(att, rwk, x2d, w_att, w_rwk, g2, wrt, br, n_exp):
    T, D = x2d.shape
    tm = ROUTE_TILE
    upper = (jnp.arange(tm)[:, None] < jnp.arange(tm)[None, :]).astype(BF16)
    const = lambda i: (0, 0)
    rows = lambda n: pl.BlockSpec((tm, n), lambda i: (i, 0))
    lanes = pl.BlockSpec((TOP_K, tm), lambda i: (0, i))
    return pl.pallas_call(
        functools.partial(_outproj_kernel, n_exp=n_exp, tm=tm),
        grid=(T // tm,),
        in_specs=[
            rows(att.shape[1]), rows(rwk.shape[1]), rows(D),
            pl.BlockSpec(w_att.shape, const), pl.BlockSpec(w_rwk.shape, const),
            pl.BlockSpec((1, D), const),
            pl.BlockSpec(wrt.shape, lambda i: (0, 0, 0)),
            pl.BlockSpec((n_exp, 1), const),
            pl.BlockSpec((tm, tm), const),
        ],
        out_specs=[rows(D), rows(D), lanes, lanes, lanes,
                   pl.BlockSpec((n_exp, LANES), lambda i: (i, 0))],
        out_shape=[
            jax.ShapeDtypeStruct((T, D), F32),
            jax.ShapeDtypeStruct((T, D), BF16),
            jax.ShapeDtypeStruct((TOP_K, T), jnp.int32),
            jax.ShapeDtypeStruct((TOP_K, T), jnp.int32),
            jax.ShapeDtypeStruct((TOP_K, T), F32),
            jax.ShapeDtypeStruct((T // tm * n_exp, LANES), jnp.int32),
        ],
        compiler_params=pltpu.CompilerParams(
            dimension_semantics=("arbitrary",), vmem_limit_bytes=VMEM_LIMIT),
        name="outproj_router",
    )(att, rwk, x2d, w_att, w_rwk, g2, wrt, br, upper)


def _pow2_sizes(limit):
    top = 1 << (limit.bit_length() - 1)
    return tuple(top >> s for s in range((top // RUN_ALIGN).bit_length()))


def _run_copies(action, length, src, src_off, dst, dst_off, sem, sizes=RUN_SIZES):
    for size in sizes:
        start = length & ~(2 * size - 1)

        @pl.when((length & size) != 0)
        def _():
            cp = pltpu.make_async_copy(
                src.at[pl.ds(pl.multiple_of(src_off + start, RUN_ALIGN), size)],
                dst.at[pl.ds(pl.multiple_of(dst_off + start, RUN_ALIGN), size)], sem)
            cp.start() if action == "start" else cp.wait()


def _dispatch_kernel(soff_ref, rlen_ref, rdst_ref, tail_dst_ref, tail_len_ref, nact_ref,
                     slot_ref, xn_ref, xs_hbm, buf, zbuf, sem, run_sems, *, n_exp, slots, n_blocks):
    j = pl.program_id(0)
    tm = ROUTE_TILE

    @pl.when(j == 0)
    def _():
        zbuf[...] = jnp.zeros_like(zbuf)

        def unused_block(action):
            def body(b, c):
                cp = pltpu.make_async_copy(
                    zbuf, xs_hbm.at[pl.ds(pl.multiple_of(b * MOE_ROWS, MOE_ROWS), MOE_ROWS)], sem)
                cp.start() if action == "start" else cp.wait()
                return c
            return body

        for action in ("start", "wait"):
            for e in range(n_exp):
                _run_copies(action, tail_len_ref[e], zbuf, 0, xs_hbm, tail_dst_ref[e], sem)
            lax.fori_loop(nact_ref[0], n_blocks, unused_block(action), 0)

    cur = lax.rem(j, 2)
    x = xn_ref[...]
    for sb in range(slots // GROUP):
        sio = lax.broadcasted_iota(jnp.int32, (GROUP, tm), 0) + sb * GROUP
        p = jnp.zeros((GROUP, tm), F32)
        for k in range(TOP_K):
            p = jnp.where(sio == slot_ref[0, k: k + 1, :], 1.0, p)
        buf[cur, sb * GROUP: (sb + 1) * GROUP, :] = _pack_bf16_pairs(_dot(p.astype(BF16), x))

    def start_runs(tile, half):
        for e in range(n_exp):
            r = tile * n_exp + e
            _run_copies("start", rlen_ref[r], buf.at[half], soff_ref[r], xs_hbm, rdst_ref[r],
                        run_sems.at[half])

    def wait_runs(tile, half):
        last = tile * n_exp + n_exp - 1
        _run_copies("wait", soff_ref[last] + rlen_ref[last], buf.at[half], 0, xs_hbm, 0,
                    run_sems.at[half], sizes=_pow2_sizes(slots))

    start_runs(j, cur)

    @pl.when(j > 0)
    def _():
        wait_runs(j - 1, 1 - cur)

    @pl.when(j == pl.num_programs(0) - 1)
    def _():
        wait_runs(j, cur)


def _dispatch(soff, rlen, rdst, tail_dst, tail_len, n_active, slot_lane, xn, n_blocks, slots):
    T, D = xn.shape
    tm = ROUTE_TILE
    n_exp = tail_dst.shape[0]
    n_rows = n_blocks * MOE_ROWS
    grid_spec = pltpu.PrefetchScalarGridSpec(
        num_scalar_prefetch=6,
        grid=(T // tm,),
        in_specs=[
            pl.BlockSpec((1, TOP_K, tm), lambda i, *_: (i, 0, 0)),
            pl.BlockSpec((tm, D), lambda i, *_: (i, 0)),
        ],
        out_specs=pl.BlockSpec(memory_space=pl.ANY),
        scratch_shapes=[pltpu.VMEM((2, slots, D // 2), jnp.uint32),
                        pltpu.VMEM((MOE_ROWS, D // 2), jnp.uint32),
                        pltpu.SemaphoreType.DMA, pltpu.SemaphoreType.DMA((2,))],
    )
    return pl.pallas_call(
        functools.partial(_dispatch_kernel, n_exp=n_exp, slots=slots, n_blocks=n_blocks),
        grid_spec=grid_spec,
        out_shape=jax.ShapeDtypeStruct((n_rows, D // 2), jnp.uint32),
        compiler_params=pltpu.CompilerParams(
            dimension_semantics=("arbitrary",), vmem_limit_bytes=VMEM_LIMIT),
        name="moe_dispatch",
    )(soff, rlen, rdst, tail_dst, tail_len, n_active, slot_lane, xn)


def _moe_kernel(bexp_ref, nact_ref, x_ref, wup_ref, bup_ref, wdn_ref, bdn_ref,
                y_ref, wup_bf, wdn_bf, *, d_exp):
    i = pl.program_id(0)

    @pl.when(i < nact_ref[0])
    def _():
        new_expert = jnp.logical_or(i == 0, bexp_ref[i] != bexp_ref[jnp.maximum(i - 1, 0)])

        @pl.when(new_expert)
        def _():
            wup_bf[...] = wup_ref[0].astype(BF16)
            wdn_bf[...] = wdn_ref[0].astype(BF16)

        x = _unpack_bf16_pairs(x_ref[...])
        h_glu = _dot(x, wup_bf[:, :d_exp]) + bup_ref[0, :, :d_exp]
        h_lin = _dot(x, wup_bf[:, d_exp:]) + bup_ref[0, :, d_exp:]
        h_glu = jnp.minimum(h_glu, SWIGLU_LIMIT)
        h_lin = jnp.clip(h_lin, -SWIGLU_LIMIT, SWIGLU_LIMIT)
        act = h_glu * _sigmoid(SWIGLU_ALPHA * h_glu) * (h_lin + 1.0)
        y_ref[...] = _pack_bf16_pairs(_dot(act.astype(BF16), wdn_bf[...]) + bdn_ref[0])

    @pl.when(i >= nact_ref[0])
    def _():
        y_ref[...] = jnp.zeros_like(y_ref)


def _moe(blk_expert, n_active, xs, w_up, b_up, w_down, b_down):
    n_blocks = blk_expert.shape[0]
    n_exp, D, two_f = w_up.shape
    d_exp = two_f // 2
    blk = lambda i, na: jnp.minimum(i, na[0] - 1)
    grid_spec = pltpu.PrefetchScalarGridSpec(
        num_scalar_prefetch=2,
        grid=(n_blocks,),
        in_specs=[
            pl.BlockSpec((MOE_ROWS, D // 2), lambda i, be, na: (blk(i, na), 0)),
            pl.BlockSpec((1, D, two_f), lambda i, be, na: (be[blk(i, na)], 0, 0)),
            pl.BlockSpec((1, 1, two_f), lambda i, be, na: (be[blk(i, na)], 0, 0)),
            pl.BlockSpec((1, d_exp, D), lambda i, be, na: (be[blk(i, na)], 0, 0)),
            pl.BlockSpec((1, 1, D), lambda i, be, na: (be[blk(i, na)], 0, 0)),
        ],
        out_specs=pl.BlockSpec((MOE_ROWS, D // 2), lambda i, be, na: (i, 0)),
        scratch_shapes=[
            pltpu.VMEM((D, two_f), BF16),
            pltpu.VMEM((d_exp, D), BF16),
        ],
    )
    return pl.pallas_call(
        functools.partial(_moe_kernel, d_exp=d_exp),
        grid_spec=grid_spec,
        out_shape=jax.ShapeDtypeStruct((n_blocks * MOE_ROWS, D // 2), jnp.uint32),
        compiler_params=pltpu.CompilerParams(
            dimension_semantics=("arbitrary",), vmem_limit_bytes=VMEM_LIMIT),
        name="moe_ffn",
    )(blk_expert, n_active, xs,
      w_up, b_up.reshape(n_exp, 1, two_f), w_down, b_down.reshape(n_exp, 1, D))


def _combine_kernel(soff_ref, rlen_ref, rdst_ref, slot_ref, gate_ref, x2_ref, y_hbm, o_ref,
                    buf, sems, *, n_exp, slots):
    j = pl.program_id(0)
    tm = ROUTE_TILE

    cur = lax.rem(j, 2)

    def start_runs(tile, half):
        for e in range(n_exp):
            r = tile * n_exp + e
            _run_copies("start", rlen_ref[r], y_hbm, rdst_ref[r], buf.at[half], soff_ref[r],
                        sems.at[half])

    @pl.when(j == 0)
    def _():
        buf[...] = jnp.zeros_like(buf)
        start_runs(0, 0)

    @pl.when(j + 1 < pl.num_programs(0))
    def _():
        start_runs(j + 1, 1 - cur)

    last = j * n_exp + n_exp - 1
    _run_copies("wait", soff_ref[last] + rlen_ref[last], y_hbm, 0, buf.at[cur], 0, sems.at[cur],
                sizes=_pow2_sizes(slots))

    slot = slot_ref[...]
    gate = gate_ref[...]
    acc = x2_ref[...]
    for sb in range(slots // GROUP):
        lio = lax.broadcasted_iota(jnp.int32, (tm, GROUP), 1) + sb * GROUP
        w = jnp.zeros((tm, GROUP), F32)
        for k in range(TOP_K):
            w = jnp.where(lio == slot[:, k: k + 1], gate[:, k: k + 1], w)
        acc = acc + _dot(w.astype(BF16),
                         _unpack_bf16_pairs(buf[cur, sb * GROUP: (sb + 1) * GROUP, :]))
    o_ref[...] = acc


def _combine(soff, rlen, rdst, slot_row, gates_row, x2, y_rows, slots):
    T, D = x2.shape
    tm = ROUTE_TILE
    n_exp = soff.shape[0] // (T // tm)
    grid_spec = pltpu.PrefetchScalarGridSpec(
        num_scalar_prefetch=3,
        grid=(T // tm,),
        in_specs=[
            pl.BlockSpec((tm, TOP_K), lambda i, *_: (i, 0)),
            pl.BlockSpec((tm, TOP_K), lambda i, *_: (i, 0)),
            pl.BlockSpec((tm, D), lambda i, *_: (i, 0)),
            pl.BlockSpec(memory_space=pl.ANY),
        ],
        out_specs=pl.BlockSpec((tm, D), lambda i, *_: (i, 0)),
        scratch_shapes=[pltpu.VMEM((2, slots, D // 2), jnp.uint32),
                        pltpu.SemaphoreType.DMA((2,))],
    )
    return pl.pallas_call(
        functools.partial(_combine_kernel, n_exp=n_exp, slots=slots),
        grid_spec=grid_spec,
        out_shape=jax.ShapeDtypeStruct((T, D), F32),
        compiler_params=pltpu.CompilerParams(
            dimension_semantics=("arbitrary",), vmem_limit_bytes=VMEM_LIMIT),
        name="moe_combine",
    )(soff, rlen, rdst, slot_row, gates_row, x2, y_rows)


def _block_diag_ones(n, blk):
    i = jnp.arange(n)
    return ((i[:, None] // blk) == (i[None, :] // blk)).astype(BF16)


def _layer(x, norm1_g, w_in, tshift_mu, q_norm_g, k_norm_g, rel_bias, w0, w_decay_up, a0,
           w_aaa_up, w_gate_up, k_k, k_a, r_k, ln_x_g, ln_x_b, w_out, norm2_g, w_router,
           b_router, w_up, b_up, w_down, b_down):
    B, S, D = x.shape
    T = B * S
    att_heads = rel_bias.shape[0]
    att_w = att_heads * HEAD_DIM
    width = w0.shape[0]
    n_exp = w_router.shape[1]
    assert att_w % LANES == 0 and width % GROUP == 0
    assert S % ATT_QROWS == 0 and ATT_QROWS % ATT_HALO == 0 and ATT_HALO % ATT_SUB == 0
    assert T % ROUTE_TILE == 0

    x2d = x.reshape(T, D)
    row = lambda a: a.reshape(1, -1).astype(F32)
    head_ones = _block_diag_ones(GROUP, HEAD_DIM)

    qkv, pr = _inproj(
        x2d, row(norm1_g), w_in.astype(BF16),
        row(jnp.tile(q_norm_g, att_heads)), row(jnp.tile(k_norm_g, att_heads)), head_ones, att_w)

    att = _attention(qkv, _attention_bias(rel_bias), B, S, att_w)

    lora = jnp.zeros((DECAY_LORA + AAA_LORA, 2 * width), F32)
    lora = lora.at[:DECAY_LORA, :width].set(w_decay_up).at[DECAY_LORA:, width:].set(w_aaa_up)
    rwk = _rwkv(pr, row(tshift_mu), row(w0), row(a0), lora.astype(BF16),
                w_gate_up.astype(BF16), row(k_k), row(k_a), row(r_k), row(ln_x_g),
                row(ln_x_b), head_ones, B, S, width)

    wrt = w_router.T.astype(F32)
    wrt_hi = wrt.astype(BF16)
    wrt_lo = (wrt - wrt_hi.astype(F32)).astype(BF16)
    x2, xn, eidx, rank, gates, counts = _outproj(
        att, rwk, x2d, w_out[:att_w].astype(BF16), w_out[att_w:].astype(BF16), row(norm2_g),
        jnp.stack([wrt_hi, wrt_lo]), b_router.reshape(n_exp, 1).astype(F32), n_exp)

    tm = ROUTE_TILE
    n_tiles = T // tm
    i32 = lambda a: a.astype(jnp.int32)
    run_n = counts.reshape(n_tiles, n_exp, LANES)[:, :, 0]
    run_len = (run_n + RUN_ALIGN - 1) // RUN_ALIGN * RUN_ALIGN
    run_off = jnp.cumsum(run_len, axis=1) - run_len
    exp_rows = jnp.sum(run_len, axis=0)
    exp_pad = (exp_rows + MOE_ROWS - 1) // MOE_ROWS * MOE_ROWS
    exp_end = jnp.cumsum(exp_pad)
    exp_start = exp_end - exp_pad
    run_dst = exp_start[None, :] + jnp.cumsum(run_len, axis=0) - run_len
    eidx_t = eidx.reshape(TOP_K, n_tiles, tm)
    experts = jnp.arange(n_exp, dtype=jnp.int32)[:, None, None, None]
    slot = rank.reshape(TOP_K, n_tiles, tm) + jnp.sum(
        jnp.where(eidx_t[None] == experts, run_off.T[:, None, :, None], 0), axis=0)
    slot_lane = i32(slot.transpose(1, 0, 2))
    slot_row = i32(slot.transpose(1, 2, 0).reshape(T, TOP_K))
    slots = -(-(tm * TOP_K + n_exp * (RUN_ALIGN - 1)) // GROUP) * GROUP
    n_blocks = -(-(T * TOP_K + n_tiles * n_exp * (RUN_ALIGN - 1)) // MOE_ROWS) + n_exp
    blk_start = jnp.arange(n_blocks, dtype=jnp.int32) * MOE_ROWS
    blk_expert = i32(jnp.minimum(
        jnp.sum(blk_start[:, None] >= exp_end[None, :], axis=1), n_exp - 1))
    n_active = i32(exp_end[-1:] // MOE_ROWS)
    soff, rlen, rdst = (i32(a.reshape(-1)) for a in (run_off, run_len, run_dst))

    xs = _dispatch(soff, rlen, rdst, i32(exp_start + exp_rows), i32(exp_pad - exp_rows),
                   n_active, slot_lane, xn, n_blocks, slots)
    y_rows = _moe(blk_expert, n_active, xs, w_up, b_up, w_down, b_down)
    out = _combine(soff, rlen, rdst, slot_row, gates.T, x2, y_rows, slots)
    return out.reshape(B, S, D)


def kernel(x, norm1_g, w_in, tshift_mu, q_norm_g, k_norm_g, rel_bias, w0, w_decay_up, a0,
           w_aaa_up, w_gate_up, k_k, k_a, r_k, ln_x_g, ln_x_b, w_out, norm2_g, w_router,
           b_router, w_up, b_up, w_down, b_down):
    params = (norm1_g, w_in, tshift_mu, q_norm_g, k_norm_g, rel_bias, w0, w_decay_up, a0,
              w_aaa_up, w_gate_up, k_k, k_a, r_k, ln_x_g, ln_x_b, w_out, norm2_g, w_router,
              b_router, w_up, b_up, w_down, b_down)
    for l in range(norm1_g.shape[0]):
        x = _layer(x, *[p[l] for p in params])
    return x
```

```python
import functools
import math

import jax
import jax.numpy as jnp
from jax import lax
from jax.experimental import pallas as pl
from jax.experimental.pallas import tpu as pltpu

F32 = jnp.float32
BF16 = jnp.bfloat16

CHUNK = 64
HEAD_DIM = 64
LEFT_CHUNKS = 8
MAX_LEFT_REL = 128
TOP_K = 4
SWIGLU_ALPHA = 1.702
SWIGLU_LIMIT = 7.0
NORM_EPS = 1e-6
LN_X_EPS = 64e-5
DECAY_LORA = 64
AAA_LORA = 64
GATE_LORA = 128

LANES = 128
GROUP = 256
NEG_BIG = -1e30

VMEM_LIMIT = 56 * 1024 * 1024

ATT_QROWS = 4096
ATT_SUB = 256
ATT_HALO = LEFT_CHUNKS * CHUNK
ATT_KEYS = ATT_SUB + ATT_HALO
ATT_MASKED_TILES = ATT_HALO // ATT_SUB
ATT_INTERLEAVE = 4

LOCAL_UNROLL = 2

MOE_ROWS = 512
ROUTE_TILE = 512
RUN_ALIGN = 8
RUN_SIZES = tuple(ROUTE_TILE >> s for s in range((ROUTE_TILE // RUN_ALIGN).bit_length()))
assert MOE_ROWS <= 2 * RUN_SIZES[0] and RUN_SIZES[-1] == RUN_ALIGN


def _dot(a, b):
    return jnp.dot(a, b, preferred_element_type=F32)


def _dot_nt(a, b):
    return lax.dot_general(a, b, (((1,), (1,)), ((), ())), preferred_element_type=F32)


def _head_sums(a, block_ones):
    a = a.astype(BF16)
    return jnp.concatenate(
        [_dot(a[:, j: j + GROUP], block_ones) for j in range(0, a.shape[1], GROUP)], axis=1)


def _pack_bf16_pairs(x):
    n = x.shape[1] // 2
    bits = pltpu.bitcast(x.astype(BF16).astype(F32), jnp.uint32)
    return bits[:, :n] | (bits[:, n:] >> 16)


def _unpack_bf16_pairs(w):
    hi = pltpu.bitcast(w & jnp.uint32(0xFFFF0000), F32)
    lo = pltpu.bitcast(w << 16, F32)
    return jnp.concatenate([hi, lo], axis=1).astype(BF16)


def _sigmoid(x):
    return 0.5 * jnp.tanh(0.5 * x) + 0.5


def _inproj_kernel(x_ref, g_ref, w_ref, qg_ref, kg_ref, bd_ref, qkv_ref, pr_ref, *, att_w):
    x = x_ref[...]
    ms = jnp.mean(x * x, axis=-1, keepdims=True)
    h = (x * lax.rsqrt(ms + NORM_EPS) * g_ref[...]).astype(BF16)
    qkv = _dot(h, w_ref[:, : 3 * att_w])
    bd = bd_ref[...]

    def head_norm(t, g):
        ss = _head_sums(t * t, bd)
        return t * lax.rsqrt(ss * (1.0 / HEAD_DIM) + NORM_EPS) * g

    q = head_norm(qkv[:, :att_w], qg_ref[...]) * (HEAD_DIM ** -0.5)
    k = head_norm(qkv[:, att_w: 2 * att_w], kg_ref[...])
    qkv_ref[:, :att_w] = q.astype(BF16)
    qkv_ref[:, att_w: 2 * att_w] = k.astype(BF16)
    qkv_ref[:, 2 * att_w:] = qkv[:, 2 * att_w:].astype(BF16)
    pr_ref[...] = _dot(h, w_ref[:, 3 * att_w:]).astype(BF16)


def _inproj(x2d, g1, w_in_bf, qg, kg, bd, att_w, tm=1024):
    T, D = x2d.shape
    in_cols = w_in_bf.shape[1]
    r_cols = in_cols - 3 * att_w
    const = lambda i: (0, 0)
    return pl.pallas_call(
        functools.partial(_inproj_kernel, att_w=att_w),
        grid=(T // tm,),
        in_specs=[
            pl.BlockSpec((tm, D), lambda i: (i, 0)),
            pl.BlockSpec((1, D), const),
            pl.BlockSpec((D, in_cols), const),
            pl.BlockSpec((1, att_w), const),
            pl.BlockSpec((1, att_w), const),
            pl.BlockSpec((GROUP, GROUP), const),
        ],
        out_specs=[
            pl.BlockSpec((tm, 3 * att_w), lambda i: (i, 0)),
            pl.BlockSpec((tm, r_cols), lambda i: (i, 0)),
        ],
        out_shape=[
            jax.ShapeDtypeStruct((T, 3 * att_w), BF16),
            jax.ShapeDtypeStruct((T, r_cols), BF16),
        ],
        compiler_params=pltpu.CompilerParams(
            dimension_semantics=("arbitrary",), vmem_limit_bytes=VMEM_LIMIT),
        name="inproj",
    )(x2d, g1, w_in_bf, qg, kg, bd)


def _attn_kernel(q_ref, kp_ref, kc_ref, vp_ref, vc_ref, bias_ref, o_ref, kbuf, vbuf):
    i = pl.program_id(2)
    kbuf[:ATT_HALO, :] = kp_ref[...]
    kbuf[ATT_HALO:, :] = kc_ref[...]
    lane_head = lax.broadcasted_iota(jnp.int32, (1, LANES), 1) >> (HEAD_DIM.bit_length() - 1)
    one = jnp.ones((), BF16)
    for h in range(2):
        vbuf[h, :ATT_HALO, :] = jnp.where(lane_head == h, vp_ref[...], one)
        vbuf[h, ATT_HALO:, :] = jnp.where(lane_head == h, vc_ref[...], one)
    tiles = range(ATT_QROWS // ATT_SUB)
    col = lax.broadcasted_iota(jnp.int32, (1, ATT_KEYS), 1)
    all_units =[(qs * ATT_SUB, h * HEAD_DIM, h, qs) for qs in tiles for h in range(2)]
    ext = []
    for u0 in range(0, len(all_units), ATT_INTERLEAVE):
        units = all_units[u0: u0 + ATT_INTERLEAVE]
        s = [_dot_nt(q_ref[r0: r0 + ATT_SUB, lo: lo + HEAD_DIM],
                     kbuf[r0: r0 + ATT_KEYS, lo: lo + HEAD_DIM]) + bias_ref[h]
             for (r0, lo, h, qs) in units]
        s = [x + jnp.where((col < ATT_HALO - qs * ATT_SUB) & (i == 0), NEG_BIG, 0.0)
             if qs < ATT_MASKED_TILES else x for x, (_, _, _, qs) in zip(s, units)]
        m = [jnp.max(x, axis=-1, keepdims=True) for x in s]
        p = [jnp.exp(x - mx).astype(BF16) for x, mx in zip(s, m)]
        ext += [_dot(x, vbuf[h, r0: r0 + ATT_KEYS, :]) for x, (r0, _, h, _) in zip(p, units)]
    for qs in tiles:
        e0, e1 = ext[2 * qs], ext[2 * qs + 1]
        o_ref[qs * ATT_SUB: (qs + 1) * ATT_SUB, :] = jnp.where(
            lane_head == 0, e0 / pltpu.roll(e0, HEAD_DIM, axis=1),
            e1 / pltpu.roll(e1, HEAD_DIM, axis=1)).astype(BF16)


def _attention(qkv, bias, B, S, att_w):
    T = B * S
    n_i = S // ATT_QROWS
    n_hp = att_w // LANES
    heads_per = LANES // HEAD_DIM
    assert heads_per == 2

    def cur(col0):
        return lambda b, hp, i: (b * n_i + i, col0 + hp)

    halo_per_blk = ATT_QROWS // ATT_HALO

    def prev(col0):
        return lambda b, hp, i: (
            b * n_i * halo_per_blk + jnp.maximum(i * halo_per_blk - 1, 0), col0 + hp)

    blk = (ATT_QROWS, LANES)
    halo = (ATT_HALO, LANES)
    return pl.pallas_call(
        _attn_kernel,
        grid=(B, n_hp, n_i),
        in_specs=[
            pl.BlockSpec(blk, cur(0)),
            pl.BlockSpec(halo, prev(n_hp)),
            pl.BlockSpec(blk, cur(n_hp)),
            pl.BlockSpec(halo, prev(2 * n_hp)),
            pl.BlockSpec(blk, cur(2 * n_hp)),
            pl.BlockSpec((heads_per,) + bias.shape[1:], lambda b, hp, i: (hp, 0, 0)),
        ],
        out_specs=pl.BlockSpec(blk, lambda b, hp, i: (b * n_i + i, hp)),
        out_shape=jax.ShapeDtypeStruct((T, att_w), BF16),
        scratch_shapes=[
            pltpu.VMEM((ATT_HALO + ATT_QROWS, LANES), BF16),
            pltpu.VMEM((2, ATT_HALO + ATT_QROWS, LANES), BF16),
        ],
        compiler_params=pltpu.CompilerParams(
            dimension_semantics=("arbitrary", "arbitrary", "arbitrary"),
            vmem_limit_bytes=VMEM_LIMIT),
        name="band_attention",
    )(qkv, qkv, qkv, qkv, qkv, bias)


def _attention_bias(rel_bias):
    H = rel_bias.shape[0]
    table = rel_bias.astype(F32)
    far = LEFT_CHUNKS * CHUNK + ATT_SUB - 1
    n_left = far - MAX_LEFT_REL + 1
    period = ATT_SUB + ATT_KEYS + 1
    ext = jnp.concatenate([
        jnp.broadcast_to(table[:, :1], (H, n_left)), table[:, 1:],
        jnp.broadcast_to(table[:, -1:], (H, period - n_left - (table.shape[1] - 1)))], axis=1)
    flat = jnp.tile(ext, (1, ATT_SUB + 1))[:, ATT_SUB - 1: ATT_SUB - 1 + ATT_SUB * (period - 1)]
    bias = flat.reshape(H, ATT_SUB, period - 1)[:, :, :ATT_KEYS]
    r = jnp.arange(ATT_SUB)[:, None]
    c = jnp.arange(ATT_KEYS)[None, :]
    band = c // CHUNK - r // CHUNK
    ok = (band >= 0) & (band <= LEFT_CHUNKS)
    return jnp.where(ok[None], bias, NEG_BIG)


def _rwkv_kernel(p_ref, pprev_ref, mu_ref, w0_ref, a0_ref, lora_ref, wg_ref, kk_ref, ka_ref,
                 rk_ref, lng_ref, lnb_ref, bd_ref, tri_ref, o_ref,
                 rt_s, at_s, bt_s, kt_s, bh_s, kh_s, v_s, wl_s, y_s, gate_s, bonus_s,
                 z_s, g_s, y0_s, m_s, n_s, *, width, tm):
    i = pl.program_id(1)
    n_groups = width // GROUP
    heads_per_group = GROUP // HEAD_DIM

    @pl.when(i == 0)
    def _():
        z_s[...] = jnp.zeros_like(z_s)

    part_rows = LOCAL_UNROLL * CHUNK
    bd = bd_ref[...]

    def prep(part):
        r0 = part * part_rows
        rows = slice(r0, r0 + part_rows)
        p = p_ref[rows, :].astype(F32)
        if part == 0:
            has_prev = jnp.where(i > 0, 1.0, 0.0)
            prev_row = pprev_ref[7:8, :].astype(F32) * has_prev
        else:
            prev_row = p_ref[r0 - 16: r0, :].astype(F32)[15:16]
        row = lax.broadcasted_iota(jnp.int32, (part_rows, 1), 0)
        prev = jnp.where(row == 0, prev_row, pltpu.roll(p, 1, axis=0))
        ps = p + (prev - p) * mu_ref[...]

        r = ps[:, :width]
        k = ps[:, width: 2 * width]
        v = ps[:, 2 * width: 3 * width]
        wa = ps[:, 3 * width: 3 * width + DECAY_LORA + AAA_LORA]
        gd = ps[:, 3 * width + DECAY_LORA + AAA_LORA:]
        lane = lax.broadcasted_iota(jnp.int32, (1, DECAY_LORA + AAA_LORA), 1)
        wa = jnp.where(lane < DECAY_LORA, jnp.tanh(wa), wa)
        lw = _dot(wa.astype(BF16), lora_ref[...])
        logw = -math.exp(-0.5) * _sigmoid(w0_ref[...] + lw[:, :width])
        eta = _sigmoid(a0_ref[...] + lw[:, width:])
        gate_s[rows, :] = _dot(_sigmoid(gd).astype(BF16), wg_ref[...])

        kk = k * kk_ref[...]
        kk = kk * lax.rsqrt(jnp.maximum(_head_sums(kk * kk, bd), 1e-24))
        km = k * (1.0 + (eta - 1.0) * ka_ref[...])
        bb = kk * eta
        bonus_s[rows, :] = _head_sums(r * km * rk_ref[...], bd) * v

        tri = tri_ref[...]
        cs_chunks = [_split_dot_left(tri, logw[c * CHUNK: (c + 1) * CHUNK])
                     for c in range(LOCAL_UNROLL)]
        cs = jnp.concatenate(cs_chunks, axis=0)
        tot = jnp.concatenate(
            [jnp.broadcast_to(x[CHUNK - 1:], (CHUNK, width)) for x in cs_chunks], axis=0)
        e_pos = jnp.exp(cs)
        e_neg = jnp.exp(-cs)
        e_tail = jnp.exp(tot - cs)
        rt_s[rows, :] = r * e_pos
        at_s[rows, :] = -kk * jnp.exp(cs - logw)
        bt_s[rows, :] = bb * e_neg
        kt_s[rows, :] = km * e_neg
        bh_s[rows, :] = bb * e_tail
        kh_s[rows, :] = km * e_tail
        v_s[rows, :] = v
        wl_s[rows, :] = jnp.exp(tot)

    rr = lax.broadcasted_iota(jnp.int32, (GROUP, GROUP), 0)
    cc = lax.broadcasted_iota(jnp.int32, (GROUP, GROUP), 1)
    shift = CHUNK.bit_length() - 1
    same = (rr >> shift) == (cc >> shift)
    strict = same & (rr > cc)
    incl = same & (rr >= cc)
    eye = jnp.where(rr == cc, 1.0, 0.0)
    lane_head = lax.broadcasted_iota(jnp.int32, (1, GROUP), 1) >> shift

    def stack_masked(x):
        return jnp.concatenate(
            [jnp.where(lane_head == h, x, 0.0) for h in range(heads_per_group)], axis=0)

    def stack_plain(x):
        return jnp.concatenate([x] * heads_per_group, axis=0)

    def local(part):
        units = []
        for u in range(LOCAL_UNROLL):
            c = part * LOCAL_UNROLL + u
            rows = slice(c * CHUNK, (c + 1) * CHUNK)
            units += [(c, gi, rows, slice(gi * GROUP, (gi + 1) * GROUP)) for gi in range(n_groups)]

        def load(ref, stack):
            return [stack(ref[rows, cols]) for (_, _, rows, cols) in units]

        def each(f, *lists):
            return [f(*args) for args in zip(*lists)]

        b16 = lambda xs: [x.astype(BF16) for x in xs]
        rt4 = load(rt_s, stack_masked)
        at4, rt4b, v4 = b16(load(at_s, stack_masked)), b16(rt4), load(v_s, stack_masked)
        v4b, bh4 = b16(v4), b16(load(bh_s, stack_masked))
        kh4 = b16(load(kh_s, stack_masked))
        bt_rep, kt_rep = b16(load(bt_s, stack_plain)), b16(load(kt_s, stack_plain))
        a_ab = each(lambda a, b: jnp.where(strict, _dot_nt(a, b), 0.0), at4, bt_rep)
        a_ak = b16(each(lambda a, b: jnp.where(strict, _dot_nt(a, b), 0.0), at4, kt_rep))
        a_rb = b16(each(lambda a, b: jnp.where(incl, _dot_nt(a, b), 0.0), rt4b, bt_rep))
        a_rk = b16(each(lambda a, b: jnp.where(incl, _dot_nt(a, b), 0.0), rt4b, kt_rep))
        tinv = [eye + a for a in a_ab]
        pw = b16(a_ab)
        span = 2
        while span < CHUNK:
            pw = b16(each(_dot, pw, pw))
            tinv = each(lambda t, p: t + _dot(p, t.astype(BF16)), tinv, pw)
            span *= 2
        tinv = b16(tinv)
        p4 = each(_dot, tinv, at4)
        q4 = each(_dot, tinv, b16(each(_dot, a_ak, v4b)))
        p4b, q4b = b16(p4), b16(q4)
        g4 = each(lambda r, a, p: r + _dot(a, p), rt4, a_rb, p4b)
        y0 = each(lambda a, q, ak, v: _dot(a, q) + _dot(ak, v), a_rb, q4b, a_rk, v4b)
        m4 = each(lambda p, b: _dot(p.T.astype(BF16), b), p4, bh4)
        n4 = each(lambda q, b, v, k: _dot(q.T.astype(BF16), b) + _dot(v.T.astype(BF16), k),
                  q4, bh4, v4, kh4)
        for (c, gi, _, _), g, y, m, n in zip(units, g4, y0, m4, n4):
            g_s[c, gi] = g.astype(BF16)
            y0_s[c, gi] = y
            m_s[c, gi] = m.astype(BF16)
            n_s[c, gi] = n

    def carry(part):
        for c in range(part * LOCAL_UNROLL, (part + 1) * LOCAL_UNROLL):
            rows = slice(c * CHUNK, (c + 1) * CHUNK)
            for gi in range(n_groups):
                cols = slice(gi * GROUP, (gi + 1) * GROUP)
                zt = z_s[gi]
                zb = zt.astype(BF16)
                y4 = _dot_nt(g_s[c, gi], zb) + y0_s[c, gi]
                z_s[gi] = (zt * wl_s[c * CHUNK: c * CHUNK + 1, cols]
                           + _dot(zb, m_s[c, gi]) + n_s[c, gi])
                ytok = y4[:CHUNK]
                for h in range(1, heads_per_group):
                    ytok = ytok + y4[h * CHUNK: (h + 1) * CHUNK]
                y_s[rows, cols] = ytok
        rows = slice(part * part_rows, (part + 1) * part_rows)
        y = y_s[rows, :]
        inv_n = 1.0 / HEAD_DIM
        mean = _head_sums(y, bd) * inv_n
        d = y - mean
        var = _head_sums(d * d, bd) * inv_n
        yn = d * lax.rsqrt(var + LN_X_EPS) * lng_ref[...] + lnb_ref[...]
        o_ref[rows, :] = ((yn + bonus_s[rows, :]) * gate_s[rows, :]).astype(BF16)

    n_parts = tm // part_rows
    prep(0)
    for part in range(n_parts):
        if part + 1 < n_parts:
            prep(part + 1)
        local(part)
        if part > 0:
            carry(part - 1)
    carry(n_parts - 1)


def _split_dot_left(a_exact, b, terms=2):
    acc = None
    rem = b
    for _ in range(terms):
        part = rem.astype(BF16)
        rem = rem - part.astype(F32)
        d = _dot(a_exact, part)
        acc = d if acc is None else acc + d
    return acc


def _rwkv(pr, mu, w0, a0, lora, wg, k_k, k_a, r_k, ln_g, ln_b, bd, B, S, width, tm=512):
    T = B * S
    n_t = S // tm
    cols = pr.shape[1]
    ti = jnp.arange(CHUNK)
    tri = (ti[:, None] >= ti[None, :]).astype(BF16)
    const = lambda b, i: (0, 0)
    vec = lambda n: pl.BlockSpec((1, n), const)
    scr = lambda: pltpu.VMEM((tm, width), F32)
    return pl.pallas_call(
        functools.partial(_rwkv_kernel, width=width, tm=tm),
        grid=(B, n_t),
        in_specs=[
            pl.BlockSpec((tm, cols), lambda b, i: (b * n_t + i, 0)),
            pl.BlockSpec((8, cols), lambda b, i: (jnp.maximum((b * n_t + i) * (tm // 8) - 1, 0), 0)),
            vec(cols), vec(width), vec(width),
            pl.BlockSpec(lora.shape, const),
            pl.BlockSpec(wg.shape, const),
            vec(width), vec(width), vec(width), vec(width), vec(width),
            pl.BlockSpec((GROUP, GROUP), const),
            pl.BlockSpec((CHUNK, CHUNK), const),
        ],
        out_specs=pl.BlockSpec((tm, width), lambda b, i: (b * n_t + i, 0)),
        out_shape=jax.ShapeDtypeStruct((T, width), BF16),
        scratch_shapes=[scr() for _ in range(11)]
        + [pltpu.VMEM((width // GROUP, GROUP, GROUP), F32)]
        + [pltpu.VMEM((tm // CHUNK, width // GROUP, GROUP, GROUP), dt)
           for dt in (BF16, F32, BF16, F32)],
        compiler_params=pltpu.CompilerParams(
            dimension_semantics=("arbitrary", "arbitrary"), vmem_limit_bytes=VMEM_LIMIT),
        name="rwkv7",
    )(pr, pr, mu, w0, a0, lora, wg, k_k, k_a, r_k, ln_g, ln_b, bd, tri)


def _outproj_kernel(att_ref, rwk_ref, x_ref, wa_ref, wr_ref, g_ref, wrt_ref, br_ref, upper_ref,
                    x2_ref, xn_ref, eidx_ref, rank_ref, gate_ref, cnt_ref, *, n_exp, tm):
    x2 = x_ref[...] + _dot(att_ref[...], wa_ref[...]) + _dot(rwk_ref[...], wr_ref[...])
    x2_ref[...] = x2
    ms = jnp.mean(x2 * x2, axis=-1, keepdims=True)
    xn = x2 * lax.rsqrt(ms + NORM_EPS) * g_ref[...]
    xn_ref[...] = xn.astype(BF16)

    x_hi = xn.astype(BF16)
    x_lo = (xn - x_hi.astype(F32)).astype(BF16)
    logits = (_dot_nt(wrt_ref[0], x_hi) + _dot_nt(wrt_ref[0], x_lo)
              + _dot_nt(wrt_ref[1], x_hi)) + br_ref[...]

    eio = lax.broadcasted_iota(jnp.int32, (n_exp, tm), 0)
    work = logits
    vals, onehots = [], []
    for k in range(TOP_K):
        m = jnp.max(work, axis=0, keepdims=True)
        idx = jnp.min(jnp.where(work == m, eio, n_exp), axis=0, keepdims=True)
        oh = eio == idx
        vals.append(m)
        onehots.append(oh)
        eidx_ref[k: k + 1, :] = idx
        work = jnp.where(oh, -jnp.inf, work)

    exps = [jnp.exp(vk - vals[0]) for vk in vals]
    denom = exps[0]
    for e in exps[1:]:
        denom = denom + e
    for k in range(TOP_K):
        gate_ref[k: k + 1, :] = exps[k] / denom

    sel = onehots[0]
    for oh in onehots[1:]:
        sel = sel | oh
    sel_f = jnp.where(sel, 1.0, 0.0)
    before = _dot(sel_f.astype(BF16), upper_ref[...])
    for k in range(TOP_K):
        rank_ref[k: k + 1, :] = jnp.sum(
            jnp.where(onehots[k], before, 0.0), axis=0, keepdims=True).astype(jnp.int32)
    cnt_ref[...] = jnp.broadcast_to(
        jnp.sum(sel_f, axis=1, keepdims=True), cnt_ref.shape).astype(jnp.int32)


def _outproj(att, rwk, x2d, w_att, w_rwk, g2, wrt, br, n_exp):
    T, D = x2d.shape
    tm = ROUTE_TILE
    upper = (jnp.arange(tm)[:, None] < jnp.arange(tm)[None, :]).astype(BF16)
    const = lambda i: (0, 0)
    rows = lambda n: pl.BlockSpec((tm, n), lambda i: (i, 0))
    lanes = pl.BlockSpec((TOP_K, tm), lambda i: (0, i))
    return pl.pallas_call(
        functools.partial(_outproj_kernel, n_exp=n_exp, tm=tm),
        grid=(T // tm,),
        in_specs=[
            rows(att.shape[1]), rows(rwk.shape[1]), rows(D),
            pl.BlockSpec(w_att.shape, const), pl.BlockSpec(w_rwk.shape, const),
            pl.BlockSpec((1, D), const),
            pl.BlockSpec(wrt.shape, lambda i: (0, 0, 0)),
            pl.BlockSpec((n_exp, 1), const),
            pl.BlockSpec((tm, tm), const),
        ],
        out_specs=[rows(D), rows(D), lanes, lanes, lanes,
                   pl.BlockSpec((n_exp, LANES), lambda i: (i, 0))],
        out_shape=[
            jax.ShapeDtypeStruct((T, D), F32),
            jax.ShapeDtypeStruct((T, D), BF16),
            jax.ShapeDtypeStruct((TOP_K, T), jnp.int32),
            jax.ShapeDtypeStruct((TOP_K, T), jnp.int32),
            jax.ShapeDtypeStruct((TOP_K, T), F32),
            jax.ShapeDtypeStruct((T // tm * n_exp, LANES), jnp.int32),
        ],
        compiler_params=pltpu.CompilerParams(
            dimension_semantics=("arbitrary",), vmem_limit_bytes=VMEM_LIMIT),
        name="outproj_router",
    )(att, rwk, x2d, w_att, w_rwk, g2, wrt, br, upper)


def _pow2_sizes(limit):
    top = 1 << (limit.bit_length() - 1)
    return tuple(top >> s for s in range((top // RUN_ALIGN).bit_length()))


def _run_copies(action, length, src, src_off, dst, dst_off, sem, sizes=RUN_SIZES):
    for k, size in enumerate(sizes):
        start = length & ~(2 * size - 1)

        @pl.when((length & size) != 0)
        def _():
            cp = pltpu.make_async_copy(
                src.at[pl.ds(pl.multiple_of(src_off + start, RUN_ALIGN), size)],
                dst.at[pl.ds(pl.multiple_of(dst_off + start, RUN_ALIGN), size)], sem)
            cp.start(priority=k % 2) if action == "start" else cp.wait()


def _dispatch_kernel(soff_ref, rlen_ref, rdst_ref, tail_dst_ref, tail_len_ref, nact_ref,
                     slot_ref, xn_ref, xs_hbm, buf, zbuf, sem, run_sems, *, n_exp, slots, n_blocks):
    j = pl.program_id(0)
    tm = ROUTE_TILE

    @pl.when(j == 0)
    def _():
        zbuf[...] = jnp.zeros_like(zbuf)

        def unused_block(action):
            def body(b, c):
                cp = pltpu.make_async_copy(
                    zbuf, xs_hbm.at[pl.ds(pl.multiple_of(b * MOE_ROWS, MOE_ROWS), MOE_ROWS)], sem)
                cp.start() if action == "start" else cp.wait()
                return c
            return body

        for action in ("start", "wait"):
            for e in range(n_exp):
                _run_copies(action, tail_len_ref[e], zbuf, 0, xs_hbm, tail_dst_ref[e], sem)
            lax.fori_loop(nact_ref[0], n_blocks, unused_block(action), 0)

    cur = lax.rem(j, 2)
    x = xn_ref[...]
    for sb in range(slots // GROUP):
        sio = lax.broadcasted_iota(jnp.int32, (GROUP, tm), 0) + sb * GROUP
        p = jnp.zeros((GROUP, tm), F32)
        for k in range(TOP_K):
            p = jnp.where(sio == slot_ref[0, k: k + 1, :], 1.0, p)
        buf[cur, sb * GROUP: (sb + 1) * GROUP, :] = _pack_bf16_pairs(_dot(p.astype(BF16), x))

    def start_runs(tile, half):
        for e in range(n_exp):
            r = tile * n_exp + e
            _run_copies("start", rlen_ref[r], buf.at[half], soff_ref[r], xs_hbm, rdst_ref[r],
                        run_sems.at[half])

    def wait_runs(tile, half):
        last = tile * n_exp + n_exp - 1
        _run_copies("wait", soff_ref[last] + rlen_ref[last], buf.at[half], 0, xs_hbm, 0,
                    run_sems.at[half], sizes=_pow2_sizes(slots))

    start_runs(j, cur)

    @pl.when(j > 0)
    def _():
        wait_runs(j - 1, 1 - cur)

    @pl.when(j == pl.num_programs(0) - 1)
    def _():
        wait_runs(j, cur)


def _dispatch(soff, rlen, rdst, tail_dst, tail_len, n_active, slot_lane, xn, n_blocks, slots):
    T, D = xn.shape
    tm = ROUTE_TILE
    n_exp = tail_dst.shape[0]
    n_rows = n_blocks * MOE_ROWS
    grid_spec = pltpu.PrefetchScalarGridSpec(
        num_scalar_prefetch=6,
        grid=(T // tm,),
        in_specs=[
            pl.BlockSpec((1, TOP_K, tm), lambda i, *_: (i, 0, 0)),
            pl.BlockSpec((tm, D), lambda i, *_: (i, 0)),
        ],
        out_specs=pl.BlockSpec(memory_space=pl.ANY),
        scratch_shapes=[pltpu.VMEM((2, slots, D // 2), jnp.uint32),
                        pltpu.VMEM((MOE_ROWS, D // 2), jnp.uint32),
                        pltpu.SemaphoreType.DMA, pltpu.SemaphoreType.DMA((2,))],
    )
    return pl.pallas_call(
        functools.partial(_dispatch_kernel, n_exp=n_exp, slots=slots, n_blocks=n_blocks),
        grid_spec=grid_spec,
        out_shape=jax.ShapeDtypeStruct((n_rows, D // 2), jnp.uint32),
        compiler_params=pltpu.CompilerParams(
            dimension_semantics=("arbitrary",), vmem_limit_bytes=VMEM_LIMIT),
        name="moe_dispatch",
    )(soff, rlen, rdst, tail_dst, tail_len, n_active, slot_lane, xn)


def _moe_kernel(bexp_ref, nact_ref, x_ref, wup_ref, bup_ref, wdn_ref, bdn_ref,
                y_ref, wup_bf, wdn_bf, *, d_exp):
    i = pl.program_id(0)

    @pl.when(i < nact_ref[0])
    def _():
        new_expert = jnp.logical_or(i == 0, bexp_ref[i] != bexp_ref[jnp.maximum(i - 1, 0)])

        @pl.when(new_expert)
        def _():
            wup_bf[...] = wup_ref[0].astype(BF16)
            wdn_bf[...] = wdn_ref[0].astype(BF16)

        x = _unpack_bf16_pairs(x_ref[...])
        h_glu = _dot(x, wup_bf[:, :d_exp]) + bup_ref[0, :, :d_exp]
        h_lin = _dot(x, wup_bf[:, d_exp:]) + bup_ref[0, :, d_exp:]
        h_glu = jnp.minimum(h_glu, SWIGLU_LIMIT)
        h_lin = jnp.clip(h_lin, -SWIGLU_LIMIT, SWIGLU_LIMIT)
        act = h_glu * _sigmoid(SWIGLU_ALPHA * h_glu) * (h_lin + 1.0)
        y_ref[...] = _pack_bf16_pairs(_dot(act.astype(BF16), wdn_bf[...]) + bdn_ref[0])

    @pl.when(i >= nact_ref[0])
    def _():
        y_ref[...] = jnp.zeros_like(y_ref)


def _moe(blk_expert, n_active, xs, w_up, b_up, w_down, b_down):
    n_blocks = blk_expert.shape[0]
    n_exp, D, two_f = w_up.shape
    d_exp = two_f // 2
    blk = lambda i, na: jnp.minimum(i, na[0] - 1)
    grid_spec = pltpu.PrefetchScalarGridSpec(
        num_scalar_prefetch=2,
        grid=(n_blocks,),
        in_specs=[
            pl.BlockSpec((MOE_ROWS, D // 2), lambda i, be, na: (blk(i, na), 0)),
            pl.BlockSpec((1, D, two_f), lambda i, be, na: (be[blk(i, na)], 0, 0)),
            pl.BlockSpec((1, 1, two_f), lambda i, be, na: (be[blk(i, na)], 0, 0)),
            pl.BlockSpec((1, d_exp, D), lambda i, be, na: (be[blk(i, na)], 0, 0)),
            pl.BlockSpec((1, 1, D), lambda i, be, na: (be[blk(i, na)], 0, 0)),
        ],
        out_specs=pl.BlockSpec((MOE_ROWS, D // 2), lambda i, be, na: (i, 0)),
        scratch_shapes=[
            pltpu.VMEM((D, two_f), BF16),
            pltpu.VMEM((d_exp, D), BF16),
        ],
    )
    return pl.pallas_call(
        functools.partial(_moe_kernel, d_exp=d_exp),
        grid_spec=grid_spec,
        out_shape=jax.ShapeDtypeStruct((n_blocks * MOE_ROWS, D // 2), jnp.uint32),
        compiler_params=pltpu.CompilerParams(
            dimension_semantics=("arbitrary",), vmem_limit_bytes=VMEM_LIMIT),
        name="moe_ffn",
    )(blk_expert, n_active, xs,
      w_up, b_up.reshape(n_exp, 1, two_f), w_down, b_down.reshape(n_exp, 1, D))


def _combine_kernel(soff_ref, rlen_ref, rdst_ref, slot_ref, gate_ref, x2_ref, y_hbm, o_ref,
                    buf, sems, *, n_exp, slots):
    j = pl.program_id(0)
    tm = ROUTE_TILE

    cur = lax.rem(j, 2)

    def start_runs(tile, half):
        for e in range(n_exp):
            r = tile * n_exp + e
            _run_copies("start", rlen_ref[r], y_hbm, rdst_ref[r], buf.at[half], soff_ref[r],
                        sems.at[half])

    @pl.when(j == 0)
    def _():
        buf[...] = jnp.zeros_like(buf)
        start_runs(0, 0)

    @pl.when(j + 1 < pl.num_programs(0))
    def _():
        start_runs(j + 1, 1 - cur)

    last = j * n_exp + n_exp - 1
    _run_copies("wait", soff_ref[last] + rlen_ref[last], y_hbm, 0, buf.at[cur], 0, sems.at[cur],
                sizes=_pow2_sizes(slots))

    slot = slot_ref[...]
    gate = gate_ref[...]
    acc = x2_ref[...]
    for sb in range(slots // GROUP):
        lio = lax.broadcasted_iota(jnp.int32, (tm, GROUP), 1) + sb * GROUP
        w = jnp.zeros((tm, GROUP), F32)
        for k in range(TOP_K):
            w = jnp.where(lio == slot[:, k: k + 1], gate[:, k: k + 1], w)
        acc = acc + _dot(w.astype(BF16),
                         _unpack_bf16_pairs(buf[cur, sb * GROUP: (sb + 1) * GROUP, :]))
    o_ref[...] = acc


def _combine(soff, rlen, rdst, slot_row, gates_row, x2, y_rows, slots):
    T, D = x2.shape
    tm = ROUTE_TILE
    n_exp = soff.shape[0] // (T // tm)
    grid_spec = pltpu.PrefetchScalarGridSpec(
        num_scalar_prefetch=3,
        grid=(T // tm,),
        in_specs=[
            pl.BlockSpec((tm, TOP_K), lambda i, *_: (i, 0)),
            pl.BlockSpec((tm, TOP_K), lambda i, *_: (i, 0)),
            pl.BlockSpec((tm, D), lambda i, *_: (i, 0)),
            pl.BlockSpec(memory_space=pl.ANY),
        ],
        out_specs=pl.BlockSpec((tm, D), lambda i, *_: (i, 0)),
        scratch_shapes=[pltpu.VMEM((2, slots, D // 2), jnp.uint32),
                        pltpu.SemaphoreType.DMA((2,))],
    )
    return pl.pallas_call(
        functools.partial(_combine_kernel, n_exp=n_exp, slots=slots),
        grid_spec=grid_spec,
        out_shape=jax.ShapeDtypeStruct((T, D), F32),
        compiler_params=pltpu.CompilerParams(
            dimension_semantics=("arbitrary",), vmem_limit_bytes=VMEM_LIMIT),
        name="moe_combine",
    )(soff, rlen, rdst, slot_row, gates_row, x2, y_rows)


def _block_diag_ones(n, blk):
    i = jnp.arange(n)
    return ((i[:, None] // blk) == (i[None, :] // blk)).astype(BF16)


def _layer(x, norm1_g, w_in, tshift_mu, q_norm_g, k_norm_g, rel_bias, w0, w_decay_up, a0,
           w_aaa_up, w_gate_up, k_k, k_a, r_k, ln_x_g, ln_x_b, w_out, norm2_g, w_router,
           b_router, w_up, b_up, w_down, b_down):
    B, S, D = x.shape
    T = B * S
    att_heads = rel_bias.shape[0]
    att_w = att_heads * HEAD_DIM
    width = w0.shape[0]
    n_exp = w_router.shape[1]
    assert att_w % LANES == 0 and width % GROUP == 0
    assert S % ATT_QROWS == 0 and ATT_QROWS % ATT_HALO == 0 and ATT_HALO % ATT_SUB == 0
    assert T % ROUTE_TILE == 0

    x2d = x.reshape(T, D)
    row = lambda a: a.reshape(1, -1).astype(F32)
    head_ones = _block_diag_ones(GROUP, HEAD_DIM)

    qkv, pr = _inproj(
        x2d, row(norm1_g), w_in.astype(BF16),
        row(jnp.tile(q_norm_g, att_heads)), row(jnp.tile(k_norm_g, att_heads)), head_ones, att_w)

    att = _attention(qkv, _attention_bias(rel_bias), B, S, att_w)

    lora = jnp.zeros((DECAY_LORA + AAA_LORA, 2 * width), F32)
    lora = lora.at[:DECAY_LORA, :width].set(w_decay_up).at[DECAY_LORA:, width:].set(w_aaa_up)
    rwk = _rwkv(pr, row(tshift_mu), row(w0), row(a0), lora.astype(BF16),
                w_gate_up.astype(BF16), row(k_k), row(k_a), row(r_k), row(ln_x_g),
                row(ln_x_b), head_ones, B, S, width)

    wrt = w_router.T.astype(F32)
    wrt_hi = wrt.astype(BF16)
    wrt_lo = (wrt - wrt_hi.astype(F32)).astype(BF16)
    x2, xn, eidx, rank, gates, counts = _outproj(
        att, rwk, x2d, w_out[:att_w].astype(BF16), w_out[att_w:].astype(BF16), row(norm2_g),
        jnp.stack([wrt_hi, wrt_lo]), b_router.reshape(n_exp, 1).astype(F32), n_exp)

    tm = ROUTE_TILE
    n_tiles = T // tm
    i32 = lambda a: a.astype(jnp.int32)
    run_n = counts.reshape(n_tiles, n_exp, LANES)[:, :, 0]
    run_len = (run_n + RUN_ALIGN - 1) // RUN_ALIGN * RUN_ALIGN
    run_off = jnp.cumsum(run_len, axis=1) - run_len
    exp_rows = jnp.sum(run_len, axis=0)
    exp_pad = (exp_rows + MOE_ROWS - 1) // MOE_ROWS * MOE_ROWS
    exp_end = jnp.cumsum(exp_pad)
    exp_start = exp_end - exp_pad
    run_dst = exp_start[None, :] + jnp.cumsum(run_len, axis=0) - run_len
    eidx_t = eidx.reshape(TOP_K, n_tiles, tm)
    experts = jnp.arange(n_exp, dtype=jnp.int32)[:, None, None, None]
    slot = rank.reshape(TOP_K, n_tiles, tm) + jnp.sum(
        jnp.where(eidx_t[None] == experts, run_off.T[:, None, :, None], 0), axis=0)
    slot_lane = i32(slot.transpose(1, 0, 2))
    slot_row = i32(slot.transpose(1, 2, 0).reshape(T, TOP_K))
    slots = -(-(tm * TOP_K + n_exp * (RUN_ALIGN - 1)) // GROUP) * GROUP
    n_blocks = -(-(T * TOP_K + n_tiles * n_exp * (RUN_ALIGN - 1)) // MOE_ROWS) + n_exp
    blk_start = jnp.arange(n_blocks, dtype=jnp.int32) * MOE_ROWS
    blk_expert = i32(jnp.minimum(
        jnp.sum(blk_start[:, None] >= exp_end[None, :], axis=1), n_exp - 1))
    n_active = i32(exp_end[-1:] // MOE_ROWS)
    soff, rlen, rdst = (i32(a.reshape(-1)) for a in (run_off, run_len, run_dst))

    xs = _dispatch(soff, rlen, rdst, i32(exp_start + exp_rows), i32(exp_pad - exp_rows),
                   n_active, slot_lane, xn, n_blocks, slots)
    y_rows = _moe(blk_expert, n_active, xs, w_up, b_up, w_down, b_down)
    out = _combine(soff, rlen, rdst, slot_row, gates.T, x2, y_rows, slots)
    return out.reshape(B, S, D)


def kernel(x, norm1_g, w_in, tshift_mu, q_norm_g, k_norm_g, rel_bias, w0, w_decay_up, a0,
           w_aaa_up, w_gate_up, k_k, k_a, r_k, ln_x_g, ln_x_b, w_out, norm2_g, w_router,
           b_router, w_up, b_up, w_down, b_down):
    params = (norm1_g, w_in, tshift_mu, q_norm_g, k_norm_g, rel_bias, w0, w_decay_up, a0,
              w_aaa_up, w_gate_up, k_k, k_a, r_k, ln_x_g, ln_x_b, w_out, norm2_g, w_router,
              b_router, w_up, b_up, w_down, b_down)
    for l in range(norm1_g.shape[0]):
        x = _layer(x, *[p[l] for p in params])
    return x
```
